```python
import jax, jax.numpy as jnp
from jax import lax
import numpy as np

D_MODEL = 2048
BATCH = 2
SEQ = 4096
DEPTH = 1

CHUNK = 64
N_MEM = 256
EPS = 1e-6
GLA_HEADS = 4
GLA_DK = D_MODEL // 16
GLA_DV = D_MODEL // 8
GLA_LOWRANK = 16
GLA_GATE_NORMALIZER = 16.0
GLA_WIDTH = GLA_HEADS * GLA_DV
HGRN_HEADS = 8
HGRN_DH = D_MODEL // 16
HGRN_WIDTH = HGRN_HEADS * HGRN_DH
MIX_WIDTH = GLA_WIDTH + HGRN_WIDTH
IN_SPLITS = (GLA_HEADS * GLA_DK, GLA_HEADS * GLA_DK, GLA_WIDTH, GLA_WIDTH, GLA_LOWRANK,
             HGRN_WIDTH, HGRN_WIDTH, HGRN_WIDTH, HGRN_WIDTH)
IN_WIDTH = 512 + 512 + 1024 + 1024 + 16 + 4 * 1024
XA_HEADS = 4
XA_DH = D_MODEL // XA_HEADS
N_GROUPS = 4
EXPERTS_PER_GROUP = 8
N_EXPERTS = N_GROUPS * EXPERTS_PER_GROUP
TOP_K = 2
D_EXPERT = D_MODEL // 2
EXPERT_BLOCK = 128

kernel_name = "hybrid_gla_hgrn2_memxattn_hmoe"


def rmsnorm(x, g):
    xf = x.astype(jnp.float32)
    y = xf * lax.rsqrt(jnp.mean(xf * xf, axis=-1, keepdims=True) + EPS) * g.astype(jnp.float32)
    return y.astype(x.dtype)


def to_chunks(t, n_heads):
    B, T, _ = t.shape
    return t.reshape(B, T // CHUNK, CHUNK, n_heads, -1).transpose(1, 0, 3, 2, 4).astype(jnp.float32)


def from_chunks(t):
    N, B, H, C, d = t.shape
    return t.transpose(1, 0, 3, 2, 4).reshape(B, N * C, H, d)


def chunk_gated_linear_attention(q, k, v, log_a, n_heads, scale):
    qc = to_chunks(q, n_heads) * scale
    kc = to_chunks(k, n_heads)
    vc = to_chunks(v, n_heads)
    bc = jnp.cumsum(to_chunks(log_a, n_heads), axis=3)
    causal = jnp.tril(jnp.ones((CHUNK, CHUNK), dtype=bool))

    def step(S, inp):
        q_, k_, v_, b_ = inp
        rel = b_[:, :, :, None, :] - b_[:, :, None, :, :]
        decay = jnp.exp(jnp.where(causal[:, :, None], rel, -jnp.inf))
        A = jnp.einsum('bhid,bhjd,bhijd->bhij', q_, k_, decay)
        b_last = b_[:, :, -1:, :]
        o = (jnp.einsum('bhij,bhjv->bhiv', A, v_)
             + jnp.einsum('bhid,bhdv->bhiv', q_ * jnp.exp(b_), S))
        S = (jnp.exp(b_last[:, :, 0, :])[..., None] * S
             + jnp.einsum('bhjd,bhjv->bhdv', k_ * jnp.exp(b_last - b_), v_))
        return S, o

    B = q.shape[0]
    S0 = jnp.zeros((B, n_heads, qc.shape[-1], vc.shape[-1]), jnp.float32)
    _, o = lax.scan(step, S0, (qc, kc, vc, bc))
    return from_chunks(o)


def hybrid_mixer(h, w_in, w_alpha_up, b_alpha, gla_norm_g, lb, hgrn_norm_g, w_out):
    B, T, _ = h.shape
    f32 = jnp.float32
    proj = h @ w_in
    offsets = [int(o) for o in np.cumsum(IN_SPLITS)[:-1]]
    gq, gk, gv, gg, g_lr, hq, hf, hi, hg = jnp.split(proj, offsets, axis=-1)

    log_alpha = jax.nn.log_sigmoid((g_lr @ w_alpha_up + b_alpha).astype(f32)) / GLA_GATE_NORMALIZER
    o_gla = chunk_gated_linear_attention(gq, gk, gv, log_alpha, GLA_HEADS, GLA_DK ** -0.5)
    o_gla = rmsnorm(o_gla, gla_norm_g) * jax.nn.silu(gg.reshape(B, T, GLA_HEADS, GLA_DV).astype(f32))

    z = hf.astype(f32)
    log_f = jnp.logaddexp(jnp.log(lb), jnp.log1p(-lb) + jax.nn.log_sigmoid(z))
    k_h = (1.0 - lb) * jax.nn.sigmoid(-z)
    o_h = chunk_gated_linear_attention(jax.nn.silu(hq.astype(f32)), k_h, hi, log_f, HGRN_HEADS, 1.0)
    o_h = rmsnorm(o_h, hgrn_norm_g) * jax.nn.silu(hg.reshape(B, T, HGRN_HEADS, HGRN_DH).astype(f32))

    o = jnp.concatenate([o_gla.reshape(B, T, GLA_WIDTH), o_h.reshape(B, T, HGRN_WIDTH)], axis=-1)
    return o.astype(h.dtype) @ w_out


def memory_cross_attention(h, mem_n, w_q, w_kv, w_o):
    B, T, D = h.shape
    M = mem_n.shape[1]
    q = (h @ w_q).reshape(B, T, XA_HEADS, XA_DH)
    k, v = jnp.split(mem_n @ w_kv, 2, axis=-1)
    k = k.reshape(B, M, XA_HEADS, XA_DH)
    v = v.reshape(B, M, XA_HEADS, XA_DH)
    s = jnp.einsum('bqhd,bkhd->bhqk', q, k).astype(jnp.float32) * (XA_DH ** -0.5)
    p = jax.nn.softmax(s, axis=-1).astype(v.dtype)
    o = jnp.einsum('bhqk,bkhd->bqhd', p, v).reshape(B, T, D)
    return o @ w_o


def grouped_expert_mlp(xt, expert_idx, w_gate, w_up, w_down):
    M, K = expert_idx.shape
    D = xt.shape[-1]
    A = M * K
    flat_e = expert_idx.reshape(A)
    order = jnp.argsort(flat_e)
    sorted_e = flat_e[order]
    counts = jnp.bincount(flat_e, length=N_EXPERTS)
    padded = ((counts + EXPERT_BLOCK - 1) // EXPERT_BLOCK) * EXPERT_BLOCK
    pad_end = jnp.cumsum(padded)
    pad_start = pad_end - padded
    start = jnp.cumsum(counts) - counts
    dest = pad_start[sorted_e] + (jnp.arange(A) - start[sorted_e])
    n_rows = A + N_EXPERTS * EXPERT_BLOCK
    n_blocks = n_rows // EXPERT_BLOCK
    buf = jnp.zeros((n_rows, D), xt.dtype).at[dest].set(xt[order // K])
    block_e = jnp.minimum(jnp.searchsorted(pad_end, jnp.arange(n_blocks) * EXPERT_BLOCK, side='right'),
                          N_EXPERTS - 1)

    def block_fn(args):
        xb, e = args
        hb = jax.nn.silu(xb @ w_gate[e]) * (xb @ w_up[e])
        return hb @ w_down[e]

    yb = lax.map(block_fn, (buf.reshape(n_blocks, EXPERT_BLOCK, D), block_e)).reshape(n_rows, D)
    y = jnp.zeros((A, D), yb.dtype).at[order].set(yb[dest])
    return y.reshape(M, K, D)


def hierarchical_moe(h, w_rg, b_rg, w_re, b_re, w_gate, w_up, w_down):
    B, T, D = h.shape
    xt = h.reshape(B * T, D)
    g_prob = jax.nn.softmax((xt @ w_rg).astype(jnp.float32) + b_rg.astype(jnp.float32), axis=-1)
    p_group, group = lax.top_k(g_prob, 1)
    e_logits = ((xt @ w_re).astype(jnp.float32) + b_re.astype(jnp.float32)).reshape(-1, N_GROUPS, EXPERTS_PER_GROUP)
    e_logits = jnp.take_along_axis(e_logits, group[:, :, None], axis=1)[:, 0]
    top_val, top_idx = lax.top_k(e_logits, TOP_K)
    gate = jax.nn.softmax(top_val, axis=-1) * p_group
    expert_idx = group * EXPERTS_PER_GROUP + top_idx
    y = grouped_expert_mlp(xt, expert_idx, w_gate, w_up, w_down)
    out = jnp.einsum('mk,mkd->md', gate.astype(y.dtype), y)
    return out.reshape(B, T, D)


def setup_inputs(seed: int = 0) -> dict:
    key = jax.random.key(seed)
    ks = jax.random.split(key, 32)
    f32 = jnp.float32
    D, L = D_MODEL, DEPTH

    def nrm(k, shape, scale):
        return jax.random.normal(k, shape, f32) * scale

    def gain(k, shape):
        return 1.0 + 0.02 * jax.random.normal(k, shape, f32)

    return {
        "x": nrm(ks[0], (BATCH, SEQ, D), 1.0),
        "mem": nrm(ks[1], (BATCH, N_MEM, D), 1.0),
        "norm_mix_g": gain(ks[2], (L, D)),
        "w_in": nrm(ks[3], (L, D, IN_WIDTH), D ** -0.5),
        "w_gla_alpha_up": nrm(ks[4], (L, GLA_LOWRANK, GLA_HEADS * GLA_DK), GLA_LOWRANK ** -0.5),
        "b_gla_alpha": nrm(ks[5], (L, GLA_HEADS * GLA_DK), 0.1),
        "gla_out_norm_g": gain(ks[6], (L, GLA_DV)),
        "hgrn_lb_logits": nrm(ks[7], (L + 1, HGRN_WIDTH), 0.5),
        "hgrn_out_norm_g": gain(ks[8], (L, HGRN_DH)),
        "w_mix_out": nrm(ks[9], (L, MIX_WIDTH, D), MIX_WIDTH ** -0.5),
        "norm_xattn_g": gain(ks[10], (L, D)),
        "norm_mem_g": gain(ks[11], (L, D)),
        "w_xattn_q": nrm(ks[12], (L, D, D), D ** -0.5),
        "w_xattn_kv": nrm(ks[13], (L, D, 2 * D), D ** -0.5),
        "w_xattn_out": nrm(ks[14], (L, D, D), D ** -0.5),
        "norm_ffn_g": gain(ks[15], (L, D)),
        "w_router_group": nrm(ks[16], (L, D, N_GROUPS), D ** -0.5),
        "b_router_group": nrm(ks[17], (L, N_GROUPS), 0.01),
        "w_router_expert": nrm(ks[18], (L, D, N_EXPERTS), D ** -0.5),
        "b_router_expert": nrm(ks[19], (L, N_EXPERTS), 0.01),
        "w_expert_gate": nrm(ks[20], (L, N_EXPERTS, D, D_EXPERT), D ** -0.5),
        "w_expert_up": nrm(ks[21], (L, N_EXPERTS, D, D_EXPERT), D ** -0.5),
        "w_expert_down": nrm(ks[22], (L, N_EXPERTS, D_EXPERT, D), D_EXPERT ** -0.5),
        "norm_final_g": gain(ks[23], (D,)),
    }


def reference(x, mem, norm_mix_g, w_in, w_gla_alpha_up, b_gla_alpha, gla_out_norm_g, hgrn_lb_logits,
              hgrn_out_norm_g, w_mix_out, norm_xattn_g, norm_mem_g, w_xattn_q, w_xattn_kv, w_xattn_out,
              norm_ffn_g, w_router_group, b_router_group, w_router_expert, b_router_expert,
              w_expert_gate, w_expert_up, w_expert_down, norm_final_g):
    lb_all = jnp.cumsum(jax.nn.softmax(hgrn_lb_logits.astype(jnp.float32), axis=0), axis=0)
    h = x
    for l in range(DEPTH):
        a = rmsnorm(h, norm_mix_g[l])
        h = h + hybrid_mixer(a, w_in[l], w_gla_alpha_up[l], b_gla_alpha[l], gla_out_norm_g[l],
                             lb_all[l], hgrn_out_norm_g[l], w_mix_out[l])
        a = rmsnorm(h, norm_xattn_g[l])
        h = h + memory_cross_attention(a, rmsnorm(mem, norm_mem_g[l]), w_xattn_q[l], w_xattn_kv[l], w_xattn_out[l])
        a = rmsnorm(h, norm_ffn_g[l])
        h = h + hierarchical_moe(a, w_router_group[l], b_router_group[l], w_router_expert[l], b_router_expert[l],
                                 w_expert_gate[l], w_expert_up[l], w_expert_down[l])
    return rmsnorm(h, norm_final_g)
```

```python
import functools

import jax
import jax.numpy as jnp
from jax import lax
from jax.experimental import pallas as pl
from jax.experimental.pallas import tpu as pltpu

F32 = jnp.float32
BF16 = jnp.bfloat16
HIGHEST = lax.Precision.HIGHEST

EPS = 1e-6
CHUNK = 64
SUB = 16
LANES = 128
GLA_HEADS, GLA_DK, GLA_DV = 4, 128, 256
HGRN_HEADS, HGRN_DH = 8, 128
XA_HEADS, XA_DH = 4, 512
N_GROUPS, EXPERTS_PER_GROUP, N_EXPERTS = 4, 8, 32
EXPERT_BLOCK = 128
VMEM_LIMIT = 56 * 1024 * 1024


def _cparams(sem, vmem=VMEM_LIMIT):
    return pltpu.CompilerParams(dimension_semantics=sem, vmem_limit_bytes=vmem)


def _log_sigmoid(z):
    return jnp.minimum(z, 0.0) - jnp.log1p(jnp.exp(-jnp.abs(z)))


def _sigmoid(z):
    return 1.0 / (1.0 + jnp.exp(-z))


def _rms(x, g):
    return x * lax.rsqrt(jnp.mean(x * x, axis=-1, keepdims=True) + EPS) * g


NORM_ROWS = 256


def _for_row_chunks(n_rows, fn):
    step = min(NORM_ROWS, n_rows)

    def body(ci, carry):
        fn(pl.ds(pl.multiple_of(ci * step, step), step))
        return carry

    lax.fori_loop(0, n_rows // step, body, 0)


def _inproj_kernel(x_ref, g_ref, w_ref, wlr_ref, wup_ref, bal_ref, proj_ref, la_ref, a_ref):
    @pl.when(pl.program_id(1) == 0)
    def _():
        def rows_fn(rows):
            a = _rms(x_ref[rows, :], g_ref[...]).astype(BF16)
            a_ref[rows, :] = a
            lr = jnp.dot(a, wlr_ref[...], preferred_element_type=F32)
            z = jnp.dot(lr.astype(BF16), wup_ref[...], preferred_element_type=F32) + bal_ref[...]
            la_ref[rows, :] = _log_sigmoid(z) * (1.0 / 16.0)

        _for_row_chunks(x_ref.shape[0], rows_fn)

    proj_ref[...] = jnp.dot(a_ref[...], w_ref[...], preferred_element_type=F32)


def _inproj(x, g, w_main, w_lr, w_up, b_al, *, tm, tn):
    m, d = x.shape
    n = w_main.shape[1]
    nk = w_up.shape[1]
    return pl.pallas_call(
        _inproj_kernel,
        grid=(m // tm, n // tn),
        in_specs=[
            pl.BlockSpec((tm, d), lambda i, j: (i, 0)),
            pl.BlockSpec((1, d), lambda i, j: (0, 0)),
            pl.BlockSpec((d, tn), lambda i, j: (0, j)),
            pl.BlockSpec((d, LANES), lambda i, j: (0, 0)),
            pl.BlockSpec((LANES, nk), lambda i, j: (0, 0)),
            pl.BlockSpec((1, nk), lambda i, j: (0, 0)),
        ],
        out_specs=[
            pl.BlockSpec((tm, tn), lambda i, j: (i, j)),
            pl.BlockSpec((tm, nk), lambda i, j: (i, 0)),
        ],
        out_shape=[jax.ShapeDtypeStruct((m, n), F32), jax.ShapeDtypeStruct((m, nk), F32)],
        scratch_shapes=[pltpu.VMEM((tm, d), BF16)],
        compiler_params=_cparams(("parallel", "arbitrary")),
        name="inproj",
    )(x, g, w_main, w_lr, w_up, b_al)


def _norm_mm_kernel(x_ref, g_ref, w_ref, o_ref, a_ref):
    @pl.when(pl.program_id(1) == 0)
    def _():
        def rows_fn(rows):
            a_ref[rows, :] = _rms(x_ref[rows, :], g_ref[...]).astype(BF16)

        _for_row_chunks(x_ref.shape[0], rows_fn)

    o_ref[...] = jnp.dot(a_ref[...], w_ref[...].astype(BF16), preferred_element_type=F32).astype(o_ref.dtype)


def _norm_mm(x, g, w, *, tm, tn, out_dtype, name):
    m, d = x.shape
    n = w.shape[1]
    return pl.pallas_call(
        _norm_mm_kernel,
        grid=(m // tm, n // tn),
        in_specs=[
            pl.BlockSpec((tm, d), lambda i, j: (i, 0)),
            pl.BlockSpec((1, d), lambda i, j: (0, 0)),
            pl.BlockSpec((d, tn), lambda i, j: (0, j)),
        ],
        out_specs=pl.BlockSpec((tm, tn), lambda i, j: (i, j)),
        out_shape=jax.ShapeDtypeStruct((m, n), out_dtype),
        scratch_shapes=[pltpu.VMEM((tm, d), BF16)],
        compiler_params=_cparams(("parallel", "arbitrary")),
        name=name,
    )(x, g, w)


def _mm_res_kernel(*refs, n_lhs):
    lhs = refs[:n_lhs]
    w_refs = refs[n_lhs:2 * n_lhs]
    res_ref, o_ref = refs[2 * n_lhs], refs[2 * n_lhs + 1]
    acc = res_ref[...]
    for l_ref, w_ref in zip(lhs, w_refs):
        acc = acc + jnp.dot(l_ref[...], w_ref[...].astype(BF16), preferred_element_type=F32)
    o_ref[...] = acc


def _mm_res(lhs_parts, w, res, *, tm, tn, name):
    m, n = res.shape
    n_lhs = len(lhs_parts)
    kp = lhs_parts[0].shape[1]
    in_specs = [pl.BlockSpec((tm, kp), lambda i, j: (i, 0)) for _ in lhs_parts]
    in_specs += [pl.BlockSpec((kp, tn), functools.partial(lambda i, j, p: (p, j), p=p)) for p in range(n_lhs)]
    in_specs += [pl.BlockSpec((tm, tn), lambda i, j: (i, j))]
    return pl.pallas_call(
        functools.partial(_mm_res_kernel, n_lhs=n_lhs),
        grid=(m // tm, n // tn),
        in_specs=in_specs,
        out_specs=pl.BlockSpec((tm, tn), lambda i, j: (i, j)),
        out_shape=jax.ShapeDtypeStruct((m, n), F32),
        compiler_params=_cparams(("parallel", "arbitrary")),
        name=name,
    )(*lhs_parts, *([w] * n_lhs), res)


def _mix_chunk(q, k, v, la, st_ref):
    c, dk = q.shape
    row = lax.broadcasted_iota(jnp.int32, (c, c), 0)
    col = lax.broadcasted_iota(jnp.int32, (c, c), 1)
    tri = (col <= row).astype(F32)
    b = jnp.dot(tri, la, precision=HIGHEST, preferred_element_type=F32)

    n_sub = c // SUB
    a_off = [jnp.zeros((SUB, c), F32)]
    for s in range(1, n_sub):
        ref = b[SUB * s - 1:SUB * s, :]
        qs = q[SUB * s:SUB * (s + 1), :] * jnp.exp(b[SUB * s:SUB * (s + 1), :] - ref)
        ks = k * jnp.exp(jnp.minimum(ref - b, 0.0))
        a_off.append(lax.dot_general(qs.astype(BF16), ks.astype(BF16), (((1,), (1,)), ((), ())),
                                     preferred_element_type=F32))
    a_off = jnp.concatenate(a_off, axis=0)

    rowk = lax.broadcasted_iota(jnp.int32, (c, dk), 0)
    sub_pos = jnp.bitwise_and(rowk, SUB - 1)
    a_diag = jnp.zeros((c, c), F32)
    for t in range(SUB):
        if t == 0:
            p = q * k
        else:
            kt = pltpu.roll(k, t, 0)
            bt = pltpu.roll(b, t, 0)
            p = jnp.where(sub_pos >= t, q * kt * jnp.exp(b - bt), 0.0)
        d = jnp.sum(p, axis=1, keepdims=True)
        a_diag = jnp.where(col == row - t, d, a_diag)

    sub_start = jnp.bitwise_and(row, -SUB)
    a = jnp.where(col < sub_start, a_off, a_diag)

    st = st_ref[...]
    o = jnp.dot(a.astype(BF16), v.astype(BF16), preferred_element_type=F32)
    o = o + lax.dot_general((q * jnp.exp(b)).astype(BF16), st.astype(BF16), (((1,), (1,)), ((), ())),
                            preferred_element_type=F32)
    b_last = b[c - 1:c, :]
    kx = k * jnp.exp(b_last - b)
    st_ref[...] = jnp.exp(b_last) * st + lax.dot_general(
        v.astype(BF16), kx.astype(BF16), (((0,), (0,)), ((), ())), preferred_element_type=F32)
    return o


def _gla_kernel(q_ref, k_ref, v_ref, g_ref, la_ref, gn_ref, o_ref, st_ref, *, n_chunk):
    @pl.when(pl.program_id(2) == 0)
    def _():
        st_ref[...] = jnp.zeros_like(st_ref)

    def body(ci, carry):
        r0 = pl.multiple_of(ci * CHUNK, CHUNK)
        rows = pl.ds(r0, CHUNK)
        o = _mix_chunk(q_ref[rows, :] * (GLA_DK ** -0.5), k_ref[rows, :], v_ref[rows, :], la_ref[rows, :], st_ref)
        g = g_ref[rows, :]
        o_ref[rows, :] = (_rms(o, gn_ref[...]) * (g * _sigmoid(g))).astype(o_ref.dtype)
        return carry

    lax.fori_loop(0, n_chunk, body, 0)


def _gla(proj, la, gn, *, batch, seq, tb):
    m = proj.shape[0]
    nt = seq // tb
    rowmap = lambda b, h, t: b * nt + t
    dk, dv = GLA_DK, GLA_DV
    return pl.pallas_call(
        functools.partial(_gla_kernel, n_chunk=tb // CHUNK),
        grid=(batch, GLA_HEADS, nt),
        in_specs=[
            pl.BlockSpec((tb, dk), lambda b, h, t: (rowmap(b, h, t), h)),
            pl.BlockSpec((tb, dk), lambda b, h, t: (rowmap(b, h, t), GLA_HEADS + h)),
            pl.BlockSpec((tb, dv), lambda b, h, t: (rowmap(b, h, t), GLA_HEADS + h)),
            pl.BlockSpec((tb, dv), lambda b, h, t: (rowmap(b, h, t), 2 * GLA_HEADS + h)),
            pl.BlockSpec((tb, dk), lambda b, h, t: (rowmap(b, h, t), h)),
            pl.BlockSpec((1, dv), lambda b, h, t: (0, 0)),
        ],
        out_specs=pl.BlockSpec((tb, dv), lambda b, h, t: (rowmap(b, h, t), h)),
        out_shape=jax.ShapeDtypeStruct((m, GLA_HEADS * dv), BF16),
        scratch_shapes=[pltpu.VMEM((dv, dk), F32)],
        compiler_params=_cparams(("parallel", "parallel", "arbitrary")),
        name="gla",
    )(proj, proj, proj, proj, la, gn)


def _hgrn_kernel(q_ref, f_ref, i_ref, g_ref, lb_ref, gn_ref, o_ref, st_ref, *, n_chunk):
    @pl.when(pl.program_id(2) == 0)
    def _():
        st_ref[...] = jnp.zeros_like(st_ref)

    log_lb = lb_ref[0:1, :]
    log_1mlb = lb_ref[1:2, :]
    one_m_lb = lb_ref[2:3, :]

    def body(ci, carry):
        r0 = pl.multiple_of(ci * CHUNK, CHUNK)
        rows = pl.ds(r0, CHUNK)
        hq = q_ref[rows, :]
        z = f_ref[rows, :]
        x1 = log_1mlb + _log_sigmoid(z)
        la = jnp.maximum(log_lb, x1) + jnp.log1p(jnp.exp(-jnp.abs(log_lb - x1)))
        k = one_m_lb * _sigmoid(-z)
        o = _mix_chunk(hq * _sigmoid(hq), k, i_ref[rows, :], la, st_ref)
        g = g_ref[rows, :]
        o_ref[rows, :] = (_rms(o, gn_ref[...]) * (g * _sigmoid(g))).astype(o_ref.dtype)
        return carry

    lax.fori_loop(0, n_chunk, body, 0)


def _hgrn(proj, lbc, gn, *, batch, seq, tb):
    m = proj.shape[0]
    nt = seq // tb
    dh = HGRN_DH
    base = 3072 // dh
    rowmap = lambda b, h, t: b * nt + t
    spec = lambda off: pl.BlockSpec((tb, dh), lambda b, h, t: (rowmap(b, h, t), base + off * HGRN_HEADS + h))
    return pl.pallas_call(
        functools.partial(_hgrn_kernel, n_chunk=tb // CHUNK),
        grid=(batch, HGRN_HEADS, nt),
        in_specs=[spec(0), spec(1), spec(2), spec(3),
                  pl.BlockSpec((8, dh), lambda b, h, t: (0, h)),
                  pl.BlockSpec((1, dh), lambda b, h, t: (0, 0))],
        out_specs=pl.BlockSpec((tb, dh), lambda b, h, t: (rowmap(b, h, t), h)),
        out_shape=jax.ShapeDtypeStruct((m, HGRN_HEADS * dh), BF16),
        scratch_shapes=[pltpu.VMEM((dh, dh), F32)],
        compiler_params=_cparams(("parallel", "parallel", "arbitrary")),
        name="hgrn",
    )(proj, proj, proj, proj, lbc, gn)


def _xattn_kernel(q_ref, k_ref, v_ref, o_ref):
    for h in range(XA_HEADS):
        cols = slice(h * XA_DH, (h + 1) * XA_DH)
        s = lax.dot_general(q_ref[:, cols], k_ref[:, cols], (((1,), (1,)), ((), ())),
                            preferred_element_type=F32) * (XA_DH ** -0.5)
        p = jnp.exp(s - jnp.max(s, axis=-1, keepdims=True))
        p = p / jnp.sum(p, axis=-1, keepdims=True)
        o_ref[:, cols] = jnp.dot(p.astype(BF16), v_ref[:, cols], preferred_element_type=F32).astype(o_ref.dtype)


def _xattn(q, kv, *, batch, seq, n_mem, tq):
    m, d = q.shape
    nt = seq // tq
    return pl.pallas_call(
        _xattn_kernel,
        grid=(batch, nt),
        in_specs=[
            pl.BlockSpec((tq, d), lambda b, t: (b * nt + t, 0)),
            pl.BlockSpec((n_mem, d), lambda b, t: (b, 0)),
            pl.BlockSpec((n_mem, d), lambda b, t: (b, 1)),
        ],
        out_specs=pl.BlockSpec((tq, d), lambda b, t: (b * nt + t, 0)),
        out_shape=jax.ShapeDtypeStruct((m, d), BF16),
        compiler_params=_cparams(("parallel", "arbitrary")),
        name="xattn",
    )(q, kv, kv)


def _router_kernel(h_ref, g_ref, wr_ref, br_ref, a_ref, meta_ref, cnt_ref, carry_ref):
    tm = h_ref.shape[0]

    @pl.when(pl.program_id(0) == 0)
    def _():
        carry_ref[...] = jnp.zeros_like(carry_ref)

    a = _rms(h_ref[...], g_ref[...])
    a_ref[...] = a
    logits = jnp.dot(a, wr_ref[...], precision=HIGHEST, preferred_element_type=F32) + br_ref[...]
    lane = lax.broadcasted_iota(jnp.int32, (tm, LANES), 1)
    lane_f = lane.astype(F32)
    neg = -jnp.inf

    gl = jnp.where(lane < N_GROUPS, logits, neg)
    gmax = jnp.max(gl, axis=1, keepdims=True)
    gidx = jnp.min(jnp.where(gl == gmax, lane_f, float(LANES)), axis=1, keepdims=True)
    p_group = 1.0 / jnp.sum(jnp.exp(gl - gmax), axis=1, keepdims=True)
    lo = float(N_GROUPS) + gidx * float(EXPERTS_PER_GROUP)
    el = jnp.where((lane_f >= lo) & (lane_f < lo + float(EXPERTS_PER_GROUP)), logits, neg)
    v1 = jnp.max(el, axis=1, keepdims=True)
    i1 = jnp.min(jnp.where(el == v1, lane_f, float(LANES)), axis=1, keepdims=True)
    el2 = jnp.where(lane_f == i1, neg, el)
    v2 = jnp.max(el2, axis=1, keepdims=True)
    i2 = jnp.min(jnp.where(el2 == v2, lane_f, float(LANES)), axis=1, keepdims=True)
    t = jnp.exp(v2 - v1)
    g1 = p_group / (1.0 + t)
    g2 = p_group * t / (1.0 + t)

    hit1 = lane_f == i1
    hit2 = lane_f == i2
    onehot = jnp.where(hit1 | hit2, 1.0, 0.0)
    row = lax.broadcasted_iota(jnp.int32, (tm, tm), 0)
    col = lax.broadcasted_iota(jnp.int32, (tm, tm), 1)
    strict = jnp.where(col < row, 1.0, 0.0).astype(BF16)
    before = jnp.dot(strict, onehot.astype(BF16), preferred_element_type=F32) + carry_ref[0:1, :]
    r1 = jnp.sum(jnp.where(hit1, before, 0.0), axis=1, keepdims=True)
    r2 = jnp.sum(jnp.where(hit2, before, 0.0), axis=1, keepdims=True)
    carry_ref[...] = carry_ref[...] + jnp.sum(onehot, axis=0, keepdims=True)
    cnt_ref[...] = carry_ref[...]

    meta = jnp.zeros((tm, LANES), F32)
    for idx, val in enumerate((i1 - float(N_GROUPS), i2 - float(N_GROUPS), r1, r2, g1, g2)):
        meta = jnp.where(lane == idx, val, meta)
    meta_ref[...] = meta


def _router(h, g, wr, br, *, tm):
    m, d = h.shape
    return pl.pallas_call(
        _router_kernel,
        grid=(m // tm,),
        in_specs=[
            pl.BlockSpec((tm, d), lambda i: (i, 0)),
            pl.BlockSpec((1, d), lambda i: (0, 0)),
            pl.BlockSpec((d, LANES), lambda i: (0, 0)),
            pl.BlockSpec((1, LANES), lambda i: (0, 0)),
        ],
        out_specs=[
            pl.BlockSpec((tm, d), lambda i: (i, 0)),
            pl.BlockSpec((tm, LANES), lambda i: (i, 0)),
            pl.BlockSpec((8, LANES), lambda i: (0, 0)),
        ],
        out_shape=[jax.ShapeDtypeStruct((m, d), F32), jax.ShapeDtypeStruct((m, LANES), F32),
                   jax.ShapeDtypeStruct((8, LANES), F32)],
        scratch_shapes=[pltpu.VMEM((8, LANES), F32)],
        compiler_params=_cparams(("arbitrary",)),
        name="router",
    )(h, g, wr, br)


def _row_gather_start(src_hbm, idx_ref, base, dst_ref, sem, n_rows):
    def body(r, carry):
        pltpu.make_async_copy(src_hbm.at[pl.ds(idx_ref[base + r], 1), :], dst_ref.at[pl.ds(r, 1), :], sem).start()
        return carry
    lax.fori_loop(0, n_rows, body, 0)


def _row_gather_wait(src_hbm, dst_ref, sem, n_rows):
    pltpu.make_async_copy(src_hbm.at[pl.ds(0, n_rows), :], dst_ref, sem).wait()


def _expert_a_kernel(src_ref, be_ref, nb_ref, a_hbm, wg_ref, wu_ref, hb_ref, xg_ref, wgb_ref, wub_ref, sem):
    i = pl.program_id(0)
    nb = nb_ref[0]
    slot = lax.rem(i, 2)
    rb = hb_ref.shape[0]

    @pl.when(i == 0)
    def _():
        _row_gather_start(a_hbm, src_ref, 0, xg_ref.at[0], sem.at[0], rb)

    @pl.when(i + 1 < nb)
    def _():
        _row_gather_start(a_hbm, src_ref, (i + 1) * rb, xg_ref.at[1 - slot], sem.at[1 - slot], rb)

    @pl.when(i < nb)
    def _():
        @pl.when((i == 0) | (be_ref[i] != be_ref[jnp.maximum(i - 1, 0)]))
        def _():
            wgb_ref[...] = wg_ref[0].astype(BF16)
            wub_ref[...] = wu_ref[0].astype(BF16)

        _row_gather_wait(a_hbm, xg_ref.at[slot], sem.at[slot], rb)
        x = xg_ref[slot].astype(BF16)
        hg = jnp.dot(x, wgb_ref[...], preferred_element_type=F32)
        hu = jnp.dot(x, wub_ref[...], preferred_element_type=F32)
        hb_ref[...] = (hg * _sigmoid(hg) * hu).astype(hb_ref.dtype)

    @pl.when(i >= nb)
    def _():
        hb_ref[...] = jnp.zeros_like(hb_ref)


def _expert_a(src_row, block_e, n_blk, a, wg, wu, *, rb):
    n_rows = src_row.shape[0]
    n_blocks = n_rows // rb
    d = a.shape[1]
    de = wg.shape[2]
    wmap = lambda i, src, be, nb: (be[jnp.minimum(i, nb[0] - 1)], 0, 0)
    grid_spec = pltpu.PrefetchScalarGridSpec(
        num_scalar_prefetch=3,
        grid=(n_blocks,),
        in_specs=[
            pl.BlockSpec(memory_space=pl.ANY),
            pl.BlockSpec((1, d, de), wmap),
            pl.BlockSpec((1, d, de), wmap),
        ],
        out_specs=pl.BlockSpec((rb, de), lambda i, src, be, nb: (i, 0)),
        scratch_shapes=[pltpu.VMEM((2, rb, d), F32), pltpu.VMEM((d, de), BF16), pltpu.VMEM((d, de), BF16),
                        pltpu.SemaphoreType.DMA((2,))],
    )
    return pl.pallas_call(
        _expert_a_kernel,
        grid_spec=grid_spec,
        out_shape=jax.ShapeDtypeStruct((n_rows, de), BF16),
        compiler_params=_cparams(("arbitrary",)),
        name="expert_up",
    )(src_row, block_e, n_blk, a, wg, wu)


def _expert_b_kernel(be_ref, nb_ref, hb_ref, wd_ref, y_ref, wdb_ref):
    i = pl.program_id(0)

    @pl.when(i < nb_ref[0])
    def _():
        @pl.when((i == 0) | (be_ref[i] != be_ref[jnp.maximum(i - 1, 0)]))
        def _():
            wdb_ref[...] = wd_ref[0].astype(BF16)

        y_ref[...] = jnp.dot(hb_ref[...], wdb_ref[...], preferred_element_type=F32)

    @pl.when(i >= nb_ref[0])
    def _():
        y_ref[...] = jnp.zeros_like(y_ref)


def _expert_b(block_e, n_blk, hb, wd, *, rb):
    n_rows, de = hb.shape
    d = wd.shape[2]
    blk = lambda i, be, nb: (i, 0)
    grid_spec = pltpu.PrefetchScalarGridSpec(
        num_scalar_prefetch=2,
        grid=(n_rows // rb,),
        in_specs=[
            pl.BlockSpec((rb, de), blk),
            pl.BlockSpec((1, de, d), lambda i, be, nb: (be[jnp.minimum(i, nb[0] - 1)], 0, 0)),
        ],
        out_specs=pl.BlockSpec((rb, d), blk),
        scratch_shapes=[pltpu.VMEM((de, d), BF16)],
    )
    return pl.pallas_call(
        _expert_b_kernel,
        grid_spec=grid_spec,
        out_shape=jax.ShapeDtypeStruct((n_rows, d), F32),
        compiler_params=_cparams(("arbitrary",)),
        name="expert_down",
    )(block_e, n_blk, hb, wd)


def _combine_kernel(d1_ref, d2_ref, y_hbm, h_ref, meta_ref, g_ref, o_ref, y1_ref, y2_ref, sem):
    i = pl.program_id(0)
    n = pl.num_programs(0)
    slot = lax.rem(i, 2)
    tm = h_ref.shape[0]

    def start(step, s):
        _row_gather_start(y_hbm, d1_ref, step * tm, y1_ref.at[s], sem.at[s], tm)
        _row_gather_start(y_hbm, d2_ref, step * tm, y2_ref.at[s], sem.at[s], tm)

    @pl.when(i == 0)
    def _():
        start(0, 0)

    @pl.when(i + 1 < n)
    def _():
        start(i + 1, 1 - slot)

    _row_gather_wait(y_hbm, y1_ref.at[slot], sem.at[slot], tm)
    _row_gather_wait(y_hbm, y2_ref.at[slot], sem.at[slot], tm)
    meta = meta_ref[...]
    out = h_ref[...] + meta[:, 4:5] * y1_ref[slot] + meta[:, 5:6] * y2_ref[slot]
    o_ref[...] = _rms(out, g_ref[...])


def _combine(dest1, dest2, y, h, meta, g, *, tm):
    m, d = h.shape
    grid_spec = pltpu.PrefetchScalarGridSpec(
        num_scalar_prefetch=2,
        grid=(m // tm,),
        in_specs=[
            pl.BlockSpec(memory_space=pl.ANY),
            pl.BlockSpec((tm, d), lambda i, d1, d2: (i, 0)),
            pl.BlockSpec((tm, LANES), lambda i, d1, d2: (i, 0)),
            pl.BlockSpec((1, d), lambda i, d1, d2: (0, 0)),
        ],
        out_specs=pl.BlockSpec((tm, d), lambda i, d1, d2: (i, 0)),
        scratch_shapes=[pltpu.VMEM((2, tm, d), F32), pltpu.VMEM((2, tm, d), F32), pltpu.SemaphoreType.DMA((2,))],
    )
    return pl.pallas_call(
        _combine_kernel,
        grid_spec=grid_spec,
        out_shape=jax.ShapeDtypeStruct((m, d), F32),
        compiler_params=_cparams(("arbitrary",)),
        name="combine",
    )(dest1, dest2, y, h, meta, g)


def kernel(x, mem, norm_mix_g, w_in, w_gla_alpha_up, b_gla_alpha, gla_out_norm_g, hgrn_lb_logits, hgrn_out_norm_g, w_mix_out, norm_xattn_g, norm_mem_g, w_xattn_q, w_xattn_kv, w_xattn_out, norm_ffn_g, w_router_group, b_router_group, w_router_expert, b_router_expert, w_expert_gate, w_expert_up, w_expert_down, norm_final_g):
    batch, seq, d = x.shape
    n_mem = mem.shape[1]
    m = batch * seq
    depth = norm_mix_g.shape[0]
    h = x.reshape(m, d)
    lb_all = jnp.cumsum(jax.nn.softmax(hgrn_lb_logits.astype(F32), axis=0), axis=0)
    gla_cols = 2 * GLA_HEADS * GLA_DK + 2 * GLA_HEADS * GLA_DV
    lr_rank = w_gla_alpha_up.shape[1]

    for l in range(depth):
        w_l = w_in[l]
        w_main = jnp.concatenate([w_l[:, :gla_cols], w_l[:, gla_cols + lr_rank:]], axis=1).astype(BF16)
        w_lr = jnp.pad(w_l[:, gla_cols:gla_cols + lr_rank], ((0, 0), (0, LANES - lr_rank))).astype(BF16)
        w_up = jnp.pad(w_gla_alpha_up[l], ((0, LANES - lr_rank), (0, 0))).astype(BF16)
        proj, la = _inproj(h, norm_mix_g[l][None, :], w_main, w_lr, w_up, b_gla_alpha[l][None, :], tm=min(m, 1024), tn=512)
        lb = lb_all[l]
        lbc = jnp.zeros((8, lb.shape[0]), F32).at[0].set(jnp.log(lb)).at[1].set(jnp.log1p(-lb)).at[2].set(1.0 - lb)
        o_gla = _gla(proj, la, gla_out_norm_g[l][None, :], batch=batch, seq=seq, tb=min(seq, 1024))
        o_h = _hgrn(proj, lbc, hgrn_out_norm_g[l][None, :], batch=batch, seq=seq, tb=min(seq, 1024))
        h = _mm_res([o_gla, o_h], w_mix_out[l], h, tm=min(m, 1024), tn=512, name="mix_out")

        kv = _norm_mm(mem.reshape(batch * n_mem, d), norm_mem_g[l][None, :], w_xattn_kv[l],
                      tm=batch * n_mem, tn=512, out_dtype=BF16, name="mem_kv")
        q = _norm_mm(h, norm_xattn_g[l][None, :], w_xattn_q[l], tm=min(m, 1024), tn=512, out_dtype=BF16, name="xattn_q")
        o = _xattn(q, kv, batch=batch, seq=seq, n_mem=n_mem, tq=min(seq, 512))
        h = _mm_res([o], w_xattn_out[l], h, tm=min(m, 1024), tn=512, name="xattn_out")

        wr = jnp.pad(jnp.concatenate([w_router_group[l], w_router_expert[l]], axis=1),
                     ((0, 0), (0, LANES - N_GROUPS - N_EXPERTS)))
        br = jnp.pad(jnp.concatenate([b_router_group[l], b_router_expert[l]]), (0, LANES - N_GROUPS - N_EXPERTS))
        a, meta, cnt = _router(h, norm_ffn_g[l][None, :], wr, br[None, :], tm=min(m, 512))

        e_idx = meta[:, 0:2].astype(jnp.int32)
        rank = meta[:, 2:4].astype(jnp.int32)
        counts = cnt[0, N_GROUPS:N_GROUPS + N_EXPERTS].astype(jnp.int32)
        padded = ((counts + EXPERT_BLOCK - 1) // EXPERT_BLOCK) * EXPERT_BLOCK
        pad_end = jnp.cumsum(padded)
        dest = (pad_end - padded)[e_idx] + rank
        n_rows = 2 * m + N_EXPERTS * EXPERT_BLOCK
        n_blocks = n_rows // EXPERT_BLOCK
        n_blk = (pad_end[-1:] // EXPERT_BLOCK).astype(jnp.int32)
        blk_first = jnp.arange(n_blocks, dtype=jnp.int32) * EXPERT_BLOCK
        block_e = jnp.minimum(jnp.sum((pad_end[None, :] <= blk_first[:, None]).astype(jnp.int32), axis=1),
                              N_EXPERTS - 1)
        tok = jnp.broadcast_to(jnp.arange(m, dtype=jnp.int32)[:, None], (m, 2))
        src_row = jnp.zeros((n_rows,), jnp.int32).at[dest.reshape(-1)].set(tok.reshape(-1))

        hb = _expert_a(src_row, block_e, n_blk, a, w_expert_gate[l], w_expert_up[l], rb=EXPERT_BLOCK)
        y = _expert_b(block_e, n_blk, hb, w_expert_down[l], rb=EXPERT_BLOCK)
        last = l == depth - 1
        g_fin = norm_final_g[None, :] if last else jnp.ones((1, d), F32)
        h = _combine(dest[:, 0], dest[:, 1], y, h, meta, g_fin, tm=min(m, 256))
        assert last, "the combine kernel fuses the final rmsnorm; deeper stacks need an un-normalised variant"

    return h.reshape(batch, seq, d)
```

```python
import functools

import jax
import jax.numpy as jnp
from jax import lax
from jax.experimental import pallas as pl
from jax.experimental.pallas import tpu as pltpu

F32 = jnp.float32
BF16 = jnp.bfloat16
HIGHEST = lax.Precision.HIGHEST

EPS = 1e-6
CHUNK = 64
LANES = 128
GLA_HEADS, GLA_DK, GLA_DV = 4, 128, 256
HGRN_HEADS, HGRN_DH = 8, 128
XA_HEADS, XA_DH = 4, 512
N_GROUPS, EXPERTS_PER_GROUP, N_EXPERTS = 4, 8, 32
EXPERT_BLOCK = 256
VMEM_LIMIT = 56 * 1024 * 1024


def _cparams(sem, vmem=VMEM_LIMIT):
    return pltpu.CompilerParams(dimension_semantics=sem, vmem_limit_bytes=vmem)


def _log1pexp_neg(t):
    return jnp.log(1.0 + jnp.exp(-t))


def _log_sigmoid(z):
    return jnp.minimum(z, 0.0) - _log1pexp_neg(jnp.abs(z))


def _sigmoid(z):
    return 1.0 / (1.0 + jnp.exp(-z))


def _rms(x, g):
    return x * lax.rsqrt(jnp.mean(x * x, axis=-1, keepdims=True) + EPS) * g


NORM_ROWS = 256


def _for_row_chunks(n_rows, fn):
    step = min(NORM_ROWS, n_rows)

    def body(ci, carry):
        fn(pl.ds(pl.multiple_of(ci * step, step), step))
        return carry

    lax.fori_loop(0, n_rows // step, body, 0)


def _inproj_kernel(x_ref, g_ref, w_ref, wlr_ref, wup_ref, bal_ref, proj_ref, la_ref, a_ref):
    @pl.when(pl.program_id(1) == 0)
    def _():
        def rows_fn(rows):
            a = _rms(x_ref[rows, :], g_ref[...]).astype(BF16)
            a_ref[rows, :] = a
            lr = jnp.dot(a, wlr_ref[...], preferred_element_type=F32)
            z = jnp.dot(lr.astype(BF16), wup_ref[...], preferred_element_type=F32) + bal_ref[...]
            la_ref[rows, :] = _log_sigmoid(z) * (1.0 / 16.0)

        _for_row_chunks(x_ref.shape[0], rows_fn)

    proj_ref[...] = jnp.dot(a_ref[...], w_ref[...], preferred_element_type=F32)


def _inproj(x, g, w_main, w_lr, w_up, b_al, *, tm, tn):
    m, d = x.shape
    n = w_main.shape[1]
    nk = w_up.shape[1]
    return pl.pallas_call(
        _inproj_kernel,
        grid=(m // tm, n // tn),
        in_specs=[
            pl.BlockSpec((tm, d), lambda i, j: (i, 0)),
            pl.BlockSpec((1, d), lambda i, j: (0, 0)),
            pl.BlockSpec((d, tn), lambda i, j: (0, j)),
            pl.BlockSpec((d, LANES), lambda i, j: (0, 0)),
            pl.BlockSpec((LANES, nk), lambda i, j: (0, 0)),
            pl.BlockSpec((1, nk), lambda i, j: (0, 0)),
        ],
        out_specs=[
            pl.BlockSpec((tm, tn), lambda i, j: (i, j)),
            pl.BlockSpec((tm, nk), lambda i, j: (i, 0)),
        ],
        out_shape=[jax.ShapeDtypeStruct((m, n), F32), jax.ShapeDtypeStruct((m, nk), F32)],
        scratch_shapes=[pltpu.VMEM((tm, d), BF16)],
        compiler_params=_cparams(("parallel", "arbitrary")),
        name="inproj",
    )(x, g, w_main, w_lr, w_up, b_al)


def _norm_mm_kernel(x_ref, g_ref, w_ref, o_ref, a_ref):
    @pl.when(pl.program_id(1) == 0)
    def _():
        def rows_fn(rows):
            a_ref[rows, :] = _rms(x_ref[rows, :], g_ref[...]).astype(BF16)

        _for_row_chunks(x_ref.shape[0], rows_fn)

    o_ref[...] = jnp.dot(a_ref[...], w_ref[...], preferred_element_type=F32).astype(o_ref.dtype)


def _norm_mm(x, g, w, *, tm, tn, out_dtype, name):
    m, d = x.shape
    n = w.shape[1]
    return pl.pallas_call(
        _norm_mm_kernel,
        grid=(m // tm, n // tn),
        in_specs=[
            pl.BlockSpec((tm, d), lambda i, j: (i, 0)),
            pl.BlockSpec((1, d), lambda i, j: (0, 0)),
            pl.BlockSpec((d, tn), lambda i, j: (0, j)),
        ],
        out_specs=pl.BlockSpec((tm, tn), lambda i, j: (i, j)),
        out_shape=jax.ShapeDtypeStruct((m, n), out_dtype),
        scratch_shapes=[pltpu.VMEM((tm, d), BF16)],
        compiler_params=_cparams(("parallel", "arbitrary")),
        name=name,
    )(x, g, w)


def _mm_res_kernel(*refs, n_lhs):
    lhs = refs[:n_lhs]
    w_refs = refs[n_lhs:2 * n_lhs]
    res_ref, o_ref = refs[2 * n_lhs], refs[2 * n_lhs + 1]
    acc = res_ref[...]
    for l_ref, w_ref in zip(lhs, w_refs):
        acc = acc + jnp.dot(l_ref[...], w_ref[...], preferred_element_type=F32)
    o_ref[...] = acc


def _mm_res(lhs_parts, w, res, *, tm, tn, name):
    m, n = res.shape
    n_lhs = len(lhs_parts)
    kp = lhs_parts[0].shape[1]
    in_specs = [pl.BlockSpec((tm, kp), lambda i, j: (i, 0)) for _ in lhs_parts]
    in_specs += [pl.BlockSpec((kp, tn), functools.partial(lambda i, j, p: (p, j), p=p)) for p in range(n_lhs)]
    in_specs += [pl.BlockSpec((tm, tn), lambda i, j: (i, j))]
    return pl.pallas_call(
        functools.partial(_mm_res_kernel, n_lhs=n_lhs),
        grid=(m // tm, n // tn),
        in_specs=in_specs,
        out_specs=pl.BlockSpec((tm, tn), lambda i, j: (i, j)),
        out_shape=jax.ShapeDtypeStruct((m, n), F32),
        compiler_params=_cparams(("parallel", "arbitrary")),
        name=name,
    )(*lhs_parts, *([w] * n_lhs), res)


LEVELS = (32, 16, 8, 4, 2, 1)
LOG2E = 1.4426950408889634


def _nt_dot(x, y):
    return lax.dot_general(x, y, (((1,), (1,)), ((), ())), preferred_element_type=F32)


def _split3_bf16(x):
    def top(v):
        bits = lax.bitcast_convert_type(v, jnp.uint32) & jnp.uint32(0xFFFF0000)
        return lax.bitcast_convert_type(bits, F32)
    hi = top(x)
    r1 = x - hi
    mid = top(r1)
    lo = r1 - mid
    return hi.astype(BF16), mid.astype(BF16), lo.astype(BF16)


def _mix_chunks(qs, ks, vs, las, st_refs):
    heads = range(len(qs))
    c, dk = qs[0].shape
    row = lax.broadcasted_iota(jnp.int32, (c, c), 0)
    col = lax.broadcasted_iota(jnp.int32, (c, c), 1)
    rowk = lax.broadcasted_iota(jnp.int32, (c, dk), 0)
    xor = jnp.bitwise_xor(row, col)

    las = [la * LOG2E for la in las]

    tri = jnp.where(col <= row, 1.0, 0.0).astype(BF16)
    b3 = [jnp.dot(tri, jnp.concatenate(_split3_bf16(la), axis=1), preferred_element_type=F32) for la in las]
    bs = [(t[:, :dk] + t[:, dk:2 * dk]) + t[:, 2 * dk:] for t in b3]

    def neg_dist(w, b, la):
        if w >= 4:
            parts = [jnp.broadcast_to(b[base + w - 1:base + w, :], (2 * w, dk)) for base in range(0, c, 2 * w)]
            m = parts[0] if len(parts) == 1 else jnp.concatenate(parts, axis=0)
            return -jnp.abs(b - m)
        if w == 2:
            r4 = jnp.bitwise_and(rowk, 3)
            nxt = pltpu.roll(la, c - 1, 0)
            prv = pltpu.roll(la, 1, 0)
            return jnp.where(r4 == 0, nxt, jnp.where(r4 == 1, 0.0, jnp.where(r4 == 2, la, la + prv)))
        return jnp.where(jnp.bitwise_and(rowk, 1) == 1, la, 0.0)

    acc = [_nt_dot(qs[h].astype(BF16), ks[h].astype(BF16)) for h in heads]
    for w in reversed(LEVELS):
        upper = jnp.bitwise_and(rowk, w) != 0
        xs = [(jnp.where(upper, qs[h], ks[h]) * jnp.exp2(neg_dist(w, bs[h], las[h]))).astype(BF16) for h in heads]
        gs = [_nt_dot(x, x) for x in xs]
        acc = [jnp.where(xor >= w, gs[h], acc[h]) for h in heads]
    ab = [jnp.where(col <= row, a, 0.0).astype(BF16) for a in acc]

    sts = [st_refs[h][...] for h in heads]
    qx = [(qs[h] * jnp.exp2(bs[h])).astype(BF16) for h in heads]
    b_last = [b[c - 1:c, :] for b in bs]
    kx = [(ks[h] * jnp.exp2(b_last[h] - bs[h])).astype(BF16) for h in heads]
    vb = [v.astype(BF16) for v in vs]
    outs = [jnp.dot(ab[h], vb[h], preferred_element_type=F32) + _nt_dot(qx[h], sts[h].astype(BF16)) for h in heads]
    for h in heads:
        st_refs[h][...] = jnp.exp2(b_last[h]) * sts[h] + lax.dot_general(
            vb[h], kx[h], (((0,), (0,)), ((), ())), preferred_element_type=F32)
    return outs


def _gated_norm(o, g, gn):
    return _rms(o, gn) * (g * _sigmoid(g))


def _gla_kernel(q_ref, k_ref, v_ref, g_ref, la_ref, gn_ref, o_ref, st_ref, *, n_chunk):
    @pl.when(pl.program_id(2) == 0)
    def _():
        st_ref[...] = jnp.zeros_like(st_ref)

    dk, dv = GLA_DK, GLA_DV
    heads = range(st_ref.shape[0])

    def body(ci, carry):
        rows = pl.ds(pl.multiple_of(ci * CHUNK, CHUNK), CHUNK)
        kc = [slice(h * dk, (h + 1) * dk) for h in heads]
        vc = [slice(h * dv, (h + 1) * dv) for h in heads]
        outs = _mix_chunks([q_ref[rows, kc[h]] * (dk ** -0.5) for h in heads], [k_ref[rows, kc[h]] for h in heads],
                           [v_ref[rows, vc[h]] for h in heads], [la_ref[rows, kc[h]] for h in heads],
                           [st_ref.at[h] for h in heads])
        for h in heads:
            o_ref[rows, vc[h]] = _gated_norm(outs[h], g_ref[rows, vc[h]], gn_ref[...]).astype(o_ref.dtype)
        return carry

    lax.fori_loop(0, n_chunk, body, 0)


def _gla(proj, la, gn, *, batch, seq, tb, hp):
    m = proj.shape[0]
    nt = seq // tb
    wk, wv = hp * GLA_DK, hp * GLA_DV
    n_grp = GLA_HEADS // hp
    spec = lambda width, first: pl.BlockSpec((tb, width), lambda b, p, t: (b * nt + t, first + p))
    return pl.pallas_call(
        functools.partial(_gla_kernel, n_chunk=tb // CHUNK),
        grid=(batch, n_grp, nt),
        in_specs=[spec(wk, 0), spec(wk, n_grp), spec(wv, n_grp), spec(wv, 2 * n_grp), spec(wk, 0),
                  pl.BlockSpec((1, GLA_DV), lambda b, p, t: (0, 0))],
        out_specs=spec(wv, 0),
        out_shape=jax.ShapeDtypeStruct((m, GLA_HEADS * GLA_DV), BF16),
        scratch_shapes=[pltpu.VMEM((hp, GLA_DV, GLA_DK), F32)],
        compiler_params=_cparams(("parallel", "parallel", "arbitrary")),
        name="gla",
    )(proj, proj, proj, proj, la, gn)


def _hgrn_kernel(q_ref, f_ref, i_ref, g_ref, lb_ref, gn_ref, o_ref, st_ref, *, n_chunk):
    @pl.when(pl.program_id(2) == 0)
    def _():
        st_ref[...] = jnp.zeros_like(st_ref)

    dh = HGRN_DH
    heads = range(st_ref.shape[0])

    def body(ci, carry):
        rows = pl.ds(pl.multiple_of(ci * CHUNK, CHUNK), CHUNK)
        hc = [slice(h * dh, (h + 1) * dh) for h in heads]
        qs, ks, las = [], [], []
        for h in heads:
            log_lb, log_1mlb, one_m_lb = lb_ref[0:1, hc[h]], lb_ref[1:2, hc[h]], lb_ref[2:3, hc[h]]
            hq = q_ref[rows, hc[h]]
            z = f_ref[rows, hc[h]]
            x1 = log_1mlb + _log_sigmoid(z)
            las.append(jnp.maximum(log_lb, x1) + _log1pexp_neg(jnp.abs(log_lb - x1)))
            qs.append(hq * _sigmoid(hq))
            ks.append(one_m_lb * _sigmoid(-z))
        outs = _mix_chunks(qs, ks, [i_ref[rows, hc[h]] for h in heads], las, [st_ref.at[h] for h in heads])
        for h in heads:
            o_ref[rows, hc[h]] = _gated_norm(outs[h], g_ref[rows, hc[h]], gn_ref[...]).astype(o_ref.dtype)
        return carry

    lax.fori_loop(0, n_chunk, body, 0)


def _hgrn(proj, lbc, gn, *, batch, seq, tb, hp):
    m = proj.shape[0]
    nt = seq // tb
    width = hp * HGRN_DH
    n_grp = HGRN_HEADS // hp
    first = 3072 // width
    spec = lambda seg: pl.BlockSpec((tb, width), lambda b, p, t: (b * nt + t, first + seg * n_grp + p))
    return pl.pallas_call(
        functools.partial(_hgrn_kernel, n_chunk=tb // CHUNK),
        grid=(batch, n_grp, nt),
        in_specs=[spec(0), spec(1), spec(2), spec(3),
                  pl.BlockSpec((8, width), lambda b, p, t: (0, p)),
                  pl.BlockSpec((1, HGRN_DH), lambda b, p, t: (0, 0))],
        out_specs=pl.BlockSpec((tb, width), lambda b, p, t: (b * nt + t, p)),
        out_shape=jax.ShapeDtypeStruct((m, HGRN_HEADS * HGRN_DH), BF16),
        scratch_shapes=[pltpu.VMEM((hp, HGRN_DH, HGRN_DH), F32)],
        compiler_params=_cparams(("parallel", "parallel", "arbitrary")),
        name="hgrn",
    )(proj, proj, proj, proj, lbc, gn)


def _xattn_kernel(q_ref, k_ref, v_ref, o_ref):
    for h in range(XA_HEADS):
        cols = slice(h * XA_DH, (h + 1) * XA_DH)
        s = lax.dot_general(q_ref[:, cols], k_ref[:, cols], (((1,), (1,)), ((), ())),
                            preferred_element_type=F32) * (XA_DH ** -0.5)
        p = jnp.exp(s - jnp.max(s, axis=-1, keepdims=True))
        p = p / jnp.sum(p, axis=-1, keepdims=True)
        o_ref[:, cols] = jnp.dot(p.astype(BF16), v_ref[:, cols], preferred_element_type=F32).astype(o_ref.dtype)


def _xattn(q, kv, *, batch, seq, n_mem, tq):
    m, d = q.shape
    nt = seq // tq
    return pl.pallas_call(
        _xattn_kernel,
        grid=(batch, nt),
        in_specs=[
            pl.BlockSpec((tq, d), lambda b, t: (b * nt + t, 0)),
            pl.BlockSpec((n_mem, d), lambda b, t: (b, 0)),
            pl.BlockSpec((n_mem, d), lambda b, t: (b, 1)),
        ],
        out_specs=pl.BlockSpec((tq, d), lambda b, t: (b * nt + t, 0)),
        out_shape=jax.ShapeDtypeStruct((m, d), BF16),
        compiler_params=_cparams(("parallel", "arbitrary")),
        name="xattn",
    )(q, kv, kv)


def _router_kernel(h_ref, g_ref, wr_ref, br_ref, a_ref, meta_ref, cnt_ref, carry_ref):
    tm = h_ref.shape[0]

    @pl.when(pl.program_id(0) == 0)
    def _():
        carry_ref[...] = jnp.zeros_like(carry_ref)

    a = _rms(h_ref[...], g_ref[...])
    a_ref[...] = a
    logits = jnp.dot(a, wr_ref[...], precision=HIGHEST, preferred_element_type=F32) + br_ref[...]
    lane = lax.broadcasted_iota(jnp.int32, (tm, LANES), 1)
    lane_f = lane.astype(F32)
    neg = -jnp.inf

    gl = jnp.where(lane < N_GROUPS, logits, neg)
    gmax = jnp.max(gl, axis=1, keepdims=True)
    gidx = jnp.min(jnp.where(gl == gmax, lane_f, float(LANES)), axis=1, keepdims=True)
    p_group = 1.0 / jnp.sum(jnp.exp(gl - gmax), axis=1, keepdims=True)
    lo = float(N_GROUPS) + gidx * float(EXPERTS_PER_GROUP)
    el = jnp.where((lane_f >= lo) & (lane_f < lo + float(EXPERTS_PER_GROUP)), logits, neg)
    v1 = jnp.max(el, axis=1, keepdims=True)
    i1 = jnp.min(jnp.where(el == v1, lane_f, float(LANES)), axis=1, keepdims=True)
    el2 = jnp.where(lane_f == i1, neg, el)
    v2 = jnp.max(el2, axis=1, keepdims=True)
    i2 = jnp.min(jnp.where(el2 == v2, lane_f, float(LANES)), axis=1, keepdims=True)
    t = jnp.exp(v2 - v1)
    g1 = p_group / (1.0 + t)
    g2 = p_group * t / (1.0 + t)

    hit1 = lane_f == i1
    hit2 = lane_f == i2
    onehot = jnp.where(hit1 | hit2, 1.0, 0.0)
    row = lax.broadcasted_iota(jnp.int32, (tm, tm), 0)
    col = lax.broadcasted_iota(jnp.int32, (tm, tm), 1)
    strict = jnp.where(col < row, 1.0, 0.0).astype(BF16)
    before = jnp.dot(strict, onehot.astype(BF16), preferred_element_type=F32) + carry_ref[0:1, :]
    r1 = jnp.sum(jnp.where(hit1, before, 0.0), axis=1, keepdims=True)
    r2 = jnp.sum(jnp.where(hit2, before, 0.0), axis=1, keepdims=True)
    carry_ref[...] = carry_ref[...] + jnp.sum(onehot, axis=0, keepdims=True)
    cnt_ref[...] = carry_ref[...]

    meta = jnp.zeros((tm, LANES), F32)
    for idx, val in enumerate((i1 - float(N_GROUPS), i2 - float(N_GROUPS), r1, r2, g1, g2)):
        meta = jnp.where(lane == idx, val, meta)
    meta_ref[...] = meta


def _router(h, g, wr, br, *, tm):
    m, d = h.shape
    return pl.pallas_call(
        _router_kernel,
        grid=(m // tm,),
        in_specs=[
            pl.BlockSpec((tm, d), lambda i: (i, 0)),
            pl.BlockSpec((1, d), lambda i: (0, 0)),
            pl.BlockSpec((d, LANES), lambda i: (0, 0)),
            pl.BlockSpec((1, LANES), lambda i: (0, 0)),
        ],
        out_specs=[
            pl.BlockSpec((tm, d), lambda i: (i, 0)),
            pl.BlockSpec((tm, LANES), lambda i: (i, 0)),
            pl.BlockSpec((8, LANES), lambda i: (0, 0)),
        ],
        out_shape=[jax.ShapeDtypeStruct((m, d), F32), jax.ShapeDtypeStruct((m, LANES), F32),
                   jax.ShapeDtypeStruct((8, LANES), F32)],
        scratch_shapes=[pltpu.VMEM((8, LANES), F32)],
        compiler_params=_cparams(("arbitrary",)),
        name="router",
    )(h, g, wr, br)


def _row_gather_start(src_hbm, idx_ref, base, dst_ref, sem, n_rows):
    def body(r, carry):
        pltpu.make_async_copy(src_hbm.at[pl.ds(idx_ref[base + r], 1), :], dst_ref.at[pl.ds(r, 1), :], sem).start()
        return carry
    lax.fori_loop(0, n_rows, body, 0)


def _row_gather_wait(src_hbm, dst_ref, sem, n_rows):
    pltpu.make_async_copy(src_hbm.at[pl.ds(0, n_rows), :], dst_ref, sem).wait()


def _expert_a_kernel(src_ref, be_ref, nb_ref, a_hbm, wg_ref, wu_ref, hb_ref, xg_ref, wgb_ref, wub_ref, sem):
    i = pl.program_id(0)
    nb = nb_ref[0]
    slot = lax.rem(i, 2)
    rb = hb_ref.shape[0]

    @pl.when(i == 0)
    def _():
        _row_gather_start(a_hbm, src_ref, 0, xg_ref.at[0], sem.at[0], rb)

    @pl.when(i + 1 < nb)
    def _():
        _row_gather_start(a_hbm, src_ref, (i + 1) * rb, xg_ref.at[1 - slot], sem.at[1 - slot], rb)

    @pl.when(i < nb)
    def _():
        @pl.when((i == 0) | (be_ref[i] != be_ref[jnp.maximum(i - 1, 0)]))
        def _():
            wgb_ref[...] = wg_ref[0].astype(BF16)
            wub_ref[...] = wu_ref[0].astype(BF16)

        _row_gather_wait(a_hbm, xg_ref.at[slot], sem.at[slot], rb)
        x = xg_ref[slot].astype(BF16)
        hg = jnp.dot(x, wgb_ref[...], preferred_element_type=F32)
        hu = jnp.dot(x, wub_ref[...], preferred_element_type=F32)
        hb_ref[...] = (hg * _sigmoid(hg) * hu).astype(hb_ref.dtype)

    @pl.when(i >= nb)
    def _():
        hb_ref[...] = jnp.zeros_like(hb_ref)


def _expert_a(src_row, block_e, n_blk, a, wg, wu, *, rb):
    n_rows = src_row.shape[0]
    n_blocks = n_rows // rb
    d = a.shape[1]
    de = wg.shape[2]
    wmap = lambda i, src, be, nb: (be[jnp.minimum(i, nb[0] - 1)], 0, 0)
    grid_spec = pltpu.PrefetchScalarGridSpec(
        num_scalar_prefetch=3,
        grid=(n_blocks,),
        in_specs=[
            pl.BlockSpec(memory_space=pl.ANY),
            pl.BlockSpec((1, d, de), wmap),
            pl.BlockSpec((1, d, de), wmap),
        ],
        out_specs=pl.BlockSpec((rb, de), lambda i, src, be, nb: (i, 0)),
        scratch_shapes=[pltpu.VMEM((2, rb, d), F32), pltpu.VMEM((d, de), BF16), pltpu.VMEM((d, de), BF16),
                        pltpu.SemaphoreType.DMA((2,))],
    )
    return pl.pallas_call(
        _expert_a_kernel,
        grid_spec=grid_spec,
        out_shape=jax.ShapeDtypeStruct((n_rows, de), BF16),
        compiler_params=_cparams(("arbitrary",)),
        name="expert_up",
    )(src_row, block_e, n_blk, a, wg, wu)


def _expert_b_kernel(be_ref, nb_ref, hb_ref, wd_ref, y_ref, wdb_ref):
    i = pl.program_id(0)

    @pl.when(i < nb_ref[0])
    def _():
        @pl.when((i == 0) | (be_ref[i] != be_ref[jnp.maximum(i - 1, 0)]))
        def _():
            wdb_ref[...] = wd_ref[0].astype(BF16)

        y_ref[...] = jnp.dot(hb_ref[...], wdb_ref[...], preferred_element_type=F32)

    @pl.when(i >= nb_ref[0])
    def _():
        y_ref[...] = jnp.zeros_like(y_ref)


def _expert_b(block_e, n_blk, hb, wd, *, rb):
    n_rows, de = hb.shape
    d = wd.shape[2]
    blk = lambda i, be, nb: (i, 0)
    grid_spec = pltpu.PrefetchScalarGridSpec(
        num_scalar_prefetch=2,
        grid=(n_rows // rb,),
        in_specs=[
            pl.BlockSpec((rb, de), blk),
            pl.BlockSpec((1, de, d), lambda i, be, nb: (be[jnp.minimum(i, nb[0] - 1)], 0, 0)),
        ],
        out_specs=pl.BlockSpec((rb, d), blk),
        scratch_shapes=[pltpu.VMEM((de, d), BF16)],
    )
    return pl.pallas_call(
        _expert_b_kernel,
        grid_spec=grid_spec,
        out_shape=jax.ShapeDtypeStruct((n_rows, d), F32),
        compiler_params=_cparams(("arbitrary",)),
        name="expert_down",
    )(block_e, n_blk, hb, wd)


def _combine_kernel(d1_ref, d2_ref, y_hbm, h_ref, meta_ref, g_ref, o_ref, y1_ref, y2_ref, sem):
    i = pl.program_id(0)
    n = pl.num_programs(0)
    slot = lax.rem(i, 2)
    tm = h_ref.shape[0]

    def start(step, s):
        _row_gather_start(y_hbm, d1_ref, step * tm, y1_ref.at[s], sem.at[s], tm)
        _row_gather_start(y_hbm, d2_ref, step * tm, y2_ref.at[s], sem.at[s], tm)

    @pl.when(i == 0)
    def _():
        start(0, 0)

    @pl.when(i + 1 < n)
    def _():
        start(i + 1, 1 - slot)

    _row_gather_wait(y_hbm, y1_ref.at[slot], sem.at[slot], tm)
    _row_gather_wait(y_hbm, y2_ref.at[slot], sem.at[slot], tm)
    meta = meta_ref[...]
    out = h_ref[...] + meta[:, 4:5] * y1_ref[slot] + meta[:, 5:6] * y2_ref[slot]
    o_ref[...] = _rms(out, g_ref[...])


def _combine(dest1, dest2, y, h, meta, g, *, tm):
    m, d = h.shape
    grid_spec = pltpu.PrefetchScalarGridSpec(
        num_scalar_prefetch=2,
        grid=(m // tm,),
        in_specs=[
            pl.BlockSpec(memory_space=pl.ANY),
            pl.BlockSpec((tm, d), lambda i, d1, d2: (i, 0)),
            pl.BlockSpec((tm, LANES), lambda i, d1, d2: (i, 0)),
            pl.BlockSpec((1, d), lambda i, d1, d2: (0, 0)),
        ],
        out_specs=pl.BlockSpec((tm, d), lambda i, d1, d2: (i, 0)),
        scratch_shapes=[pltpu.VMEM((2, tm, d), F32), pltpu.VMEM((2, tm, d), F32), pltpu.SemaphoreType.DMA((2,))],
    )
    return pl.pallas_call(
        _combine_kernel,
        grid_spec=grid_spec,
        out_shape=jax.ShapeDtypeStruct((m, d), F32),
        compiler_params=_cparams(("arbitrary",)),
        name="combine",
    )(dest1, dest2, y, h, meta, g)


def kernel(x, mem, norm_mix_g, w_in, w_gla_alpha_up, b_gla_alpha, gla_out_norm_g, hgrn_lb_logits, hgrn_out_norm_g, w_mix_out, norm_xattn_g, norm_mem_g, w_xattn_q, w_xattn_kv, w_xattn_out, norm_ffn_g, w_router_group, b_router_group, w_router_expert, b_router_expert, w_expert_gate, w_expert_up, w_expert_down, norm_final_g):
    batch, seq, d = x.shape
    n_mem = mem.shape[1]
    m = batch * seq
    depth = norm_mix_g.shape[0]
    h = x.reshape(m, d)
    lb_all = jnp.cumsum(jax.nn.softmax(hgrn_lb_logits.astype(F32), axis=0), axis=0)
    gla_cols = 2 * GLA_HEADS * GLA_DK + 2 * GLA_HEADS * GLA_DV
    lr_rank = w_gla_alpha_up.shape[1]

    for l in range(depth):
        w_l = w_in[l]
        w_main = jnp.concatenate([w_l[:, :gla_cols], w_l[:, gla_cols + lr_rank:]], axis=1).astype(BF16)
        w_lr = jnp.pad(w_l[:, gla_cols:gla_cols + lr_rank], ((0, 0), (0, LANES - lr_rank))).astype(BF16)
        w_up = jnp.pad(w_gla_alpha_up[l], ((0, LANES - lr_rank), (0, 0))).astype(BF16)
        proj, la = _inproj(h, norm_mix_g[l][None, :], w_main, w_lr, w_up, b_gla_alpha[l][None, :], tm=min(m, 1024), tn=1024)
        lb = lb_all[l]
        lbc = jnp.zeros((8, lb.shape[0]), F32).at[0].set(jnp.log(lb)).at[1].set(jnp.log1p(-lb)).at[2].set(1.0 - lb)
        o_gla = _gla(proj, la, gla_out_norm_g[l][None, :], batch=batch, seq=seq, tb=min(seq, 512), hp=4)
        o_h = _hgrn(proj, lbc, hgrn_out_norm_g[l][None, :], batch=batch, seq=seq, tb=min(seq, 1024), hp=4)
        h = _mm_res([o_gla, o_h], w_mix_out[l].astype(BF16), h, tm=min(m, 512), tn=d, name="mix_out")

        kv = _norm_mm(mem.reshape(batch * n_mem, d), norm_mem_g[l][None, :], w_xattn_kv[l].astype(BF16),
                      tm=batch * n_mem, tn=1024, out_dtype=BF16, name="mem_kv")
        q = _norm_mm(h, norm_xattn_g[l][None, :], w_xattn_q[l].astype(BF16), tm=min(m, 512), tn=d, out_dtype=BF16,
                     name="xattn_q")
        o = _xattn(q, kv, batch=batch, seq=seq, n_mem=n_mem, tq=min(seq, 512))
        h = _mm_res([o], w_xattn_out[l].astype(BF16), h, tm=min(m, 512), tn=d, name="xattn_out")

        wr = jnp.pad(jnp.concatenate([w_router_group[l], w_router_expert[l]], axis=1),
                     ((0, 0), (0, LANES - N_GROUPS - N_EXPERTS)))
        br = jnp.pad(jnp.concatenate([b_router_group[l], b_router_expert[l]]), (0, LANES - N_GROUPS - N_EXPERTS))
        a, meta, cnt = _router(h, norm_ffn_g[l][None, :], wr, br[None, :], tm=min(m, 512))

        e_idx = meta[:, 0:2].astype(jnp.int32)
        rank = meta[:, 2:4].astype(jnp.int32)
        counts = cnt[0, N_GROUPS:N_GROUPS + N_EXPERTS].astype(jnp.int32)
        padded = ((counts + EXPERT_BLOCK - 1) // EXPERT_BLOCK) * EXPERT_BLOCK
        pad_end = jnp.cumsum(padded)
        dest = (pad_end - padded)[e_idx] + rank
        n_rows = 2 * m + N_EXPERTS * EXPERT_BLOCK
        n_blocks = n_rows // EXPERT_BLOCK
        n_blk = (pad_end[-1:] // EXPERT_BLOCK).astype(jnp.int32)
        blk_first = jnp.arange(n_blocks, dtype=jnp.int32) * EXPERT_BLOCK
        block_e = jnp.minimum(jnp.sum((pad_end[None, :] <= blk_first[:, None]).astype(jnp.int32), axis=1),
                              N_EXPERTS - 1)
        tok = jnp.broadcast_to(jnp.arange(m, dtype=jnp.int32)[:, None], (m, 2))
        src_row = jnp.zeros((n_rows,), jnp.int32).at[dest.reshape(-1)].set(tok.reshape(-1))

        hb = _expert_a(src_row, block_e, n_blk, a, w_expert_gate[l], w_expert_up[l], rb=EXPERT_BLOCK)
        y = _expert_b(block_e, n_blk, hb, w_expert_down[l], rb=EXPERT_BLOCK)
        last = l == depth - 1
        g_fin = norm_final_g[None, :] if last else jnp.ones((1, d), F32)
        h = _combine(dest[:, 0], dest[:, 1], y, h, meta, g_fin, tm=min(m, 256))
        assert last, "the combine kernel fuses the final rmsnorm; deeper stacks need an un-normalised variant"

    return h.reshape(batch, seq, d)
```

```python
import functools

import jax
import jax.numpy as jnp
from jax import lax
from jax.experimental import pallas as pl
from jax.experimental.pallas import tpu as pltpu

F32 = jnp.float32
BF16 = jnp.bfloat16
HIGHEST = lax.Precision.HIGHEST

EPS = 1e-6
CHUNK = 64
LANES = 128
GLA_HEADS, GLA_DK, GLA_DV = 4, 128, 256
HGRN_HEADS, HGRN_DH = 8, 128
XA_HEADS, XA_DH = 4, 512
N_GROUPS, EXPERTS_PER_GROUP, N_EXPERTS = 4, 8, 32
EXPERT_BLOCK = 256
VMEM_LIMIT = 56 * 1024 * 1024


def _cparams(sem, vmem=VMEM_LIMIT):
    return pltpu.CompilerParams(dimension_semantics=sem, vmem_limit_bytes=vmem)


def _log1pexp_neg(t):
    return jnp.log(1.0 + jnp.exp(-t))


def _log_sigmoid(z):
    return jnp.minimum(z, 0.0) - _log1pexp_neg(jnp.abs(z))


def _sigmoid(z):
    return 1.0 / (1.0 + jnp.exp(-z))


def _rms(x, g):
    return x * lax.rsqrt(jnp.mean(x * x, axis=-1, keepdims=True) + EPS) * g


def _nt_dot(x, y):
    return lax.dot_general(x, y, (((1,), (1,)), ((), ())), preferred_element_type=F32)


NORM_ROWS = 256


def _for_row_chunks(n_rows, fn):
    step = min(NORM_ROWS, n_rows)

    def body(ci, carry):
        fn(pl.ds(pl.multiple_of(ci * step, step), step))
        return carry

    lax.fori_loop(0, n_rows // step, body, 0)


def _inproj_kernel(x_ref, g_ref, w_ref, wlr_ref, wup_ref, bal_ref, proj_ref, la_ref, a_ref):
    @pl.when(pl.program_id(1) == 0)
    def _():
        def rows_fn(rows):
            a = _rms(x_ref[rows, :], g_ref[...]).astype(BF16)
            a_ref[rows, :] = a
            lr = _nt_dot(a, wlr_ref[...])
            z = jnp.dot(lr.astype(BF16), wup_ref[...], preferred_element_type=F32) + bal_ref[...]
            la_ref[rows, :] = _log_sigmoid(z) * (1.0 / 16.0)

        _for_row_chunks(x_ref.shape[0], rows_fn)

    proj_ref[...] = _nt_dot(a_ref[...], w_ref[...].astype(BF16))


def _inproj(x, g, w_t, w_lr_t, w_up, b_al, *, tm, tn, skip_from, skip):
    m, d = x.shape
    n = w_t.shape[0] - skip
    nk = w_up.shape[1]
    first_after = skip_from // tn

    def w_rows(i, j):
        return (pl.multiple_of(j * tn + jnp.where(j >= first_after, skip, 0), 8), 0)

    return pl.pallas_call(
        _inproj_kernel,
        grid=(m // tm, n // tn),
        in_specs=[
            pl.BlockSpec((tm, d), lambda i, j: (i, 0)),
            pl.BlockSpec((1, d), lambda i, j: (0, 0)),
            pl.BlockSpec((pl.Element(tn), pl.Element(d)), w_rows),
            pl.BlockSpec((LANES, d), lambda i, j: (0, 0)),
            pl.BlockSpec((LANES, nk), lambda i, j: (0, 0)),
            pl.BlockSpec((1, nk), lambda i, j: (0, 0)),
        ],
        out_specs=[
            pl.BlockSpec((tm, tn), lambda i, j: (i, j)),
            pl.BlockSpec((tm, nk), lambda i, j: (i, 0)),
        ],
        out_shape=[jax.ShapeDtypeStruct((m, n), F32), jax.ShapeDtypeStruct((m, nk), F32)],
        scratch_shapes=[pltpu.VMEM((tm, d), BF16)],
        compiler_params=_cparams(("parallel", "arbitrary")),
        name="inproj",
    )(x, g, w_t, w_lr_t, w_up, b_al)


def _weight_spec(k, n, tn):
    if tn == n:
        return pl.BlockSpec((k, n), lambda i, j: (0, 0), pipeline_mode=pl.Buffered(1))
    return pl.BlockSpec((k, tn), lambda i, j: (0, j))


def _cast_weight(w_ref, wb_ref, resident):
    if resident:
        @pl.when((pl.program_id(0) == 0) & (pl.program_id(1) == 0))
        def _():
            wb_ref[...] = w_ref[...].astype(BF16)
    else:
        wb_ref[...] = w_ref[...].astype(BF16)


def _norm_mm_kernel(x_ref, g_ref, w_ref, o_ref, a_ref, wb_ref, *, resident):
    @pl.when(pl.program_id(1) == 0)
    def _():
        def rows_fn(rows):
            a_ref[rows, :] = _rms(x_ref[rows, :], g_ref[...]).astype(BF16)

        _for_row_chunks(x_ref.shape[0], rows_fn)

    _cast_weight(w_ref, wb_ref, resident)
    o_ref[...] = jnp.dot(a_ref[...], wb_ref[...], preferred_element_type=F32).astype(o_ref.dtype)


def _norm_mm(x, g, w, *, tm, tn, out_dtype, name):
    m, d = x.shape
    n = w.shape[1]
    return pl.pallas_call(
        functools.partial(_norm_mm_kernel, resident=tn == n),
        grid=(m // tm, n // tn),
        in_specs=[
            pl.BlockSpec((tm, d), lambda i, j: (i, 0)),
            pl.BlockSpec((1, d), lambda i, j: (0, 0)),
            _weight_spec(d, n, tn),
        ],
        out_specs=pl.BlockSpec((tm, tn), lambda i, j: (i, j)),
        out_shape=jax.ShapeDtypeStruct((m, n), out_dtype),
        scratch_shapes=[pltpu.VMEM((tm, d), BF16), pltpu.VMEM((d, tn), BF16)],
        compiler_params=_cparams(("arbitrary", "arbitrary")),
        name=name,
    )(x, g, w)


def _mm_res_kernel(*refs, n_lhs, resident):
    lhs = refs[:n_lhs]
    w_ref, res_ref, o_ref, wb_ref = refs[n_lhs:]
    _cast_weight(w_ref, wb_ref, resident)
    acc = res_ref[...]
    k0 = 0
    for l_ref in lhs:
        kp = l_ref.shape[1]
        acc = acc + jnp.dot(l_ref[...], wb_ref[k0:k0 + kp, :], preferred_element_type=F32)
        k0 += kp
    o_ref[...] = acc


def _mm_res(lhs_parts, w, res, *, tm, tn, name):
    m, n = res.shape
    k = w.shape[0]
    n_lhs = len(lhs_parts)
    in_specs = [pl.BlockSpec((tm, p.shape[1]), lambda i, j: (i, 0)) for p in lhs_parts]
    in_specs += [_weight_spec(k, n, tn), pl.BlockSpec((tm, tn), lambda i, j: (i, j))]
    return pl.pallas_call(
        functools.partial(_mm_res_kernel, n_lhs=n_lhs, resident=tn == n),
        grid=(m // tm, n // tn),
        in_specs=in_specs,
        out_specs=pl.BlockSpec((tm, tn), lambda i, j: (i, j)),
        out_shape=jax.ShapeDtypeStruct((m, n), F32),
        scratch_shapes=[pltpu.VMEM((k, tn), BF16)],
        compiler_params=_cparams(("arbitrary", "arbitrary")),
        name=name,
    )(*lhs_parts, w, res)


LEVELS = (32, 16, 8, 4, 2, 1)
LOG2E = 1.4426950408889634


def _split3_bf16(x):
    def top(v):
        bits = lax.bitcast_convert_type(v, jnp.uint32) & jnp.uint32(0xFFFF0000)
        return lax.bitcast_convert_type(bits, F32)
    hi = top(x)
    r1 = x - hi
    mid = top(r1)
    lo = r1 - mid
    return hi.astype(BF16), mid.astype(BF16), lo.astype(BF16)


def _mix_chunks(qs, ks, vs, las, st_refs):
    heads = range(len(qs))
    c, dk = qs[0].shape
    row = lax.broadcasted_iota(jnp.int32, (c, c), 0)
    col = lax.broadcasted_iota(jnp.int32, (c, c), 1)
    rowk = lax.broadcasted_iota(jnp.int32, (c, dk), 0)
    xor = jnp.bitwise_xor(row, col)

    las = [la * LOG2E for la in las]

    tri = jnp.where(col <= row, 1.0, 0.0).astype(BF16)
    b3 = [jnp.dot(tri, jnp.concatenate(_split3_bf16(la), axis=1), preferred_element_type=F32) for la in las]
    bs = [(t[:, :dk] + t[:, dk:2 * dk]) + t[:, 2 * dk:] for t in b3]

    def neg_dist(w, b, la):
        if w >= 4:
            parts = [jnp.broadcast_to(b[base + w - 1:base + w, :], (2 * w, dk)) for base in range(0, c, 2 * w)]
            m = parts[0] if len(parts) == 1 else jnp.concatenate(parts, axis=0)
            return -jnp.abs(b - m)
        if w == 2:
            r4 = jnp.bitwise_and(rowk, 3)
            nxt = pltpu.roll(la, c - 1, 0)
            prv = pltpu.roll(la, 1, 0)
            return jnp.where(r4 == 0, nxt, jnp.where(r4 == 1, 0.0, jnp.where(r4 == 2, la, la + prv)))
        return jnp.where(jnp.bitwise_and(rowk, 1) == 1, la, 0.0)

    acc = [_nt_dot(qs[h].astype(BF16), ks[h].astype(BF16)) for h in heads]
    for w in reversed(LEVELS):
        upper = jnp.bitwise_and(rowk, w) != 0
        xs = [(jnp.where(upper, qs[h], ks[h]) * jnp.exp2(neg_dist(w, bs[h], las[h]))).astype(BF16) for h in heads]
        gs = [_nt_dot(x, x) for x in xs]
        acc = [jnp.where(xor >= w, gs[h], acc[h]) for h in heads]
    ab = [jnp.where(col <= row, a, 0.0).astype(BF16) for a in acc]

    sts = [st_refs[h][...] for h in heads]
    qx = [(qs[h] * jnp.exp2(bs[h])).astype(BF16) for h in heads]
    b_last = [b[c - 1:c, :] for b in bs]
    kx = [(ks[h] * jnp.exp2(b_last[h] - bs[h])).astype(BF16) for h in heads]
    vb = [v.astype(BF16) for v in vs]
    outs = [jnp.dot(ab[h], vb[h], preferred_element_type=F32) + _nt_dot(qx[h], sts[h].astype(BF16)) for h in heads]
    for h in heads:
        st_refs[h][...] = jnp.exp2(b_last[h]) * sts[h] + lax.dot_general(
            vb[h], kx[h], (((0,), (0,)), ((), ())), preferred_element_type=F32)
    return outs


def _gated_norm(o, g, gn):
    return _rms(o, gn) * (g * _sigmoid(g))


def _gla_kernel(q_ref, k_ref, v_ref, g_ref, la_ref, gn_ref, o_ref, st_ref, *, n_chunk):
    @pl.when(pl.program_id(2) == 0)
    def _():
        st_ref[...] = jnp.zeros_like(st_ref)

    dk, dv = GLA_DK, GLA_DV
    heads = range(st_ref.shape[0])

    def body(ci, carry):
        rows = pl.ds(pl.multiple_of(ci * CHUNK, CHUNK), CHUNK)
        kc = [slice(h * dk, (h + 1) * dk) for h in heads]
        vc = [slice(h * dv, (h + 1) * dv) for h in heads]
        outs = _mix_chunks([q_ref[rows, kc[h]] * (dk ** -0.5) for h in heads], [k_ref[rows, kc[h]] for h in heads],
                           [v_ref[rows, vc[h]] for h in heads], [la_ref[rows, kc[h]] for h in heads],
                           [st_ref.at[h] for h in heads])
        for h in heads:
            o_ref[rows, vc[h]] = _gated_norm(outs[h], g_ref[rows, vc[h]], gn_ref[...]).astype(o_ref.dtype)
        return carry

    lax.fori_loop(0, n_chunk, body, 0)


def _gla(proj, la, gn, *, batch, seq, tb, hp):
    m = proj.shape[0]
    nt = seq // tb
    wk, wv = hp * GLA_DK, hp * GLA_DV
    n_grp = GLA_HEADS // hp
    spec = lambda width, first: pl.BlockSpec((tb, width), lambda b, p, t: (b * nt + t, first + p))
    return pl.pallas_call(
        functools.partial(_gla_kernel, n_chunk=tb // CHUNK),
        grid=(batch, n_grp, nt),
        in_specs=[spec(wk, 0), spec(wk, n_grp), spec(wv, n_grp), spec(wv, 2 * n_grp), spec(wk, 0),
                  pl.BlockSpec((1, GLA_DV), lambda b, p, t: (0, 0))],
        out_specs=spec(wv, 0),
        out_shape=jax.ShapeDtypeStruct((m, GLA_HEADS * GLA_DV), BF16),
        scratch_shapes=[pltpu.VMEM((hp, GLA_DV, GLA_DK), F32)],
        compiler_params=_cparams(("parallel", "parallel", "arbitrary")),
        name="gla",
    )(proj, proj, proj, proj, la, gn)


def _hgrn_kernel(q_ref, f_ref, i_ref, g_ref, lb_ref, gn_ref, o_ref, st_ref, *, n_chunk):
    @pl.when(pl.program_id(2) == 0)
    def _():
        st_ref[...] = jnp.zeros_like(st_ref)

    dh = HGRN_DH
    heads = range(st_ref.shape[0])

    def body(ci, carry):
        rows = pl.ds(pl.multiple_of(ci * CHUNK, CHUNK), CHUNK)
        hc = [slice(h * dh, (h + 1) * dh) for h in heads]
        qs, ks, las = [], [], []
        for h in heads:
            log_lb, log_1mlb, one_m_lb = lb_ref[0:1, hc[h]], lb_ref[1:2, hc[h]], lb_ref[2:3, hc[h]]
            hq = q_ref[rows, hc[h]]
            z = f_ref[rows, hc[h]]
            x1 = log_1mlb + _log_sigmoid(z)
            las.append(jnp.maximum(log_lb, x1) + _log1pexp_neg(jnp.abs(log_lb - x1)))
            qs.append(hq * _sigmoid(hq))
            ks.append(one_m_lb * _sigmoid(-z))
        outs = _mix_chunks(qs, ks, [i_ref[rows, hc[h]] for h in heads], las, [st_ref.at[h] for h in heads])
        for h in heads:
            o_ref[rows, hc[h]] = _gated_norm(outs[h], g_ref[rows, hc[h]], gn_ref[...]).astype(o_ref.dtype)
        return carry

    lax.fori_loop(0, n_chunk, body, 0)


def _hgrn(proj, lbc, gn, *, batch, seq, tb, hp):
    m = proj.shape[0]
    nt = seq // tb
    width = hp * HGRN_DH
    n_grp = HGRN_HEADS // hp
    first = 3072 // width
    spec = lambda seg: pl.BlockSpec((tb, width), lambda b, p, t: (b * nt + t, first + seg * n_grp + p))
    return pl.pallas_call(
        functools.partial(_hgrn_kernel, n_chunk=tb // CHUNK),
        grid=(batch, n_grp, nt),
        in_specs=[spec(0), spec(1), spec(2), spec(3),
                  pl.BlockSpec((8, width), lambda b, p, t: (0, p)),
                  pl.BlockSpec((1, HGRN_DH), lambda b, p, t: (0, 0))],
        out_specs=pl.BlockSpec((tb, width), lambda b, p, t: (b * nt + t, p)),
        out_shape=jax.ShapeDtypeStruct((m, HGRN_HEADS * HGRN_DH), BF16),
        scratch_shapes=[pltpu.VMEM((hp, HGRN_DH, HGRN_DH), F32)],
        compiler_params=_cparams(("parallel", "parallel", "arbitrary")),
        name="hgrn",
    )(proj, proj, proj, proj, lbc, gn)


def _xattn_kernel(q_ref, k_ref, v_ref, o_ref):
    for h in range(XA_HEADS):
        cols = slice(h * XA_DH, (h + 1) * XA_DH)
        s = lax.dot_general(q_ref[:, cols], k_ref[:, cols], (((1,), (1,)), ((), ())),
                            preferred_element_type=F32) * (XA_DH ** -0.5)
        p = jnp.exp(s - jnp.max(s, axis=-1, keepdims=True))
        p = p / jnp.sum(p, axis=-1, keepdims=True)
        o_ref[:, cols] = jnp.dot(p.astype(BF16), v_ref[:, cols], preferred_element_type=F32).astype(o_ref.dtype)


def _xattn(q, kv, *, batch, seq, n_mem, tq):
    m, d = q.shape
    nt = seq // tq
    return pl.pallas_call(
        _xattn_kernel,
        grid=(batch, nt),
        in_specs=[
            pl.BlockSpec((tq, d), lambda b, t: (b * nt + t, 0)),
            pl.BlockSpec((n_mem, d), lambda b, t: (b, 0)),
            pl.BlockSpec((n_mem, d), lambda b, t: (b, 1)),
        ],
        out_specs=pl.BlockSpec((tq, d), lambda b, t: (b * nt + t, 0)),
        out_shape=jax.ShapeDtypeStruct((m, d), BF16),
        compiler_params=_cparams(("parallel", "arbitrary")),
        name="xattn",
    )(q, kv, kv)


def _router_kernel(h_ref, g_ref, wr_ref, br_ref, a_ref, meta_ref, cnt_ref, carry_ref):
    tm = h_ref.shape[0]

    @pl.when(pl.program_id(0) == 0)
    def _():
        carry_ref[...] = jnp.zeros_like(carry_ref)

    a = _rms(h_ref[...], g_ref[...])
    a_ref[...] = a.reshape(a_ref.shape)
    logits = jnp.dot(a, wr_ref[...], precision=HIGHEST, preferred_element_type=F32) + br_ref[...]
    lane = lax.broadcasted_iota(jnp.int32, (tm, LANES), 1)
    lane_f = lane.astype(F32)
    neg = -jnp.inf

    gl = jnp.where(lane < N_GROUPS, logits, neg)
    gmax = jnp.max(gl, axis=1, keepdims=True)
    gidx = jnp.min(jnp.where(gl == gmax, lane_f, float(LANES)), axis=1, keepdims=True)
    p_group = 1.0 / jnp.sum(jnp.exp(gl - gmax), axis=1, keepdims=True)
    lo = float(N_GROUPS) + gidx * float(EXPERTS_PER_GROUP)
    el = jnp.where((lane_f >= lo) & (lane_f < lo + float(EXPERTS_PER_GROUP)), logits, neg)
    v1 = jnp.max(el, axis=1, keepdims=True)
    i1 = jnp.min(jnp.where(el == v1, lane_f, float(LANES)), axis=1, keepdims=True)
    el2 = jnp.where(lane_f == i1, neg, el)
    v2 = jnp.max(el2, axis=1, keepdims=True)
    i2 = jnp.min(jnp.where(el2 == v2, lane_f, float(LANES)), axis=1, keepdims=True)
    t = jnp.exp(v2 - v1)
    g1 = p_group / (1.0 + t)
    g2 = p_group * t / (1.0 + t)

    hit1 = lane_f == i1
    hit2 = lane_f == i2
    onehot = jnp.where(hit1 | hit2, 1.0, 0.0)
    row = lax.broadcasted_iota(jnp.int32, (tm, tm), 0)
    col = lax.broadcasted_iota(jnp.int32, (tm, tm), 1)
    strict = jnp.where(col < row, 1.0, 0.0).astype(BF16)
    before = jnp.dot(strict, onehot.astype(BF16), preferred_element_type=F32) + carry_ref[0:1, :]
    r1 = jnp.sum(jnp.where(hit1, before, 0.0), axis=1, keepdims=True)
    r2 = jnp.sum(jnp.where(hit2, before, 0.0), axis=1, keepdims=True)
    carry_ref[...] = carry_ref[...] + jnp.sum(onehot, axis=0, keepdims=True)
    cnt_ref[...] = carry_ref[...]

    meta = jnp.zeros((tm, LANES), F32)
    for idx, val in enumerate((i1 - float(N_GROUPS), i2 - float(N_GROUPS), r1, r2, g1, g2)):
        meta = jnp.where(lane == idx, val, meta)
    meta_ref[...] = meta


def _router(h, g, wr, br, *, tm):
    m, d = h.shape
    slab = d // LANES
    return pl.pallas_call(
        _router_kernel,
        grid=(m // tm,),
        in_specs=[
            pl.BlockSpec((tm, d), lambda i: (i, 0)),
            pl.BlockSpec((1, d), lambda i: (0, 0)),
            pl.BlockSpec((d, LANES), lambda i: (0, 0)),
            pl.BlockSpec((1, LANES), lambda i: (0, 0)),
        ],
        out_specs=[
            pl.BlockSpec((tm, slab, LANES), lambda i: (i, 0, 0)),
            pl.BlockSpec((tm, LANES), lambda i: (i, 0)),
            pl.BlockSpec((8, LANES), lambda i: (0, 0)),
        ],
        out_shape=[jax.ShapeDtypeStruct((m, slab, LANES), F32), jax.ShapeDtypeStruct((m, LANES), F32),
                   jax.ShapeDtypeStruct((8, LANES), F32)],
        scratch_shapes=[pltpu.VMEM((8, LANES), F32)],
        compiler_params=_cparams(("arbitrary",)),
        name="router",
    )(h, g, wr, br)


def _slot_token(v):
    return jnp.maximum(v, 0) >> 1


def _expert_a_kernel(slot_ref, be_ref, nb_ref, a_hbm, wg_ref, wu_ref, hb_ref, xg0, xg1, wgb_ref, wub_ref, sem):
    i = pl.program_id(0)
    nb = nb_ref[0]
    rb, de = hb_ref.shape
    bufs = (xg0, xg1)

    def row_copy(blk, r, buf, s):
        return pltpu.make_async_copy(a_hbm.at[_slot_token(slot_ref[blk * rb + r])], buf.at[r], sem.at[s])

    def wait_block(buf, s):
        pltpu.make_async_copy(a_hbm.at[pl.ds(0, rb)], buf, sem.at[s]).wait()

    @pl.when(i == 0)
    def _():
        def body(r, carry):
            row_copy(0, r, xg0, 0).start()
            return carry
        lax.fori_loop(0, rb, body, 0)

    for s in (0, 1):
        @pl.when((i < nb) & (lax.rem(i, 2) == s))
        def _():
            @pl.when((i == 0) | (be_ref[i] != be_ref[jnp.maximum(i - 1, 0)]))
            def _():
                wgb_ref[...] = wg_ref[0].astype(BF16)
                wub_ref[...] = wu_ref[0].astype(BF16)

            wait_block(bufs[s], s)
            for r in range(rb):
                row_copy(i + 1, r, bufs[1 - s], 1 - s).start()
            x = bufs[s][...].reshape(rb, -1).astype(BF16)
            hg = jnp.dot(x, wgb_ref[...], preferred_element_type=F32)
            hu = jnp.dot(x, wub_ref[...], preferred_element_type=F32)
            hb_ref[...] = (hg * _sigmoid(hg) * hu).astype(hb_ref.dtype)

        @pl.when((i == nb) & (lax.rem(i, 2) == s))
        def _():
            wait_block(bufs[s], s)

    @pl.when(i >= nb)
    def _():
        hb_ref[...] = jnp.zeros_like(hb_ref)


def _expert_a(slot, block_e, n_blk, a, wg, wu, *, rb):
    n_steps = slot.shape[0] // rb
    slab = a.shape[1]
    d = slab * LANES
    de = wg.shape[2]
    wmap = lambda i, sl, be, nb: (be[jnp.minimum(i, nb[0] - 1)], 0, 0)
    grid_spec = pltpu.PrefetchScalarGridSpec(
        num_scalar_prefetch=3,
        grid=(n_steps,),
        in_specs=[
            pl.BlockSpec(memory_space=pl.ANY),
            pl.BlockSpec((1, d, de), wmap),
            pl.BlockSpec((1, d, de), wmap),
        ],
        out_specs=pl.BlockSpec((rb, de), lambda i, sl, be, nb: (i, 0)),
        scratch_shapes=[pltpu.VMEM((rb, slab, LANES), F32), pltpu.VMEM((rb, slab, LANES), F32),
                        pltpu.VMEM((d, de), BF16), pltpu.VMEM((d, de), BF16), pltpu.SemaphoreType.DMA((2,))],
    )
    return pl.pallas_call(
        _expert_a_kernel,
        grid_spec=grid_spec,
        out_shape=jax.ShapeDtypeStruct((n_steps * rb, de), BF16),
        compiler_params=_cparams(("arbitrary",)),
        name="expert_up",
    )(slot, block_e, n_blk, a, wg, wu)


def _expert_b_kernel(slot_ref, be_ref, nb_ref, hb_ref, wd_ref, y_hbm, wdb_ref, ys0, ys1, sem, *, n_tok):
    i = pl.program_id(0)
    n_steps = pl.num_programs(0)
    nb = nb_ref[0]
    rb = hb_ref.shape[0]
    bufs = (ys0, ys1)

    def wait_block(buf, s):
        pltpu.make_async_copy(buf, y_hbm.at[pl.ds(0, rb)], sem.at[s]).wait()

    @pl.when(i == 0)
    def _():
        ys0[...] = jnp.zeros_like(ys0)
        for s in (0, 1):
            spare = pltpu.make_async_copy(ys0, y_hbm.at[pl.ds(2 * n_tok + s * rb, rb)], sem.at[0])
            spare.start()
            spare.wait()

    for s in (0, 1):
        @pl.when((i < nb) & (lax.rem(i, 2) == s))
        def _():
            @pl.when((i == 0) | (be_ref[i] != be_ref[jnp.maximum(i - 1, 0)]))
            def _():
                wdb_ref[...] = wd_ref[0].astype(BF16)

            @pl.when(i >= 2)
            def _():
                wait_block(bufs[s], s)

            y = jnp.dot(hb_ref[...], wdb_ref[...], preferred_element_type=F32)
            bufs[s][...] = y.reshape(bufs[s].shape)
            for r in range(rb):
                v = slot_ref[i * rb + r]
                dst = jnp.where(v >= 0, (v & 1) * n_tok + (v >> 1), 2 * n_tok + s * rb + r)
                pltpu.make_async_copy(bufs[s].at[r], y_hbm.at[dst], sem.at[s]).start()

    @pl.when(i == n_steps - 1)
    def _():
        for s in (0, 1):
            @pl.when(((nb >= 1) & (lax.rem(nb - 1, 2) == s)) | ((nb >= 2) & (lax.rem(nb, 2) == s)))
            def _():
                wait_block(bufs[s], s)


def _expert_b(slot, block_e, n_blk, hb, wd, *, rb, n_tok):
    n_steps = hb.shape[0] // rb
    de = hb.shape[1]
    d = wd.shape[2]
    slab = d // LANES
    grid_spec = pltpu.PrefetchScalarGridSpec(
        num_scalar_prefetch=3,
        grid=(n_steps,),
        in_specs=[
            pl.BlockSpec((rb, de), lambda i, sl, be, nb: (i, 0)),
            pl.BlockSpec((1, de, d), lambda i, sl, be, nb: (be[jnp.minimum(i, nb[0] - 1)], 0, 0)),
        ],
        out_specs=pl.BlockSpec(memory_space=pl.ANY),
        scratch_shapes=[pltpu.VMEM((de, d), BF16), pltpu.VMEM((rb, slab, LANES), F32),
                        pltpu.VMEM((rb, slab, LANES), F32), pltpu.SemaphoreType.DMA((2,))],
    )
    return pl.pallas_call(
        functools.partial(_expert_b_kernel, n_tok=n_tok),
        grid_spec=grid_spec,
        out_shape=jax.ShapeDtypeStruct((2 * n_tok + 2 * rb, slab, LANES), F32),
        compiler_params=_cparams(("arbitrary",)),
        name="expert_down",
    )(slot, block_e, n_blk, hb, wd)


def _combine_kernel(y0_ref, y1_ref, h_ref, meta_ref, g_ref, o_ref):
    tm, d = h_ref.shape
    meta = meta_ref[...]
    out = h_ref[...] + meta[:, 4:5] * y0_ref[...].reshape(tm, d) + meta[:, 5:6] * y1_ref[...].reshape(tm, d)
    o_ref[...] = _rms(out, g_ref[...])


def _combine(y, h, meta, g, *, tm):
    m, d = h.shape
    slab = d // LANES
    nt = m // tm
    return pl.pallas_call(
        _combine_kernel,
        grid=(nt,),
        in_specs=[
            pl.BlockSpec((tm, slab, LANES), lambda i: (i, 0, 0)),
            pl.BlockSpec((tm, slab, LANES), lambda i: (nt + i, 0, 0)),
            pl.BlockSpec((tm, d), lambda i: (i, 0)),
            pl.BlockSpec((tm, LANES), lambda i: (i, 0)),
            pl.BlockSpec((1, d), lambda i: (0, 0)),
        ],
        out_specs=pl.BlockSpec((tm, d), lambda i: (i, 0)),
        out_shape=jax.ShapeDtypeStruct((m, d), F32),
        compiler_params=_cparams(("parallel",)),
        name="combine",
    )(y, y, h, meta, g)


def kernel(x, mem, norm_mix_g, w_in, w_gla_alpha_up, b_gla_alpha, gla_out_norm_g, hgrn_lb_logits, hgrn_out_norm_g, w_mix_out, norm_xattn_g, norm_mem_g, w_xattn_q, w_xattn_kv, w_xattn_out, norm_ffn_g, w_router_group, b_router_group, w_router_expert, b_router_expert, w_expert_gate, w_expert_up, w_expert_down, norm_final_g):
    batch, seq, d = x.shape
    n_mem = mem.shape[1]
    m = batch * seq
    depth = norm_mix_g.shape[0]
    h = x.reshape(m, d)
    lb_all = jnp.cumsum(jax.nn.softmax(hgrn_lb_logits.astype(F32), axis=0), axis=0)
    gla_cols = 2 * GLA_HEADS * GLA_DK + 2 * GLA_HEADS * GLA_DV
    lr_rank = w_gla_alpha_up.shape[1]

    for l in range(depth):
        w_t = jnp.swapaxes(w_in[l], 0, 1)
        w_lr_t = jnp.pad(w_t[gla_cols:gla_cols + lr_rank], ((0, LANES - lr_rank), (0, 0))).astype(BF16)
        w_up = jnp.pad(w_gla_alpha_up[l], ((0, LANES - lr_rank), (0, 0))).astype(BF16)
        proj, la = _inproj(h, norm_mix_g[l][None, :], w_t, w_lr_t, w_up, b_gla_alpha[l][None, :],
                           tm=min(m, 1024), tn=512, skip_from=gla_cols, skip=lr_rank)
        lb = lb_all[l]
        lbc = jnp.zeros((8, lb.shape[0]), F32).at[0].set(jnp.log(lb)).at[1].set(jnp.log1p(-lb)).at[2].set(1.0 - lb)
        o_gla = _gla(proj, la, gla_out_norm_g[l][None, :], batch=batch, seq=seq, tb=min(seq, 512), hp=4)
        o_h = _hgrn(proj, lbc, hgrn_out_norm_g[l][None, :], batch=batch, seq=seq, tb=min(seq, 1024), hp=4)
        h = _mm_res([o_gla, o_h], w_mix_out[l], h, tm=min(m, 512), tn=d, name="mix_out")

        kv = _norm_mm(mem.reshape(batch * n_mem, d), norm_mem_g[l][None, :], w_xattn_kv[l],
                      tm=batch * n_mem, tn=1024, out_dtype=BF16, name="mem_kv")
        q = _norm_mm(h, norm_xattn_g[l][None, :], w_xattn_q[l], tm=min(m, 512), tn=d, out_dtype=BF16,
                     name="xattn_q")
        o = _xattn(q, kv, batch=batch, seq=seq, n_mem=n_mem, tq=min(seq, 512))
        h = _mm_res([o], w_xattn_out[l], h, tm=min(m, 512), tn=d, name="xattn_out")

        wr = jnp.pad(jnp.concatenate([w_router_group[l], w_router_expert[l]], axis=1),
                     ((0, 0), (0, LANES - N_GROUPS - N_EXPERTS)))
        br = jnp.pad(jnp.concatenate([b_router_group[l], b_router_expert[l]]), (0, LANES - N_GROUPS - N_EXPERTS))
        a, meta, cnt = _router(h, norm_ffn_g[l][None, :], wr, br[None, :], tm=min(m, 512))

        e_idx = meta[:, 0:2].astype(jnp.int32)
        rank = meta[:, 2:4].astype(jnp.int32)
        counts = cnt[0, N_GROUPS:N_GROUPS + N_EXPERTS].astype(jnp.int32)
        padded = ((counts + EXPERT_BLOCK - 1) // EXPERT_BLOCK) * EXPERT_BLOCK
        pad_end = jnp.cumsum(padded)
        dest = (pad_end - padded)[e_idx] + rank
        n_rows = 2 * m + N_EXPERTS * EXPERT_BLOCK
        n_blocks = n_rows // EXPERT_BLOCK
        n_blk = (pad_end[-1:] // EXPERT_BLOCK).astype(jnp.int32)
        blk_first = jnp.arange(n_blocks + 1, dtype=jnp.int32) * EXPERT_BLOCK
        block_e = jnp.minimum(jnp.sum((pad_end[None, :] <= blk_first[:, None]).astype(jnp.int32), axis=1),
                              N_EXPERTS - 1)
        assign = 2 * jnp.arange(m, dtype=jnp.int32)[:, None] + jnp.arange(2, dtype=jnp.int32)[None, :]
        slot = jnp.full((n_rows + EXPERT_BLOCK,), -1, jnp.int32).at[dest.reshape(-1)].set(assign.reshape(-1))

        hb = _expert_a(slot, block_e, n_blk, a, w_expert_gate[l], w_expert_up[l], rb=EXPERT_BLOCK)
        y = _expert_b(slot, block_e, n_blk, hb, w_expert_down[l], rb=EXPERT_BLOCK, n_tok=m)
        last = l == depth - 1
        g_fin = norm_final_g[None, :] if last else jnp.ones((1, d), F32)
        h = _combine(y, h, meta, g_fin, tm=min(m, 256))
        assert last, "the combine kernel fuses the final rmsnorm; deeper stacks need an un-normalised variant"

    return h.reshape(batch, seq, d)
```

```python
import functools

import jax
import jax.numpy as jnp
from jax import lax
from jax.experimental import pallas as pl
from jax.experimental.pallas import tpu as pltpu

F32 = jnp.float32
BF16 = jnp.bfloat16
HIGHEST = lax.Precision.HIGHEST

EPS = 1e-6
CHUNK = 64
LANES = 128
GLA_HEADS, GLA_DK, GLA_DV = 4, 128, 256
HGRN_HEADS, HGRN_DH = 8, 128
XA_HEADS, XA_DH = 4, 512
N_GROUPS, EXPERTS_PER_GROUP, N_EXPERTS = 4, 8, 32
EXPERT_BLOCK = 256
WEIGHT_CHUNKS = 4
VMEM_LIMIT = 56 * 1024 * 1024


def _cparams(sem, vmem=VMEM_LIMIT):
    return pltpu.CompilerParams(dimension_semantics=sem, vmem_limit_bytes=vmem)


def _log1pexp_neg(t):
    return jnp.log(1.0 + jnp.exp(-t))


def _log_sigmoid(z):
    return jnp.minimum(z, 0.0) - _log1pexp_neg(jnp.abs(z))


def _sigmoid(z):
    return 1.0 / (1.0 + jnp.exp(-z))


def _rms(x, g):
    return x * lax.rsqrt(jnp.mean(x * x, axis=-1, keepdims=True) + EPS) * g


def _nt_dot(x, y):
    return lax.dot_general(x, y, (((1,), (1,)), ((), ())), preferred_element_type=F32)


NORM_ROWS = 256


def _for_row_chunks(n_rows, fn):
    step = min(NORM_ROWS, n_rows)

    def body(ci, carry):
        fn(pl.ds(pl.multiple_of(ci * step, step), step))
        return carry

    lax.fori_loop(0, n_rows // step, body, 0)


def _norm_gate_kernel(x_ref, g_ref, wlr_ref, wup_ref, bal_ref, a_ref, la_ref):
    a = _rms(x_ref[...], g_ref[...]).astype(BF16)
    a_ref[...] = a
    lr = _nt_dot(a, wlr_ref[...])
    z = jnp.dot(lr.astype(BF16), wup_ref[...], preferred_element_type=F32) + bal_ref[...]
    la_ref[...] = _log_sigmoid(z) * (1.0 / 16.0)


def _norm_gate(x, g, w_lr_t, w_up, b_al, *, tm):
    m, d = x.shape
    nk = w_up.shape[1]
    return pl.pallas_call(
        _norm_gate_kernel,
        grid=(m // tm,),
        in_specs=[
            pl.BlockSpec((tm, d), lambda i: (i, 0)),
            pl.BlockSpec((1, d), lambda i: (0, 0)),
            pl.BlockSpec((LANES, d), lambda i: (0, 0)),
            pl.BlockSpec((LANES, nk), lambda i: (0, 0)),
            pl.BlockSpec((1, nk), lambda i: (0, 0)),
        ],
        out_specs=[pl.BlockSpec((tm, d), lambda i: (i, 0)), pl.BlockSpec((tm, nk), lambda i: (i, 0))],
        out_shape=[jax.ShapeDtypeStruct((m, d), BF16), jax.ShapeDtypeStruct((m, nk), F32)],
        compiler_params=_cparams(("parallel",)),
        name="norm_gate",
    )(x, g, w_lr_t, w_up, b_al)


def _inproj_kernel(a_ref, w_ref, proj_ref):
    proj_ref[...] = _nt_dot(a_ref[...], w_ref[...].astype(BF16))


def _inproj(a, w_t, *, tm, tn, skip_from, skip):
    m, d = a.shape
    n = w_t.shape[0] - skip
    first_after = skip_from // tn

    def w_rows(i, j):
        return (pl.multiple_of(j * tn + jnp.where(j >= first_after, skip, 0), 8), 0)

    return pl.pallas_call(
        _inproj_kernel,
        grid=(m // tm, n // tn),
        in_specs=[
            pl.BlockSpec((tm, d), lambda i, j: (i, 0), pipeline_mode=pl.Buffered(1)),
            pl.BlockSpec((pl.Element(tn), pl.Element(d)), w_rows),
        ],
        out_specs=pl.BlockSpec((tm, tn), lambda i, j: (i, j)),
        out_shape=jax.ShapeDtypeStruct((m, n), F32),
        compiler_params=_cparams(("parallel", "arbitrary")),
        name="inproj",
    )(a, w_t)


def _weight_spec(k, n, tn):
    if tn == n:
        return pl.BlockSpec((k, n), lambda i, j: (0, 0), pipeline_mode=pl.Buffered(1))
    return pl.BlockSpec((k, tn), lambda i, j: (0, j))


def _cast_weight(w_ref, wb_ref, resident):
    if resident:
        @pl.when((pl.program_id(0) == 0) & (pl.program_id(1) == 0))
        def _():
            wb_ref[...] = w_ref[...].astype(BF16)
    else:
        wb_ref[...] = w_ref[...].astype(BF16)


def _norm_mm_kernel(x_ref, g_ref, w_ref, o_ref, a_ref, wb_ref, *, resident):
    @pl.when(pl.program_id(1) == 0)
    def _():
        def rows_fn(rows):
            a_ref[rows, :] = _rms(x_ref[rows, :], g_ref[...]).astype(BF16)

        _for_row_chunks(x_ref.shape[0], rows_fn)

    _cast_weight(w_ref, wb_ref, resident)
    o_ref[...] = jnp.dot(a_ref[...], wb_ref[...], preferred_element_type=F32).astype(o_ref.dtype)


def _norm_mm(x, g, w, *, tm, tn, out_dtype, name):
    m, d = x.shape
    n = w.shape[1]
    return pl.pallas_call(
        functools.partial(_norm_mm_kernel, resident=tn == n),
        grid=(m // tm, n // tn),
        in_specs=[
            pl.BlockSpec((tm, d), lambda i, j: (i, 0)),
            pl.BlockSpec((1, d), lambda i, j: (0, 0)),
            _weight_spec(d, n, tn),
        ],
        out_specs=pl.BlockSpec((tm, tn), lambda i, j: (i, j)),
        out_shape=jax.ShapeDtypeStruct((m, n), out_dtype),
        scratch_shapes=[pltpu.VMEM((tm, d), BF16), pltpu.VMEM((d, tn), BF16)],
        compiler_params=_cparams(("arbitrary", "arbitrary")),
        name=name,
    )(x, g, w)


def _mm_res_kernel(*refs, n_lhs, resident):
    lhs = refs[:n_lhs]
    w_ref, res_ref, o_ref, wb_ref = refs[n_lhs:]
    _cast_weight(w_ref, wb_ref, resident)
    acc = res_ref[...]
    k0 = 0
    for l_ref in lhs:
        kp = l_ref.shape[1]
        acc = acc + jnp.dot(l_ref[...], wb_ref[k0:k0 + kp, :], preferred_element_type=F32)
        k0 += kp
    o_ref[...] = acc


def _mm_res(lhs_parts, w, res, *, tm, tn, name):
    m, n = res.shape
    k = w.shape[0]
    n_lhs = len(lhs_parts)
    in_specs = [pl.BlockSpec((tm, p.shape[1]), lambda i, j: (i, 0)) for p in lhs_parts]
    in_specs += [_weight_spec(k, n, tn), pl.BlockSpec((tm, tn), lambda i, j: (i, j))]
    return pl.pallas_call(
        functools.partial(_mm_res_kernel, n_lhs=n_lhs, resident=tn == n),
        grid=(m // tm, n // tn),
        in_specs=in_specs,
        out_specs=pl.BlockSpec((tm, tn), lambda i, j: (i, j)),
        out_shape=jax.ShapeDtypeStruct((m, n), F32),
        scratch_shapes=[pltpu.VMEM((k, tn), BF16)],
        compiler_params=_cparams(("arbitrary", "arbitrary")),
        name=name,
    )(*lhs_parts, w, res)


LEVELS = (32, 16, 8, 4, 2, 1)
LOG2E = 1.4426950408889634


def _split3_bf16(x):
    def top(v):
        bits = lax.bitcast_convert_type(v, jnp.uint32) & jnp.uint32(0xFFFF0000)
        return lax.bitcast_convert_type(bits, F32)
    hi = top(x)
    r1 = x - hi
    mid = top(r1)
    lo = r1 - mid
    return hi.astype(BF16), mid.astype(BF16), lo.astype(BF16)


def _mix_chunks(qs, ks, vs, las, st_refs):
    heads = range(len(qs))
    c, dk = qs[0].shape
    row = lax.broadcasted_iota(jnp.int32, (c, c), 0)
    col = lax.broadcasted_iota(jnp.int32, (c, c), 1)
    rowk = lax.broadcasted_iota(jnp.int32, (c, dk), 0)
    xor = jnp.bitwise_xor(row, col)

    las = [la * LOG2E for la in las]

    tri = jnp.where(col <= row, 1.0, 0.0).astype(BF16)
    b3 = [jnp.dot(tri, jnp.concatenate(_split3_bf16(la), axis=1), preferred_element_type=F32) for la in las]
    bs = [(t[:, :dk] + t[:, dk:2 * dk]) + t[:, 2 * dk:] for t in b3]

    def neg_dist(w, b, la):
        if w >= 4:
            parts = [jnp.broadcast_to(b[base + w - 1:base + w, :], (2 * w, dk)) for base in range(0, c, 2 * w)]
            m = parts[0] if len(parts) == 1 else jnp.concatenate(parts, axis=0)
            return -jnp.abs(b - m)
        if w == 2:
            r4 = jnp.bitwise_and(rowk, 3)
            nxt = pltpu.roll(la, c - 1, 0)
            prv = pltpu.roll(la, 1, 0)
            return jnp.where(r4 == 0, nxt, jnp.where(r4 == 1, 0.0, jnp.where(r4 == 2, la, la + prv)))
        return jnp.where(jnp.bitwise_and(rowk, 1) == 1, la, 0.0)

    acc = [_nt_dot(qs[h].astype(BF16), ks[h].astype(BF16)) for h in heads]
    for w in reversed(LEVELS):
        upper = jnp.bitwise_and(rowk, w) != 0
        xs = [(jnp.where(upper, qs[h], ks[h]) * jnp.exp2(neg_dist(w, bs[h], las[h]))).astype(BF16) for h in heads]
        gs = [_nt_dot(x, x) for x in xs]
        acc = [jnp.where(xor >= w, gs[h], acc[h]) for h in heads]
    ab = [jnp.where(col <= row, a, 0.0).astype(BF16) for a in acc]

    sts = [st_refs[h][...] for h in heads]
    qx = [(qs[h] * jnp.exp2(bs[h])).astype(BF16) for h in heads]
    b_last = [b[c - 1:c, :] for b in bs]
    kx = [(ks[h] * jnp.exp2(b_last[h] - bs[h])).astype(BF16) for h in heads]
    vb = [v.astype(BF16) for v in vs]
    outs = [jnp.dot(ab[h], vb[h], preferred_element_type=F32) + _nt_dot(qx[h], sts[h].astype(BF16)) for h in heads]
    for h in heads:
        st_refs[h][...] = jnp.exp2(b_last[h]) * sts[h] + lax.dot_general(
            vb[h], kx[h], (((0,), (0,)), ((), ())), preferred_element_type=F32)
    return outs


def _gated_norm(o, g, gn):
    return _rms(o, gn) * (g * _sigmoid(g))


def _gla_kernel(q_ref, k_ref, v_ref, g_ref, la_ref, gn_ref, o_ref, st_ref, *, n_chunk):
    @pl.when(pl.program_id(2) == 0)
    def _():
        st_ref[...] = jnp.zeros_like(st_ref)

    dk, dv = GLA_DK, GLA_DV
    heads = range(st_ref.shape[0])

    def body(ci, carry):
        rows = pl.ds(pl.multiple_of(ci * CHUNK, CHUNK), CHUNK)
        kc = [slice(h * dk, (h + 1) * dk) for h in heads]
        vc = [slice(h * dv, (h + 1) * dv) for h in heads]
        outs = _mix_chunks([q_ref[rows, kc[h]] * (dk ** -0.5) for h in heads], [k_ref[rows, kc[h]] for h in heads],
                           [v_ref[rows, vc[h]] for h in heads], [la_ref[rows, kc[h]] for h in heads],
                           [st_ref.at[h] for h in heads])
        for h in heads:
            o_ref[rows, vc[h]] = _gated_norm(outs[h], g_ref[rows, vc[h]], gn_ref[...]).astype(o_ref.dtype)
        return carry

    lax.fori_loop(0, n_chunk, body, 0)


def _gla(proj, la, gn, *, batch, seq, tb, hp):
    m = proj.shape[0]
    nt = seq // tb
    wk, wv = hp * GLA_DK, hp * GLA_DV
    n_grp = GLA_HEADS // hp
    spec = lambda width, first: pl.BlockSpec((tb, width), lambda b, p, t: (b * nt + t, first + p))
    return pl.pallas_call(
        functools.partial(_gla_kernel, n_chunk=tb // CHUNK),
        grid=(batch, n_grp, nt),
        in_specs=[spec(wk, 0), spec(wk, n_grp), spec(wv, n_grp), spec(wv, 2 * n_grp), spec(wk, 0),
                  pl.BlockSpec((1, GLA_DV), lambda b, p, t: (0, 0))],
        out_specs=spec(wv, 0),
        out_shape=jax.ShapeDtypeStruct((m, GLA_HEADS * GLA_DV), BF16),
        scratch_shapes=[pltpu.VMEM((hp, GLA_DV, GLA_DK), F32)],
        compiler_params=_cparams(("parallel", "parallel", "arbitrary")),
        name="gla",
    )(proj, proj, proj, proj, la, gn)


def _hgrn_kernel(q_ref, f_ref, i_ref, g_ref, lb_ref, gn_ref, o_ref, st_ref, *, n_chunk):
    @pl.when(pl.program_id(2) == 0)
    def _():
        st_ref[...] = jnp.zeros_like(st_ref)

    dh = HGRN_DH
    heads = range(st_ref.shape[0])

    def body(ci, carry):
        rows = pl.ds(pl.multiple_of(ci * CHUNK, CHUNK), CHUNK)
        hc = [slice(h * dh, (h + 1) * dh) for h in heads]
        qs, ks, las = [], [], []
        for h in heads:
            log_lb, log_1mlb, one_m_lb = lb_ref[0:1, hc[h]], lb_ref[1:2, hc[h]], lb_ref[2:3, hc[h]]
            hq = q_ref[rows, hc[h]]
            z = f_ref[rows, hc[h]]
            x1 = log_1mlb + _log_sigmoid(z)
            las.append(jnp.maximum(log_lb, x1) + _log1pexp_neg(jnp.abs(log_lb - x1)))
            qs.append(hq * _sigmoid(hq))
            ks.append(one_m_lb * _sigmoid(-z))
        outs = _mix_chunks(qs, ks, [i_ref[rows, hc[h]] for h in heads], las, [st_ref.at[h] for h in heads])
        for h in heads:
            o_ref[rows, hc[h]] = _gated_norm(outs[h], g_ref[rows, hc[h]], gn_ref[...]).astype(o_ref.dtype)
        return carry

    lax.fori_loop(0, n_chunk, body, 0)


def _hgrn(proj, lbc, gn, *, batch, seq, tb, hp):
    m = proj.shape[0]
    nt = seq // tb
    width = hp * HGRN_DH
    n_grp = HGRN_HEADS // hp
    first = 3072 // width
    spec = lambda seg: pl.BlockSpec((tb, width), lambda b, p, t: (b * nt + t, first + seg * n_grp + p))
    return pl.pallas_call(
        functools.partial(_hgrn_kernel, n_chunk=tb // CHUNK),
        grid=(batch, n_grp, nt),
        in_specs=[spec(0), spec(1), spec(2), spec(3),
                  pl.BlockSpec((8, width), lambda b, p, t: (0, p)),
                  pl.BlockSpec((1, HGRN_DH), lambda b, p, t: (0, 0))],
        out_specs=pl.BlockSpec((tb, width), lambda b, p, t: (b * nt + t, p)),
        out_shape=jax.ShapeDtypeStruct((m, HGRN_HEADS * HGRN_DH), BF16),
        scratch_shapes=[pltpu.VMEM((hp, HGRN_DH, HGRN_DH), F32)],
        compiler_params=_cparams(("parallel", "parallel", "arbitrary")),
        name="hgrn",
    )(proj, proj, proj, proj, lbc, gn)


def _xattn_kernel(q_ref, k_ref, v_ref, o_ref):
    for h in range(XA_HEADS):
        cols = slice(h * XA_DH, (h + 1) * XA_DH)
        s = lax.dot_general(q_ref[:, cols], k_ref[:, cols], (((1,), (1,)), ((), ())),
                            preferred_element_type=F32) * (XA_DH ** -0.5)
        p = jnp.exp(s - jnp.max(s, axis=-1, keepdims=True))
        p = p / jnp.sum(p, axis=-1, keepdims=True)
        o_ref[:, cols] = jnp.dot(p.astype(BF16), v_ref[:, cols], preferred_element_type=F32).astype(o_ref.dtype)


def _xattn(q, kv, *, batch, seq, n_mem, tq):
    m, d = q.shape
    nt = seq // tq
    return pl.pallas_call(
        _xattn_kernel,
        grid=(batch, nt),
        in_specs=[
            pl.BlockSpec((tq, d), lambda b, t: (b * nt + t, 0)),
            pl.BlockSpec((n_mem, d), lambda b, t: (b, 0)),
            pl.BlockSpec((n_mem, d), lambda b, t: (b, 1)),
        ],
        out_specs=pl.BlockSpec((tq, d), lambda b, t: (b * nt + t, 0)),
        out_shape=jax.ShapeDtypeStruct((m, d), BF16),
        compiler_params=_cparams(("parallel", "arbitrary")),
        name="xattn",
    )(q, kv, kv)


def _router_kernel(h_ref, g_ref, wr_ref, br_ref, a_ref, meta_ref, cnt_ref, carry_ref):
    tm = h_ref.shape[0]

    @pl.when(pl.program_id(0) == 0)
    def _():
        carry_ref[...] = jnp.zeros_like(carry_ref)

    a = _rms(h_ref[...], g_ref[...])
    a_ref[...] = a.reshape(a_ref.shape)
    logits = jnp.dot(a, wr_ref[...], precision=HIGHEST, preferred_element_type=F32) + br_ref[...]
    lane = lax.broadcasted_iota(jnp.int32, (tm, LANES), 1)
    lane_f = lane.astype(F32)
    neg = -jnp.inf

    gl = jnp.where(lane < N_GROUPS, logits, neg)
    gmax = jnp.max(gl, axis=1, keepdims=True)
    gidx = jnp.min(jnp.where(gl == gmax, lane_f, float(LANES)), axis=1, keepdims=True)
    p_group = 1.0 / jnp.sum(jnp.exp(gl - gmax), axis=1, keepdims=True)
    lo = float(N_GROUPS) + gidx * float(EXPERTS_PER_GROUP)
    el = jnp.where((lane_f >= lo) & (lane_f < lo + float(EXPERTS_PER_GROUP)), logits, neg)
    v1 = jnp.max(el, axis=1, keepdims=True)
    i1 = jnp.min(jnp.where(el == v1, lane_f, float(LANES)), axis=1, keepdims=True)
    el2 = jnp.where(lane_f == i1, neg, el)
    v2 = jnp.max(el2, axis=1, keepdims=True)
    i2 = jnp.min(jnp.where(el2 == v2, lane_f, float(LANES)), axis=1, keepdims=True)
    t = jnp.exp(v2 - v1)
    g1 = p_group / (1.0 + t)
    g2 = p_group * t / (1.0 + t)

    hit1 = lane_f == i1
    hit2 = lane_f == i2
    onehot = jnp.where(hit1 | hit2, 1.0, 0.0)
    row = lax.broadcasted_iota(jnp.int32, (tm, tm), 0)
    col = lax.broadcasted_iota(jnp.int32, (tm, tm), 1)
    strict = jnp.where(col < row, 1.0, 0.0).astype(BF16)
    before = jnp.dot(strict, onehot.astype(BF16), preferred_element_type=F32) + carry_ref[0:1, :]
    r1 = jnp.sum(jnp.where(hit1, before, 0.0), axis=1, keepdims=True)
    r2 = jnp.sum(jnp.where(hit2, before, 0.0), axis=1, keepdims=True)
    carry_ref[...] = carry_ref[...] + jnp.sum(onehot, axis=0, keepdims=True)
    cnt_ref[...] = carry_ref[...]

    meta = jnp.zeros((tm, LANES), F32)
    for idx, val in enumerate((i1 - float(N_GROUPS), i2 - float(N_GROUPS), r1, r2, g1, g2)):
        meta = jnp.where(lane == idx, val, meta)
    meta_ref[...] = meta


def _router(h, g, wr, br, *, tm):
    m, d = h.shape
    slab = d // LANES
    return pl.pallas_call(
        _router_kernel,
        grid=(m // tm,),
        in_specs=[
            pl.BlockSpec((tm, d), lambda i: (i, 0)),
            pl.BlockSpec((1, d), lambda i: (0, 0)),
            pl.BlockSpec((d, LANES), lambda i: (0, 0)),
            pl.BlockSpec((1, LANES), lambda i: (0, 0)),
        ],
        out_specs=[
            pl.BlockSpec((tm, slab, LANES), lambda i: (i, 0, 0)),
            pl.BlockSpec((tm, LANES), lambda i: (i, 0)),
            pl.BlockSpec((8, LANES), lambda i: (0, 0)),
        ],
        out_shape=[jax.ShapeDtypeStruct((m, slab, LANES), F32), jax.ShapeDtypeStruct((m, LANES), F32),
                   jax.ShapeDtypeStruct((8, LANES), F32)],
        scratch_shapes=[pltpu.VMEM((8, LANES), F32)],
        compiler_params=_cparams(("arbitrary",)),
        name="router",
    )(h, g, wr, br)


def _slot_token(v):
    return jnp.maximum(v, 0) >> 1


def _weight_copies(w_hbms, e, wst_ref, wsem):
    copies = []
    for t, w_hbm in enumerate(w_hbms):
        rows_per = w_hbm.shape[1] // WEIGHT_CHUNKS
        for c in range(WEIGHT_CHUNKS):
            rows = pl.ds(c * rows_per, rows_per)
            copies.append(pltpu.make_async_copy(w_hbm.at[e, rows, :], wst_ref.at[t, rows, :], wsem.at[0]))
    return copies


def _load_expert_weights(i, be_ref, nx_ref, w_hbms, wst_ref, wb_refs, wsem):
    @pl.when(i == 0)
    def _():
        for cp in _weight_copies(w_hbms, be_ref[0], wst_ref, wsem):
            cp.start()

    @pl.when((i == 0) | (be_ref[i] != be_ref[jnp.maximum(i - 1, 0)]))
    def _():
        for cp in _weight_copies(w_hbms, be_ref[i], wst_ref, wsem):
            cp.wait()
        for t, wb_ref in enumerate(wb_refs):
            wb_ref[...] = wst_ref[t].astype(BF16)

        @pl.when(nx_ref[i] >= 0)
        def _():
            for cp in _weight_copies(w_hbms, nx_ref[i], wst_ref, wsem):
                cp.start()


def _expert_a_kernel(slot_ref, be_ref, nx_ref, nb_ref, a_hbm, wg_hbm, wu_hbm, hb_ref, xg0, xg1, wst_ref, wgb_ref,
                     wub_ref, sem, wsem):
    i = pl.program_id(0)
    nb = nb_ref[0]
    rb, de = hb_ref.shape
    bufs = (xg0, xg1)

    def row_copy(blk, r, buf, s):
        return pltpu.make_async_copy(a_hbm.at[_slot_token(slot_ref[blk * rb + r])], buf.at[r], sem.at[s])

    def wait_block(buf, s):
        pltpu.make_async_copy(a_hbm.at[pl.ds(0, rb)], buf, sem.at[s]).wait()

    @pl.when(i == 0)
    def _():
        def body(r, carry):
            row_copy(0, r, xg0, 0).start()
            return carry
        lax.fori_loop(0, rb, body, 0)

    for s in (0, 1):
        @pl.when((i < nb) & (lax.rem(i, 2) == s))
        def _():
            _load_expert_weights(i, be_ref, nx_ref, (wg_hbm, wu_hbm), wst_ref, (wgb_ref, wub_ref), wsem)
            wait_block(bufs[s], s)
            for r in range(rb):
                row_copy(i + 1, r, bufs[1 - s], 1 - s).start()
            x = bufs[s][...].reshape(rb, -1).astype(BF16)
            hg = jnp.dot(x, wgb_ref[...], preferred_element_type=F32)
            hu = jnp.dot(x, wub_ref[...], preferred_element_type=F32)
            hb_ref[...] = (hg * _sigmoid(hg) * hu).astype(hb_ref.dtype)

        @pl.when((i == nb) & (lax.rem(i, 2) == s))
        def _():
            wait_block(bufs[s], s)

    @pl.when(i >= nb)
    def _():
        hb_ref[...] = jnp.zeros_like(hb_ref)


def _expert_a(slot, block_e, next_e, n_blk, a, wg, wu, *, rb):
    n_steps = slot.shape[0] // rb
    slab = a.shape[1]
    d = slab * LANES
    de = wg.shape[2]
    grid_spec = pltpu.PrefetchScalarGridSpec(
        num_scalar_prefetch=4,
        grid=(n_steps,),
        in_specs=[pl.BlockSpec(memory_space=pl.ANY)] * 3,
        out_specs=pl.BlockSpec((rb, de), lambda i, sl, be, nx, nb: (i, 0)),
        scratch_shapes=[pltpu.VMEM((rb, slab, LANES), F32), pltpu.VMEM((rb, slab, LANES), F32),
                        pltpu.VMEM((2, d, de), F32), pltpu.VMEM((d, de), BF16), pltpu.VMEM((d, de), BF16),
                        pltpu.SemaphoreType.DMA((2,)), pltpu.SemaphoreType.DMA((1,))],
    )
    return pl.pallas_call(
        _expert_a_kernel,
        grid_spec=grid_spec,
        out_shape=jax.ShapeDtypeStruct((n_steps * rb, de), BF16),
        compiler_params=_cparams(("arbitrary",)),
        name="expert_up",
    )(slot, block_e, next_e, n_blk, a, wg, wu)


def _expert_b_kernel(slot_ref, be_ref, nx_ref, nb_ref, hb_ref, wd_hbm, y_hbm, wst_ref, wdb_ref, ys0, ys1, sem, wsem,
                     *, n_tok):
    i = pl.program_id(0)
    n_steps = pl.num_programs(0)
    nb = nb_ref[0]
    rb = hb_ref.shape[0]
    bufs = (ys0, ys1)

    def wait_block(buf, s):
        pltpu.make_async_copy(buf, y_hbm.at[pl.ds(0, rb)], sem.at[s]).wait()

    @pl.when(i == 0)
    def _():
        ys0[...] = jnp.zeros_like(ys0)
        for s in (0, 1):
            spare = pltpu.make_async_copy(ys0, y_hbm.at[pl.ds(2 * n_tok + s * rb, rb)], sem.at[0])
            spare.start()
            spare.wait()

    for s in (0, 1):
        @pl.when((i < nb) & (lax.rem(i, 2) == s))
        def _():
            _load_expert_weights(i, be_ref, nx_ref, (wd_hbm,), wst_ref, (wdb_ref,), wsem)

            @pl.when(i >= 2)
            def _():
                wait_block(bufs[s], s)

            y = jnp.dot(hb_ref[...], wdb_ref[...], preferred_element_type=F32)
            bufs[s][...] = y.reshape(bufs[s].shape)
            for r in range(rb):
                v = slot_ref[i * rb + r]
                dst = jnp.where(v >= 0, (v & 1) * n_tok + (v >> 1), 2 * n_tok + s * rb + r)
                pltpu.make_async_copy(bufs[s].at[r], y_hbm.at[dst], sem.at[s]).start()

    @pl.when(i == n_steps - 1)
    def _():
        for s in (0, 1):
            @pl.when(((nb >= 1) & (lax.rem(nb - 1, 2) == s)) | ((nb >= 2) & (lax.rem(nb, 2) == s)))
            def _():
                wait_block(bufs[s], s)


def _expert_b(slot, block_e, next_e, n_blk, hb, wd, *, rb, n_tok):
    n_steps = hb.shape[0] // rb
    de = hb.shape[1]
    d = wd.shape[2]
    slab = d // LANES
    grid_spec = pltpu.PrefetchScalarGridSpec(
        num_scalar_prefetch=4,
        grid=(n_steps,),
        in_specs=[
            pl.BlockSpec((rb, de), lambda i, sl, be, nx, nb: (i, 0)),
            pl.BlockSpec(memory_space=pl.ANY),
        ],
        out_specs=pl.BlockSpec(memory_space=pl.ANY),
        scratch_shapes=[pltpu.VMEM((1, de, d), F32), pltpu.VMEM((de, d), BF16), pltpu.VMEM((rb, slab, LANES), F32),
                        pltpu.VMEM((rb, slab, LANES), F32), pltpu.SemaphoreType.DMA((2,)),
                        pltpu.SemaphoreType.DMA((1,))],
    )
    return pl.pallas_call(
        functools.partial(_expert_b_kernel, n_tok=n_tok),
        grid_spec=grid_spec,
        out_shape=jax.ShapeDtypeStruct((2 * n_tok + 2 * rb, slab, LANES), F32),
        compiler_params=_cparams(("arbitrary",)),
        name="expert_down",
    )(slot, block_e, next_e, n_blk, hb, wd)


def _combine_kernel(y0_ref, y1_ref, h_ref, meta_ref, g_ref, o_ref):
    tm, d = h_ref.shape
    meta = meta_ref[...]
    out = h_ref[...] + meta[:, 4:5] * y0_ref[...].reshape(tm, d) + meta[:, 5:6] * y1_ref[...].reshape(tm, d)
    o_ref[...] = _rms(out, g_ref[...])


def _combine(y, h, meta, g, *, tm):
    m, d = h.shape
    slab = d // LANES
    nt = m // tm
    return pl.pallas_call(
        _combine_kernel,
        grid=(nt,),
        in_specs=[
            pl.BlockSpec((tm, slab, LANES), lambda i: (i, 0, 0)),
            pl.BlockSpec((tm, slab, LANES), lambda i: (nt + i, 0, 0)),
            pl.BlockSpec((tm, d), lambda i: (i, 0)),
            pl.BlockSpec((tm, LANES), lambda i: (i, 0)),
            pl.BlockSpec((1, d), lambda i: (0, 0)),
        ],
        out_specs=pl.BlockSpec((tm, d), lambda i: (i, 0)),
        out_shape=jax.ShapeDtypeStruct((m, d), F32),
        compiler_params=_cparams(("parallel",)),
        name="combine",
    )(y, y, h, meta, g)


def kernel(x, mem, norm_mix_g, w_in, w_gla_alpha_up, b_gla_alpha, gla_out_norm_g, hgrn_lb_logits, hgrn_out_norm_g, w_mix_out, norm_xattn_g, norm_mem_g, w_xattn_q, w_xattn_kv, w_xattn_out, norm_ffn_g, w_router_group, b_router_group, w_router_expert, b_router_expert, w_expert_gate, w_expert_up, w_expert_down, norm_final_g):
    batch, seq, d = x.shape
    n_mem = mem.shape[1]
    m = batch * seq
    depth = norm_mix_g.shape[0]
    h = x.reshape(m, d)
    lb_all = jnp.cumsum(jax.nn.softmax(hgrn_lb_logits.astype(F32), axis=0), axis=0)
    gla_cols = 2 * GLA_HEADS * GLA_DK + 2 * GLA_HEADS * GLA_DV
    lr_rank = w_gla_alpha_up.shape[1]

    for l in range(depth):
        w_t = jnp.swapaxes(w_in[l], 0, 1)
        w_lr_t = jnp.pad(w_t[gla_cols:gla_cols + lr_rank], ((0, LANES - lr_rank), (0, 0))).astype(BF16)
        w_up = jnp.pad(w_gla_alpha_up[l], ((0, LANES - lr_rank), (0, 0))).astype(BF16)
        a_mix, la = _norm_gate(h, norm_mix_g[l][None, :], w_lr_t, w_up, b_gla_alpha[l][None, :], tm=min(m, 256))
        proj = _inproj(a_mix, w_t, tm=min(m, 4096), tn=512, skip_from=gla_cols, skip=lr_rank)
        lb = lb_all[l]
        lbc = jnp.zeros((8, lb.shape[0]), F32).at[0].set(jnp.log(lb)).at[1].set(jnp.log1p(-lb)).at[2].set(1.0 - lb)
        o_gla = _gla(proj, la, gla_out_norm_g[l][None, :], batch=batch, seq=seq, tb=min(seq, 512), hp=4)
        o_h = _hgrn(proj, lbc, hgrn_out_norm_g[l][None, :], batch=batch, seq=seq, tb=min(seq, 1024), hp=4)
        h = _mm_res([o_gla, o_h], w_mix_out[l], h, tm=min(m, 512), tn=d, name="mix_out")

        kv = _norm_mm(mem.reshape(batch * n_mem, d), norm_mem_g[l][None, :], w_xattn_kv[l],
                      tm=batch * n_mem, tn=1024, out_dtype=BF16, name="mem_kv")
        q = _norm_mm(h, norm_xattn_g[l][None, :], w_xattn_q[l], tm=min(m, 512), tn=d, out_dtype=BF16,
                     name="xattn_q")
        o = _xattn(q, kv, batch=batch, seq=seq, n_mem=n_mem, tq=min(seq, 512))
        h = _mm_res([o], w_xattn_out[l], h, tm=min(m, 512), tn=d, name="xattn_out")

        wr = jnp.pad(jnp.concatenate([w_router_group[l], w_router_expert[l]], axis=1),
                     ((0, 0), (0, LANES - N_GROUPS - N_EXPERTS)))
        br = jnp.pad(jnp.concatenate([b_router_group[l], b_router_expert[l]]), (0, LANES - N_GROUPS - N_EXPERTS))
        a, meta, cnt = _router(h, norm_ffn_g[l][None, :], wr, br[None, :], tm=min(m, 512))

        e_idx = meta[:, 0:2].astype(jnp.int32)
        rank = meta[:, 2:4].astype(jnp.int32)
        counts = cnt[0, N_GROUPS:N_GROUPS + N_EXPERTS].astype(jnp.int32)
        padded = ((counts + EXPERT_BLOCK - 1) // EXPERT_BLOCK) * EXPERT_BLOCK
        pad_end = jnp.cumsum(padded)
        dest = (pad_end - padded)[e_idx] + rank
        n_rows = 2 * m + N_EXPERTS * EXPERT_BLOCK
        n_blocks = n_rows // EXPERT_BLOCK
        n_blk = (pad_end[-1:] // EXPERT_BLOCK).astype(jnp.int32)
        blk_first = jnp.arange(n_blocks + 1, dtype=jnp.int32) * EXPERT_BLOCK
        block_e = jnp.minimum(jnp.sum((pad_end[None, :] <= blk_first[:, None]).astype(jnp.int32), axis=1),
                              N_EXPERTS - 1)
        assign = 2 * jnp.arange(m, dtype=jnp.int32)[:, None] + jnp.arange(2, dtype=jnp.int32)[None, :]
        slot = jnp.full((n_rows + EXPERT_BLOCK,), -1, jnp.int32).at[dest.reshape(-1)].set(assign.reshape(-1))

        after = (pad_end // EXPERT_BLOCK)[block_e]
        next_e = jnp.where(after < n_blk[0], block_e[jnp.minimum(after, n_blocks)], -1).astype(jnp.int32)

        hb = _expert_a(slot, block_e, next_e, n_blk, a, w_expert_gate[l], w_expert_up[l], rb=EXPERT_BLOCK)
        y = _expert_b(slot, block_e, next_e, n_blk, hb, w_expert_down[l], rb=EXPERT_BLOCK, n_tok=m)
        last = l == depth - 1
        g_fin = norm_final_g[None, :] if last else jnp.ones((1, d), F32)
        h = _combine(y, h, meta, g_fin, tm=min(m, 256))
        assert last, "the combine kernel fuses the final rmsnorm; deeper stacks need an un-normalised variant"

    return h.reshape(batch, seq, d)
```

```python
import functools

import jax
import jax.numpy as jnp
from jax import lax
from jax.experimental import pallas as pl
from jax.experimental.pallas import tpu as pltpu

F32 = jnp.float32
BF16 = jnp.bfloat16

EPS = 1e-6
CHUNK = 64
LANES = 128
GLA_HEADS, GLA_DK, GLA_DV = 4, 128, 256
HGRN_HEADS, HGRN_DH = 8, 128
XA_HEADS, XA_DH = 4, 512
N_GROUPS, EXPERTS_PER_GROUP, N_EXPERTS = 4, 8, 32
EXPERT_BLOCK = 256
WEIGHT_CHUNKS = 4
WEIGHT_DMA_PRIORITY = 1
VMEM_LIMIT = 56 * 1024 * 1024


def _cparams(sem, vmem=VMEM_LIMIT):
    return pltpu.CompilerParams(dimension_semantics=sem, vmem_limit_bytes=vmem)


def _log1pexp_neg(t):
    return jnp.log(1.0 + jnp.exp(-t))


def _log_sigmoid(z):
    return jnp.minimum(z, 0.0) - _log1pexp_neg(jnp.abs(z))


def _sigmoid(z):
    return 1.0 / (1.0 + jnp.exp(-z))


def _rms(x, g):
    return x * lax.rsqrt(jnp.mean(x * x, axis=-1, keepdims=True) + EPS) * g


def _nt_dot(x, y):
    return lax.dot_general(x, y, (((1,), (1,)), ((), ())), preferred_element_type=F32)


NORM_ROWS = 256


def _for_row_chunks(n_rows, fn):
    step = min(NORM_ROWS, n_rows)

    def body(ci, carry):
        fn(pl.ds(pl.multiple_of(ci * step, step), step))
        return carry

    lax.fori_loop(0, n_rows // step, body, 0)


def _norm_gate_kernel(x_ref, g_ref, wlr_ref, wup_ref, bal_ref, a_ref, la_ref):
    a = _rms(x_ref[...], g_ref[...]).astype(BF16)
    a_ref[...] = a
    lr = _nt_dot(a, wlr_ref[...])
    z = jnp.dot(lr.astype(BF16), wup_ref[...], preferred_element_type=F32) + bal_ref[...]
    la_ref[...] = _log_sigmoid(z) * (1.0 / 16.0)


def _norm_gate(x, g, w_lr_t, w_up, b_al, *, tm):
    m, d = x.shape
    nk = w_up.shape[1]
    return pl.pallas_call(
        _norm_gate_kernel,
        grid=(m // tm,),
        in_specs=[
            pl.BlockSpec((tm, d), lambda i: (i, 0)),
            pl.BlockSpec((1, d), lambda i: (0, 0)),
            pl.BlockSpec((LANES, d), lambda i: (0, 0)),
            pl.BlockSpec((LANES, nk), lambda i: (0, 0)),
            pl.BlockSpec((1, nk), lambda i: (0, 0)),
        ],
        out_specs=[pl.BlockSpec((tm, d), lambda i: (i, 0)), pl.BlockSpec((tm, nk), lambda i: (i, 0))],
        out_shape=[jax.ShapeDtypeStruct((m, d), BF16), jax.ShapeDtypeStruct((m, nk), F32)],
        compiler_params=_cparams(("parallel",)),
        name="norm_gate",
    )(x, g, w_lr_t, w_up, b_al)


def _inproj_kernel(a_ref, w_ref, proj_ref):
    proj_ref[...] = _nt_dot(a_ref[...], w_ref[...].astype(BF16))


def _inproj(a, w_t, *, tm, tn, skip_from, skip):
    m, d = a.shape
    n = w_t.shape[0] - skip
    first_after = skip_from // tn

    def w_rows(i, j):
        return (pl.multiple_of(j * tn + jnp.where(j >= first_after, skip, 0), 8), 0)

    return pl.pallas_call(
        _inproj_kernel,
        grid=(m // tm, n // tn),
        in_specs=[
            pl.BlockSpec((tm, d), lambda i, j: (i, 0), pipeline_mode=pl.Buffered(1)),
            pl.BlockSpec((pl.Element(tn), pl.Element(d)), w_rows),
        ],
        out_specs=pl.BlockSpec((tm, tn), lambda i, j: (i, j)),
        out_shape=jax.ShapeDtypeStruct((m, n), F32),
        compiler_params=_cparams(("parallel", "arbitrary")),
        name="inproj",
    )(a, w_t)


def _weight_spec(k, n, tn):
    if tn == n:
        return pl.BlockSpec((k, n), lambda i, j: (0, 0), pipeline_mode=pl.Buffered(1))
    return pl.BlockSpec((k, tn), lambda i, j: (0, j))


def _cast_weight(w_ref, wb_ref, resident):
    if resident:
        @pl.when((pl.program_id(0) == 0) & (pl.program_id(1) == 0))
        def _():
            wb_ref[...] = w_ref[...].astype(BF16)
    else:
        wb_ref[...] = w_ref[...].astype(BF16)


def _norm_mm_kernel(x_ref, g_ref, w_ref, o_ref, a_ref, wb_ref, *, resident):
    @pl.when(pl.program_id(1) == 0)
    def _():
        def rows_fn(rows):
            a_ref[rows, :] = _rms(x_ref[rows, :], g_ref[...]).astype(BF16)

        _for_row_chunks(x_ref.shape[0], rows_fn)

    _cast_weight(w_ref, wb_ref, resident)
    o_ref[...] = jnp.dot(a_ref[...], wb_ref[...], preferred_element_type=F32).astype(o_ref.dtype)


def _norm_mm(x, g, w, *, tm, tn, out_dtype, name):
    m, d = x.shape
    n = w.shape[1]
    return pl.pallas_call(
        functools.partial(_norm_mm_kernel, resident=tn == n),
        grid=(m // tm, n // tn),
        in_specs=[
            pl.BlockSpec((tm, d), lambda i, j: (i, 0)),
            pl.BlockSpec((1, d), lambda i, j: (0, 0)),
            _weight_spec(d, n, tn),
        ],
        out_specs=pl.BlockSpec((tm, tn), lambda i, j: (i, j)),
        out_shape=jax.ShapeDtypeStruct((m, n), out_dtype),
        scratch_shapes=[pltpu.VMEM((tm, d), BF16), pltpu.VMEM((d, tn), BF16)],
        compiler_params=_cparams(("arbitrary", "arbitrary")),
        name=name,
    )(x, g, w)


def _mm_res_kernel(*refs, n_lhs, resident):
    lhs = refs[:n_lhs]
    w_ref, res_ref, o_ref, wb_ref = refs[n_lhs:]
    _cast_weight(w_ref, wb_ref, resident)
    acc = res_ref[...]
    k0 = 0
    for l_ref in lhs:
        kp = l_ref.shape[1]
        acc = acc + jnp.dot(l_ref[...], wb_ref[k0:k0 + kp, :], preferred_element_type=F32)
        k0 += kp
    o_ref[...] = acc


def _mm_res(lhs_parts, w, res, *, tm, tn, name):
    m, n = res.shape
    k = w.shape[0]
    n_lhs = len(lhs_parts)
    in_specs = [pl.BlockSpec((tm, p.shape[1]), lambda i, j: (i, 0)) for p in lhs_parts]
    in_specs += [_weight_spec(k, n, tn), pl.BlockSpec((tm, tn), lambda i, j: (i, j))]
    return pl.pallas_call(
        functools.partial(_mm_res_kernel, n_lhs=n_lhs, resident=tn == n),
        grid=(m // tm, n // tn),
        in_specs=in_specs,
        out_specs=pl.BlockSpec((tm, tn), lambda i, j: (i, j)),
        out_shape=jax.ShapeDtypeStruct((m, n), F32),
        scratch_shapes=[pltpu.VMEM((k, tn), BF16)],
        compiler_params=_cparams(("arbitrary", "arbitrary")),
        name=name,
    )(*lhs_parts, w, res)


LEVELS = (32, 16, 8, 4, 2, 1)
LOG2E = 1.4426950408889634


def _split3_bf16(x):
    def top(v):
        bits = lax.bitcast_convert_type(v, jnp.uint32) & jnp.uint32(0xFFFF0000)
        return lax.bitcast_convert_type(bits, F32)
    hi = top(x)
    r1 = x - hi
    mid = top(r1)
    lo = r1 - mid
    return hi.astype(BF16), mid.astype(BF16), lo.astype(BF16)


def _mix_chunks(qs, ks, vs, las, st_refs):
    heads = range(len(qs))
    c, dk = qs[0].shape
    row = lax.broadcasted_iota(jnp.int32, (c, c), 0)
    col = lax.broadcasted_iota(jnp.int32, (c, c), 1)
    rowk = lax.broadcasted_iota(jnp.int32, (c, dk), 0)
    xor = jnp.bitwise_xor(row, col)

    las = [la * LOG2E for la in las]

    tri = jnp.where(col <= row, 1.0, 0.0).astype(BF16)
    b3 = [jnp.dot(tri, jnp.concatenate(_split3_bf16(la), axis=1), preferred_element_type=F32) for la in las]
    bs = [(t[:, :dk] + t[:, dk:2 * dk]) + t[:, 2 * dk:] for t in b3]

    def neg_dist(w, b, la):
        if w >= 4:
            parts = [jnp.broadcast_to(b[base + w - 1:base + w, :], (2 * w, dk)) for base in range(0, c, 2 * w)]
            m = parts[0] if len(parts) == 1 else jnp.concatenate(parts, axis=0)
            return -jnp.abs(b - m)
        if w == 2:
            r4 = jnp.bitwise_and(rowk, 3)
            nxt = pltpu.roll(la, c - 1, 0)
            prv = pltpu.roll(la, 1, 0)
            return jnp.where(r4 == 0, nxt, jnp.where(r4 == 1, 0.0, jnp.where(r4 == 2, la, la + prv)))
        return jnp.where(jnp.bitwise_and(rowk, 1) == 1, la, 0.0)

    acc = [_nt_dot(qs[h].astype(BF16), ks[h].astype(BF16)) for h in heads]
    for w in reversed(LEVELS):
        upper = jnp.bitwise_and(rowk, w) != 0
        xs = [(jnp.where(upper, qs[h], ks[h]) * jnp.exp2(neg_dist(w, bs[h], las[h]))).astype(BF16) for h in heads]
        gs = [_nt_dot(x, x) for x in xs]
        acc = [jnp.where(xor >= w, gs[h], acc[h]) for h in heads]
    ab = [jnp.where(col <= row, a, 0.0).astype(BF16) for a in acc]

    sts = [st_refs[h][...] for h in heads]
    qx = [(qs[h] * jnp.exp2(bs[h])).astype(BF16) for h in heads]
    b_last = [b[c - 1:c, :] for b in bs]
    kx = [(ks[h] * jnp.exp2(b_last[h] - bs[h])).astype(BF16) for h in heads]
    vb = [v.astype(BF16) for v in vs]
    outs = [jnp.dot(ab[h], vb[h], preferred_element_type=F32) + _nt_dot(qx[h], sts[h].astype(BF16)) for h in heads]
    for h in heads:
        st_refs[h][...] = jnp.exp2(b_last[h]) * sts[h] + lax.dot_general(
            vb[h], kx[h], (((0,), (0,)), ((), ())), preferred_element_type=F32)
    return outs


def _gated_norm(o, g, gn):
    return _rms(o, gn) * (g * _sigmoid(g))


def _gla_kernel(q_ref, k_ref, v_ref, g_ref, la_ref, gn_ref, o_ref, st_ref, *, n_chunk):
    @pl.when(pl.program_id(2) == 0)
    def _():
        st_ref[...] = jnp.zeros_like(st_ref)

    dk, dv = GLA_DK, GLA_DV
    heads = range(st_ref.shape[0])

    def body(ci, carry):
        rows = pl.ds(pl.multiple_of(ci * CHUNK, CHUNK), CHUNK)
        kc = [slice(h * dk, (h + 1) * dk) for h in heads]
        vc = [slice(h * dv, (h + 1) * dv) for h in heads]
        outs = _mix_chunks([q_ref[rows, kc[h]] * (dk ** -0.5) for h in heads], [k_ref[rows, kc[h]] for h in heads],
                           [v_ref[rows, vc[h]] for h in heads], [la_ref[rows, kc[h]] for h in heads],
                           [st_ref.at[h] for h in heads])
        for h in heads:
            o_ref[rows, vc[h]] = _gated_norm(outs[h], g_ref[rows, vc[h]], gn_ref[...]).astype(o_ref.dtype)
        return carry

    lax.fori_loop(0, n_chunk, body, 0)


def _gla(proj, la, gn, *, batch, seq, tb, hp):
    m = proj.shape[0]
    nt = seq // tb
    wk, wv = hp * GLA_DK, hp * GLA_DV
    n_grp = GLA_HEADS // hp
    spec = lambda width, first: pl.BlockSpec((tb, width), lambda b, p, t: (b * nt + t, first + p))
    return pl.pallas_call(
        functools.partial(_gla_kernel, n_chunk=tb // CHUNK),
        grid=(batch, n_grp, nt),
        in_specs=[spec(wk, 0), spec(wk, n_grp), spec(wv, n_grp), spec(wv, 2 * n_grp), spec(wk, 0),
                  pl.BlockSpec((1, GLA_DV), lambda b, p, t: (0, 0))],
        out_specs=spec(wv, 0),
        out_shape=jax.ShapeDtypeStruct((m, GLA_HEADS * GLA_DV), BF16),
        scratch_shapes=[pltpu.VMEM((hp, GLA_DV, GLA_DK), F32)],
        compiler_params=_cparams(("parallel", "parallel", "arbitrary")),
        name="gla",
    )(proj, proj, proj, proj, la, gn)


def _hgrn_kernel(q_ref, f_ref, i_ref, g_ref, lb_ref, gn_ref, o_ref, st_ref, *, n_chunk):
    @pl.when(pl.program_id(2) == 0)
    def _():
        st_ref[...] = jnp.zeros_like(st_ref)

    dh = HGRN_DH
    heads = range(st_ref.shape[0])

    def body(ci, carry):
        rows = pl.ds(pl.multiple_of(ci * CHUNK, CHUNK), CHUNK)
        hc = [slice(h * dh, (h + 1) * dh) for h in heads]
        qs, ks, las = [], [], []
        for h in heads:
            log_lb, log_1mlb, one_m_lb = lb_ref[0:1, hc[h]], lb_ref[1:2, hc[h]], lb_ref[2:3, hc[h]]
            hq = q_ref[rows, hc[h]]
            z = f_ref[rows, hc[h]]
            x1 = log_1mlb + _log_sigmoid(z)
            las.append(jnp.maximum(log_lb, x1) + _log1pexp_neg(jnp.abs(log_lb - x1)))
            qs.append(hq * _sigmoid(hq))
            ks.append(one_m_lb * _sigmoid(-z))
        outs = _mix_chunks(qs, ks, [i_ref[rows, hc[h]] for h in heads], las, [st_ref.at[h] for h in heads])
        for h in heads:
            o_ref[rows, hc[h]] = _gated_norm(outs[h], g_ref[rows, hc[h]], gn_ref[...]).astype(o_ref.dtype)
        return carry

    lax.fori_loop(0, n_chunk, body, 0)


def _hgrn(proj, lbc, gn, *, batch, seq, tb, hp):
    m = proj.shape[0]
    nt = seq // tb
    width = hp * HGRN_DH
    n_grp = HGRN_HEADS // hp
    first = 3072 // width
    spec = lambda seg: pl.BlockSpec((tb, width), lambda b, p, t: (b * nt + t, first + seg * n_grp + p))
    return pl.pallas_call(
        functools.partial(_hgrn_kernel, n_chunk=tb // CHUNK),
        grid=(batch, n_grp, nt),
        in_specs=[spec(0), spec(1), spec(2), spec(3),
                  pl.BlockSpec((8, width), lambda b, p, t: (0, p)),
                  pl.BlockSpec((1, HGRN_DH), lambda b, p, t: (0, 0))],
        out_specs=pl.BlockSpec((tb, width), lambda b, p, t: (b * nt + t, p)),
        out_shape=jax.ShapeDtypeStruct((m, HGRN_HEADS * HGRN_DH), BF16),
        scratch_shapes=[pltpu.VMEM((hp, HGRN_DH, HGRN_DH), F32)],
        compiler_params=_cparams(("parallel", "parallel", "arbitrary")),
        name="hgrn",
    )(proj, proj, proj, proj, lbc, gn)


def _xattn_kernel(q_ref, k_ref, v_ref, o_ref):
    for h in range(XA_HEADS):
        cols = slice(h * XA_DH, (h + 1) * XA_DH)
        s = lax.dot_general(q_ref[:, cols], k_ref[:, cols], (((1,), (1,)), ((), ())),
                            preferred_element_type=F32) * (XA_DH ** -0.5)
        p = jnp.exp(s - jnp.max(s, axis=-1, keepdims=True))
        p = p / jnp.sum(p, axis=-1, keepdims=True)
        o_ref[:, cols] = jnp.dot(p.astype(BF16), v_ref[:, cols], preferred_element_type=F32).astype(o_ref.dtype)


def _xattn(q, kv, *, batch, seq, n_mem, tq):
    m, d = q.shape
    nt = seq // tq
    return pl.pallas_call(
        _xattn_kernel,
        grid=(batch, nt),
        in_specs=[
            pl.BlockSpec((tq, d), lambda b, t: (b * nt + t, 0)),
            pl.BlockSpec((n_mem, d), lambda b, t: (b, 0)),
            pl.BlockSpec((n_mem, d), lambda b, t: (b, 1)),
        ],
        out_specs=pl.BlockSpec((tq, d), lambda b, t: (b * nt + t, 0)),
        out_shape=jax.ShapeDtypeStruct((m, d), BF16),
        compiler_params=_cparams(("parallel", "arbitrary")),
        name="xattn",
    )(q, kv, kv)


def _router_kernel(h_ref, g_ref, wr_ref, br_ref, a_ref, meta_ref, cnt_ref, carry_ref):
    tm = h_ref.shape[0]

    @pl.when(pl.program_id(0) == 0)
    def _():
        carry_ref[...] = jnp.zeros_like(carry_ref)

    a = _rms(h_ref[...], g_ref[...])
    a_ref[...] = a.reshape(a_ref.shape)
    a_hi = a.astype(BF16)
    a_lo = (a - a_hi.astype(F32)).astype(BF16)
    w = wr_ref[...]
    w_hi = w.astype(BF16)
    w_lo = (w - w_hi.astype(F32)).astype(BF16)
    logits = (jnp.dot(a_hi, w_hi, preferred_element_type=F32) + jnp.dot(a_lo, w_hi, preferred_element_type=F32)
              + jnp.dot(a_hi, w_lo, preferred_element_type=F32)) + br_ref[...]
    lane = lax.broadcasted_iota(jnp.int32, (tm, LANES), 1)
    lane_f = lane.astype(F32)
    neg = -jnp.inf

    gl = jnp.where(lane < N_GROUPS, logits, neg)
    gmax = jnp.max(gl, axis=1, keepdims=True)
    gidx = jnp.min(jnp.where(gl == gmax, lane_f, float(LANES)), axis=1, keepdims=True)
    p_group = 1.0 / jnp.sum(jnp.exp(gl - gmax), axis=1, keepdims=True)
    lo = float(N_GROUPS) + gidx * float(EXPERTS_PER_GROUP)
    el = jnp.where((lane_f >= lo) & (lane_f < lo + float(EXPERTS_PER_GROUP)), logits, neg)
    v1 = jnp.max(el, axis=1, keepdims=True)
    i1 = jnp.min(jnp.where(el == v1, lane_f, float(LANES)), axis=1, keepdims=True)
    el2 = jnp.where(lane_f == i1, neg, el)
    v2 = jnp.max(el2, axis=1, keepdims=True)
    i2 = jnp.min(jnp.where(el2 == v2, lane_f, float(LANES)), axis=1, keepdims=True)
    t = jnp.exp(v2 - v1)
    g1 = p_group / (1.0 + t)
    g2 = p_group * t / (1.0 + t)

    hit1 = lane_f == i1
    hit2 = lane_f == i2
    onehot = jnp.where(hit1 | hit2, 1.0, 0.0)
    row = lax.broadcasted_iota(jnp.int32, (tm, tm), 0)
    col = lax.broadcasted_iota(jnp.int32, (tm, tm), 1)
    strict = jnp.where(col < row, 1.0, 0.0).astype(BF16)
    before = jnp.dot(strict, onehot.astype(BF16), preferred_element_type=F32) + carry_ref[0:1, :]
    r1 = jnp.sum(jnp.where(hit1, before, 0.0), axis=1, keepdims=True)
    r2 = jnp.sum(jnp.where(hit2, before, 0.0), axis=1, keepdims=True)
    carry_ref[...] = carry_ref[...] + jnp.sum(onehot, axis=0, keepdims=True)
    cnt_ref[...] = carry_ref[...]

    meta = jnp.zeros((tm, LANES), F32)
    for idx, val in enumerate((i1 - float(N_GROUPS), i2 - float(N_GROUPS), r1, r2, g1, g2)):
        meta = jnp.where(lane == idx, val, meta)
    meta_ref[...] = meta


def _router(h, g, wr, br, *, tm):
    m, d = h.shape
    slab = d // LANES
    return pl.pallas_call(
        _router_kernel,
        grid=(m // tm,),
        in_specs=[
            pl.BlockSpec((tm, d), lambda i: (i, 0)),
            pl.BlockSpec((1, d), lambda i: (0, 0)),
            pl.BlockSpec((d, LANES), lambda i: (0, 0)),
            pl.BlockSpec((1, LANES), lambda i: (0, 0)),
        ],
        out_specs=[
            pl.BlockSpec((tm, slab, LANES), lambda i: (i, 0, 0)),
            pl.BlockSpec((tm, LANES), lambda i: (i, 0)),
            pl.BlockSpec((8, LANES), lambda i: (0, 0)),
        ],
        out_shape=[jax.ShapeDtypeStruct((m, slab, LANES), F32), jax.ShapeDtypeStruct((m, LANES), F32),
                   jax.ShapeDtypeStruct((8, LANES), F32)],
        scratch_shapes=[pltpu.VMEM((8, LANES), F32)],
        compiler_params=_cparams(("arbitrary",)),
        name="router",
    )(h, g, wr, br)


def _slot_token(v):
    return jnp.maximum(v, 0) >> 1


def _weight_copies(w_hbms, e, wst_ref, wsem):
    copies = []
    for t, w_hbm in enumerate(w_hbms):
        rows_per = w_hbm.shape[1] // WEIGHT_CHUNKS
        for c in range(WEIGHT_CHUNKS):
            rows = pl.ds(c * rows_per, rows_per)
            copies.append(pltpu.make_async_copy(w_hbm.at[e, rows, :], wst_ref.at[t, rows, :], wsem.at[0]))
    return copies


def _load_expert_weights(i, be_ref, nx_ref, w_hbms, wst_ref, wb_refs, wsem):
    @pl.when(i == 0)
    def _():
        for cp in _weight_copies(w_hbms, be_ref[0], wst_ref, wsem):
            cp.start(priority=WEIGHT_DMA_PRIORITY)

    @pl.when((i == 0) | (be_ref[i] != be_ref[jnp.maximum(i - 1, 0)]))
    def _():
        for cp in _weight_copies(w_hbms, be_ref[i], wst_ref, wsem):
            cp.wait()
        for t, wb_ref in enumerate(wb_refs):
            wb_ref[...] = wst_ref[t].astype(BF16)

        @pl.when(nx_ref[i] >= 0)
        def _():
            for cp in _weight_copies(w_hbms, nx_ref[i], wst_ref, wsem):
                cp.start(priority=WEIGHT_DMA_PRIORITY)


def _expert_a_kernel(slot_ref, be_ref, nx_ref, nb_ref, a_hbm, wg_hbm, wu_hbm, hb_ref, xg0, xg1, wst_ref, wgb_ref,
                     wub_ref, sem, wsem):
    i = pl.program_id(0)
    nb = nb_ref[0]
    rb, de = hb_ref.shape
    bufs = (xg0, xg1)

    def row_copy(blk, r, buf, s):
        return pltpu.make_async_copy(a_hbm.at[_slot_token(slot_ref[blk * rb + r])], buf.at[r], sem.at[s])

    def wait_block(buf, s):
        pltpu.make_async_copy(a_hbm.at[pl.ds(0, rb)], buf, sem.at[s]).wait()

    @pl.when(i == 0)
    def _():
        def body(r, carry):
            row_copy(0, r, xg0, 0).start()
            return carry
        lax.fori_loop(0, rb, body, 0)

    for s in (0, 1):
        @pl.when((i < nb) & (lax.rem(i, 2) == s))
        def _():
            _load_expert_weights(i, be_ref, nx_ref, (wg_hbm, wu_hbm), wst_ref, (wgb_ref, wub_ref), wsem)
            wait_block(bufs[s], s)
            for r in range(rb):
                row_copy(i + 1, r, bufs[1 - s], 1 - s).start()
            x = bufs[s][...].reshape(rb, -1).astype(BF16)
            hg = jnp.dot(x, wgb_ref[...], preferred_element_type=F32)
            hu = jnp.dot(x, wub_ref[...], preferred_element_type=F32)
            hb_ref[...] = (hg * _sigmoid(hg) * hu).astype(hb_ref.dtype)

        @pl.when((i == nb) & (lax.rem(i, 2) == s))
        def _():
            wait_block(bufs[s], s)

    @pl.when(i >= nb)
    def _():
        hb_ref[...] = jnp.zeros_like(hb_ref)


def _expert_a(slot, block_e, next_e, n_blk, a, wg, wu, *, rb):
    n_steps = slot.shape[0] // rb
    slab = a.shape[1]
    d = slab * LANES
    de = wg.shape[2]
    grid_spec = pltpu.PrefetchScalarGridSpec(
        num_scalar_prefetch=4,
        grid=(n_steps,),
        in_specs=[pl.BlockSpec(memory_space=pl.ANY)] * 3,
        out_specs=pl.BlockSpec((rb, de), lambda i, sl, be, nx, nb: (i, 0)),
        scratch_shapes=[pltpu.VMEM((rb, slab, LANES), F32), pltpu.VMEM((rb, slab, LANES), F32),
                        pltpu.VMEM((2, d, de), F32), pltpu.VMEM((d, de), BF16), pltpu.VMEM((d, de), BF16),
                        pltpu.SemaphoreType.DMA((2,)), pltpu.SemaphoreType.DMA((1,))],
    )
    return pl.pallas_call(
        _expert_a_kernel,
        grid_spec=grid_spec,
        out_shape=jax.ShapeDtypeStruct((n_steps * rb, de), BF16),
        compiler_params=_cparams(("arbitrary",)),
        name="expert_up",
    )(slot, block_e, next_e, n_blk, a, wg, wu)


def _expert_b_kernel(slot_ref, be_ref, nx_ref, nb_ref, hb_ref, wd_hbm, y_hbm, wst_ref, wdb_ref, ys0, ys1, sem, wsem,
                     *, n_tok):
    i = pl.program_id(0)
    n_steps = pl.num_programs(0)
    nb = nb_ref[0]
    rb = hb_ref.shape[0]
    bufs = (ys0, ys1)

    def wait_block(buf, s):
        pltpu.make_async_copy(buf, y_hbm.at[pl.ds(0, rb)], sem.at[s]).wait()

    @pl.when(i == 0)
    def _():
        ys0[...] = jnp.zeros_like(ys0)
        for s in (0, 1):
            spare = pltpu.make_async_copy(ys0, y_hbm.at[pl.ds(2 * n_tok + s * rb, rb)], sem.at[0])
            spare.start()
            spare.wait()

    for s in (0, 1):
        @pl.when((i < nb) & (lax.rem(i, 2) == s))
        def _():
            _load_expert_weights(i, be_ref, nx_ref, (wd_hbm,), wst_ref, (wdb_ref,), wsem)

            @pl.when(i >= 2)
            def _():
                wait_block(bufs[s], s)

            y = jnp.dot(hb_ref[...], wdb_ref[...], preferred_element_type=F32)
            bufs[s][...] = y.reshape(bufs[s].shape)
            for r in range(rb):
                v = slot_ref[i * rb + r]
                dst = jnp.where(v >= 0, (v & 1) * n_tok + (v >> 1), 2 * n_tok + s * rb + r)
                pltpu.make_async_copy(bufs[s].at[r], y_hbm.at[dst], sem.at[s]).start()

    @pl.when(i == n_steps - 1)
    def _():
        for s in (0, 1):
            @pl.when(((nb >= 1) & (lax.rem(nb - 1, 2) == s)) | ((nb >= 2) & (lax.rem(nb, 2) == s)))
            def _():
                wait_block(bufs[s], s)


def _expert_b(slot, block_e, next_e, n_blk, hb, wd, *, rb, n_tok):
    n_steps = hb.shape[0] // rb
    de = hb.shape[1]
    d = wd.shape[2]
    slab = d // LANES
    grid_spec = pltpu.PrefetchScalarGridSpec(
        num_scalar_prefetch=4,
        grid=(n_steps,),
        in_specs=[
            pl.BlockSpec((rb, de), lambda i, sl, be, nx, nb: (i, 0)),
            pl.BlockSpec(memory_space=pl.ANY),
        ],
        out_specs=pl.BlockSpec(memory_space=pl.ANY),
        scratch_shapes=[pltpu.VMEM((1, de, d), F32), pltpu.VMEM((de, d), BF16), pltpu.VMEM((rb, slab, LANES), F32),
                        pltpu.VMEM((rb, slab, LANES), F32), pltpu.SemaphoreType.DMA((2,)),
                        pltpu.SemaphoreType.DMA((1,))],
    )
    return pl.pallas_call(
        functools.partial(_expert_b_kernel, n_tok=n_tok),
        grid_spec=grid_spec,
        out_shape=jax.ShapeDtypeStruct((2 * n_tok + 2 * rb, slab, LANES), F32),
        compiler_params=_cparams(("arbitrary",)),
        name="expert_down",
    )(slot, block_e, next_e, n_blk, hb, wd)


def _combine_kernel(y0_ref, y1_ref, h_ref, meta_ref, g_ref, o_ref):
    tm, d = h_ref.shape
    meta = meta_ref[...]
    out = h_ref[...] + meta[:, 4:5] * y0_ref[...].reshape(tm, d) + meta[:, 5:6] * y1_ref[...].reshape(tm, d)
    o_ref[...] = _rms(out, g_ref[...])


def _combine(y, h, meta, g, *, tm):
    m, d = h.shape
    slab = d // LANES
    nt = m // tm
    return pl.pallas_call(
        _combine_kernel,
        grid=(nt,),
        in_specs=[
            pl.BlockSpec((tm, slab, LANES), lambda i: (i, 0, 0)),
            pl.BlockSpec((tm, slab, LANES), lambda i: (nt + i, 0, 0)),
            pl.BlockSpec((tm, d), lambda i: (i, 0)),
            pl.BlockSpec((tm, LANES), lambda i: (i, 0)),
            pl.BlockSpec((1, d), lambda i: (0, 0)),
        ],
        out_specs=pl.BlockSpec((tm, d), lambda i: (i, 0)),
        out_shape=jax.ShapeDtypeStruct((m, d), F32),
        compiler_params=_cparams(("parallel",)),
        name="combine",
    )(y, y, h, meta, g)


def kernel(x, mem, norm_mix_g, w_in, w_gla_alpha_up, b_gla_alpha, gla_out_norm_g, hgrn_lb_logits, hgrn_out_norm_g, w_mix_out, norm_xattn_g, norm_mem_g, w_xattn_q, w_xattn_kv, w_xattn_out, norm_ffn_g, w_router_group, b_router_group, w_router_expert, b_router_expert, w_expert_gate, w_expert_up, w_expert_down, norm_final_g):
    batch, seq, d = x.shape
    n_mem = mem.shape[1]
    m = batch * seq
    depth = norm_mix_g.shape[0]
    h = x.reshape(m, d)
    lb_all = jnp.cumsum(jax.nn.softmax(hgrn_lb_logits.astype(F32), axis=0), axis=0)
    gla_cols = 2 * GLA_HEADS * GLA_DK + 2 * GLA_HEADS * GLA_DV
    lr_rank = w_gla_alpha_up.shape[1]

    for l in range(depth):
        w_t = jnp.swapaxes(w_in[l], 0, 1)
        w_lr_t = jnp.pad(w_t[gla_cols:gla_cols + lr_rank], ((0, LANES - lr_rank), (0, 0))).astype(BF16)
        w_up = jnp.pad(w_gla_alpha_up[l], ((0, LANES - lr_rank), (0, 0))).astype(BF16)
        a_mix, la = _norm_gate(h, norm_mix_g[l][None, :], w_lr_t, w_up, b_gla_alpha[l][None, :], tm=min(m, 512))
        proj = _inproj(a_mix, w_t, tm=min(m, 4096), tn=512, skip_from=gla_cols, skip=lr_rank)
        lb = lb_all[l]
        lbc = jnp.zeros((8, lb.shape[0]), F32).at[0].set(jnp.log(lb)).at[1].set(jnp.log1p(-lb)).at[2].set(1.0 - lb)
        o_gla = _gla(proj, la, gla_out_norm_g[l][None, :], batch=batch, seq=seq, tb=min(seq, 512), hp=4)
        o_h = _hgrn(proj, lbc, hgrn_out_norm_g[l][None, :], batch=batch, seq=seq, tb=min(seq, 1024), hp=4)
        h = _mm_res([o_gla, o_h], w_mix_out[l], h, tm=min(m, 512), tn=d, name="mix_out")

        kv = _norm_mm(mem.reshape(batch * n_mem, d), norm_mem_g[l][None, :], w_xattn_kv[l],
                      tm=batch * n_mem, tn=1024, out_dtype=BF16, name="mem_kv")
        q = _norm_mm(h, norm_xattn_g[l][None, :], w_xattn_q[l], tm=min(m, 512), tn=d, out_dtype=BF16,
                     name="xattn_q")
        o = _xattn(q, kv, batch=batch, seq=seq, n_mem=n_mem, tq=min(seq, 512))
        h = _mm_res([o], w_xattn_out[l], h, tm=min(m, 512), tn=d, name="xattn_out")

        wr = jnp.pad(jnp.concatenate([w_router_group[l], w_router_expert[l]], axis=1),
                     ((0, 0), (0, LANES - N_GROUPS - N_EXPERTS)))
        br = jnp.pad(jnp.concatenate([b_router_group[l], b_router_expert[l]]), (0, LANES - N_GROUPS - N_EXPERTS))
        a, meta, cnt = _router(h, norm_ffn_g[l][None, :], wr, br[None, :], tm=min(m, 512))

        e_idx = meta[:, 0:2].astype(jnp.int32)
        rank = meta[:, 2:4].astype(jnp.int32)
        counts = cnt[0, N_GROUPS:N_GROUPS + N_EXPERTS].astype(jnp.int32)
        padded = ((counts + EXPERT_BLOCK - 1) // EXPERT_BLOCK) * EXPERT_BLOCK
        pad_end = jnp.cumsum(padded)
        dest = (pad_end - padded)[e_idx] + rank
        n_rows = 2 * m + N_EXPERTS * EXPERT_BLOCK
        n_blocks = n_rows // EXPERT_BLOCK
        n_blk = (pad_end[-1:] // EXPERT_BLOCK).astype(jnp.int32)
        blk_first = jnp.arange(n_blocks + 1, dtype=jnp.int32) * EXPERT_BLOCK
        block_e = jnp.minimum(jnp.sum((pad_end[None, :] <= blk_first[:, None]).astype(jnp.int32), axis=1),
                              N_EXPERTS - 1)
        assign = 2 * jnp.arange(m, dtype=jnp.int32)[:, None] + jnp.arange(2, dtype=jnp.int32)[None, :]
        slot = jnp.full((n_rows + EXPERT_BLOCK,), -1, jnp.int32).at[dest.reshape(-1)].set(assign.reshape(-1))

        after = (pad_end // EXPERT_BLOCK)[block_e]
        next_e = jnp.where(after < n_blk[0], block_e[jnp.minimum(after, n_blocks)], -1).astype(jnp.int32)

        hb = _expert_a(slot, block_e, next_e, n_blk, a, w_expert_gate[l], w_expert_up[l], rb=EXPERT_BLOCK)
        y = _expert_b(slot, block_e, next_e, n_blk, hb, w_expert_down[l], rb=EXPERT_BLOCK, n_tok=m)
        last = l == depth - 1
        g_fin = norm_final_g[None, :] if last else jnp.ones((1, d), F32)
        h = _combine(y, h, meta, g_fin, tm=min(m, 256))
        assert last, "the combine kernel fuses the final rmsnorm; deeper stacks need an un-normalised variant"

    return h.reshape(batch, seq, d)
```

```python
import functools

import jax
import jax.numpy as jnp
from jax import lax
from jax.experimental import pallas as pl
from jax.experimental.pallas import tpu as pltpu

F32 = jnp.float32
BF16 = jnp.bfloat16

EPS = 1e-6
CHUNK = 64
LANES = 128
GLA_HEADS, GLA_DK, GLA_DV = 4, 128, 256
HGRN_HEADS, HGRN_DH = 8, 128
XA_HEADS, XA_DH = 4, 512
N_GROUPS, EXPERTS_PER_GROUP, N_EXPERTS = 4, 8, 32
EXPERT_BLOCK = 256
WEIGHT_CHUNKS = 4
WEIGHT_DMA_PRIORITY = 1
VMEM_LIMIT = 56 * 1024 * 1024


def _cparams(sem, vmem=VMEM_LIMIT):
    return pltpu.CompilerParams(dimension_semantics=sem, vmem_limit_bytes=vmem)


def _log1pexp_neg(t):
    return jnp.log(1.0 + jnp.exp(-t))


def _log_sigmoid(z):
    return jnp.minimum(z, 0.0) - _log1pexp_neg(jnp.abs(z))


def _sigmoid(z):
    return 1.0 / (1.0 + jnp.exp(-z))


def _rms(x, g):
    return x * lax.rsqrt(jnp.mean(x * x, axis=-1, keepdims=True) + EPS) * g


def _nt_dot(x, y):
    return lax.dot_general(x, y, (((1,), (1,)), ((), ())), preferred_element_type=F32)


NORM_ROWS = 256


def _for_row_chunks(n_rows, fn):
    step = min(NORM_ROWS, n_rows)

    def body(ci, carry):
        fn(pl.ds(pl.multiple_of(ci * step, step), step))
        return carry

    lax.fori_loop(0, n_rows // step, body, 0)


def _norm_gate_kernel(x_ref, g_ref, wlr_ref, wup_ref, bal_ref, a_ref, la_ref):
    a = _rms(x_ref[...], g_ref[...]).astype(BF16)
    a_ref[...] = a
    lr = _nt_dot(a, wlr_ref[...])
    z = jnp.dot(lr.astype(BF16), wup_ref[...], preferred_element_type=F32) + bal_ref[...]
    la_ref[...] = _log_sigmoid(z) * (1.0 / 16.0)


def _norm_gate(x, g, w_lr_t, w_up, b_al, *, tm):
    m, d = x.shape
    nk = w_up.shape[1]
    return pl.pallas_call(
        _norm_gate_kernel,
        grid=(m // tm,),
        in_specs=[
            pl.BlockSpec((tm, d), lambda i: (i, 0)),
            pl.BlockSpec((1, d), lambda i: (0, 0)),
            pl.BlockSpec((LANES, d), lambda i: (0, 0)),
            pl.BlockSpec((LANES, nk), lambda i: (0, 0)),
            pl.BlockSpec((1, nk), lambda i: (0, 0)),
        ],
        out_specs=[pl.BlockSpec((tm, d), lambda i: (i, 0)), pl.BlockSpec((tm, nk), lambda i: (i, 0))],
        out_shape=[jax.ShapeDtypeStruct((m, d), BF16), jax.ShapeDtypeStruct((m, nk), F32)],
        compiler_params=_cparams(("parallel",)),
        name="norm_gate",
    )(x, g, w_lr_t, w_up, b_al)


def _inproj_kernel(a_ref, w_ref, proj_ref):
    proj_ref[...] = _nt_dot(a_ref[...], w_ref[...].astype(BF16))


def _inproj(a, w_t, *, tm, tn, skip_from, skip):
    m, d = a.shape
    n = w_t.shape[0] - skip
    first_after = skip_from // tn

    def w_rows(i, j):
        return (pl.multiple_of(j * tn + jnp.where(j >= first_after, skip, 0), 8), 0)

    return pl.pallas_call(
        _inproj_kernel,
        grid=(m // tm, n // tn),
        in_specs=[
            pl.BlockSpec((tm, d), lambda i, j: (i, 0), pipeline_mode=pl.Buffered(1)),
            pl.BlockSpec((pl.Element(tn), pl.Element(d)), w_rows),
        ],
        out_specs=pl.BlockSpec((tm, tn), lambda i, j: (i, j)),
        out_shape=jax.ShapeDtypeStruct((m, n), F32),
        compiler_params=_cparams(("parallel", "arbitrary")),
        name="inproj",
    )(a, w_t)


def _weight_spec(k, n, tn):
    if tn == n:
        return pl.BlockSpec((k, n), lambda i, j: (0, 0), pipeline_mode=pl.Buffered(1))
    return pl.BlockSpec((k, tn), lambda i, j: (0, j))


def _cast_weight(w_ref, wb_ref, resident):
    if resident:
        @pl.when((pl.program_id(0) == 0) & (pl.program_id(1) == 0))
        def _():
            wb_ref[...] = w_ref[...].astype(BF16)
    else:
        wb_ref[...] = w_ref[...].astype(BF16)


def _norm_mm_kernel(x_ref, g_ref, w_ref, o_ref, a_ref, wb_ref, *, resident):
    @pl.when(pl.program_id(1) == 0)
    def _():
        def rows_fn(rows):
            a_ref[rows, :] = _rms(x_ref[rows, :], g_ref[...]).astype(BF16)

        _for_row_chunks(x_ref.shape[0], rows_fn)

    _cast_weight(w_ref, wb_ref, resident)
    o_ref[...] = jnp.dot(a_ref[...], wb_ref[...], preferred_element_type=F32).astype(o_ref.dtype)


def _norm_mm(x, g, w, *, tm, tn, out_dtype, name):
    m, d = x.shape
    n = w.shape[1]
    return pl.pallas_call(
        functools.partial(_norm_mm_kernel, resident=tn == n),
        grid=(m // tm, n // tn),
        in_specs=[
            pl.BlockSpec((tm, d), lambda i, j: (i, 0)),
            pl.BlockSpec((1, d), lambda i, j: (0, 0)),
            _weight_spec(d, n, tn),
        ],
        out_specs=pl.BlockSpec((tm, tn), lambda i, j: (i, j)),
        out_shape=jax.ShapeDtypeStruct((m, n), out_dtype),
        scratch_shapes=[pltpu.VMEM((tm, d), BF16), pltpu.VMEM((d, tn), BF16)],
        compiler_params=_cparams(("arbitrary", "arbitrary")),
        name=name,
    )(x, g, w)


def _mm_res_kernel(*refs, n_lhs, resident):
    lhs = refs[:n_lhs]
    w_ref, res_ref, o_ref, wb_ref = refs[n_lhs:]
    _cast_weight(w_ref, wb_ref, resident)
    acc = res_ref[...]
    k0 = 0
    for l_ref in lhs:
        kp = l_ref.shape[1]
        acc = acc + jnp.dot(l_ref[...], wb_ref[k0:k0 + kp, :], preferred_element_type=F32)
        k0 += kp
    o_ref[...] = acc


def _mm_res(lhs_parts, w, res, *, tm, tn, name):
    m, n = res.shape
    k = w.shape[0]
    n_lhs = len(lhs_parts)
    in_specs = [pl.BlockSpec((tm, p.shape[1]), lambda i, j: (i, 0)) for p in lhs_parts]
    in_specs += [_weight_spec(k, n, tn), pl.BlockSpec((tm, tn), lambda i, j: (i, j))]
    return pl.pallas_call(
        functools.partial(_mm_res_kernel, n_lhs=n_lhs, resident=tn == n),
        grid=(m // tm, n // tn),
        in_specs=in_specs,
        out_specs=pl.BlockSpec((tm, tn), lambda i, j: (i, j)),
        out_shape=jax.ShapeDtypeStruct((m, n), F32),
        scratch_shapes=[pltpu.VMEM((k, tn), BF16)],
        compiler_params=_cparams(("arbitrary", "arbitrary")),
        name=name,
    )(*lhs_parts, w, res)


LEVELS = (32, 16, 8, 4, 2, 1)
LOG2E = 1.4426950408889634


def _split3_bf16(x):
    def top(v):
        bits = lax.bitcast_convert_type(v, jnp.uint32) & jnp.uint32(0xFFFF0000)
        return lax.bitcast_convert_type(bits, F32)
    hi = top(x)
    r1 = x - hi
    mid = top(r1)
    lo = r1 - mid
    return hi.astype(BF16), mid.astype(BF16), lo.astype(BF16)


def _mix_chunks(qs, ks, vs, las, st_refs):
    heads = range(len(qs))
    c, dk = qs[0].shape
    row = lax.broadcasted_iota(jnp.int32, (c, c), 0)
    col = lax.broadcasted_iota(jnp.int32, (c, c), 1)
    rowk = lax.broadcasted_iota(jnp.int32, (c, dk), 0)
    xor = jnp.bitwise_xor(row, col)

    las = [la * LOG2E for la in las]

    tri = jnp.where(col <= row, 1.0, 0.0).astype(BF16)
    b3 = [jnp.dot(tri, jnp.concatenate(_split3_bf16(la), axis=1), preferred_element_type=F32) for la in las]
    bs = [(t[:, :dk] + t[:, dk:2 * dk]) + t[:, 2 * dk:] for t in b3]

    def neg_dist(w, b, la):
        if w >= 4:
            parts = [jnp.broadcast_to(b[base + w - 1:base + w, :], (2 * w, dk)) for base in range(0, c, 2 * w)]
            m = parts[0] if len(parts) == 1 else jnp.concatenate(parts, axis=0)
            return -jnp.abs(b - m)
        if w == 2:
            r4 = jnp.bitwise_and(rowk, 3)
            nxt = pltpu.roll(la, c - 1, 0)
            prv = pltpu.roll(la, 1, 0)
            return jnp.where(r4 == 0, nxt, jnp.where(r4 == 1, 0.0, jnp.where(r4 == 2, la, la + prv)))
        return jnp.where(jnp.bitwise_and(rowk, 1) == 1, la, 0.0)

    acc = [_nt_dot(qs[h].astype(BF16), ks[h].astype(BF16)) for h in heads]
    for w in reversed(LEVELS):
        upper = jnp.bitwise_and(rowk, w) != 0
        xs = [(jnp.where(upper, qs[h], ks[h]) * jnp.exp2(neg_dist(w, bs[h], las[h]))).astype(BF16) for h in heads]
        gs = [_nt_dot(x, x) for x in xs]
        acc = [jnp.where(xor >= w, gs[h], acc[h]) for h in heads]
    ab = [jnp.where(col <= row, a, 0.0).astype(BF16) for a in acc]

    sts = [st_refs[h][...] for h in heads]
    qx = [(qs[h] * jnp.exp2(bs[h])).astype(BF16) for h in heads]
    b_last = [b[c - 1:c, :] for b in bs]
    kx = [(ks[h] * jnp.exp2(b_last[h] - bs[h])).astype(BF16) for h in heads]
    vb = [v.astype(BF16) for v in vs]
    outs = [jnp.dot(ab[h], vb[h], preferred_element_type=F32) + _nt_dot(qx[h], sts[h].astype(BF16)) for h in heads]
    for h in heads:
        st_refs[h][...] = jnp.exp2(b_last[h]) * sts[h] + lax.dot_general(
            vb[h], kx[h], (((0,), (0,)), ((), ())), preferred_element_type=F32)
    return outs


def _gated_norm(o, g, gn):
    return _rms(o, gn) * (g * _sigmoid(g))


def _gla_kernel(q_ref, k_ref, v_ref, g_ref, la_ref, gn_ref, o_ref, st_ref, *, n_chunk):
    @pl.when(pl.program_id(2) == 0)
    def _():
        st_ref[...] = jnp.zeros_like(st_ref)

    dk, dv = GLA_DK, GLA_DV
    heads = range(st_ref.shape[0])

    def body(ci, carry):
        rows = pl.ds(pl.multiple_of(ci * CHUNK, CHUNK), CHUNK)
        kc = [slice(h * dk, (h + 1) * dk) for h in heads]
        vc = [slice(h * dv, (h + 1) * dv) for h in heads]
        outs = _mix_chunks([q_ref[rows, kc[h]] * (dk ** -0.5) for h in heads], [k_ref[rows, kc[h]] for h in heads],
                           [v_ref[rows, vc[h]] for h in heads], [la_ref[rows, kc[h]] for h in heads],
                           [st_ref.at[h] for h in heads])
        for h in heads:
            o_ref[rows, vc[h]] = _gated_norm(outs[h], g_ref[rows, vc[h]], gn_ref[...]).astype(o_ref.dtype)
        return carry

    lax.fori_loop(0, n_chunk, body, 0)


def _gla(proj, la, gn, *, batch, seq, tb, hp):
    m = proj.shape[0]
    nt = seq // tb
    wk, wv = hp * GLA_DK, hp * GLA_DV
    n_grp = GLA_HEADS // hp
    spec = lambda width, first: pl.BlockSpec((tb, width), lambda b, p, t: (b * nt + t, first + p))
    return pl.pallas_call(
        functools.partial(_gla_kernel, n_chunk=tb // CHUNK),
        grid=(batch, n_grp, nt),
        in_specs=[spec(wk, 0), spec(wk, n_grp), spec(wv, n_grp), spec(wv, 2 * n_grp), spec(wk, 0),
                  pl.BlockSpec((1, GLA_DV), lambda b, p, t: (0, 0))],
        out_specs=spec(wv, 0),
        out_shape=jax.ShapeDtypeStruct((m, GLA_HEADS * GLA_DV), BF16),
        scratch_shapes=[pltpu.VMEM((hp, GLA_DV, GLA_DK), F32)],
        compiler_params=_cparams(("parallel", "parallel", "arbitrary")),
        name="gla",
    )(proj, proj, proj, proj, la, gn)


def _hgrn_kernel(q_ref, f_ref, i_ref, g_ref, lb_ref, gn_ref, o_ref, st_ref, *, n_chunk):
    @pl.when(pl.program_id(2) == 0)
    def _():
        st_ref[...] = jnp.zeros_like(st_ref)

    dh = HGRN_DH
    heads = range(st_ref.shape[0])

    def body(ci, carry):
        rows = pl.ds(pl.multiple_of(ci * CHUNK, CHUNK), CHUNK)
        hc = [slice(h * dh, (h + 1) * dh) for h in heads]
        qs, ks, las = [], [], []
        for h in heads:
            log_lb, log_1mlb, one_m_lb = lb_ref[0:1, hc[h]], lb_ref[1:2, hc[h]], lb_ref[2:3, hc[h]]
            hq = q_ref[rows, hc[h]]
            z = f_ref[rows, hc[h]]
            x1 = log_1mlb + _log_sigmoid(z)
            las.append(jnp.maximum(log_lb, x1) + _log1pexp_neg(jnp.abs(log_lb - x1)))
            qs.append(hq * _sigmoid(hq))
            ks.append(one_m_lb * _sigmoid(-z))
        outs = _mix_chunks(qs, ks, [i_ref[rows, hc[h]] for h in heads], las, [st_ref.at[h] for h in heads])
        for h in heads:
            o_ref[rows, hc[h]] = _gated_norm(outs[h], g_ref[rows, hc[h]], gn_ref[...]).astype(o_ref.dtype)
        return carry

    lax.fori_loop(0, n_chunk, body, 0)


def _hgrn(proj, lbc, gn, *, batch, seq, tb, hp):
    m = proj.shape[0]
    nt = seq // tb
    width = hp * HGRN_DH
    n_grp = HGRN_HEADS // hp
    first = 3072 // width
    spec = lambda seg: pl.BlockSpec((tb, width), lambda b, p, t: (b * nt + t, first + seg * n_grp + p))
    return pl.pallas_call(
        functools.partial(_hgrn_kernel, n_chunk=tb // CHUNK),
        grid=(batch, n_grp, nt),
        in_specs=[spec(0), spec(1), spec(2), spec(3),
                  pl.BlockSpec((8, width), lambda b, p, t: (0, p)),
                  pl.BlockSpec((1, HGRN_DH), lambda b, p, t: (0, 0))],
        out_specs=pl.BlockSpec((tb, width), lambda b, p, t: (b * nt + t, p)),
        out_shape=jax.ShapeDtypeStruct((m, HGRN_HEADS * HGRN_DH), BF16),
        scratch_shapes=[pltpu.VMEM((hp, HGRN_DH, HGRN_DH), F32)],
        compiler_params=_cparams(("parallel", "parallel", "arbitrary")),
        name="hgrn",
    )(proj, proj, proj, proj, lbc, gn)


def _xattn_kernel(q_ref, k_ref, v_ref, o_ref):
    for h in range(XA_HEADS):
        cols = slice(h * XA_DH, (h + 1) * XA_DH)
        s = lax.dot_general(q_ref[:, cols], k_ref[:, cols], (((1,), (1,)), ((), ())),
                            preferred_element_type=F32) * (XA_DH ** -0.5)
        p = jnp.exp(s - jnp.max(s, axis=-1, keepdims=True))
        p = p / jnp.sum(p, axis=-1, keepdims=True)
        o_ref[:, cols] = jnp.dot(p.astype(BF16), v_ref[:, cols], preferred_element_type=F32).astype(o_ref.dtype)


def _xattn(q, kv, *, batch, seq, n_mem, tq):
    m, d = q.shape
    nt = seq // tq
    return pl.pallas_call(
        _xattn_kernel,
        grid=(batch, nt),
        in_specs=[
            pl.BlockSpec((tq, d), lambda b, t: (b * nt + t, 0)),
            pl.BlockSpec((n_mem, d), lambda b, t: (b, 0)),
            pl.BlockSpec((n_mem, d), lambda b, t: (b, 1)),
        ],
        out_specs=pl.BlockSpec((tq, d), lambda b, t: (b * nt + t, 0)),
        out_shape=jax.ShapeDtypeStruct((m, d), BF16),
        compiler_params=_cparams(("parallel", "arbitrary")),
        name="xattn",
    )(q, kv, kv)


def _router_kernel(h_ref, g_ref, wr_ref, br_ref, a_ref, meta_ref, cnt_ref, carry_ref):
    tm = h_ref.shape[0]

    @pl.when(pl.program_id(0) == 0)
    def _():
        carry_ref[...] = jnp.zeros_like(carry_ref)

    a = _rms(h_ref[...], g_ref[...])
    a_ref[...] = a.astype(BF16).reshape(a_ref.shape)
    a_hi = a.astype(BF16)
    a_lo = (a - a_hi.astype(F32)).astype(BF16)
    w = wr_ref[...]
    w_hi = w.astype(BF16)
    w_lo = (w - w_hi.astype(F32)).astype(BF16)
    logits = (jnp.dot(a_hi, w_hi, preferred_element_type=F32) + jnp.dot(a_lo, w_hi, preferred_element_type=F32)
              + jnp.dot(a_hi, w_lo, preferred_element_type=F32)) + br_ref[...]
    lane = lax.broadcasted_iota(jnp.int32, (tm, LANES), 1)
    lane_f = lane.astype(F32)
    neg = -jnp.inf

    gl = jnp.where(lane < N_GROUPS, logits, neg)
    gmax = jnp.max(gl, axis=1, keepdims=True)
    gidx = jnp.min(jnp.where(gl == gmax, lane_f, float(LANES)), axis=1, keepdims=True)
    p_group = 1.0 / jnp.sum(jnp.exp(gl - gmax), axis=1, keepdims=True)
    lo = float(N_GROUPS) + gidx * float(EXPERTS_PER_GROUP)
    el = jnp.where((lane_f >= lo) & (lane_f < lo + float(EXPERTS_PER_GROUP)), logits, neg)
    v1 = jnp.max(el, axis=1, keepdims=True)
    i1 = jnp.min(jnp.where(el == v1, lane_f, float(LANES)), axis=1, keepdims=True)
    el2 = jnp.where(lane_f == i1, neg, el)
    v2 = jnp.max(el2, axis=1, keepdims=True)
    i2 = jnp.min(jnp.where(el2 == v2, lane_f, float(LANES)), axis=1, keepdims=True)
    t = jnp.exp(v2 - v1)
    g1 = p_group / (1.0 + t)
    g2 = p_group * t / (1.0 + t)

    hit1 = lane_f == i1
    hit2 = lane_f == i2
    onehot = jnp.where(hit1 | hit2, 1.0, 0.0)
    row = lax.broadcasted_iota(jnp.int32, (tm, tm), 0)
    col = lax.broadcasted_iota(jnp.int32, (tm, tm), 1)
    strict = jnp.where(col < row, 1.0, 0.0).astype(BF16)
    before = jnp.dot(strict, onehot.astype(BF16), preferred_element_type=F32) + carry_ref[0:1, :]
    r1 = jnp.sum(jnp.where(hit1, before, 0.0), axis=1, keepdims=True)
    r2 = jnp.sum(jnp.where(hit2, before, 0.0), axis=1, keepdims=True)
    carry_ref[...] = carry_ref[...] + jnp.sum(onehot, axis=0, keepdims=True)
    cnt_ref[...] = carry_ref[...]

    meta = jnp.zeros((tm, LANES), F32)
    for idx, val in enumerate((i1 - float(N_GROUPS), i2 - float(N_GROUPS), r1, r2, g1, g2)):
        meta = jnp.where(lane == idx, val, meta)
    meta_ref[...] = meta


def _router(h, g, wr, br, *, tm):
    m, d = h.shape
    slab = d // LANES
    return pl.pallas_call(
        _router_kernel,
        grid=(m // tm,),
        in_specs=[
            pl.BlockSpec((tm, d), lambda i: (i, 0)),
            pl.BlockSpec((1, d), lambda i: (0, 0)),
            pl.BlockSpec((d, LANES), lambda i: (0, 0)),
            pl.BlockSpec((1, LANES), lambda i: (0, 0)),
        ],
        out_specs=[
            pl.BlockSpec((tm, slab, LANES), lambda i: (i, 0, 0)),
            pl.BlockSpec((tm, LANES), lambda i: (i, 0)),
            pl.BlockSpec((8, LANES), lambda i: (0, 0)),
        ],
        out_shape=[jax.ShapeDtypeStruct((m, slab, LANES), BF16), jax.ShapeDtypeStruct((m, LANES), F32),
                   jax.ShapeDtypeStruct((8, LANES), F32)],
        scratch_shapes=[pltpu.VMEM((8, LANES), F32)],
        compiler_params=_cparams(("arbitrary",)),
        name="router",
    )(h, g, wr, br)


PAD_PIECES = tuple(EXPERT_BLOCK >> (b + 1) for b in range(EXPERT_BLOCK.bit_length() - 1))


def _dispatch_kernel(d1_ref, d2_ref, fill_ref, npad_ref, nb_ref, a_ref, xs_hbm, st0, st1, zbuf, sem, zsem):
    i = pl.program_id(0)
    n_steps = pl.num_programs(0)
    tm = a_ref.shape[0]
    bufs = (st0, st1)
    rb = zbuf.shape[0]
    n_blocks = xs_hbm.shape[0] // rb
    min_blocks = (2 * tm * n_steps) // rb

    def zero_copies():
        out = []
        for e in range(N_EXPERTS):
            p = npad_ref[e]
            for piece in PAD_PIECES:
                out.append((p & piece != 0, pltpu.make_async_copy(
                    zbuf.at[pl.ds(0, piece)], xs_hbm.at[pl.ds(fill_ref[e] + (p & -(2 * piece)), piece)], zsem.at[0])))
        for b in range(min_blocks, n_blocks):
            out.append((b >= nb_ref[0], pltpu.make_async_copy(zbuf, xs_hbm.at[pl.ds(b * rb, rb)], zsem.at[0])))
        return out

    def wait_rows(s):
        for _ in range(2):
            pltpu.make_async_copy(bufs[s], xs_hbm.at[pl.ds(0, tm)], sem.at[s]).wait()

    @pl.when(i == 0)
    def _():
        zbuf[...] = jnp.zeros_like(zbuf)
        for cond, cp in zero_copies():
            @pl.when(cond)
            def _():
                cp.start()

    for s in (0, 1):
        @pl.when(lax.rem(i, 2) == s)
        def _():
            @pl.when(i >= 2)
            def _():
                wait_rows(s)

            bufs[s][...] = a_ref[...]
            for r in range(tm):
                pltpu.make_async_copy(bufs[s].at[r], xs_hbm.at[d1_ref[i * tm + r]], sem.at[s]).start(priority=0)
                pltpu.make_async_copy(bufs[s].at[r], xs_hbm.at[d2_ref[i * tm + r]], sem.at[s]).start(priority=1)

    @pl.when(i == n_steps - 1)
    def _():
        for s in (0, 1):
            @pl.when((lax.rem(n_steps - 1, 2) == s) | ((n_steps >= 2) & (lax.rem(n_steps, 2) == s)))
            def _():
                wait_rows(s)
        for cond, cp in zero_copies():
            @pl.when(cond)
            def _():
                cp.wait()


def _dispatch(dest1, dest2, fill, npad, n_blk, a, *, n_rows, tm):
    m, slab, _ = a.shape
    grid_spec = pltpu.PrefetchScalarGridSpec(
        num_scalar_prefetch=5,
        grid=(m // tm,),
        in_specs=[pl.BlockSpec((tm, slab, LANES), lambda i, d1, d2, fl, npd, nb: (i, 0, 0))],
        out_specs=pl.BlockSpec(memory_space=pl.ANY),
        scratch_shapes=[pltpu.VMEM((tm, slab, LANES), a.dtype), pltpu.VMEM((tm, slab, LANES), a.dtype),
                        pltpu.VMEM((EXPERT_BLOCK, slab, LANES), a.dtype),
                        pltpu.SemaphoreType.DMA((2,)), pltpu.SemaphoreType.DMA((1,))],
    )
    return pl.pallas_call(
        _dispatch_kernel,
        grid_spec=grid_spec,
        out_shape=jax.ShapeDtypeStruct((n_rows, slab, LANES), a.dtype),
        compiler_params=_cparams(("arbitrary",)),
        name="dispatch",
    )(dest1, dest2, fill, npad, n_blk, a)


def _weight_copies(w_hbms, e, wst_ref, wsem):
    copies = []
    for t, w_hbm in enumerate(w_hbms):
        rows_per = w_hbm.shape[1] // WEIGHT_CHUNKS
        for c in range(WEIGHT_CHUNKS):
            rows = pl.ds(c * rows_per, rows_per)
            copies.append(pltpu.make_async_copy(w_hbm.at[e, rows, :], wst_ref.at[t, rows, :], wsem.at[0]))
    return copies


def _load_expert_weights(i, be_ref, nx_ref, w_hbms, wst_ref, wb_refs, wsem):
    @pl.when(i == 0)
    def _():
        for cp in _weight_copies(w_hbms, be_ref[0], wst_ref, wsem):
            cp.start(priority=WEIGHT_DMA_PRIORITY)

    @pl.when((i == 0) | (be_ref[i] != be_ref[jnp.maximum(i - 1, 0)]))
    def _():
        for cp in _weight_copies(w_hbms, be_ref[i], wst_ref, wsem):
            cp.wait()
        for t, wb_ref in enumerate(wb_refs):
            wb_ref[...] = wst_ref[t].astype(BF16)

        @pl.when(nx_ref[i] >= 0)
        def _():
            for cp in _weight_copies(w_hbms, nx_ref[i], wst_ref, wsem):
                cp.start(priority=WEIGHT_DMA_PRIORITY)


def _expert_a_kernel(be_ref, nx_ref, nb_ref, x_ref, wg_hbm, wu_hbm, hb_ref, wst_ref, wgb_ref, wub_ref, wsem):
    i = pl.program_id(0)
    nb = nb_ref[0]
    rb = hb_ref.shape[0]

    @pl.when(i < nb)
    def _():
        _load_expert_weights(i, be_ref, nx_ref, (wg_hbm, wu_hbm), wst_ref, (wgb_ref, wub_ref), wsem)
        x = x_ref[...].reshape(rb, -1)
        hg = jnp.dot(x, wgb_ref[...], preferred_element_type=F32)
        hu = jnp.dot(x, wub_ref[...], preferred_element_type=F32)
        hb_ref[...] = (hg * _sigmoid(hg) * hu).astype(hb_ref.dtype)

    @pl.when(i >= nb)
    def _():
        hb_ref[...] = jnp.zeros_like(hb_ref)


def _expert_a(block_e, next_e, n_blk, xs, wg, wu, *, rb):
    n_rows, slab, _ = xs.shape
    d = slab * LANES
    de = wg.shape[2]
    grid_spec = pltpu.PrefetchScalarGridSpec(
        num_scalar_prefetch=3,
        grid=(n_rows // rb,),
        in_specs=[pl.BlockSpec((rb, slab, LANES), lambda i, be, nx, nb: (jnp.minimum(i, nb[0] - 1), 0, 0)),
                  pl.BlockSpec(memory_space=pl.ANY), pl.BlockSpec(memory_space=pl.ANY)],
        out_specs=pl.BlockSpec((rb, de), lambda i, be, nx, nb: (i, 0)),
        scratch_shapes=[pltpu.VMEM((2, d, de), F32), pltpu.VMEM((d, de), BF16), pltpu.VMEM((d, de), BF16),
                        pltpu.SemaphoreType.DMA((1,))],
    )
    return pl.pallas_call(
        _expert_a_kernel,
        grid_spec=grid_spec,
        out_shape=jax.ShapeDtypeStruct((n_rows, de), BF16),
        compiler_params=_cparams(("arbitrary",)),
        name="expert_up",
    )(block_e, next_e, n_blk, xs, wg, wu)


def _expert_b_kernel(slot_ref, be_ref, nx_ref, nb_ref, hb_ref, wd_hbm, y_hbm, wst_ref, wdb_ref, ys0, ys1, sem, wsem,
                     *, n_tok):
    i = pl.program_id(0)
    n_steps = pl.num_programs(0)
    nb = nb_ref[0]
    rb = hb_ref.shape[0]
    bufs = (ys0, ys1)

    def wait_block(buf, s):
        pltpu.make_async_copy(buf, y_hbm.at[pl.ds(0, rb)], sem.at[s]).wait()

    @pl.when(i == 0)
    def _():
        ys0[...] = jnp.zeros_like(ys0)
        for s in (0, 1):
            spare = pltpu.make_async_copy(ys0, y_hbm.at[pl.ds(2 * n_tok + s * rb, rb)], sem.at[0])
            spare.start()
            spare.wait()

    for s in (0, 1):
        @pl.when((i < nb) & (lax.rem(i, 2) == s))
        def _():
            _load_expert_weights(i, be_ref, nx_ref, (wd_hbm,), wst_ref, (wdb_ref,), wsem)

            @pl.when(i >= 2)
            def _():
                wait_block(bufs[s], s)

            y = jnp.dot(hb_ref[...], wdb_ref[...], preferred_element_type=F32)
            bufs[s][...] = y.reshape(bufs[s].shape)
            for r in range(rb):
                v = slot_ref[i * rb + r]
                dst = jnp.where(v >= 0, (v & 1) * n_tok + (v >> 1), 2 * n_tok + s * rb + r)
                pltpu.make_async_copy(bufs[s].at[r], y_hbm.at[dst], sem.at[s]).start()

    @pl.when(i == n_steps - 1)
    def _():
        for s in (0, 1):
            @pl.when(((nb >= 1) & (lax.rem(nb - 1, 2) == s)) | ((nb >= 2) & (lax.rem(nb, 2) == s)))
            def _():
                wait_block(bufs[s], s)


def _expert_b(slot, block_e, next_e, n_blk, hb, wd, *, rb, n_tok):
    n_steps = hb.shape[0] // rb
    de = hb.shape[1]
    d = wd.shape[2]
    slab = d // LANES
    grid_spec = pltpu.PrefetchScalarGridSpec(
        num_scalar_prefetch=4,
        grid=(n_steps,),
        in_specs=[
            pl.BlockSpec((rb, de), lambda i, sl, be, nx, nb: (i, 0)),
            pl.BlockSpec(memory_space=pl.ANY),
        ],
        out_specs=pl.BlockSpec(memory_space=pl.ANY),
        scratch_shapes=[pltpu.VMEM((1, de, d), F32), pltpu.VMEM((de, d), BF16), pltpu.VMEM((rb, slab, LANES), F32),
                        pltpu.VMEM((rb, slab, LANES), F32), pltpu.SemaphoreType.DMA((2,)),
                        pltpu.SemaphoreType.DMA((1,))],
    )
    return pl.pallas_call(
        functools.partial(_expert_b_kernel, n_tok=n_tok),
        grid_spec=grid_spec,
        out_shape=jax.ShapeDtypeStruct((2 * n_tok + 2 * rb, slab, LANES), F32),
        compiler_params=_cparams(("arbitrary",)),
        name="expert_down",
    )(slot, block_e, next_e, n_blk, hb, wd)


def _combine_kernel(y0_ref, y1_ref, h_ref, meta_ref, g_ref, o_ref):
    tm, d = h_ref.shape
    meta = meta_ref[...]
    out = h_ref[...] + meta[:, 4:5] * y0_ref[...].reshape(tm, d) + meta[:, 5:6] * y1_ref[...].reshape(tm, d)
    o_ref[...] = _rms(out, g_ref[...])


def _combine(y, h, meta, g, *, tm):
    m, d = h.shape
    slab = d // LANES
    nt = m // tm
    return pl.pallas_call(
        _combine_kernel,
        grid=(nt,),
        in_specs=[
            pl.BlockSpec((tm, slab, LANES), lambda i: (i, 0, 0)),
            pl.BlockSpec((tm, slab, LANES), lambda i: (nt + i, 0, 0)),
            pl.BlockSpec((tm, d), lambda i: (i, 0)),
            pl.BlockSpec((tm, LANES), lambda i: (i, 0)),
            pl.BlockSpec((1, d), lambda i: (0, 0)),
        ],
        out_specs=pl.BlockSpec((tm, d), lambda i: (i, 0)),
        out_shape=jax.ShapeDtypeStruct((m, d), F32),
        compiler_params=_cparams(("parallel",)),
        name="combine",
    )(y, y, h, meta, g)


def kernel(x, mem, norm_mix_g, w_in, w_gla_alpha_up, b_gla_alpha, gla_out_norm_g, hgrn_lb_logits, hgrn_out_norm_g, w_mix_out, norm_xattn_g, norm_mem_g, w_xattn_q, w_xattn_kv, w_xattn_out, norm_ffn_g, w_router_group, b_router_group, w_router_expert, b_router_expert, w_expert_gate, w_expert_up, w_expert_down, norm_final_g):
    batch, seq, d = x.shape
    n_mem = mem.shape[1]
    m = batch * seq
    depth = norm_mix_g.shape[0]
    h = x.reshape(m, d)
    lb_all = jnp.cumsum(jax.nn.softmax(hgrn_lb_logits.astype(F32), axis=0), axis=0)
    gla_cols = 2 * GLA_HEADS * GLA_DK + 2 * GLA_HEADS * GLA_DV
    lr_rank = w_gla_alpha_up.shape[1]

    for l in range(depth):
        w_t = jnp.swapaxes(w_in[l], 0, 1)
        w_lr_t = jnp.pad(w_t[gla_cols:gla_cols + lr_rank], ((0, LANES - lr_rank), (0, 0))).astype(BF16)
        w_up = jnp.pad(w_gla_alpha_up[l], ((0, LANES - lr_rank), (0, 0))).astype(BF16)
        a_mix, la = _norm_gate(h, norm_mix_g[l][None, :], w_lr_t, w_up, b_gla_alpha[l][None, :], tm=min(m, 512))
        proj = _inproj(a_mix, w_t, tm=min(m, 4096), tn=512, skip_from=gla_cols, skip=lr_rank)
        lb = lb_all[l]
        lbc = jnp.zeros((8, lb.shape[0]), F32).at[0].set(jnp.log(lb)).at[1].set(jnp.log1p(-lb)).at[2].set(1.0 - lb)
        o_gla = _gla(proj, la, gla_out_norm_g[l][None, :], batch=batch, seq=seq, tb=min(seq, 512), hp=4)
        o_h = _hgrn(proj, lbc, hgrn_out_norm_g[l][None, :], batch=batch, seq=seq, tb=min(seq, 1024), hp=4)
        h = _mm_res([o_gla, o_h], w_mix_out[l], h, tm=min(m, 512), tn=d, name="mix_out")

        kv = _norm_mm(mem.reshape(batch * n_mem, d), norm_mem_g[l][None, :], w_xattn_kv[l],
                      tm=batch * n_mem, tn=1024, out_dtype=BF16, name="mem_kv")
        q = _norm_mm(h, norm_xattn_g[l][None, :], w_xattn_q[l], tm=min(m, 512), tn=d, out_dtype=BF16,
                     name="xattn_q")
        o = _xattn(q, kv, batch=batch, seq=seq, n_mem=n_mem, tq=min(seq, 512))
        h = _mm_res([o], w_xattn_out[l], h, tm=min(m, 512), tn=d, name="xattn_out")

        wr = jnp.pad(jnp.concatenate([w_router_group[l], w_router_expert[l]], axis=1),
                     ((0, 0), (0, LANES - N_GROUPS - N_EXPERTS)))
        br = jnp.pad(jnp.concatenate([b_router_group[l], b_router_expert[l]]), (0, LANES - N_GROUPS - N_EXPERTS))
        a, meta, cnt = _router(h, norm_ffn_g[l][None, :], wr, br[None, :], tm=min(m, 512))

        e_idx = meta[:, 0:2].astype(jnp.int32)
        rank = meta[:, 2:4].astype(jnp.int32)
        counts = cnt[0, N_GROUPS:N_GROUPS + N_EXPERTS].astype(jnp.int32)
        padded = ((counts + EXPERT_BLOCK - 1) // EXPERT_BLOCK) * EXPERT_BLOCK
        pad_end = jnp.cumsum(padded)
        dest = (pad_end - padded)[e_idx] + rank
        n_rows = 2 * m + N_EXPERTS * EXPERT_BLOCK
        n_blocks = n_rows // EXPERT_BLOCK
        n_blk = (pad_end[-1:] // EXPERT_BLOCK).astype(jnp.int32)
        blk_first = jnp.arange(n_blocks, dtype=jnp.int32) * EXPERT_BLOCK
        block_e = jnp.minimum(jnp.sum((pad_end[None, :] <= blk_first[:, None]).astype(jnp.int32), axis=1),
                              N_EXPERTS - 1)
        assign = 2 * jnp.arange(m, dtype=jnp.int32)[:, None] + jnp.arange(2, dtype=jnp.int32)[None, :]
        slot = jnp.full((n_rows,), -1, jnp.int32).at[dest.reshape(-1)].set(assign.reshape(-1))
        after = (pad_end // EXPERT_BLOCK)[block_e]
        next_e = jnp.where(after < n_blk[0], block_e[jnp.minimum(after, n_blocks - 1)], -1).astype(jnp.int32)

        xs = _dispatch(dest[:, 0], dest[:, 1], pad_end - padded + counts, padded - counts, n_blk, a, n_rows=n_rows,
                       tm=min(m, 256))
        hb = _expert_a(block_e, next_e, n_blk, xs, w_expert_gate[l], w_expert_up[l], rb=EXPERT_BLOCK)
        y = _expert_b(slot, block_e, next_e, n_blk, hb, w_expert_down[l], rb=EXPERT_BLOCK, n_tok=m)
        last = l == depth - 1
        g_fin = norm_final_g[None, :] if last else jnp.ones((1, d), F32)
        h = _combine(y, h, meta, g_fin, tm=min(m, 256))
        assert last, "the combine kernel fuses the final rmsnorm; deeper stacks need an un-normalised variant"

    return h.reshape(batch, seq, d)
```

```python
import functools

import jax
import jax.numpy as jnp
from jax import lax
from jax.experimental import pallas as pl
from jax.experimental.pallas import tpu as pltpu

F32 = jnp.float32
BF16 = jnp.bfloat16

EPS = 1e-6
CHUNK = 64
LANES = 128
GLA_HEADS, GLA_DK, GLA_DV = 4, 128, 256
HGRN_HEADS, HGRN_DH = 8, 128
XA_HEADS, XA_DH = 4, 512
N_GROUPS, EXPERTS_PER_GROUP, N_EXPERTS = 4, 8, 32
EXPERT_BLOCK = 256
WEIGHT_CHUNKS = 8
WEIGHT_DMA_PRIORITY = 1
VMEM_LIMIT = 56 * 1024 * 1024


def _cparams(sem, vmem=VMEM_LIMIT):
    return pltpu.CompilerParams(dimension_semantics=sem, vmem_limit_bytes=vmem)


def _log1pexp_neg(t):
    return jnp.log(1.0 + jnp.exp(-t))


def _log_sigmoid(z):
    return jnp.minimum(z, 0.0) - _log1pexp_neg(jnp.abs(z))


def _sigmoid(z):
    return 1.0 / (1.0 + jnp.exp(-z))


def _rms(x, g):
    return x * lax.rsqrt(jnp.mean(x * x, axis=-1, keepdims=True) + EPS) * g


def _nt_dot(x, y):
    return lax.dot_general(x, y, (((1,), (1,)), ((), ())), preferred_element_type=F32)


NORM_ROWS = 256


def _for_row_chunks(n_rows, fn):
    step = min(NORM_ROWS, n_rows)

    def body(ci, carry):
        fn(pl.ds(pl.multiple_of(ci * step, step), step))
        return carry

    lax.fori_loop(0, n_rows // step, body, 0)


def _norm_gate_kernel(x_ref, g_ref, wlr_ref, wup_ref, bal_ref, a_ref, la_ref):
    a = _rms(x_ref[...], g_ref[...]).astype(BF16)
    a_ref[...] = a
    lr = _nt_dot(a, wlr_ref[...])
    z = jnp.dot(lr.astype(BF16), wup_ref[...], preferred_element_type=F32) + bal_ref[...]
    la_ref[...] = _log_sigmoid(z) * (1.0 / 16.0)


def _norm_gate(x, g, w_lr_t, w_up, b_al, *, tm):
    m, d = x.shape
    nk = w_up.shape[1]
    return pl.pallas_call(
        _norm_gate_kernel,
        grid=(m // tm,),
        in_specs=[
            pl.BlockSpec((tm, d), lambda i: (i, 0)),
            pl.BlockSpec((1, d), lambda i: (0, 0)),
            pl.BlockSpec((LANES, d), lambda i: (0, 0)),
            pl.BlockSpec((LANES, nk), lambda i: (0, 0)),
            pl.BlockSpec((1, nk), lambda i: (0, 0)),
        ],
        out_specs=[pl.BlockSpec((tm, d), lambda i: (i, 0)), pl.BlockSpec((tm, nk), lambda i: (i, 0))],
        out_shape=[jax.ShapeDtypeStruct((m, d), BF16), jax.ShapeDtypeStruct((m, nk), F32)],
        compiler_params=_cparams(("parallel",)),
        name="norm_gate",
    )(x, g, w_lr_t, w_up, b_al)


def _inproj_kernel(a_ref, w_ref, proj_ref):
    proj_ref[...] = _nt_dot(a_ref[...], w_ref[...].astype(BF16))


def _inproj(a, w_t, *, tm, tn, skip_from, skip):
    m, d = a.shape
    n = w_t.shape[0] - skip
    first_after = skip_from // tn

    def w_rows(i, j):
        return (pl.multiple_of(j * tn + jnp.where(j >= first_after, skip, 0), 8), 0)

    return pl.pallas_call(
        _inproj_kernel,
        grid=(m // tm, n // tn),
        in_specs=[
            pl.BlockSpec((tm, d), lambda i, j: (i, 0), pipeline_mode=pl.Buffered(1)),
            pl.BlockSpec((pl.Element(tn), pl.Element(d)), w_rows),
        ],
        out_specs=pl.BlockSpec((tm, tn), lambda i, j: (i, j)),
        out_shape=jax.ShapeDtypeStruct((m, n), F32),
        compiler_params=_cparams(("parallel", "arbitrary")),
        name="inproj",
    )(a, w_t)


def _weight_spec(k, n, tn):
    if tn == n:
        return pl.BlockSpec((k, n), lambda i, j: (0, 0), pipeline_mode=pl.Buffered(1))
    return pl.BlockSpec((k, tn), lambda i, j: (0, j))


def _cast_weight(w_ref, wb_ref, resident):
    if resident:
        @pl.when((pl.program_id(0) == 0) & (pl.program_id(1) == 0))
        def _():
            wb_ref[...] = w_ref[...].astype(BF16)
    else:
        wb_ref[...] = w_ref[...].astype(BF16)


def _norm_mm_kernel(x_ref, g_ref, w_ref, o_ref, a_ref, wb_ref, *, resident):
    @pl.when(pl.program_id(1) == 0)
    def _():
        def rows_fn(rows):
            a_ref[rows, :] = _rms(x_ref[rows, :], g_ref[...]).astype(BF16)

        _for_row_chunks(x_ref.shape[0], rows_fn)

    _cast_weight(w_ref, wb_ref, resident)
    o_ref[...] = jnp.dot(a_ref[...], wb_ref[...], preferred_element_type=F32).astype(o_ref.dtype)


def _norm_mm(x, g, w, *, tm, tn, out_dtype, name):
    m, d = x.shape
    n = w.shape[1]
    return pl.pallas_call(
        functools.partial(_norm_mm_kernel, resident=tn == n),
        grid=(m // tm, n // tn),
        in_specs=[
            pl.BlockSpec((tm, d), lambda i, j: (i, 0)),
            pl.BlockSpec((1, d), lambda i, j: (0, 0)),
            _weight_spec(d, n, tn),
        ],
        out_specs=pl.BlockSpec((tm, tn), lambda i, j: (i, j)),
        out_shape=jax.ShapeDtypeStruct((m, n), out_dtype),
        scratch_shapes=[pltpu.VMEM((tm, d), BF16), pltpu.VMEM((d, tn), BF16)],
        compiler_params=_cparams(("arbitrary", "arbitrary")),
        name=name,
    )(x, g, w)


def _mm_res_kernel(*refs, n_lhs, resident):
    lhs = refs[:n_lhs]
    w_ref, res_ref, o_ref, wb_ref = refs[n_lhs:]
    _cast_weight(w_ref, wb_ref, resident)
    acc = res_ref[...]
    k0 = 0
    for l_ref in lhs:
        kp = l_ref.shape[1]
        acc = acc + jnp.dot(l_ref[...], wb_ref[k0:k0 + kp, :], preferred_element_type=F32)
        k0 += kp
    o_ref[...] = acc


def _mm_res(lhs_parts, w, res, *, tm, tn, name):
    m, n = res.shape
    k = w.shape[0]
    n_lhs = len(lhs_parts)
    in_specs = [pl.BlockSpec((tm, p.shape[1]), lambda i, j: (i, 0)) for p in lhs_parts]
    in_specs += [_weight_spec(k, n, tn), pl.BlockSpec((tm, tn), lambda i, j: (i, j))]
    return pl.pallas_call(
        functools.partial(_mm_res_kernel, n_lhs=n_lhs, resident=tn == n),
        grid=(m // tm, n // tn),
        in_specs=in_specs,
        out_specs=pl.BlockSpec((tm, tn), lambda i, j: (i, j)),
        out_shape=jax.ShapeDtypeStruct((m, n), F32),
        scratch_shapes=[pltpu.VMEM((k, tn), BF16)],
        compiler_params=_cparams(("arbitrary", "arbitrary")),
        name=name,
    )(*lhs_parts, w, res)


LEVELS = (32, 16, 8, 4, 2, 1)
LOG2E = 1.4426950408889634


def _split3_bf16(x):
    def top(v):
        bits = lax.bitcast_convert_type(v, jnp.uint32) & jnp.uint32(0xFFFF0000)
        return lax.bitcast_convert_type(bits, F32)
    hi = top(x)
    r1 = x - hi
    mid = top(r1)
    lo = r1 - mid
    return hi.astype(BF16), mid.astype(BF16), lo.astype(BF16)


def _mix_chunks(qs, ks, vs, las, st_refs):
    heads = range(len(qs))
    c, dk = qs[0].shape
    row = lax.broadcasted_iota(jnp.int32, (c, c), 0)
    col = lax.broadcasted_iota(jnp.int32, (c, c), 1)
    rowk = lax.broadcasted_iota(jnp.int32, (c, dk), 0)
    xor = jnp.bitwise_xor(row, col)

    las = [la * LOG2E for la in las]

    tri = jnp.where(col <= row, 1.0, 0.0).astype(BF16)
    b3 = [jnp.dot(tri, jnp.concatenate(_split3_bf16(la), axis=1), preferred_element_type=F32) for la in las]
    bs = [(t[:, :dk] + t[:, dk:2 * dk]) + t[:, 2 * dk:] for t in b3]

    def neg_dist(w, b, la):
        if w >= 4:
            parts = [jnp.broadcast_to(b[base + w - 1:base + w, :], (2 * w, dk)) for base in range(0, c, 2 * w)]
            m = parts[0] if len(parts) == 1 else jnp.concatenate(parts, axis=0)
            return -jnp.abs(b - m)
        if w == 2:
            r4 = jnp.bitwise_and(rowk, 3)
            nxt = pltpu.roll(la, c - 1, 0)
            prv = pltpu.roll(la, 1, 0)
            return jnp.where(r4 == 0, nxt, jnp.where(r4 == 1, 0.0, jnp.where(r4 == 2, la, la + prv)))
        return jnp.where(jnp.bitwise_and(rowk, 1) == 1, la, 0.0)

    acc = [_nt_dot(qs[h].astype(BF16), ks[h].astype(BF16)) for h in heads]
    for w in reversed(LEVELS):
        upper = jnp.bitwise_and(rowk, w) != 0
        xs = [(jnp.where(upper, qs[h], ks[h]) * jnp.exp2(neg_dist(w, bs[h], las[h]))).astype(BF16) for h in heads]
        gs = [_nt_dot(x, x) for x in xs]
        acc = [jnp.where(xor >= w, gs[h], acc[h]) for h in heads]
    ab = [jnp.where(col <= row, a, 0.0).astype(BF16) for a in acc]

    sts = [st_refs[h][...] for h in heads]
    qx = [(qs[h] * jnp.exp2(bs[h])).astype(BF16) for h in heads]
    b_last = [b[c - 1:c, :] for b in bs]
    kx = [(ks[h] * jnp.exp2(b_last[h] - bs[h])).astype(BF16) for h in heads]
    vb = [v.astype(BF16) for v in vs]
    outs = [jnp.dot(ab[h], vb[h], preferred_element_type=F32) + _nt_dot(qx[h], sts[h].astype(BF16)) for h in heads]
    for h in heads:
        st_refs[h][...] = jnp.exp2(b_last[h]) * sts[h] + lax.dot_general(
            vb[h], kx[h], (((0,), (0,)), ((), ())), preferred_element_type=F32)
    return outs


def _gated_norm(o, g, gn):
    return _rms(o, gn) * (g * _sigmoid(g))


def _gla_kernel(q_ref, k_ref, v_ref, g_ref, la_ref, gn_ref, o_ref, st_ref, *, n_chunk):
    @pl.when(pl.program_id(2) == 0)
    def _():
        st_ref[...] = jnp.zeros_like(st_ref)

    dk, dv = GLA_DK, GLA_DV
    heads = range(st_ref.shape[0])

    def body(ci, carry):
        rows = pl.ds(pl.multiple_of(ci * CHUNK, CHUNK), CHUNK)
        kc = [slice(h * dk, (h + 1) * dk) for h in heads]
        vc = [slice(h * dv, (h + 1) * dv) for h in heads]
        outs = _mix_chunks([q_ref[rows, kc[h]] * (dk ** -0.5) for h in heads], [k_ref[rows, kc[h]] for h in heads],
                           [v_ref[rows, vc[h]] for h in heads], [la_ref[rows, kc[h]] for h in heads],
                           [st_ref.at[h] for h in heads])
        for h in heads:
            o_ref[rows, vc[h]] = _gated_norm(outs[h], g_ref[rows, vc[h]], gn_ref[...]).astype(o_ref.dtype)
        return carry

    lax.fori_loop(0, n_chunk, body, 0)


def _gla(proj, la, gn, *, batch, seq, tb, hp):
    m = proj.shape[0]
    nt = seq // tb
    wk, wv = hp * GLA_DK, hp * GLA_DV
    n_grp = GLA_HEADS // hp
    spec = lambda width, first: pl.BlockSpec((tb, width), lambda b, p, t: (b * nt + t, first + p))
    return pl.pallas_call(
        functools.partial(_gla_kernel, n_chunk=tb // CHUNK),
        grid=(batch, n_grp, nt),
        in_specs=[spec(wk, 0), spec(wk, n_grp), spec(wv, n_grp), spec(wv, 2 * n_grp), spec(wk, 0),
                  pl.BlockSpec((1, GLA_DV), lambda b, p, t: (0, 0))],
        out_specs=spec(wv, 0),
        out_shape=jax.ShapeDtypeStruct((m, GLA_HEADS * GLA_DV), BF16),
        scratch_shapes=[pltpu.VMEM((hp, GLA_DV, GLA_DK), F32)],
        compiler_params=_cparams(("parallel", "parallel", "arbitrary")),
        name="gla",
    )(proj, proj, proj, proj, la, gn)


def _hgrn_kernel(q_ref, f_ref, i_ref, g_ref, lb_ref, gn_ref, o_ref, st_ref, *, n_chunk):
    @pl.when(pl.program_id(2) == 0)
    def _():
        st_ref[...] = jnp.zeros_like(st_ref)

    dh = HGRN_DH
    heads = range(st_ref.shape[0])

    def body(ci, carry):
        rows = pl.ds(pl.multiple_of(ci * CHUNK, CHUNK), CHUNK)
        hc = [slice(h * dh, (h + 1) * dh) for h in heads]
        qs, ks, las = [], [], []
        for h in heads:
            log_lb, log_1mlb, one_m_lb = lb_ref[0:1, hc[h]], lb_ref[1:2, hc[h]], lb_ref[2:3, hc[h]]
            hq = q_ref[rows, hc[h]]
            z = f_ref[rows, hc[h]]
            x1 = log_1mlb + _log_sigmoid(z)
            las.append(jnp.maximum(log_lb, x1) + _log1pexp_neg(jnp.abs(log_lb - x1)))
            qs.append(hq * _sigmoid(hq))
            ks.append(one_m_lb * _sigmoid(-z))
        outs = _mix_chunks(qs, ks, [i_ref[rows, hc[h]] for h in heads], las, [st_ref.at[h] for h in heads])
        for h in heads:
            o_ref[rows, hc[h]] = _gated_norm(outs[h], g_ref[rows, hc[h]], gn_ref[...]).astype(o_ref.dtype)
        return carry

    lax.fori_loop(0, n_chunk, body, 0)


def _hgrn(proj, lbc, gn, *, batch, seq, tb, hp):
    m = proj.shape[0]
    nt = seq // tb
    width = hp * HGRN_DH
    n_grp = HGRN_HEADS // hp
    first = 3072 // width
    spec = lambda seg: pl.BlockSpec((tb, width), lambda b, p, t: (b * nt + t, first + seg * n_grp + p))
    return pl.pallas_call(
        functools.partial(_hgrn_kernel, n_chunk=tb // CHUNK),
        grid=(batch, n_grp, nt),
        in_specs=[spec(0), spec(1), spec(2), spec(3),
                  pl.BlockSpec((8, width), lambda b, p, t: (0, p)),
                  pl.BlockSpec((1, HGRN_DH), lambda b, p, t: (0, 0))],
        out_specs=pl.BlockSpec((tb, width), lambda b, p, t: (b * nt + t, p)),
        out_shape=jax.ShapeDtypeStruct((m, HGRN_HEADS * HGRN_DH), BF16),
        scratch_shapes=[pltpu.VMEM((hp, HGRN_DH, HGRN_DH), F32)],
        compiler_params=_cparams(("parallel", "parallel", "arbitrary")),
        name="hgrn",
    )(proj, proj, proj, proj, lbc, gn)


def _xattn_kernel(q_ref, k_ref, v_ref, o_ref):
    for h in range(XA_HEADS):
        cols = slice(h * XA_DH, (h + 1) * XA_DH)
        s = lax.dot_general(q_ref[:, cols], k_ref[:, cols], (((1,), (1,)), ((), ())),
                            preferred_element_type=F32) * (XA_DH ** -0.5)
        p = jnp.exp(s - jnp.max(s, axis=-1, keepdims=True))
        p = p / jnp.sum(p, axis=-1, keepdims=True)
        o_ref[:, cols] = jnp.dot(p.astype(BF16), v_ref[:, cols], preferred_element_type=F32).astype(o_ref.dtype)


def _xattn(q, kv, *, batch, seq, n_mem, tq):
    m, d = q.shape
    nt = seq // tq
    return pl.pallas_call(
        _xattn_kernel,
        grid=(batch, nt),
        in_specs=[
            pl.BlockSpec((tq, d), lambda b, t: (b * nt + t, 0)),
            pl.BlockSpec((n_mem, d), lambda b, t: (b, 0)),
            pl.BlockSpec((n_mem, d), lambda b, t: (b, 1)),
        ],
        out_specs=pl.BlockSpec((tq, d), lambda b, t: (b * nt + t, 0)),
        out_shape=jax.ShapeDtypeStruct((m, d), BF16),
        compiler_params=_cparams(("parallel", "arbitrary")),
        name="xattn",
    )(q, kv, kv)


def _router_kernel(h_ref, g_ref, wr_ref, br_ref, a_ref, meta_ref, cnt_ref, carry_ref):
    tm = h_ref.shape[0]

    @pl.when(pl.program_id(0) == 0)
    def _():
        carry_ref[...] = jnp.zeros_like(carry_ref)

    a = _rms(h_ref[...], g_ref[...])
    a_ref[...] = a.astype(BF16).reshape(a_ref.shape)
    a_hi = a.astype(BF16)
    a_lo = (a - a_hi.astype(F32)).astype(BF16)
    w = wr_ref[...]
    w_hi = w.astype(BF16)
    w_lo = (w - w_hi.astype(F32)).astype(BF16)
    logits = (jnp.dot(a_hi, w_hi, preferred_element_type=F32) + jnp.dot(a_lo, w_hi, preferred_element_type=F32)
              + jnp.dot(a_hi, w_lo, preferred_element_type=F32)) + br_ref[...]
    lane = lax.broadcasted_iota(jnp.int32, (tm, LANES), 1)
    lane_f = lane.astype(F32)
    neg = -jnp.inf

    gl = jnp.where(lane < N_GROUPS, logits, neg)
    gmax = jnp.max(gl, axis=1, keepdims=True)
    gidx = jnp.min(jnp.where(gl == gmax, lane_f, float(LANES)), axis=1, keepdims=True)
    p_group = 1.0 / jnp.sum(jnp.exp(gl - gmax), axis=1, keepdims=True)
    lo = float(N_GROUPS) + gidx * float(EXPERTS_PER_GROUP)
    el = jnp.where((lane_f >= lo) & (lane_f < lo + float(EXPERTS_PER_GROUP)), logits, neg)
    v1 = jnp.max(el, axis=1, keepdims=True)
    i1 = jnp.min(jnp.where(el == v1, lane_f, float(LANES)), axis=1, keepdims=True)
    el2 = jnp.where(lane_f == i1, neg, el)
    v2 = jnp.max(el2, axis=1, keepdims=True)
    i2 = jnp.min(jnp.where(el2 == v2, lane_f, float(LANES)), axis=1, keepdims=True)
    t = jnp.exp(v2 - v1)
    g1 = p_group / (1.0 + t)
    g2 = p_group * t / (1.0 + t)

    hit1 = lane_f == i1
    hit2 = lane_f == i2
    onehot = jnp.where(hit1 | hit2, 1.0, 0.0)
    row = lax.broadcasted_iota(jnp.int32, (tm, tm), 0)
    col = lax.broadcasted_iota(jnp.int32, (tm, tm), 1)
    strict = jnp.where(col < row, 1.0, 0.0).astype(BF16)
    before = jnp.dot(strict, onehot.astype(BF16), preferred_element_type=F32) + carry_ref[0:1, :]
    r1 = jnp.sum(jnp.where(hit1, before, 0.0), axis=1, keepdims=True)
    r2 = jnp.sum(jnp.where(hit2, before, 0.0), axis=1, keepdims=True)
    carry_ref[...] = carry_ref[...] + jnp.sum(onehot, axis=0, keepdims=True)
    cnt_ref[...] = carry_ref[...]

    meta = jnp.zeros((tm, LANES), F32)
    for idx, val in enumerate((i1 - float(N_GROUPS), i2 - float(N_GROUPS), r1, r2, g1, g2)):
        meta = jnp.where(lane == idx, val, meta)
    meta_ref[...] = meta


def _router(h, g, wr, br, *, tm):
    m, d = h.shape
    slab = d // LANES
    return pl.pallas_call(
        _router_kernel,
        grid=(m // tm,),
        in_specs=[
            pl.BlockSpec((tm, d), lambda i: (i, 0)),
            pl.BlockSpec((1, d), lambda i: (0, 0)),
            pl.BlockSpec((d, LANES), lambda i: (0, 0)),
            pl.BlockSpec((1, LANES), lambda i: (0, 0)),
        ],
        out_specs=[
            pl.BlockSpec((tm, slab, LANES), lambda i: (i, 0, 0)),
            pl.BlockSpec((tm, LANES), lambda i: (i, 0)),
            pl.BlockSpec((8, LANES), lambda i: (0, 0)),
        ],
        out_shape=[jax.ShapeDtypeStruct((m, slab, LANES), BF16), jax.ShapeDtypeStruct((m, LANES), F32),
                   jax.ShapeDtypeStruct((8, LANES), F32)],
        scratch_shapes=[pltpu.VMEM((8, LANES), F32)],
        compiler_params=_cparams(("arbitrary",)),
        name="router",
    )(h, g, wr, br)


PAD_PIECES = tuple(EXPERT_BLOCK >> (b + 1) for b in range(EXPERT_BLOCK.bit_length() - 1))


def _dispatch_kernel(d1_ref, d2_ref, fill_ref, npad_ref, nb_ref, a_ref, xs_hbm, st0, st1, zbuf, sem, zsem):
    i = pl.program_id(0)
    n_steps = pl.num_programs(0)
    tm = a_ref.shape[0]
    bufs = (st0, st1)
    rb = zbuf.shape[0]
    n_blocks = xs_hbm.shape[0] // rb
    min_blocks = (2 * tm * n_steps) // rb

    def zero_copies():
        out = []
        for e in range(N_EXPERTS):
            p = npad_ref[e]
            for piece in PAD_PIECES:
                out.append((p & piece != 0, pltpu.make_async_copy(
                    zbuf.at[pl.ds(0, piece)], xs_hbm.at[pl.ds(fill_ref[e] + (p & -(2 * piece)), piece)], zsem.at[0])))
        for b in range(min_blocks, n_blocks):
            out.append((b >= nb_ref[0], pltpu.make_async_copy(zbuf, xs_hbm.at[pl.ds(b * rb, rb)], zsem.at[0])))
        return out

    def wait_rows(s):
        for _ in range(2):
            pltpu.make_async_copy(bufs[s], xs_hbm.at[pl.ds(0, tm)], sem.at[s]).wait()

    @pl.when(i == 0)
    def _():
        zbuf[...] = jnp.zeros_like(zbuf)
        for cond, cp in zero_copies():
            @pl.when(cond)
            def _():
                cp.start()

    for s in (0, 1):
        @pl.when(lax.rem(i, 2) == s)
        def _():
            @pl.when(i >= 2)
            def _():
                wait_rows(s)

            bufs[s][...] = a_ref[...]
            for r in range(tm):
                pltpu.make_async_copy(bufs[s].at[r], xs_hbm.at[d1_ref[i * tm + r]], sem.at[s]).start(priority=0)
                pltpu.make_async_copy(bufs[s].at[r], xs_hbm.at[d2_ref[i * tm + r]], sem.at[s]).start(priority=1)

    @pl.when(i == n_steps - 1)
    def _():
        for s in (0, 1):
            @pl.when((lax.rem(n_steps - 1, 2) == s) | ((n_steps >= 2) & (lax.rem(n_steps, 2) == s)))
            def _():
                wait_rows(s)
        for cond, cp in zero_copies():
            @pl.when(cond)
            def _():
                cp.wait()


def _dispatch(dest1, dest2, fill, npad, n_blk, a, *, n_rows, tm):
    m, slab, _ = a.shape
    grid_spec = pltpu.PrefetchScalarGridSpec(
        num_scalar_prefetch=5,
        grid=(m // tm,),
        in_specs=[pl.BlockSpec((tm, slab, LANES), lambda i, d1, d2, fl, npd, nb: (i, 0, 0))],
        out_specs=pl.BlockSpec(memory_space=pl.ANY),
        scratch_shapes=[pltpu.VMEM((tm, slab, LANES), a.dtype), pltpu.VMEM((tm, slab, LANES), a.dtype),
                        pltpu.VMEM((EXPERT_BLOCK, slab, LANES), a.dtype),
                        pltpu.SemaphoreType.DMA((2,)), pltpu.SemaphoreType.DMA((1,))],
    )
    return pl.pallas_call(
        _dispatch_kernel,
        grid_spec=grid_spec,
        out_shape=jax.ShapeDtypeStruct((n_rows, slab, LANES), a.dtype),
        compiler_params=_cparams(("arbitrary",)),
        name="dispatch",
    )(dest1, dest2, fill, npad, n_blk, a)


def _weight_copies(w_hbms, e, wst_ref, wsem):
    copies = []
    for t, w_hbm in enumerate(w_hbms):
        rows_per = w_hbm.shape[1] // WEIGHT_CHUNKS
        for c in range(WEIGHT_CHUNKS):
            rows = pl.ds(c * rows_per, rows_per)
            copies.append((pltpu.make_async_copy(w_hbm.at[e, rows, :], wst_ref.at[t, rows, :], wsem.at[0]), c % 2))
    return copies


def _load_expert_weights(i, be_ref, nx_ref, w_hbms, wst_ref, wb_refs, wsem, both_queues):
    def start(e):
        for cp, q in _weight_copies(w_hbms, e, wst_ref, wsem):
            cp.start(priority=q if both_queues else WEIGHT_DMA_PRIORITY)

    @pl.when(i == 0)
    def _():
        start(be_ref[0])

    @pl.when((i == 0) | (be_ref[i] != be_ref[jnp.maximum(i - 1, 0)]))
    def _():
        for cp, _ in _weight_copies(w_hbms, be_ref[i], wst_ref, wsem):
            cp.wait()
        for t, wb_ref in enumerate(wb_refs):
            wb_ref[...] = wst_ref[t].astype(BF16)

        @pl.when(nx_ref[i] >= 0)
        def _():
            start(nx_ref[i])


def _expert_a_kernel(be_ref, nx_ref, nb_ref, x_ref, wg_hbm, wu_hbm, hb_ref, wst_ref, wgb_ref, wub_ref, wsem):
    i = pl.program_id(0)
    nb = nb_ref[0]
    rb = hb_ref.shape[0]

    @pl.when(i < nb)
    def _():
        _load_expert_weights(i, be_ref, nx_ref, (wg_hbm, wu_hbm), wst_ref, (wgb_ref, wub_ref), wsem, True)
        x = x_ref[...].reshape(rb, -1)
        hg = jnp.dot(x, wgb_ref[...], preferred_element_type=F32)
        hu = jnp.dot(x, wub_ref[...], preferred_element_type=F32)
        hb_ref[...] = (hg * _sigmoid(hg) * hu).astype(hb_ref.dtype)

    @pl.when(i >= nb)
    def _():
        hb_ref[...] = jnp.zeros_like(hb_ref)


def _expert_a(block_e, next_e, n_blk, xs, wg, wu, *, rb):
    n_rows, slab, _ = xs.shape
    d = slab * LANES
    de = wg.shape[2]
    grid_spec = pltpu.PrefetchScalarGridSpec(
        num_scalar_prefetch=3,
        grid=(n_rows // rb,),
        in_specs=[pl.BlockSpec((rb, slab, LANES), lambda i, be, nx, nb: (jnp.minimum(i, nb[0] - 1), 0, 0)),
                  pl.BlockSpec(memory_space=pl.ANY), pl.BlockSpec(memory_space=pl.ANY)],
        out_specs=pl.BlockSpec((rb, de), lambda i, be, nx, nb: (i, 0)),
        scratch_shapes=[pltpu.VMEM((2, d, de), F32), pltpu.VMEM((d, de), BF16), pltpu.VMEM((d, de), BF16),
                        pltpu.SemaphoreType.DMA((1,))],
    )
    return pl.pallas_call(
        _expert_a_kernel,
        grid_spec=grid_spec,
        out_shape=jax.ShapeDtypeStruct((n_rows, de), BF16),
        compiler_params=_cparams(("arbitrary",)),
        name="expert_up",
    )(block_e, next_e, n_blk, xs, wg, wu)


def _expert_b_kernel(slot_ref, be_ref, nx_ref, nb_ref, hb_ref, wd_hbm, y_hbm, wst_ref, wdb_ref, ys0, ys1, sem, wsem,
                     *, n_tok):
    i = pl.program_id(0)
    n_steps = pl.num_programs(0)
    nb = nb_ref[0]
    rb = hb_ref.shape[0]
    bufs = (ys0, ys1)

    def wait_block(buf, s):
        pltpu.make_async_copy(buf, y_hbm.at[pl.ds(0, rb)], sem.at[s]).wait()

    @pl.when(i == 0)
    def _():
        ys0[...] = jnp.zeros_like(ys0)
        for s in (0, 1):
            spare = pltpu.make_async_copy(ys0, y_hbm.at[pl.ds(2 * n_tok + s * rb, rb)], sem.at[0])
            spare.start()
            spare.wait()

    for s in (0, 1):
        @pl.when((i < nb) & (lax.rem(i, 2) == s))
        def _():
            _load_expert_weights(i, be_ref, nx_ref, (wd_hbm,), wst_ref, (wdb_ref,), wsem, False)

            @pl.when(i >= 2)
            def _():
                wait_block(bufs[s], s)

            y = jnp.dot(hb_ref[...], wdb_ref[...], preferred_element_type=F32)
            bufs[s][...] = y.reshape(bufs[s].shape)
            for r in range(rb):
                v = slot_ref[i * rb + r]
                dst = jnp.where(v >= 0, (v & 1) * n_tok + (v >> 1), 2 * n_tok + s * rb + r)
                pltpu.make_async_copy(bufs[s].at[r], y_hbm.at[dst], sem.at[s]).start()

    @pl.when(i == n_steps - 1)
    def _():
        for s in (0, 1):
            @pl.when(((nb >= 1) & (lax.rem(nb - 1, 2) == s)) | ((nb >= 2) & (lax.rem(nb, 2) == s)))
            def _():
                wait_block(bufs[s], s)


def _expert_b(slot, block_e, next_e, n_blk, hb, wd, *, rb, n_tok):
    n_steps = hb.shape[0] // rb
    de = hb.shape[1]
    d = wd.shape[2]
    slab = d // LANES
    grid_spec = pltpu.PrefetchScalarGridSpec(
        num_scalar_prefetch=4,
        grid=(n_steps,),
        in_specs=[
            pl.BlockSpec((rb, de), lambda i, sl, be, nx, nb: (i, 0)),
            pl.BlockSpec(memory_space=pl.ANY),
        ],
        out_specs=pl.BlockSpec(memory_space=pl.ANY),
        scratch_shapes=[pltpu.VMEM((1, de, d), F32), pltpu.VMEM((de, d), BF16), pltpu.VMEM((rb, slab, LANES), F32),
                        pltpu.VMEM((rb, slab, LANES), F32), pltpu.SemaphoreType.DMA((2,)),
                        pltpu.SemaphoreType.DMA((1,))],
    )
    return pl.pallas_call(
        functools.partial(_expert_b_kernel, n_tok=n_tok),
        grid_spec=grid_spec,
        out_shape=jax.ShapeDtypeStruct((2 * n_tok + 2 * rb, slab, LANES), F32),
        compiler_params=_cparams(("arbitrary",)),
        name="expert_down",
    )(slot, block_e, next_e, n_blk, hb, wd)


def _combine_kernel(y0_ref, y1_ref, h_ref, meta_ref, g_ref, o_ref):
    tm, d = h_ref.shape
    meta = meta_ref[...]
    out = h_ref[...] + meta[:, 4:5] * y0_ref[...].reshape(tm, d) + meta[:, 5:6] * y1_ref[...].reshape(tm, d)
    o_ref[...] = _rms(out, g_ref[...])


def _combine(y, h, meta, g, *, tm):
    m, d = h.shape
    slab = d // LANES
    nt = m // tm
    return pl.pallas_call(
        _combine_kernel,
        grid=(nt,),
        in_specs=[
            pl.BlockSpec((tm, slab, LANES), lambda i: (i, 0, 0)),
            pl.BlockSpec((tm, slab, LANES), lambda i: (nt + i, 0, 0)),
            pl.BlockSpec((tm, d), lambda i: (i, 0)),
            pl.BlockSpec((tm, LANES), lambda i: (i, 0)),
            pl.BlockSpec((1, d), lambda i: (0, 0)),
        ],
        out_specs=pl.BlockSpec((tm, d), lambda i: (i, 0)),
        out_shape=jax.ShapeDtypeStruct((m, d), F32),
        compiler_params=_cparams(("parallel",)),
        name="combine",
    )(y, y, h, meta, g)


def kernel(x, mem, norm_mix_g, w_in, w_gla_alpha_up, b_gla_alpha, gla_out_norm_g, hgrn_lb_logits, hgrn_out_norm_g, w_mix_out, norm_xattn_g, norm_mem_g, w_xattn_q, w_xattn_kv, w_xattn_out, norm_ffn_g, w_router_group, b_router_group, w_router_expert, b_router_expert, w_expert_gate, w_expert_up, w_expert_down, norm_final_g):
    batch, seq, d = x.shape
    n_mem = mem.shape[1]
    m = batch * seq
    depth = norm_mix_g.shape[0]
    h = x.reshape(m, d)
    lb_all = jnp.cumsum(jax.nn.softmax(hgrn_lb_logits.astype(F32), axis=0), axis=0)
    gla_cols = 2 * GLA_HEADS * GLA_DK + 2 * GLA_HEADS * GLA_DV
    lr_rank = w_gla_alpha_up.shape[1]

    for l in range(depth):
        w_t = jnp.swapaxes(w_in[l], 0, 1)
        w_lr_t = jnp.pad(w_t[gla_cols:gla_cols + lr_rank], ((0, LANES - lr_rank), (0, 0))).astype(BF16)
        w_up = jnp.pad(w_gla_alpha_up[l], ((0, LANES - lr_rank), (0, 0))).astype(BF16)
        a_mix, la = _norm_gate(h, norm_mix_g[l][None, :], w_lr_t, w_up, b_gla_alpha[l][None, :], tm=min(m, 512))
        proj = _inproj(a_mix, w_t, tm=min(m, 4096), tn=512, skip_from=gla_cols, skip=lr_rank)
        lb = lb_all[l]
        lbc = jnp.zeros((8, lb.shape[0]), F32).at[0].set(jnp.log(lb)).at[1].set(jnp.log1p(-lb)).at[2].set(1.0 - lb)
        o_gla = _gla(proj, la, gla_out_norm_g[l][None, :], batch=batch, seq=seq, tb=min(seq, 512), hp=4)
        o_h = _hgrn(proj, lbc, hgrn_out_norm_g[l][None, :], batch=batch, seq=seq, tb=min(seq, 512), hp=8)
        h = _mm_res([o_gla, o_h], w_mix_out[l], h, tm=min(m, 512), tn=d, name="mix_out")

        kv = _norm_mm(mem.reshape(batch * n_mem, d), norm_mem_g[l][None, :], w_xattn_kv[l],
                      tm=batch * n_mem, tn=1024, out_dtype=BF16, name="mem_kv")
        q = _norm_mm(h, norm_xattn_g[l][None, :], w_xattn_q[l], tm=min(m, 512), tn=d, out_dtype=BF16,
                     name="xattn_q")
        o = _xattn(q, kv, batch=batch, seq=seq, n_mem=n_mem, tq=min(seq, 512))
        h = _mm_res([o], w_xattn_out[l], h, tm=min(m, 512), tn=d, name="xattn_out")

        wr = jnp.pad(jnp.concatenate([w_router_group[l], w_router_expert[l]], axis=1),
                     ((0, 0), (0, LANES - N_GROUPS - N_EXPERTS)))
        br = jnp.pad(jnp.concatenate([b_router_group[l], b_router_expert[l]]), (0, LANES - N_GROUPS - N_EXPERTS))
        a, meta, cnt = _router(h, norm_ffn_g[l][None, :], wr, br[None, :], tm=min(m, 512))

        e_idx = meta[:, 0:2].astype(jnp.int32)
        rank = meta[:, 2:4].astype(jnp.int32)
        counts = cnt[0, N_GROUPS:N_GROUPS + N_EXPERTS].astype(jnp.int32)
        padded = ((counts + EXPERT_BLOCK - 1) // EXPERT_BLOCK) * EXPERT_BLOCK
        pad_end = jnp.cumsum(padded)
        dest = (pad_end - padded)[e_idx] + rank
        n_rows = 2 * m + N_EXPERTS * EXPERT_BLOCK
        n_blocks = n_rows // EXPERT_BLOCK
        n_blk = (pad_end[-1:] // EXPERT_BLOCK).astype(jnp.int32)
        blk_first = jnp.arange(n_blocks, dtype=jnp.int32) * EXPERT_BLOCK
        block_e = jnp.minimum(jnp.sum((pad_end[None, :] <= blk_first[:, None]).astype(jnp.int32), axis=1),
                              N_EXPERTS - 1)
        assign = 2 * jnp.arange(m, dtype=jnp.int32)[:, None] + jnp.arange(2, dtype=jnp.int32)[None, :]
        slot = jnp.full((n_rows,), -1, jnp.int32).at[dest.reshape(-1)].set(assign.reshape(-1))
        after = (pad_end // EXPERT_BLOCK)[block_e]
        next_e = jnp.where(after < n_blk[0], block_e[jnp.minimum(after, n_blocks - 1)], -1).astype(jnp.int32)

        xs = _dispatch(dest[:, 0], dest[:, 1], pad_end - padded + counts, padded - counts, n_blk, a, n_rows=n_rows,
                       tm=min(m, 256))
        hb = _expert_a(block_e, next_e, n_blk, xs, w_expert_gate[l], w_expert_up[l], rb=EXPERT_BLOCK)
        y = _expert_b(slot, block_e, next_e, n_blk, hb, w_expert_down[l], rb=EXPERT_BLOCK, n_tok=m)
        last = l == depth - 1
        g_fin = norm_final_g[None, :] if last else jnp.ones((1, d), F32)
        h = _combine(y, h, meta, g_fin, tm=min(m, 256))
        assert last, "the combine kernel fuses the final rmsnorm; deeper stacks need an un-normalised variant"

    return h.reshape(batch, seq, d)
```

```python
import functools

import jax
import jax.numpy as jnp
from jax import lax
from jax.experimental import pallas as pl
from jax.experimental.pallas import tpu as pltpu

F32 = jnp.float32
BF16 = jnp.bfloat16

EPS = 1e-6
CHUNK = 64
LANES = 128
GLA_HEADS, GLA_DK, GLA_DV = 4, 128, 256
HGRN_HEADS, HGRN_DH = 8, 128
XA_HEADS, XA_DH = 4, 512
N_GROUPS, EXPERTS_PER_GROUP, N_EXPERTS = 4, 8, 32
EXPERT_BLOCK = 256
WEIGHT_CHUNKS = 8
WEIGHT_DMA_PRIORITY = 1
VMEM_LIMIT = 56 * 1024 * 1024
EXPERT_VMEM_LIMIT = 60 * 1024 * 1024


def _cparams(sem, vmem=VMEM_LIMIT):
    return pltpu.CompilerParams(dimension_semantics=sem, vmem_limit_bytes=vmem)


def _log1pexp_neg(t):
    return jnp.log(1.0 + jnp.exp(-t))


def _log_sigmoid(z):
    return jnp.minimum(z, 0.0) - _log1pexp_neg(jnp.abs(z))


def _sigmoid(z):
    return 1.0 / (1.0 + jnp.exp(-z))


def _rms(x, g):
    return x * lax.rsqrt(jnp.mean(x * x, axis=-1, keepdims=True) + EPS) * g


def _nt_dot(x, y):
    return lax.dot_general(x, y, (((1,), (1,)), ((), ())), preferred_element_type=F32)


NORM_ROWS = 256


def _for_row_chunks(n_rows, fn):
    step = min(NORM_ROWS, n_rows)

    def body(ci, carry):
        fn(pl.ds(pl.multiple_of(ci * step, step), step))
        return carry

    lax.fori_loop(0, n_rows // step, body, 0)


def _norm_gate_kernel(x_ref, g_ref, wlr_ref, wup_ref, bal_ref, a_ref, la_ref):
    a = _rms(x_ref[...], g_ref[...]).astype(BF16)
    a_ref[...] = a
    lr = _nt_dot(a, wlr_ref[...])
    z = jnp.dot(lr.astype(BF16), wup_ref[...], preferred_element_type=F32) + bal_ref[...]
    la_ref[...] = _log_sigmoid(z) * (1.0 / 16.0)


def _norm_gate(x, g, w_lr_t, w_up, b_al, *, tm):
    m, d = x.shape
    nk = w_up.shape[1]
    return pl.pallas_call(
        _norm_gate_kernel,
        grid=(m // tm,),
        in_specs=[
            pl.BlockSpec((tm, d), lambda i: (i, 0)),
            pl.BlockSpec((1, d), lambda i: (0, 0)),
            pl.BlockSpec((LANES, d), lambda i: (0, 0)),
            pl.BlockSpec((LANES, nk), lambda i: (0, 0)),
            pl.BlockSpec((1, nk), lambda i: (0, 0)),
        ],
        out_specs=[pl.BlockSpec((tm, d), lambda i: (i, 0)), pl.BlockSpec((tm, nk), lambda i: (i, 0))],
        out_shape=[jax.ShapeDtypeStruct((m, d), BF16), jax.ShapeDtypeStruct((m, nk), F32)],
        compiler_params=_cparams(("parallel",)),
        name="norm_gate",
    )(x, g, w_lr_t, w_up, b_al)


def _inproj_kernel(a_ref, w_ref, proj_ref):
    proj_ref[...] = _nt_dot(a_ref[...], w_ref[...].astype(BF16))


def _inproj(a, w_t, *, tm, tn, skip_from, skip):
    m, d = a.shape
    n = w_t.shape[0] - skip
    first_after = skip_from // tn

    def w_rows(i, j):
        return (pl.multiple_of(j * tn + jnp.where(j >= first_after, skip, 0), 8), 0)

    return pl.pallas_call(
        _inproj_kernel,
        grid=(m // tm, n // tn),
        in_specs=[
            pl.BlockSpec((tm, d), lambda i, j: (i, 0), pipeline_mode=pl.Buffered(1)),
            pl.BlockSpec((pl.Element(tn), pl.Element(d)), w_rows),
        ],
        out_specs=pl.BlockSpec((tm, tn), lambda i, j: (i, j)),
        out_shape=jax.ShapeDtypeStruct((m, n), F32),
        compiler_params=_cparams(("parallel", "arbitrary")),
        name="inproj",
    )(a, w_t)


def _weight_spec(k, n, tn):
    if tn == n:
        return pl.BlockSpec((k, n), lambda i, j: (0, 0), pipeline_mode=pl.Buffered(1))
    return pl.BlockSpec((k, tn), lambda i, j: (0, j))


def _cast_weight(w_ref, wb_ref, resident):
    if resident:
        @pl.when((pl.program_id(0) == 0) & (pl.program_id(1) == 0))
        def _():
            wb_ref[...] = w_ref[...].astype(BF16)
    else:
        wb_ref[...] = w_ref[...].astype(BF16)


def _norm_mm_kernel(x_ref, g_ref, w_ref, o_ref, a_ref, wb_ref, *, resident):
    @pl.when(pl.program_id(1) == 0)
    def _():
        def rows_fn(rows):
            a_ref[rows, :] = _rms(x_ref[rows, :], g_ref[...]).astype(BF16)

        _for_row_chunks(x_ref.shape[0], rows_fn)

    _cast_weight(w_ref, wb_ref, resident)
    o_ref[...] = jnp.dot(a_ref[...], wb_ref[...], preferred_element_type=F32).astype(o_ref.dtype)


def _norm_mm(x, g, w, *, tm, tn, out_dtype, name):
    m, d = x.shape
    n = w.shape[1]
    return pl.pallas_call(
        functools.partial(_norm_mm_kernel, resident=tn == n),
        grid=(m // tm, n // tn),
        in_specs=[
            pl.BlockSpec((tm, d), lambda i, j: (i, 0)),
            pl.BlockSpec((1, d), lambda i, j: (0, 0)),
            _weight_spec(d, n, tn),
        ],
        out_specs=pl.BlockSpec((tm, tn), lambda i, j: (i, j)),
        out_shape=jax.ShapeDtypeStruct((m, n), out_dtype),
        scratch_shapes=[pltpu.VMEM((tm, d), BF16), pltpu.VMEM((d, tn), BF16)],
        compiler_params=_cparams(("arbitrary", "arbitrary")),
        name=name,
    )(x, g, w)


def _mm_res_kernel(*refs, n_lhs, resident):
    lhs = refs[:n_lhs]
    w_ref, res_ref, o_ref, wb_ref = refs[n_lhs:]
    _cast_weight(w_ref, wb_ref, resident)
    acc = res_ref[...]
    k0 = 0
    for l_ref in lhs:
        kp = l_ref.shape[1]
        acc = acc + jnp.dot(l_ref[...], wb_ref[k0:k0 + kp, :], preferred_element_type=F32)
        k0 += kp
    o_ref[...] = acc


def _mm_res(lhs_parts, w, res, *, tm, tn, name):
    m, n = res.shape
    k = w.shape[0]
    n_lhs = len(lhs_parts)
    in_specs = [pl.BlockSpec((tm, p.shape[1]), lambda i, j: (i, 0)) for p in lhs_parts]
    in_specs += [_weight_spec(k, n, tn), pl.BlockSpec((tm, tn), lambda i, j: (i, j))]
    return pl.pallas_call(
        functools.partial(_mm_res_kernel, n_lhs=n_lhs, resident=tn == n),
        grid=(m // tm, n // tn),
        in_specs=in_specs,
        out_specs=pl.BlockSpec((tm, tn), lambda i, j: (i, j)),
        out_shape=jax.ShapeDtypeStruct((m, n), F32),
        scratch_shapes=[pltpu.VMEM((k, tn), BF16)],
        compiler_params=_cparams(("arbitrary", "arbitrary")),
        name=name,
    )(*lhs_parts, w, res)


LEVELS = (32, 16, 8, 4, 2, 1)
LOG2E = 1.4426950408889634


def _split3_bf16(x):
    def top(v):
        bits = lax.bitcast_convert_type(v, jnp.uint32) & jnp.uint32(0xFFFF0000)
        return lax.bitcast_convert_type(bits, F32)
    hi = top(x)
    r1 = x - hi
    mid = top(r1)
    lo = r1 - mid
    return hi.astype(BF16), mid.astype(BF16), lo.astype(BF16)


def _mix_chunks(qs, ks, vs, las, st_refs):
    heads = range(len(qs))
    c, dk = qs[0].shape
    row = lax.broadcasted_iota(jnp.int32, (c, c), 0)
    col = lax.broadcasted_iota(jnp.int32, (c, c), 1)
    rowk = lax.broadcasted_iota(jnp.int32, (c, dk), 0)
    xor = jnp.bitwise_xor(row, col)

    las = [la * LOG2E for la in las]

    tri = jnp.where(col <= row, 1.0, 0.0).astype(BF16)
    b3 = [jnp.dot(tri, jnp.concatenate(_split3_bf16(la), axis=1), preferred_element_type=F32) for la in las]
    bs = [(t[:, :dk] + t[:, dk:2 * dk]) + t[:, 2 * dk:] for t in b3]

    def neg_dist(w, b, la):
        if w >= 4:
            parts = [jnp.broadcast_to(b[base + w - 1:base + w, :], (2 * w, dk)) for base in range(0, c, 2 * w)]
            m = parts[0] if len(parts) == 1 else jnp.concatenate(parts, axis=0)
            return -jnp.abs(b - m)
        if w == 2:
            r4 = jnp.bitwise_and(rowk, 3)
            nxt = pltpu.roll(la, c - 1, 0)
            prv = pltpu.roll(la, 1, 0)
            return jnp.where(r4 == 0, nxt, jnp.where(r4 == 1, 0.0, jnp.where(r4 == 2, la, la + prv)))
        return jnp.where(jnp.bitwise_and(rowk, 1) == 1, la, 0.0)

    acc = [_nt_dot(qs[h].astype(BF16), ks[h].astype(BF16)) for h in heads]
    for w in reversed(LEVELS):
        upper = jnp.bitwise_and(rowk, w) != 0
        xs = [(jnp.where(upper, qs[h], ks[h]) * jnp.exp2(neg_dist(w, bs[h], las[h]))).astype(BF16) for h in heads]
        gs = [_nt_dot(x, x) for x in xs]
        acc = [jnp.where(xor >= w, gs[h], acc[h]) for h in heads]
    ab = [jnp.where(col <= row, a, 0.0).astype(BF16) for a in acc]

    sts = [st_refs[h][...] for h in heads]
    qx = [(qs[h] * jnp.exp2(bs[h])).astype(BF16) for h in heads]
    b_last = [b[c - 1:c, :] for b in bs]
    kx = [(ks[h] * jnp.exp2(b_last[h] - bs[h])).astype(BF16) for h in heads]
    vb = [v.astype(BF16) for v in vs]
    outs = [jnp.dot(ab[h], vb[h], preferred_element_type=F32) + _nt_dot(qx[h], sts[h].astype(BF16)) for h in heads]
    for h in heads:
        st_refs[h][...] = jnp.exp2(b_last[h]) * sts[h] + lax.dot_general(
            vb[h], kx[h], (((0,), (0,)), ((), ())), preferred_element_type=F32)
    return outs


def _gated_norm(o, g, gn):
    return _rms(o, gn) * (g * _sigmoid(g))


def _gla_kernel(q_ref, k_ref, v_ref, g_ref, la_ref, gn_ref, o_ref, st_ref, *, n_chunk):
    @pl.when(pl.program_id(2) == 0)
    def _():
        st_ref[...] = jnp.zeros_like(st_ref)

    dk, dv = GLA_DK, GLA_DV
    heads = range(st_ref.shape[0])

    def body(ci, carry):
        rows = pl.ds(pl.multiple_of(ci * CHUNK, CHUNK), CHUNK)
        kc = [slice(h * dk, (h + 1) * dk) for h in heads]
        vc = [slice(h * dv, (h + 1) * dv) for h in heads]
        outs = _mix_chunks([q_ref[rows, kc[h]] * (dk ** -0.5) for h in heads], [k_ref[rows, kc[h]] for h in heads],
                           [v_ref[rows, vc[h]] for h in heads], [la_ref[rows, kc[h]] for h in heads],
                           [st_ref.at[h] for h in heads])
        for h in heads:
            o_ref[rows, vc[h]] = _gated_norm(outs[h], g_ref[rows, vc[h]], gn_ref[...]).astype(o_ref.dtype)
        return carry

    lax.fori_loop(0, n_chunk, body, 0)


def _gla(proj, la, gn, *, batch, seq, tb, hp):
    m = proj.shape[0]
    nt = seq // tb
    wk, wv = hp * GLA_DK, hp * GLA_DV
    n_grp = GLA_HEADS // hp
    spec = lambda width, first: pl.BlockSpec((tb, width), lambda b, p, t: (b * nt + t, first + p))
    return pl.pallas_call(
        functools.partial(_gla_kernel, n_chunk=tb // CHUNK),
        grid=(batch, n_grp, nt),
        in_specs=[spec(wk, 0), spec(wk, n_grp), spec(wv, n_grp), spec(wv, 2 * n_grp), spec(wk, 0),
                  pl.BlockSpec((1, GLA_DV), lambda b, p, t: (0, 0))],
        out_specs=spec(wv, 0),
        out_shape=jax.ShapeDtypeStruct((m, GLA_HEADS * GLA_DV), BF16),
        scratch_shapes=[pltpu.VMEM((hp, GLA_DV, GLA_DK), F32)],
        compiler_params=_cparams(("parallel", "parallel", "arbitrary")),
        name="gla",
    )(proj, proj, proj, proj, la, gn)


def _hgrn_kernel(q_ref, f_ref, i_ref, g_ref, lb_ref, gn_ref, o_ref, st_ref, *, n_chunk):
    @pl.when(pl.program_id(2) == 0)
    def _():
        st_ref[...] = jnp.zeros_like(st_ref)

    dh = HGRN_DH
    heads = range(st_ref.shape[0])

    def body(ci, carry):
        rows = pl.ds(pl.multiple_of(ci * CHUNK, CHUNK), CHUNK)
        hc = [slice(h * dh, (h + 1) * dh) for h in heads]
        qs, ks, las = [], [], []
        for h in heads:
            log_lb, log_1mlb, one_m_lb = lb_ref[0:1, hc[h]], lb_ref[1:2, hc[h]], lb_ref[2:3, hc[h]]
            hq = q_ref[rows, hc[h]]
            z = f_ref[rows, hc[h]]
            x1 = log_1mlb + _log_sigmoid(z)
            las.append(jnp.maximum(log_lb, x1) + _log1pexp_neg(jnp.abs(log_lb - x1)))
            qs.append(hq * _sigmoid(hq))
            ks.append(one_m_lb * _sigmoid(-z))
        outs = _mix_chunks(qs, ks, [i_ref[rows, hc[h]] for h in heads], las, [st_ref.at[h] for h in heads])
        for h in heads:
            o_ref[rows, hc[h]] = _gated_norm(outs[h], g_ref[rows, hc[h]], gn_ref[...]).astype(o_ref.dtype)
        return carry

    lax.fori_loop(0, n_chunk, body, 0)


def _hgrn(proj, lbc, gn, *, batch, seq, tb, hp):
    m = proj.shape[0]
    nt = seq // tb
    width = hp * HGRN_DH
    n_grp = HGRN_HEADS // hp
    first = 3072 // width
    spec = lambda seg: pl.BlockSpec((tb, width), lambda b, p, t: (b * nt + t, first + seg * n_grp + p))
    return pl.pallas_call(
        functools.partial(_hgrn_kernel, n_chunk=tb // CHUNK),
        grid=(batch, n_grp, nt),
        in_specs=[spec(0), spec(1), spec(2), spec(3),
                  pl.BlockSpec((8, width), lambda b, p, t: (0, p)),
                  pl.BlockSpec((1, HGRN_DH), lambda b, p, t: (0, 0))],
        out_specs=pl.BlockSpec((tb, width), lambda b, p, t: (b * nt + t, p)),
        out_shape=jax.ShapeDtypeStruct((m, HGRN_HEADS * HGRN_DH), BF16),
        scratch_shapes=[pltpu.VMEM((hp, HGRN_DH, HGRN_DH), F32)],
        compiler_params=_cparams(("parallel", "parallel", "arbitrary")),
        name="hgrn",
    )(proj, proj, proj, proj, lbc, gn)


def _xattn_kernel(q_ref, k_ref, v_ref, o_ref):
    for h in range(XA_HEADS):
        cols = slice(h * XA_DH, (h + 1) * XA_DH)
        s = lax.dot_general(q_ref[:, cols], k_ref[:, cols], (((1,), (1,)), ((), ())),
                            preferred_element_type=F32) * (XA_DH ** -0.5)
        p = jnp.exp(s - jnp.max(s, axis=-1, keepdims=True))
        p = p / jnp.sum(p, axis=-1, keepdims=True)
        o_ref[:, cols] = jnp.dot(p.astype(BF16), v_ref[:, cols], preferred_element_type=F32).astype(o_ref.dtype)


def _xattn(q, kv, *, batch, seq, n_mem, tq):
    m, d = q.shape
    nt = seq // tq
    return pl.pallas_call(
        _xattn_kernel,
        grid=(batch, nt),
        in_specs=[
            pl.BlockSpec((tq, d), lambda b, t: (b * nt + t, 0)),
            pl.BlockSpec((n_mem, d), lambda b, t: (b, 0)),
            pl.BlockSpec((n_mem, d), lambda b, t: (b, 1)),
        ],
        out_specs=pl.BlockSpec((tq, d), lambda b, t: (b * nt + t, 0)),
        out_shape=jax.ShapeDtypeStruct((m, d), BF16),
        compiler_params=_cparams(("parallel", "arbitrary")),
        name="xattn",
    )(q, kv, kv)


def _router_kernel(h_ref, g_ref, wr_ref, br_ref, a_ref, meta_ref, cnt_ref, carry_ref):
    tm = h_ref.shape[0]

    @pl.when(pl.program_id(0) == 0)
    def _():
        carry_ref[...] = jnp.zeros_like(carry_ref)

    a = _rms(h_ref[...], g_ref[...])
    a_ref[...] = a.astype(BF16).reshape(a_ref.shape)
    a_hi = a.astype(BF16)
    a_lo = (a - a_hi.astype(F32)).astype(BF16)
    w = wr_ref[...]
    w_hi = w.astype(BF16)
    w_lo = (w - w_hi.astype(F32)).astype(BF16)
    logits = (jnp.dot(a_hi, w_hi, preferred_element_type=F32) + jnp.dot(a_lo, w_hi, preferred_element_type=F32)
              + jnp.dot(a_hi, w_lo, preferred_element_type=F32)) + br_ref[...]
    lane = lax.broadcasted_iota(jnp.int32, (tm, LANES), 1)
    lane_f = lane.astype(F32)
    neg = -jnp.inf

    gl = jnp.where(lane < N_GROUPS, logits, neg)
    gmax = jnp.max(gl, axis=1, keepdims=True)
    gidx = jnp.min(jnp.where(gl == gmax, lane_f, float(LANES)), axis=1, keepdims=True)
    p_group = 1.0 / jnp.sum(jnp.exp(gl - gmax), axis=1, keepdims=True)
    lo = float(N_GROUPS) + gidx * float(EXPERTS_PER_GROUP)
    el = jnp.where((lane_f >= lo) & (lane_f < lo + float(EXPERTS_PER_GROUP)), logits, neg)
    v1 = jnp.max(el, axis=1, keepdims=True)
    i1 = jnp.min(jnp.where(el == v1, lane_f, float(LANES)), axis=1, keepdims=True)
    el2 = jnp.where(lane_f == i1, neg, el)
    v2 = jnp.max(el2, axis=1, keepdims=True)
    i2 = jnp.min(jnp.where(el2 == v2, lane_f, float(LANES)), axis=1, keepdims=True)
    t = jnp.exp(v2 - v1)
    g1 = p_group / (1.0 + t)
    g2 = p_group * t / (1.0 + t)

    hit1 = lane_f == i1
    hit2 = lane_f == i2
    onehot = jnp.where(hit1 | hit2, 1.0, 0.0)
    row = lax.broadcasted_iota(jnp.int32, (tm, tm), 0)
    col = lax.broadcasted_iota(jnp.int32, (tm, tm), 1)
    strict = jnp.where(col < row, 1.0, 0.0).astype(BF16)
    before = jnp.dot(strict, onehot.astype(BF16), preferred_element_type=F32) + carry_ref[0:1, :]
    r1 = jnp.sum(jnp.where(hit1, before, 0.0), axis=1, keepdims=True)
    r2 = jnp.sum(jnp.where(hit2, before, 0.0), axis=1, keepdims=True)
    carry_ref[...] = carry_ref[...] + jnp.sum(onehot, axis=0, keepdims=True)
    cnt_ref[...] = carry_ref[...]

    meta = jnp.zeros((tm, LANES), F32)
    for idx, val in enumerate((i1 - float(N_GROUPS), i2 - float(N_GROUPS), r1, r2, g1, g2)):
        meta = jnp.where(lane == idx, val, meta)
    meta_ref[...] = meta


def _router(h, g, wr, br, *, tm):
    m, d = h.shape
    slab = d // LANES
    return pl.pallas_call(
        _router_kernel,
        grid=(m // tm,),
        in_specs=[
            pl.BlockSpec((tm, d), lambda i: (i, 0)),
            pl.BlockSpec((1, d), lambda i: (0, 0)),
            pl.BlockSpec((d, LANES), lambda i: (0, 0)),
            pl.BlockSpec((1, LANES), lambda i: (0, 0)),
        ],
        out_specs=[
            pl.BlockSpec((tm, slab, LANES), lambda i: (i, 0, 0)),
            pl.BlockSpec((tm, LANES), lambda i: (i, 0)),
            pl.BlockSpec((8, LANES), lambda i: (0, 0)),
        ],
        out_shape=[jax.ShapeDtypeStruct((m, slab, LANES), BF16), jax.ShapeDtypeStruct((m, LANES), F32),
                   jax.ShapeDtypeStruct((8, LANES), F32)],
        scratch_shapes=[pltpu.VMEM((8, LANES), F32)],
        compiler_params=_cparams(("arbitrary",)),
        name="router",
    )(h, g, wr, br)


PAD_PIECES = tuple(EXPERT_BLOCK >> (b + 1) for b in range(EXPERT_BLOCK.bit_length() - 1))


def _dispatch_kernel(d1_ref, d2_ref, fill_ref, npad_ref, nb_ref, a_ref, xs_hbm, st0, st1, zbuf, sem, zsem):
    i = pl.program_id(0)
    n_steps = pl.num_programs(0)
    tm = a_ref.shape[0]
    bufs = (st0, st1)
    rb = zbuf.shape[0]
    n_blocks = xs_hbm.shape[0] // rb
    min_blocks = (2 * tm * n_steps) // rb

    def zero_copies():
        out = []
        for e in range(N_EXPERTS):
            p = npad_ref[e]
            for piece in PAD_PIECES:
                out.append((p & piece != 0, pltpu.make_async_copy(
                    zbuf.at[pl.ds(0, piece)], xs_hbm.at[pl.ds(fill_ref[e] + (p & -(2 * piece)), piece)], zsem.at[0])))
        for b in range(min_blocks, n_blocks):
            out.append((b >= nb_ref[0], pltpu.make_async_copy(zbuf, xs_hbm.at[pl.ds(b * rb, rb)], zsem.at[0])))
        return out

    def wait_rows(s):
        for _ in range(2):
            pltpu.make_async_copy(bufs[s], xs_hbm.at[pl.ds(0, tm)], sem.at[s]).wait()

    @pl.when(i == 0)
    def _():
        zbuf[...] = jnp.zeros_like(zbuf)
        for cond, cp in zero_copies():
            @pl.when(cond)
            def _():
                cp.start()

    for s in (0, 1):
        @pl.when(lax.rem(i, 2) == s)
        def _():
            @pl.when(i >= 2)
            def _():
                wait_rows(s)

            bufs[s][...] = a_ref[...]
            for r in range(tm):
                pltpu.make_async_copy(bufs[s].at[r], xs_hbm.at[d1_ref[i * tm + r]], sem.at[s]).start(priority=0)
                pltpu.make_async_copy(bufs[s].at[r], xs_hbm.at[d2_ref[i * tm + r]], sem.at[s]).start(priority=1)

    @pl.when(i == n_steps - 1)
    def _():
        for s in (0, 1):
            @pl.when((lax.rem(n_steps - 1, 2) == s) | ((n_steps >= 2) & (lax.rem(n_steps, 2) == s)))
            def _():
                wait_rows(s)
        for cond, cp in zero_copies():
            @pl.when(cond)
            def _():
                cp.wait()


def _dispatch(dest1, dest2, fill, npad, n_blk, a, *, n_rows, tm):
    m, slab, _ = a.shape
    grid_spec = pltpu.PrefetchScalarGridSpec(
        num_scalar_prefetch=5,
        grid=(m // tm,),
        in_specs=[pl.BlockSpec((tm, slab, LANES), lambda i, d1, d2, fl, npd, nb: (i, 0, 0))],
        out_specs=pl.BlockSpec(memory_space=pl.ANY),
        scratch_shapes=[pltpu.VMEM((tm, slab, LANES), a.dtype), pltpu.VMEM((tm, slab, LANES), a.dtype),
                        pltpu.VMEM((EXPERT_BLOCK, slab, LANES), a.dtype),
                        pltpu.SemaphoreType.DMA((2,)), pltpu.SemaphoreType.DMA((1,))],
    )
    return pl.pallas_call(
        _dispatch_kernel,
        grid_spec=grid_spec,
        out_shape=jax.ShapeDtypeStruct((n_rows, slab, LANES), a.dtype),
        compiler_params=_cparams(("arbitrary",)),
        name="dispatch",
    )(dest1, dest2, fill, npad, n_blk, a)


def _weight_copies(w_hbms, e, wst_refs, wsem):
    copies = []
    for w_hbm, wst_ref in zip(w_hbms, wst_refs):
        rows_per = w_hbm.shape[1] // WEIGHT_CHUNKS
        for c in range(WEIGHT_CHUNKS):
            rows = pl.ds(c * rows_per, rows_per)
            copies.append(pltpu.make_async_copy(w_hbm.at[e, rows, :], wst_ref.at[rows, :], wsem.at[0]))
    return copies


def _load_expert_weights(i, be_ref, nx_ref, w_hbms, wst_refs, wb_refs, wsem):
    def start(e):
        for cp in _weight_copies(w_hbms, e, wst_refs, wsem):
            cp.start(priority=WEIGHT_DMA_PRIORITY)

    @pl.when(i == 0)
    def _():
        start(be_ref[0])

    @pl.when((i == 0) | (be_ref[i] != be_ref[jnp.maximum(i - 1, 0)]))
    def _():
        for cp in _weight_copies(w_hbms, be_ref[i], wst_refs, wsem):
            cp.wait()
        for wst_ref, wb_ref in zip(wst_refs, wb_refs):
            _for_row_chunks(wst_ref.shape[0], lambda rows: wb_ref.__setitem__((rows, slice(None)),
                                                                              wst_ref[rows, :].astype(BF16)))

        @pl.when(nx_ref[i] >= 0)
        def _():
            start(nx_ref[i])


def _expert_kernel(slot_ref, be_ref, nx_ref, nb_ref, x_ref, wg_hbm, wu_hbm, wd_hbm, y_hbm, wsg, wsu, wsd, wgb, wub, wdb,
                   ys0, ys1, sem, wsem, *, n_tok):
    i = pl.program_id(0)
    n_steps = pl.num_programs(0)
    nb = nb_ref[0]
    rb = x_ref.shape[0]
    bufs = (ys0, ys1)

    def wait_block(buf, s):
        pltpu.make_async_copy(buf, y_hbm.at[pl.ds(0, rb)], sem.at[s]).wait()

    @pl.when(i == 0)
    def _():
        ys0[...] = jnp.zeros_like(ys0)
        for s in (0, 1):
            spare = pltpu.make_async_copy(ys0, y_hbm.at[pl.ds(2 * n_tok + s * rb, rb)], sem.at[0])
            spare.start()
            spare.wait()

    for s in (0, 1):
        @pl.when((i < nb) & (lax.rem(i, 2) == s))
        def _():
            _load_expert_weights(i, be_ref, nx_ref, (wg_hbm, wu_hbm, wd_hbm), (wsg, wsu, wsd), (wgb, wub, wdb), wsem)

            @pl.when(i >= 2)
            def _():
                wait_block(bufs[s], s)

            x = x_ref[...].reshape(rb, -1)
            hg = jnp.dot(x, wgb[...], preferred_element_type=F32)
            hu = jnp.dot(x, wub[...], preferred_element_type=F32)
            hb = (hg * _sigmoid(hg) * hu).astype(BF16)
            y = jnp.dot(hb, wdb[...], preferred_element_type=F32)
            bufs[s][...] = y.reshape(bufs[s].shape)
            for r in range(rb):
                v = slot_ref[i * rb + r]
                dst = jnp.where(v >= 0, (v & 1) * n_tok + (v >> 1), 2 * n_tok + s * rb + r)
                pltpu.make_async_copy(bufs[s].at[r], y_hbm.at[dst], sem.at[s]).start()

    @pl.when(i == n_steps - 1)
    def _():
        for s in (0, 1):
            @pl.when(((nb >= 1) & (lax.rem(nb - 1, 2) == s)) | ((nb >= 2) & (lax.rem(nb, 2) == s)))
            def _():
                wait_block(bufs[s], s)


def _experts(slot, block_e, next_e, n_blk, xs, wg, wu, wd, *, rb, n_tok):
    n_rows, slab, _ = xs.shape
    d = slab * LANES
    de = wg.shape[2]
    grid_spec = pltpu.PrefetchScalarGridSpec(
        num_scalar_prefetch=4,
        grid=(n_rows // rb,),
        in_specs=[pl.BlockSpec((rb, slab, LANES), lambda i, sl, be, nx, nb: (jnp.minimum(i, nb[0] - 1), 0, 0))]
        + [pl.BlockSpec(memory_space=pl.ANY)] * 3,
        out_specs=pl.BlockSpec(memory_space=pl.ANY),
        scratch_shapes=[pltpu.VMEM((d, de), F32), pltpu.VMEM((d, de), F32), pltpu.VMEM((de, d), F32),
                        pltpu.VMEM((d, de), BF16), pltpu.VMEM((d, de), BF16), pltpu.VMEM((de, d), BF16),
                        pltpu.VMEM((rb, slab, LANES), F32), pltpu.VMEM((rb, slab, LANES), F32),
                        pltpu.SemaphoreType.DMA((2,)), pltpu.SemaphoreType.DMA((1,))],
    )
    return pl.pallas_call(
        functools.partial(_expert_kernel, n_tok=n_tok),
        grid_spec=grid_spec,
        out_shape=jax.ShapeDtypeStruct((2 * n_tok + 2 * rb, slab, LANES), F32),
        compiler_params=_cparams(("arbitrary",), vmem=EXPERT_VMEM_LIMIT),
        name="experts",
    )(slot, block_e, next_e, n_blk, xs, wg, wu, wd)


def _combine_kernel(y0_ref, y1_ref, h_ref, meta_ref, g_ref, o_ref):
    tm, d = h_ref.shape
    meta = meta_ref[...]
    out = h_ref[...] + meta[:, 4:5] * y0_ref[...].reshape(tm, d) + meta[:, 5:6] * y1_ref[...].reshape(tm, d)
    o_ref[...] = _rms(out, g_ref[...])


def _combine(y, h, meta, g, *, tm):
    m, d = h.shape
    slab = d // LANES
    nt = m // tm
    return pl.pallas_call(
        _combine_kernel,
        grid=(nt,),
        in_specs=[
            pl.BlockSpec((tm, slab, LANES), lambda i: (i, 0, 0)),
            pl.BlockSpec((tm, slab, LANES), lambda i: (nt + i, 0, 0)),
            pl.BlockSpec((tm, d), lambda i: (i, 0)),
            pl.BlockSpec((tm, LANES), lambda i: (i, 0)),
            pl.BlockSpec((1, d), lambda i: (0, 0)),
        ],
        out_specs=pl.BlockSpec((tm, d), lambda i: (i, 0)),
        out_shape=jax.ShapeDtypeStruct((m, d), F32),
        compiler_params=_cparams(("parallel",)),
        name="combine",
    )(y, y, h, meta, g)


def kernel(x, mem, norm_mix_g, w_in, w_gla_alpha_up, b_gla_alpha, gla_out_norm_g, hgrn_lb_logits, hgrn_out_norm_g, w_mix_out, norm_xattn_g, norm_mem_g, w_xattn_q, w_xattn_kv, w_xattn_out, norm_ffn_g, w_router_group, b_router_group, w_router_expert, b_router_expert, w_expert_gate, w_expert_up, w_expert_down, norm_final_g):
    batch, seq, d = x.shape
    n_mem = mem.shape[1]
    m = batch * seq
    depth = norm_mix_g.shape[0]
    h = x.reshape(m, d)
    lb_all = jnp.cumsum(jax.nn.softmax(hgrn_lb_logits.astype(F32), axis=0), axis=0)
    gla_cols = 2 * GLA_HEADS * GLA_DK + 2 * GLA_HEADS * GLA_DV
    lr_rank = w_gla_alpha_up.shape[1]

    for l in range(depth):
        w_t = jnp.swapaxes(w_in[l], 0, 1)
        w_lr_t = jnp.pad(w_t[gla_cols:gla_cols + lr_rank], ((0, LANES - lr_rank), (0, 0))).astype(BF16)
        w_up = jnp.pad(w_gla_alpha_up[l], ((0, LANES - lr_rank), (0, 0))).astype(BF16)
        a_mix, la = _norm_gate(h, norm_mix_g[l][None, :], w_lr_t, w_up, b_gla_alpha[l][None, :], tm=min(m, 512))
        proj = _inproj(a_mix, w_t, tm=min(m, 4096), tn=512, skip_from=gla_cols, skip=lr_rank)
        lb = lb_all[l]
        lbc = jnp.zeros((8, lb.shape[0]), F32).at[0].set(jnp.log(lb)).at[1].set(jnp.log1p(-lb)).at[2].set(1.0 - lb)
        o_gla = _gla(proj, la, gla_out_norm_g[l][None, :], batch=batch, seq=seq, tb=min(seq, 512), hp=4)
        o_h = _hgrn(proj, lbc, hgrn_out_norm_g[l][None, :], batch=batch, seq=seq, tb=min(seq, 512), hp=8)
        h = _mm_res([o_gla, o_h], w_mix_out[l], h, tm=min(m, 512), tn=d, name="mix_out")

        kv = _norm_mm(mem.reshape(batch * n_mem, d), norm_mem_g[l][None, :], w_xattn_kv[l],
                      tm=batch * n_mem, tn=1024, out_dtype=BF16, name="mem_kv")
        q = _norm_mm(h, norm_xattn_g[l][None, :], w_xattn_q[l], tm=min(m, 512), tn=d, out_dtype=BF16,
                     name="xattn_q")
        o = _xattn(q, kv, batch=batch, seq=seq, n_mem=n_mem, tq=min(seq, 512))
        h = _mm_res([o], w_xattn_out[l], h, tm=min(m, 512), tn=d, name="xattn_out")

        wr = jnp.pad(jnp.concatenate([w_router_group[l], w_router_expert[l]], axis=1),
                     ((0, 0), (0, LANES - N_GROUPS - N_EXPERTS)))
        br = jnp.pad(jnp.concatenate([b_router_group[l], b_router_expert[l]]), (0, LANES - N_GROUPS - N_EXPERTS))
        a, meta, cnt = _router(h, norm_ffn_g[l][None, :], wr, br[None, :], tm=min(m, 512))

        e_idx = meta[:, 0:2].astype(jnp.int32)
        rank = meta[:, 2:4].astype(jnp.int32)
        counts = cnt[0, N_GROUPS:N_GROUPS + N_EXPERTS].astype(jnp.int32)
        padded = ((counts + EXPERT_BLOCK - 1) // EXPERT_BLOCK) * EXPERT_BLOCK
        pad_end = jnp.cumsum(padded)
        dest = (pad_end - padded)[e_idx] + rank
        n_rows = 2 * m + N_EXPERTS * EXPERT_BLOCK
        n_blocks = n_rows // EXPERT_BLOCK
        n_blk = (pad_end[-1:] // EXPERT_BLOCK).astype(jnp.int32)
        blk_first = jnp.arange(n_blocks, dtype=jnp.int32) * EXPERT_BLOCK
        block_e = jnp.minimum(jnp.sum((pad_end[None, :] <= blk_first[:, None]).astype(jnp.int32), axis=1),
                              N_EXPERTS - 1)
        assign = 2 * jnp.arange(m, dtype=jnp.int32)[:, None] + jnp.arange(2, dtype=jnp.int32)[None, :]
        slot = jnp.full((n_rows,), -1, jnp.int32).at[dest.reshape(-1)].set(assign.reshape(-1))
        after = (pad_end // EXPERT_BLOCK)[block_e]
        next_e = jnp.where(after < n_blk[0], block_e[jnp.minimum(after, n_blocks - 1)], -1).astype(jnp.int32)

        xs = _dispatch(dest[:, 0], dest[:, 1], pad_end - padded + counts, padded - counts, n_blk, a, n_rows=n_rows,
                       tm=min(m, 256))
        y = _experts(slot, block_e, next_e, n_blk, xs, w_expert_gate[l], w_expert_up[l], w_expert_down[l],
                     rb=EXPERT_BLOCK, n_tok=m)
        last = l == depth - 1
        g_fin = norm_final_g[None, :] if last else jnp.ones((1, d), F32)
        h = _combine(y, h, meta, g_fin, tm=min(m, 256))
        assert last, "the combine kernel fuses the final rmsnorm; deeper stacks need an un-normalised variant"

    return h.reshape(batch, seq, d)
```

```python
import functools

import jax
import jax.numpy as jnp
from jax import lax
from jax.experimental import pallas as pl
from jax.experimental.pallas import tpu as pltpu

F32 = jnp.float32
BF16 = jnp.bfloat16

EPS = 1e-6
CHUNK = 64
LANES = 128
GLA_HEADS, GLA_DK, GLA_DV = 4, 128, 256
HGRN_HEADS, HGRN_DH = 8, 128
XA_HEADS, XA_DH = 4, 512
N_GROUPS, EXPERTS_PER_GROUP, N_EXPERTS = 4, 8, 32
EXPERT_BLOCK = 256
WEIGHT_CHUNKS = 8
WEIGHT_DMA_PRIORITY = 1
VMEM_LIMIT = 56 * 1024 * 1024
EXPERT_VMEM_LIMIT = 60 * 1024 * 1024


def _cparams(sem, vmem=VMEM_LIMIT):
    return pltpu.CompilerParams(dimension_semantics=sem, vmem_limit_bytes=vmem)


def _log1pexp_neg(t):
    return jnp.log(1.0 + jnp.exp(-t))


def _log_sigmoid(z):
    return jnp.minimum(z, 0.0) - _log1pexp_neg(jnp.abs(z))


def _sigmoid(z):
    return 1.0 / (1.0 + jnp.exp(-z))


def _rms(x, g):
    return x * lax.rsqrt(jnp.mean(x * x, axis=-1, keepdims=True) + EPS) * g


def _nt_dot(x, y):
    return lax.dot_general(x, y, (((1,), (1,)), ((), ())), preferred_element_type=F32)


NORM_ROWS = 256


def _for_row_chunks(n_rows, fn):
    step = min(NORM_ROWS, n_rows)

    def body(ci, carry):
        fn(pl.ds(pl.multiple_of(ci * step, step), step))
        return carry

    lax.fori_loop(0, n_rows // step, body, 0)


def _norm_gate_kernel(x_ref, g_ref, wlr_ref, wup_ref, bal_ref, a_ref, la_ref):
    a = _rms(x_ref[...], g_ref[...]).astype(BF16)
    a_ref[...] = a
    lr = _nt_dot(a, wlr_ref[...])
    z = jnp.dot(lr.astype(BF16), wup_ref[...], preferred_element_type=F32) + bal_ref[...]
    la_ref[...] = _log_sigmoid(z) * (1.0 / 16.0)


def _norm_gate(x, g, w_lr_t, w_up, b_al, *, tm):
    m, d = x.shape
    nk = w_up.shape[1]
    return pl.pallas_call(
        _norm_gate_kernel,
        grid=(m // tm,),
        in_specs=[
            pl.BlockSpec((tm, d), lambda i: (i, 0)),
            pl.BlockSpec((1, d), lambda i: (0, 0)),
            pl.BlockSpec((LANES, d), lambda i: (0, 0)),
            pl.BlockSpec((LANES, nk), lambda i: (0, 0)),
            pl.BlockSpec((1, nk), lambda i: (0, 0)),
        ],
        out_specs=[pl.BlockSpec((tm, d), lambda i: (i, 0)), pl.BlockSpec((tm, nk), lambda i: (i, 0))],
        out_shape=[jax.ShapeDtypeStruct((m, d), BF16), jax.ShapeDtypeStruct((m, nk), F32)],
        compiler_params=_cparams(("parallel",)),
        name="norm_gate",
    )(x, g, w_lr_t, w_up, b_al)


def _inproj_kernel(a_ref, w_ref, proj_ref):
    proj_ref[...] = _nt_dot(a_ref[...], w_ref[...].astype(BF16))


def _inproj(a, w_t, *, tm, tn, skip_from, skip):
    m, d = a.shape
    n = w_t.shape[0] - skip
    first_after = skip_from // tn

    def w_rows(i, j):
        return (pl.multiple_of(j * tn + jnp.where(j >= first_after, skip, 0), 8), 0)

    return pl.pallas_call(
        _inproj_kernel,
        grid=(m // tm, n // tn),
        in_specs=[
            pl.BlockSpec((tm, d), lambda i, j: (i, 0), pipeline_mode=pl.Buffered(1)),
            pl.BlockSpec((pl.Element(tn), pl.Element(d)), w_rows),
        ],
        out_specs=pl.BlockSpec((tm, tn), lambda i, j: (i, j)),
        out_shape=jax.ShapeDtypeStruct((m, n), F32),
        compiler_params=_cparams(("parallel", "arbitrary")),
        name="inproj",
    )(a, w_t)


def _weight_spec(k, n, tn):
    if tn == n:
        return pl.BlockSpec((k, n), lambda i, j: (0, 0), pipeline_mode=pl.Buffered(1))
    return pl.BlockSpec((k, tn), lambda i, j: (0, j))


def _cast_weight(w_ref, wb_ref, resident):
    if resident:
        @pl.when((pl.program_id(0) == 0) & (pl.program_id(1) == 0))
        def _():
            wb_ref[...] = w_ref[...].astype(BF16)
    else:
        wb_ref[...] = w_ref[...].astype(BF16)


def _norm_mm_kernel(x_ref, g_ref, w_ref, o_ref, a_ref, wb_ref, *, resident):
    @pl.when(pl.program_id(1) == 0)
    def _():
        def rows_fn(rows):
            a_ref[rows, :] = _rms(x_ref[rows, :], g_ref[...]).astype(BF16)

        _for_row_chunks(x_ref.shape[0], rows_fn)

    _cast_weight(w_ref, wb_ref, resident)
    o_ref[...] = jnp.dot(a_ref[...], wb_ref[...], preferred_element_type=F32).astype(o_ref.dtype)


def _norm_mm(x, g, w, *, tm, tn, out_dtype, name):
    m, d = x.shape
    n = w.shape[1]
    return pl.pallas_call(
        functools.partial(_norm_mm_kernel, resident=tn == n),
        grid=(m // tm, n // tn),
        in_specs=[
            pl.BlockSpec((tm, d), lambda i, j: (i, 0)),
            pl.BlockSpec((1, d), lambda i, j: (0, 0)),
            _weight_spec(d, n, tn),
        ],
        out_specs=pl.BlockSpec((tm, tn), lambda i, j: (i, j)),
        out_shape=jax.ShapeDtypeStruct((m, n), out_dtype),
        scratch_shapes=[pltpu.VMEM((tm, d), BF16), pltpu.VMEM((d, tn), BF16)],
        compiler_params=_cparams(("arbitrary", "arbitrary")),
        name=name,
    )(x, g, w)


def _mm_res_kernel(*refs, n_lhs, resident):
    lhs = refs[:n_lhs]
    w_ref, res_ref, o_ref, wb_ref = refs[n_lhs:]
    _cast_weight(w_ref, wb_ref, resident)
    acc = res_ref[...]
    k0 = 0
    for l_ref in lhs:
        kp = l_ref.shape[1]
        acc = acc + jnp.dot(l_ref[...], wb_ref[k0:k0 + kp, :], preferred_element_type=F32)
        k0 += kp
    o_ref[...] = acc


def _mm_res(lhs_parts, w, res, *, tm, tn, name):
    m, n = res.shape
    k = w.shape[0]
    n_lhs = len(lhs_parts)
    in_specs = [pl.BlockSpec((tm, p.shape[1]), lambda i, j: (i, 0)) for p in lhs_parts]
    in_specs += [_weight_spec(k, n, tn), pl.BlockSpec((tm, tn), lambda i, j: (i, j))]
    return pl.pallas_call(
        functools.partial(_mm_res_kernel, n_lhs=n_lhs, resident=tn == n),
        grid=(m // tm, n // tn),
        in_specs=in_specs,
        out_specs=pl.BlockSpec((tm, tn), lambda i, j: (i, j)),
        out_shape=jax.ShapeDtypeStruct((m, n), F32),
        scratch_shapes=[pltpu.VMEM((k, tn), BF16)],
        compiler_params=_cparams(("arbitrary", "arbitrary")),
        name=name,
    )(*lhs_parts, w, res)


LEVELS = (32, 16, 8, 4, 2, 1)
LOG2E = 1.4426950408889634


def _split3_bf16(x):
    def top(v):
        bits = lax.bitcast_convert_type(v, jnp.uint32) & jnp.uint32(0xFFFF0000)
        return lax.bitcast_convert_type(bits, F32)
    hi = top(x)
    r1 = x - hi
    mid = top(r1)
    lo = r1 - mid
    return hi.astype(BF16), mid.astype(BF16), lo.astype(BF16)


def _mix_chunks(qs, ks, vs, las, st_refs):
    heads = range(len(qs))
    c, dk = qs[0].shape
    row = lax.broadcasted_iota(jnp.int32, (c, c), 0)
    col = lax.broadcasted_iota(jnp.int32, (c, c), 1)
    rowk = lax.broadcasted_iota(jnp.int32, (c, dk), 0)
    xor = jnp.bitwise_xor(row, col)

    las = [la * LOG2E for la in las]

    tri = jnp.where(col <= row, 1.0, 0.0).astype(BF16)
    b3 = [jnp.dot(tri, jnp.concatenate(_split3_bf16(la), axis=1), preferred_element_type=F32) for la in las]
    bs = [(t[:, :dk] + t[:, dk:2 * dk]) + t[:, 2 * dk:] for t in b3]

    def neg_dist(w, b, la):
        if w >= 4:
            parts = [jnp.broadcast_to(b[base + w - 1:base + w, :], (2 * w, dk)) for base in range(0, c, 2 * w)]
            m = parts[0] if len(parts) == 1 else jnp.concatenate(parts, axis=0)
            return -jnp.abs(b - m)
        if w == 2:
            r4 = jnp.bitwise_and(rowk, 3)
            nxt = pltpu.roll(la, c - 1, 0)
            prv = pltpu.roll(la, 1, 0)
            return jnp.where(r4 == 0, nxt, jnp.where(r4 == 1, 0.0, jnp.where(r4 == 2, la, la + prv)))
        return jnp.where(jnp.bitwise_and(rowk, 1) == 1, la, 0.0)

    acc = [_nt_dot(qs[h].astype(BF16), ks[h].astype(BF16)) for h in heads]
    for w in reversed(LEVELS):
        upper = jnp.bitwise_and(rowk, w) != 0
        xs = [(jnp.where(upper, qs[h], ks[h]) * jnp.exp2(neg_dist(w, bs[h], las[h]))).astype(BF16) for h in heads]
        gs = [_nt_dot(x, x) for x in xs]
        acc = [jnp.where(xor >= w, gs[h], acc[h]) for h in heads]
    ab = [jnp.where(col <= row, a, 0.0).astype(BF16) for a in acc]

    sts = [st_refs[h][...] for h in heads]
    qx = [(qs[h] * jnp.exp2(bs[h])).astype(BF16) for h in heads]
    b_last = [b[c - 1:c, :] for b in bs]
    kx = [(ks[h] * jnp.exp2(b_last[h] - bs[h])).astype(BF16) for h in heads]
    vb = [v.astype(BF16) for v in vs]
    outs = [jnp.dot(ab[h], vb[h], preferred_element_type=F32) + _nt_dot(qx[h], sts[h].astype(BF16)) for h in heads]
    for h in heads:
        st_refs[h][...] = jnp.exp2(b_last[h]) * sts[h] + lax.dot_general(
            vb[h], kx[h], (((0,), (0,)), ((), ())), preferred_element_type=F32)
    return outs


def _gated_norm(o, g, gn):
    return _rms(o, gn) * (g * _sigmoid(g))


def _gla_kernel(q_ref, k_ref, v_ref, g_ref, la_ref, gn_ref, o_ref, st_ref, *, n_chunk):
    @pl.when(pl.program_id(2) == 0)
    def _():
        st_ref[...] = jnp.zeros_like(st_ref)

    dk, dv = GLA_DK, GLA_DV
    heads = range(st_ref.shape[0])

    def body(ci, carry):
        rows = pl.ds(pl.multiple_of(ci * CHUNK, CHUNK), CHUNK)
        kc = [slice(h * dk, (h + 1) * dk) for h in heads]
        vc = [slice(h * dv, (h + 1) * dv) for h in heads]
        outs = _mix_chunks([q_ref[rows, kc[h]] * (dk ** -0.5) for h in heads], [k_ref[rows, kc[h]] for h in heads],
                           [v_ref[rows, vc[h]] for h in heads], [la_ref[rows, kc[h]] for h in heads],
                           [st_ref.at[h] for h in heads])
        for h in heads:
            o_ref[rows, vc[h]] = _gated_norm(outs[h], g_ref[rows, vc[h]], gn_ref[...]).astype(o_ref.dtype)
        return carry

    lax.fori_loop(0, n_chunk, body, 0)


def _gla(proj, la, gn, *, batch, seq, tb, hp):
    m = proj.shape[0]
    nt = seq // tb
    wk, wv = hp * GLA_DK, hp * GLA_DV
    n_grp = GLA_HEADS // hp
    spec = lambda width, first: pl.BlockSpec((tb, width), lambda b, p, t: (b * nt + t, first + p))
    return pl.pallas_call(
        functools.partial(_gla_kernel, n_chunk=tb // CHUNK),
        grid=(batch, n_grp, nt),
        in_specs=[spec(wk, 0), spec(wk, n_grp), spec(wv, n_grp), spec(wv, 2 * n_grp), spec(wk, 0),
                  pl.BlockSpec((1, GLA_DV), lambda b, p, t: (0, 0))],
        out_specs=spec(wv, 0),
        out_shape=jax.ShapeDtypeStruct((m, GLA_HEADS * GLA_DV), BF16),
        scratch_shapes=[pltpu.VMEM((hp, GLA_DV, GLA_DK), F32)],
        compiler_params=_cparams(("parallel", "parallel", "arbitrary")),
        name="gla",
    )(proj, proj, proj, proj, la, gn)


def _hgrn_kernel(q_ref, f_ref, i_ref, g_ref, lb_ref, gn_ref, o_ref, st_ref, *, n_chunk):
    @pl.when(pl.program_id(2) == 0)
    def _():
        st_ref[...] = jnp.zeros_like(st_ref)

    dh = HGRN_DH
    heads = range(st_ref.shape[0])

    def body(ci, carry):
        rows = pl.ds(pl.multiple_of(ci * CHUNK, CHUNK), CHUNK)
        hc = [slice(h * dh, (h + 1) * dh) for h in heads]
        qs, ks, las = [], [], []
        for h in heads:
            log_lb, log_1mlb, one_m_lb = lb_ref[0:1, hc[h]], lb_ref[1:2, hc[h]], lb_ref[2:3, hc[h]]
            hq = q_ref[rows, hc[h]]
            z = f_ref[rows, hc[h]]
            x1 = log_1mlb + _log_sigmoid(z)
            las.append(jnp.maximum(log_lb, x1) + _log1pexp_neg(jnp.abs(log_lb - x1)))
            qs.append(hq * _sigmoid(hq))
            ks.append(one_m_lb * _sigmoid(-z))
        outs = _mix_chunks(qs, ks, [i_ref[rows, hc[h]] for h in heads], las, [st_ref.at[h] for h in heads])
        for h in heads:
            o_ref[rows, hc[h]] = _gated_norm(outs[h], g_ref[rows, hc[h]], gn_ref[...]).astype(o_ref.dtype)
        return carry

    lax.fori_loop(0, n_chunk, body, 0)


def _hgrn(proj, lbc, gn, *, batch, seq, tb, hp):
    m = proj.shape[0]
    nt = seq // tb
    width = hp * HGRN_DH
    n_grp = HGRN_HEADS // hp
    first = 3072 // width
    spec = lambda seg: pl.BlockSpec((tb, width), lambda b, p, t: (b * nt + t, first + seg * n_grp + p))
    return pl.pallas_call(
        functools.partial(_hgrn_kernel, n_chunk=tb // CHUNK),
        grid=(batch, n_grp, nt),
        in_specs=[spec(0), spec(1), spec(2), spec(3),
                  pl.BlockSpec((8, width), lambda b, p, t: (0, p)),
                  pl.BlockSpec((1, HGRN_DH), lambda b, p, t: (0, 0))],
        out_specs=pl.BlockSpec((tb, width), lambda b, p, t: (b * nt + t, p)),
        out_shape=jax.ShapeDtypeStruct((m, HGRN_HEADS * HGRN_DH), BF16),
        scratch_shapes=[pltpu.VMEM((hp, HGRN_DH, HGRN_DH), F32)],
        compiler_params=_cparams(("parallel", "parallel", "arbitrary")),
        name="hgrn",
    )(proj, proj, proj, proj, lbc, gn)


def _xattn_kernel(q_ref, k_ref, v_ref, o_ref):
    for h in range(XA_HEADS):
        cols = slice(h * XA_DH, (h + 1) * XA_DH)
        s = lax.dot_general(q_ref[:, cols], k_ref[:, cols], (((1,), (1,)), ((), ())),
                            preferred_element_type=F32) * (XA_DH ** -0.5)
        p = jnp.exp(s - jnp.max(s, axis=-1, keepdims=True))
        p = p / jnp.sum(p, axis=-1, keepdims=True)
        o_ref[:, cols] = jnp.dot(p.astype(BF16), v_ref[:, cols], preferred_element_type=F32).astype(o_ref.dtype)


def _xattn(q, kv, *, batch, seq, n_mem, tq):
    m, d = q.shape
    nt = seq // tq
    return pl.pallas_call(
        _xattn_kernel,
        grid=(batch, nt),
        in_specs=[
            pl.BlockSpec((tq, d), lambda b, t: (b * nt + t, 0)),
            pl.BlockSpec((n_mem, d), lambda b, t: (b, 0)),
            pl.BlockSpec((n_mem, d), lambda b, t: (b, 1)),
        ],
        out_specs=pl.BlockSpec((tq, d), lambda b, t: (b * nt + t, 0)),
        out_shape=jax.ShapeDtypeStruct((m, d), BF16),
        compiler_params=_cparams(("parallel", "arbitrary")),
        name="xattn",
    )(q, kv, kv)


def _router_kernel(h_ref, g_ref, wr_ref, br_ref, a_ref, meta_ref, cnt_ref, carry_ref):
    tm = h_ref.shape[0]

    @pl.when(pl.program_id(0) == 0)
    def _():
        carry_ref[...] = jnp.zeros_like(carry_ref)

    a = _rms(h_ref[...], g_ref[...])
    a_ref[...] = a.astype(BF16).reshape(a_ref.shape)
    a_hi = a.astype(BF16)
    a_lo = (a - a_hi.astype(F32)).astype(BF16)
    w = wr_ref[...]
    w_hi = w.astype(BF16)
    w_lo = (w - w_hi.astype(F32)).astype(BF16)
    logits = (jnp.dot(a_hi, w_hi, preferred_element_type=F32) + jnp.dot(a_lo, w_hi, preferred_element_type=F32)
              + jnp.dot(a_hi, w_lo, preferred_element_type=F32)) + br_ref[...]
    lane = lax.broadcasted_iota(jnp.int32, (tm, LANES), 1)
    lane_f = lane.astype(F32)
    neg = -jnp.inf

    gl = jnp.where(lane < N_GROUPS, logits, neg)
    gmax = jnp.max(gl, axis=1, keepdims=True)
    gidx = jnp.min(jnp.where(gl == gmax, lane_f, float(LANES)), axis=1, keepdims=True)
    p_group = 1.0 / jnp.sum(jnp.exp(gl - gmax), axis=1, keepdims=True)
    lo = float(N_GROUPS) + gidx * float(EXPERTS_PER_GROUP)
    el = jnp.where((lane_f >= lo) & (lane_f < lo + float(EXPERTS_PER_GROUP)), logits, neg)
    v1 = jnp.max(el, axis=1, keepdims=True)
    i1 = jnp.min(jnp.where(el == v1, lane_f, float(LANES)), axis=1, keepdims=True)
    el2 = jnp.where(lane_f == i1, neg, el)
    v2 = jnp.max(el2, axis=1, keepdims=True)
    i2 = jnp.min(jnp.where(el2 == v2, lane_f, float(LANES)), axis=1, keepdims=True)
    t = jnp.exp(v2 - v1)
    g1 = p_group / (1.0 + t)
    g2 = p_group * t / (1.0 + t)

    hit1 = lane_f == i1
    hit2 = lane_f == i2
    onehot = jnp.where(hit1 | hit2, 1.0, 0.0)
    row = lax.broadcasted_iota(jnp.int32, (tm, tm), 0)
    col = lax.broadcasted_iota(jnp.int32, (tm, tm), 1)
    strict = jnp.where(col < row, 1.0, 0.0).astype(BF16)
    before = jnp.dot(strict, onehot.astype(BF16), preferred_element_type=F32) + carry_ref[0:1, :]
    r1 = jnp.sum(jnp.where(hit1, before, 0.0), axis=1, keepdims=True)
    r2 = jnp.sum(jnp.where(hit2, before, 0.0), axis=1, keepdims=True)
    carry_ref[...] = carry_ref[...] + jnp.sum(onehot, axis=0, keepdims=True)
    cnt_ref[...] = carry_ref[...]

    meta = jnp.zeros((tm, LANES), F32)
    for idx, val in enumerate((i1 - float(N_GROUPS), i2 - float(N_GROUPS), r1, r2, g1, g2)):
        meta = jnp.where(lane == idx, val, meta)
    meta_ref[...] = meta


def _router(h, g, wr, br, *, tm):
    m, d = h.shape
    slab = d // LANES
    return pl.pallas_call(
        _router_kernel,
        grid=(m // tm,),
        in_specs=[
            pl.BlockSpec((tm, d), lambda i: (i, 0)),
            pl.BlockSpec((1, d), lambda i: (0, 0)),
            pl.BlockSpec((d, LANES), lambda i: (0, 0)),
            pl.BlockSpec((1, LANES), lambda i: (0, 0)),
        ],
        out_specs=[
            pl.BlockSpec((tm, slab, LANES), lambda i: (i, 0, 0)),
            pl.BlockSpec((tm, LANES), lambda i: (i, 0)),
            pl.BlockSpec((8, LANES), lambda i: (0, 0)),
        ],
        out_shape=[jax.ShapeDtypeStruct((m, slab, LANES), BF16), jax.ShapeDtypeStruct((m, LANES), F32),
                   jax.ShapeDtypeStruct((8, LANES), F32)],
        scratch_shapes=[pltpu.VMEM((8, LANES), F32)],
        compiler_params=_cparams(("arbitrary",)),
        name="router",
    )(h, g, wr, br)


PAD_PIECES = tuple(EXPERT_BLOCK >> (b + 1) for b in range(EXPERT_BLOCK.bit_length() - 1))


def _dispatch_kernel(d1_ref, d2_ref, fill_ref, npad_ref, nb_ref, a_ref, xs_hbm, slot_ref, st0, st1, zbuf, sem, zsem):
    i = pl.program_id(0)
    n_steps = pl.num_programs(0)
    tm = a_ref.shape[0]
    bufs = (st0, st1)
    rb = zbuf.shape[0]
    n_blocks = xs_hbm.shape[0] // rb
    min_blocks = (2 * tm * n_steps) // rb

    def zero_copies():
        out = []
        for e in range(N_EXPERTS):
            p = npad_ref[e]
            for piece in PAD_PIECES:
                out.append((p & piece != 0, pltpu.make_async_copy(
                    zbuf.at[pl.ds(0, piece)], xs_hbm.at[pl.ds(fill_ref[e] + (p & -(2 * piece)), piece)], zsem.at[0])))
        for b in range(min_blocks, n_blocks):
            out.append((b >= nb_ref[0], pltpu.make_async_copy(zbuf, xs_hbm.at[pl.ds(b * rb, rb)], zsem.at[0])))
        return out

    def wait_rows(s):
        for _ in range(2):
            pltpu.make_async_copy(bufs[s], xs_hbm.at[pl.ds(0, tm)], sem.at[s]).wait()

    @pl.when(i == 0)
    def _():
        def mark_unused(j, carry):
            slot_ref[j] = -1
            return carry
        for e in range(N_EXPERTS):
            lax.fori_loop(fill_ref[e], fill_ref[e] + npad_ref[e], mark_unused, 0)
        lax.fori_loop(nb_ref[0] * rb, slot_ref.shape[0], mark_unused, 0)
        zbuf[...] = jnp.zeros_like(zbuf)
        for cond, cp in zero_copies():
            @pl.when(cond)
            def _():
                cp.start()

    for s in (0, 1):
        @pl.when(lax.rem(i, 2) == s)
        def _():
            @pl.when(i >= 2)
            def _():
                wait_rows(s)

            bufs[s][...] = a_ref[...]
            for r in range(tm):
                tok = i * tm + r
                d1 = d1_ref[tok]
                d2 = d2_ref[tok]
                slot_ref[d1] = 2 * tok
                slot_ref[d2] = 2 * tok + 1
                pltpu.make_async_copy(bufs[s].at[r], xs_hbm.at[d1], sem.at[s]).start(priority=0)
                pltpu.make_async_copy(bufs[s].at[r], xs_hbm.at[d2], sem.at[s]).start(priority=1)

    @pl.when(i == n_steps - 1)
    def _():
        for s in (0, 1):
            @pl.when((lax.rem(n_steps - 1, 2) == s) | ((n_steps >= 2) & (lax.rem(n_steps, 2) == s)))
            def _():
                wait_rows(s)
        for cond, cp in zero_copies():
            @pl.when(cond)
            def _():
                cp.wait()


def _dispatch(dest1, dest2, fill, npad, n_blk, a, *, n_rows, tm):
    m, slab, _ = a.shape
    grid_spec = pltpu.PrefetchScalarGridSpec(
        num_scalar_prefetch=5,
        grid=(m // tm,),
        in_specs=[pl.BlockSpec((tm, slab, LANES), lambda i, *_: (i, 0, 0))],
        out_specs=[pl.BlockSpec(memory_space=pl.ANY), pl.BlockSpec(memory_space=pltpu.SMEM)],
        scratch_shapes=[pltpu.VMEM((tm, slab, LANES), a.dtype), pltpu.VMEM((tm, slab, LANES), a.dtype),
                        pltpu.VMEM((EXPERT_BLOCK, slab, LANES), a.dtype),
                        pltpu.SemaphoreType.DMA((2,)), pltpu.SemaphoreType.DMA((1,))],
    )
    return pl.pallas_call(
        _dispatch_kernel,
        grid_spec=grid_spec,
        out_shape=[jax.ShapeDtypeStruct((n_rows, slab, LANES), a.dtype), jax.ShapeDtypeStruct((n_rows,), jnp.int32)],
        compiler_params=_cparams(("arbitrary",)),
        name="dispatch",
    )(dest1, dest2, fill, npad, n_blk, a)


def _weight_copies(w_hbms, e, wst_refs, wsem):
    copies = []
    for w_hbm, wst_ref in zip(w_hbms, wst_refs):
        rows_per = w_hbm.shape[1] // WEIGHT_CHUNKS
        for c in range(WEIGHT_CHUNKS):
            rows = pl.ds(c * rows_per, rows_per)
            copies.append(pltpu.make_async_copy(w_hbm.at[e, rows, :], wst_ref.at[rows, :], wsem.at[0]))
    return copies


def _load_expert_weights(i, be_ref, nx_ref, w_hbms, wst_refs, wb_refs, wsem):
    def start(e):
        for cp in _weight_copies(w_hbms, e, wst_refs, wsem):
            cp.start(priority=WEIGHT_DMA_PRIORITY)

    @pl.when(i == 0)
    def _():
        start(be_ref[0])

    @pl.when((i == 0) | (be_ref[i] != be_ref[jnp.maximum(i - 1, 0)]))
    def _():
        for cp in _weight_copies(w_hbms, be_ref[i], wst_refs, wsem):
            cp.wait()
        for wst_ref, wb_ref in zip(wst_refs, wb_refs):
            _for_row_chunks(wst_ref.shape[0], lambda rows: wb_ref.__setitem__((rows, slice(None)),
                                                                              wst_ref[rows, :].astype(BF16)))

        @pl.when(nx_ref[i] >= 0)
        def _():
            start(nx_ref[i])


def _expert_kernel(slot_ref, be_ref, nx_ref, nb_ref, x_ref, wg_hbm, wu_hbm, wd_hbm, y_hbm, wsg, wsu, wsd, wgb, wub, wdb,
                   ys0, ys1, sem, wsem, *, n_tok):
    i = pl.program_id(0)
    n_steps = pl.num_programs(0)
    nb = nb_ref[0]
    rb = x_ref.shape[0]
    bufs = (ys0, ys1)

    def wait_block(buf, s):
        pltpu.make_async_copy(buf, y_hbm.at[pl.ds(0, rb)], sem.at[s]).wait()

    @pl.when(i == 0)
    def _():
        ys0[...] = jnp.zeros_like(ys0)
        for s in (0, 1):
            spare = pltpu.make_async_copy(ys0, y_hbm.at[pl.ds(2 * n_tok + s * rb, rb)], sem.at[0])
            spare.start()
            spare.wait()

    for s in (0, 1):
        @pl.when((i < nb) & (lax.rem(i, 2) == s))
        def _():
            _load_expert_weights(i, be_ref, nx_ref, (wg_hbm, wu_hbm, wd_hbm), (wsg, wsu, wsd), (wgb, wub, wdb), wsem)

            @pl.when(i >= 2)
            def _():
                wait_block(bufs[s], s)

            x = x_ref[...].reshape(rb, -1)
            hg = jnp.dot(x, wgb[...], preferred_element_type=F32)
            hu = jnp.dot(x, wub[...], preferred_element_type=F32)
            hb = (hg * _sigmoid(hg) * hu).astype(BF16)
            y = jnp.dot(hb, wdb[...], preferred_element_type=F32)
            bufs[s][...] = y.reshape(bufs[s].shape)
            for r in range(rb):
                v = slot_ref[i * rb + r]
                dst = jnp.where(v >= 0, (v & 1) * n_tok + (v >> 1), 2 * n_tok + s * rb + r)
                pltpu.make_async_copy(bufs[s].at[r], y_hbm.at[dst], sem.at[s]).start()

    @pl.when(i == n_steps - 1)
    def _():
        for s in (0, 1):
            @pl.when(((nb >= 1) & (lax.rem(nb - 1, 2) == s)) | ((nb >= 2) & (lax.rem(nb, 2) == s)))
            def _():
                wait_block(bufs[s], s)


def _experts(slot, block_e, next_e, n_blk, xs, wg, wu, wd, *, rb, n_tok):
    n_rows, slab, _ = xs.shape
    d = slab * LANES
    de = wg.shape[2]
    grid_spec = pltpu.PrefetchScalarGridSpec(
        num_scalar_prefetch=4,
        grid=(n_rows // rb,),
        in_specs=[pl.BlockSpec((rb, slab, LANES), lambda i, sl, be, nx, nb: (jnp.minimum(i, nb[0] - 1), 0, 0))]
        + [pl.BlockSpec(memory_space=pl.ANY)] * 3,
        out_specs=pl.BlockSpec(memory_space=pl.ANY),
        scratch_shapes=[pltpu.VMEM((d, de), F32), pltpu.VMEM((d, de), F32), pltpu.VMEM((de, d), F32),
                        pltpu.VMEM((d, de), BF16), pltpu.VMEM((d, de), BF16), pltpu.VMEM((de, d), BF16),
                        pltpu.VMEM((rb, slab, LANES), F32), pltpu.VMEM((rb, slab, LANES), F32),
                        pltpu.SemaphoreType.DMA((2,)), pltpu.SemaphoreType.DMA((1,))],
    )
    return pl.pallas_call(
        functools.partial(_expert_kernel, n_tok=n_tok),
        grid_spec=grid_spec,
        out_shape=jax.ShapeDtypeStruct((2 * n_tok + 2 * rb, slab, LANES), F32),
        compiler_params=_cparams(("arbitrary",), vmem=EXPERT_VMEM_LIMIT),
        name="experts",
    )(slot, block_e, next_e, n_blk, xs, wg, wu, wd)


def _combine_kernel(y0_ref, y1_ref, h_ref, meta_ref, g_ref, o_ref):
    tm, d = h_ref.shape
    meta = meta_ref[...]
    out = h_ref[...] + meta[:, 4:5] * y0_ref[...].reshape(tm, d) + meta[:, 5:6] * y1_ref[...].reshape(tm, d)
    o_ref[...] = _rms(out, g_ref[...])


def _combine(y, h, meta, g, *, tm):
    m, d = h.shape
    slab = d // LANES
    nt = m // tm
    return pl.pallas_call(
        _combine_kernel,
        grid=(nt,),
        in_specs=[
            pl.BlockSpec((tm, slab, LANES), lambda i: (i, 0, 0)),
            pl.BlockSpec((tm, slab, LANES), lambda i: (nt + i, 0, 0)),
            pl.BlockSpec((tm, d), lambda i: (i, 0)),
            pl.BlockSpec((tm, LANES), lambda i: (i, 0)),
            pl.BlockSpec((1, d), lambda i: (0, 0)),
        ],
        out_specs=pl.BlockSpec((tm, d), lambda i: (i, 0)),
        out_shape=jax.ShapeDtypeStruct((m, d), F32),
        compiler_params=_cparams(("parallel",)),
        name="combine",
    )(y, y, h, meta, g)


def kernel(x, mem, norm_mix_g, w_in, w_gla_alpha_up, b_gla_alpha, gla_out_norm_g, hgrn_lb_logits, hgrn_out_norm_g, w_mix_out, norm_xattn_g, norm_mem_g, w_xattn_q, w_xattn_kv, w_xattn_out, norm_ffn_g, w_router_group, b_router_group, w_router_expert, b_router_expert, w_expert_gate, w_expert_up, w_expert_down, norm_final_g):
    batch, seq, d = x.shape
    n_mem = mem.shape[1]
    m = batch * seq
    depth = norm_mix_g.shape[0]
    h = x.reshape(m, d)
    lb_all = jnp.cumsum(jax.nn.softmax(hgrn_lb_logits.astype(F32), axis=0), axis=0)
    gla_cols = 2 * GLA_HEADS * GLA_DK + 2 * GLA_HEADS * GLA_DV
    lr_rank = w_gla_alpha_up.shape[1]

    for l in range(depth):
        w_t = jnp.swapaxes(w_in[l], 0, 1)
        w_lr_t = jnp.pad(w_t[gla_cols:gla_cols + lr_rank], ((0, LANES - lr_rank), (0, 0))).astype(BF16)
        w_up = jnp.pad(w_gla_alpha_up[l], ((0, LANES - lr_rank), (0, 0))).astype(BF16)
        a_mix, la = _norm_gate(h, norm_mix_g[l][None, :], w_lr_t, w_up, b_gla_alpha[l][None, :], tm=min(m, 512))
        proj = _inproj(a_mix, w_t, tm=min(m, 4096), tn=512, skip_from=gla_cols, skip=lr_rank)
        lb = lb_all[l]
        lbc = jnp.zeros((8, lb.shape[0]), F32).at[0].set(jnp.log(lb)).at[1].set(jnp.log1p(-lb)).at[2].set(1.0 - lb)
        o_gla = _gla(proj, la, gla_out_norm_g[l][None, :], batch=batch, seq=seq, tb=min(seq, 512), hp=4)
        o_h = _hgrn(proj, lbc, hgrn_out_norm_g[l][None, :], batch=batch, seq=seq, tb=min(seq, 512), hp=8)
        h = _mm_res([o_gla, o_h], w_mix_out[l], h, tm=min(m, 512), tn=d, name="mix_out")

        kv = _norm_mm(mem.reshape(batch * n_mem, d), norm_mem_g[l][None, :], w_xattn_kv[l],
                      tm=batch * n_mem, tn=1024, out_dtype=BF16, name="mem_kv")
        q = _norm_mm(h, norm_xattn_g[l][None, :], w_xattn_q[l], tm=min(m, 512), tn=d, out_dtype=BF16,
                     name="xattn_q")
        o = _xattn(q, kv, batch=batch, seq=seq, n_mem=n_mem, tq=min(seq, 512))
        h = _mm_res([o], w_xattn_out[l], h, tm=min(m, 512), tn=d, name="xattn_out")

        wr = jnp.pad(jnp.concatenate([w_router_group[l], w_router_expert[l]], axis=1),
                     ((0, 0), (0, LANES - N_GROUPS - N_EXPERTS)))
        br = jnp.pad(jnp.concatenate([b_router_group[l], b_router_expert[l]]), (0, LANES - N_GROUPS - N_EXPERTS))
        a, meta, cnt = _router(h, norm_ffn_g[l][None, :], wr, br[None, :], tm=min(m, 512))

        e_idx = meta[:, 0:2].astype(jnp.int32)
        rank = meta[:, 2:4].astype(jnp.int32)
        counts = cnt[0, N_GROUPS:N_GROUPS + N_EXPERTS].astype(jnp.int32)
        padded = ((counts + EXPERT_BLOCK - 1) // EXPERT_BLOCK) * EXPERT_BLOCK
        pad_end = jnp.cumsum(padded)
        pad_start = pad_end - padded
        n_rows = 2 * m + N_EXPERTS * EXPERT_BLOCK
        n_blocks = n_rows // EXPERT_BLOCK
        n_blk = (pad_end[-1:] // EXPERT_BLOCK).astype(jnp.int32)
        blk_first = jnp.arange(n_blocks, dtype=jnp.int32) * EXPERT_BLOCK
        block_e = jnp.minimum(jnp.sum((pad_end[None, :] <= blk_first[:, None]).astype(jnp.int32), axis=1),
                              N_EXPERTS - 1)
        after = (pad_end // EXPERT_BLOCK)[block_e]
        next_e = jnp.where(after < n_blk[0], block_e[jnp.minimum(after, n_blocks - 1)], -1).astype(jnp.int32)

        experts = jnp.arange(N_EXPERTS, dtype=jnp.int32)
        dest = [jnp.sum(jnp.where(e_idx[:, k, None] == experts[None, :], pad_start[None, :], 0), axis=1) + rank[:, k]
                for k in range(2)]
        xs, slot = _dispatch(dest[0], dest[1], pad_start + counts, padded - counts, n_blk, a, n_rows=n_rows,
                             tm=min(m, 256))
        y = _experts(slot, block_e, next_e, n_blk, xs, w_expert_gate[l], w_expert_up[l], w_expert_down[l],
                     rb=EXPERT_BLOCK, n_tok=m)
        last = l == depth - 1
        g_fin = norm_final_g[None, :] if last else jnp.ones((1, d), F32)
        h = _combine(y, h, meta, g_fin, tm=min(m, 256))
        assert last, "the combine kernel fuses the final rmsnorm; deeper stacks need an un-normalised variant"

    return h.reshape(batch, seq, d)
```

```python
import functools

import jax
import jax.numpy as jnp
from jax import lax
from jax.experimental import pallas as pl
from jax.experimental.pallas import tpu as pltpu

F32 = jnp.float32
BF16 = jnp.bfloat16

EPS = 1e-6
CHUNK = 64
LANES = 128
GLA_HEADS, GLA_DK, GLA_DV = 4, 128, 256
HGRN_HEADS, HGRN_DH = 8, 128
XA_HEADS, XA_DH = 4, 512
N_GROUPS, EXPERTS_PER_GROUP, N_EXPERTS = 4, 8, 32
EXPERT_BLOCK = 256
WEIGHT_CHUNKS = 8
WEIGHT_DMA_PRIORITY = 1
VMEM_LIMIT = 56 * 1024 * 1024
EXPERT_VMEM_LIMIT = 60 * 1024 * 1024


def _cparams(sem, vmem=VMEM_LIMIT):
    return pltpu.CompilerParams(dimension_semantics=sem, vmem_limit_bytes=vmem)


def _log1pexp_neg(t):
    return jnp.log(1.0 + jnp.exp(-t))


def _log_sigmoid(z):
    return jnp.minimum(z, 0.0) - _log1pexp_neg(jnp.abs(z))


def _sigmoid(z):
    return 1.0 / (1.0 + jnp.exp(-z))


def _rms(x, g):
    return x * lax.rsqrt(jnp.mean(x * x, axis=-1, keepdims=True) + EPS) * g


def _nt_dot(x, y):
    return lax.dot_general(x, y, (((1,), (1,)), ((), ())), preferred_element_type=F32)


NORM_ROWS = 256


def _for_row_chunks(n_rows, fn):
    step = min(NORM_ROWS, n_rows)

    def body(ci, carry):
        fn(pl.ds(pl.multiple_of(ci * step, step), step))
        return carry

    lax.fori_loop(0, n_rows // step, body, 0)


def _norm_gate_kernel(x_ref, g_ref, wlr_ref, wup_ref, bal_ref, a_ref, la_ref):
    a = _rms(x_ref[...], g_ref[...]).astype(BF16)
    a_ref[...] = a
    lr = _nt_dot(a, wlr_ref[...])
    z = jnp.dot(lr.astype(BF16), wup_ref[...], preferred_element_type=F32) + bal_ref[...]
    la_ref[...] = _log_sigmoid(z) * (1.0 / 16.0)


def _norm_gate(x, g, w_lr_t, w_up, b_al, *, tm):
    m, d = x.shape
    nk = w_up.shape[1]
    return pl.pallas_call(
        _norm_gate_kernel,
        grid=(m // tm,),
        in_specs=[
            pl.BlockSpec((tm, d), lambda i: (i, 0)),
            pl.BlockSpec((1, d), lambda i: (0, 0)),
            pl.BlockSpec((LANES, d), lambda i: (0, 0)),
            pl.BlockSpec((LANES, nk), lambda i: (0, 0)),
            pl.BlockSpec((1, nk), lambda i: (0, 0)),
        ],
        out_specs=[pl.BlockSpec((tm, d), lambda i: (i, 0)), pl.BlockSpec((tm, nk), lambda i: (i, 0))],
        out_shape=[jax.ShapeDtypeStruct((m, d), BF16), jax.ShapeDtypeStruct((m, nk), F32)],
        compiler_params=_cparams(("parallel",)),
        name="norm_gate",
    )(x, g, w_lr_t, w_up, b_al)


def _inproj_kernel(a_ref, w_ref, proj_ref):
    proj_ref[...] = _nt_dot(a_ref[...], w_ref[...].astype(BF16))


def _inproj(a, w_t, *, tm, tn, skip_from, skip):
    m, d = a.shape
    n = w_t.shape[0] - skip
    first_after = skip_from // tn

    def w_rows(i, j):
        return (pl.multiple_of(j * tn + jnp.where(j >= first_after, skip, 0), 8), 0)

    return pl.pallas_call(
        _inproj_kernel,
        grid=(m // tm, n // tn),
        in_specs=[
            pl.BlockSpec((tm, d), lambda i, j: (i, 0), pipeline_mode=pl.Buffered(1)),
            pl.BlockSpec((pl.Element(tn), pl.Element(d)), w_rows),
        ],
        out_specs=pl.BlockSpec((tm, tn), lambda i, j: (i, j)),
        out_shape=jax.ShapeDtypeStruct((m, n), F32),
        compiler_params=_cparams(("parallel", "arbitrary")),
        name="inproj",
    )(a, w_t)


def _weight_spec(k, n, tn):
    if tn == n:
        return pl.BlockSpec((k, n), lambda i, j: (0, 0), pipeline_mode=pl.Buffered(1))
    return pl.BlockSpec((k, tn), lambda i, j: (0, j))


def _cast_weight(w_ref, wb_ref, resident):
    if resident:
        @pl.when((pl.program_id(0) == 0) & (pl.program_id(1) == 0))
        def _():
            wb_ref[...] = w_ref[...].astype(BF16)
    else:
        wb_ref[...] = w_ref[...].astype(BF16)


def _norm_mm_kernel(x_ref, g_ref, w_ref, o_ref, a_ref, wb_ref, *, resident):
    @pl.when(pl.program_id(1) == 0)
    def _():
        def rows_fn(rows):
            a_ref[rows, :] = _rms(x_ref[rows, :], g_ref[...]).astype(BF16)

        _for_row_chunks(x_ref.shape[0], rows_fn)

    _cast_weight(w_ref, wb_ref, resident)
    o_ref[...] = jnp.dot(a_ref[...], wb_ref[...], preferred_element_type=F32).astype(o_ref.dtype)


def _norm_mm(x, g, w, *, tm, tn, out_dtype, name):
    m, d = x.shape
    n = w.shape[1]
    return pl.pallas_call(
        functools.partial(_norm_mm_kernel, resident=tn == n),
        grid=(m // tm, n // tn),
        in_specs=[
            pl.BlockSpec((tm, d), lambda i, j: (i, 0)),
            pl.BlockSpec((1, d), lambda i, j: (0, 0)),
            _weight_spec(d, n, tn),
        ],
        out_specs=pl.BlockSpec((tm, tn), lambda i, j: (i, j)),
        out_shape=jax.ShapeDtypeStruct((m, n), out_dtype),
        scratch_shapes=[pltpu.VMEM((tm, d), BF16), pltpu.VMEM((d, tn), BF16)],
        compiler_params=_cparams(("arbitrary", "arbitrary")),
        name=name,
    )(x, g, w)


def _mm_res_kernel(*refs, n_lhs, resident):
    lhs = refs[:n_lhs]
    w_ref, res_ref, o_ref, wb_ref = refs[n_lhs:]
    _cast_weight(w_ref, wb_ref, resident)
    acc = res_ref[...]
    k0 = 0
    for l_ref in lhs:
        kp = l_ref.shape[1]
        acc = acc + jnp.dot(l_ref[...], wb_ref[k0:k0 + kp, :], preferred_element_type=F32)
        k0 += kp
    o_ref[...] = acc


def _mm_res(lhs_parts, w, res, *, tm, tn, name):
    m, n = res.shape
    k = w.shape[0]
    n_lhs = len(lhs_parts)
    in_specs = [pl.BlockSpec((tm, p.shape[1]), lambda i, j: (i, 0)) for p in lhs_parts]
    in_specs += [_weight_spec(k, n, tn), pl.BlockSpec((tm, tn), lambda i, j: (i, j))]
    return pl.pallas_call(
        functools.partial(_mm_res_kernel, n_lhs=n_lhs, resident=tn == n),
        grid=(m // tm, n // tn),
        in_specs=in_specs,
        out_specs=pl.BlockSpec((tm, tn), lambda i, j: (i, j)),
        out_shape=jax.ShapeDtypeStruct((m, n), F32),
        scratch_shapes=[pltpu.VMEM((k, tn), BF16)],
        compiler_params=_cparams(("arbitrary", "arbitrary")),
        name=name,
    )(*lhs_parts, w, res)


LEVELS = (32, 16, 8, 4, 2, 1)
LOG2E = 1.4426950408889634


def _split3_bf16(x):
    def top(v):
        bits = lax.bitcast_convert_type(v, jnp.uint32) & jnp.uint32(0xFFFF0000)
        return lax.bitcast_convert_type(bits, F32)
    hi = top(x)
    r1 = x - hi
    mid = top(r1)
    lo = r1 - mid
    return hi.astype(BF16), mid.astype(BF16), lo.astype(BF16)


def _mix_chunks(qs, ks, vs, las, st_refs):
    heads = range(len(qs))
    c, dk = qs[0].shape
    row = lax.broadcasted_iota(jnp.int32, (c, c), 0)
    col = lax.broadcasted_iota(jnp.int32, (c, c), 1)
    rowk = lax.broadcasted_iota(jnp.int32, (c, dk), 0)
    xor = jnp.bitwise_xor(row, col)

    las = [la * LOG2E for la in las]

    tri = jnp.where(col <= row, 1.0, 0.0).astype(BF16)
    b3 = [jnp.dot(tri, jnp.concatenate(_split3_bf16(la), axis=1), preferred_element_type=F32) for la in las]
    bs = [(t[:, :dk] + t[:, dk:2 * dk]) + t[:, 2 * dk:] for t in b3]

    def neg_dist(w, b, la):
        if w >= 4:
            parts = [jnp.broadcast_to(b[base + w - 1:base + w, :], (2 * w, dk)) for base in range(0, c, 2 * w)]
            m = parts[0] if len(parts) == 1 else jnp.concatenate(parts, axis=0)
            return -jnp.abs(b - m)
        if w == 2:
            r4 = jnp.bitwise_and(rowk, 3)
            nxt = pltpu.roll(la, c - 1, 0)
            prv = pltpu.roll(la, 1, 0)
            return jnp.where(r4 == 0, nxt, jnp.where(r4 == 1, 0.0, jnp.where(r4 == 2, la, la + prv)))
        return jnp.where(jnp.bitwise_and(rowk, 1) == 1, la, 0.0)

    acc = [_nt_dot(qs[h].astype(BF16), ks[h].astype(BF16)) for h in heads]
    for w in reversed(LEVELS):
        upper = jnp.bitwise_and(rowk, w) != 0
        xs = [(jnp.where(upper, qs[h], ks[h]) * jnp.exp2(neg_dist(w, bs[h], las[h]))).astype(BF16) for h in heads]
        gs = [_nt_dot(x, x) for x in xs]
        acc = [jnp.where(xor >= w, gs[h], acc[h]) for h in heads]
    ab = [jnp.where(col <= row, a, 0.0).astype(BF16) for a in acc]

    sts = [st_refs[h][...] for h in heads]
    qx = [(qs[h] * jnp.exp2(bs[h])).astype(BF16) for h in heads]
    b_last = [b[c - 1:c, :] for b in bs]
    kx = [(ks[h] * jnp.exp2(b_last[h] - bs[h])).astype(BF16) for h in heads]
    vb = [v.astype(BF16) for v in vs]
    outs = [jnp.dot(ab[h], vb[h], preferred_element_type=F32) + _nt_dot(qx[h], sts[h].astype(BF16)) for h in heads]
    for h in heads:
        st_refs[h][...] = jnp.exp2(b_last[h]) * sts[h] + lax.dot_general(
            vb[h], kx[h], (((0,), (0,)), ((), ())), preferred_element_type=F32)
    return outs


def _gated_norm(o, g, gn):
    return _rms(o, gn) * (g * _sigmoid(g))


def _gla_kernel(q_ref, k_ref, v_ref, g_ref, la_ref, gn_ref, o_ref, st_ref, *, n_chunk):
    @pl.when(pl.program_id(1) == 0)
    def _():
        st_ref[...] = jnp.zeros_like(st_ref)

    dk, dv = GLA_DK, GLA_DV
    n_b = q_ref.shape[0]
    n_h = st_ref.shape[0] // n_b
    streams = [(b, h) for b in range(n_b) for h in range(n_h)]
    kc = [slice(h * dk, (h + 1) * dk) for h in range(n_h)]
    vc = [slice(h * dv, (h + 1) * dv) for h in range(n_h)]

    def body(ci, carry):
        rows = pl.ds(pl.multiple_of(ci * CHUNK, CHUNK), CHUNK)
        outs = _mix_chunks([q_ref[b, rows, kc[h]] * (dk ** -0.5) for b, h in streams],
                           [k_ref[b, rows, kc[h]] for b, h in streams], [v_ref[b, rows, vc[h]] for b, h in streams],
                           [la_ref[b, rows, kc[h]] for b, h in streams],
                           [st_ref.at[n] for n in range(len(streams))])
        for n, (b, h) in enumerate(streams):
            o_ref[b, rows, vc[h]] = _gated_norm(outs[n], g_ref[b, rows, vc[h]], gn_ref[...]).astype(o_ref.dtype)
        return carry

    lax.fori_loop(0, n_chunk, body, 0)


def _gla(proj, la, gn, *, tb, hp):
    batch, seq, _ = proj.shape
    wk, wv = hp * GLA_DK, hp * GLA_DV
    n_grp = GLA_HEADS // hp
    spec = lambda width, first: pl.BlockSpec((batch, tb, width), lambda p, t: (0, t, first + p))
    return pl.pallas_call(
        functools.partial(_gla_kernel, n_chunk=tb // CHUNK),
        grid=(n_grp, seq // tb),
        in_specs=[spec(wk, 0), spec(wk, n_grp), spec(wv, n_grp), spec(wv, 2 * n_grp), spec(wk, 0),
                  pl.BlockSpec((1, GLA_DV), lambda p, t: (0, 0))],
        out_specs=spec(wv, 0),
        out_shape=jax.ShapeDtypeStruct((batch, seq, GLA_HEADS * GLA_DV), BF16),
        scratch_shapes=[pltpu.VMEM((batch * hp, GLA_DV, GLA_DK), F32)],
        compiler_params=_cparams(("parallel", "arbitrary")),
        name="gla",
    )(proj, proj, proj, proj, la, gn)


def _hgrn_kernel(q_ref, f_ref, i_ref, g_ref, lb_ref, gn_ref, o_ref, st_ref, *, n_chunk):
    @pl.when(pl.program_id(2) == 0)
    def _():
        st_ref[...] = jnp.zeros_like(st_ref)

    dh = HGRN_DH
    heads = range(st_ref.shape[0])

    def body(ci, carry):
        rows = pl.ds(pl.multiple_of(ci * CHUNK, CHUNK), CHUNK)
        hc = [slice(h * dh, (h + 1) * dh) for h in heads]
        qs, ks, las = [], [], []
        for h in heads:
            log_lb, log_1mlb, one_m_lb = lb_ref[0:1, hc[h]], lb_ref[1:2, hc[h]], lb_ref[2:3, hc[h]]
            hq = q_ref[rows, hc[h]]
            z = f_ref[rows, hc[h]]
            x1 = log_1mlb + _log_sigmoid(z)
            las.append(jnp.maximum(log_lb, x1) + _log1pexp_neg(jnp.abs(log_lb - x1)))
            qs.append(hq * _sigmoid(hq))
            ks.append(one_m_lb * _sigmoid(-z))
        outs = _mix_chunks(qs, ks, [i_ref[rows, hc[h]] for h in heads], las, [st_ref.at[h] for h in heads])
        for h in heads:
            o_ref[rows, hc[h]] = _gated_norm(outs[h], g_ref[rows, hc[h]], gn_ref[...]).astype(o_ref.dtype)
        return carry

    lax.fori_loop(0, n_chunk, body, 0)


def _hgrn(proj, lbc, gn, *, batch, seq, tb, hp):
    m = proj.shape[0]
    nt = seq // tb
    width = hp * HGRN_DH
    n_grp = HGRN_HEADS // hp
    first = 3072 // width
    spec = lambda seg: pl.BlockSpec((tb, width), lambda b, p, t: (b * nt + t, first + seg * n_grp + p))
    return pl.pallas_call(
        functools.partial(_hgrn_kernel, n_chunk=tb // CHUNK),
        grid=(batch, n_grp, nt),
        in_specs=[spec(0), spec(1), spec(2), spec(3),
                  pl.BlockSpec((8, width), lambda b, p, t: (0, p)),
                  pl.BlockSpec((1, HGRN_DH), lambda b, p, t: (0, 0))],
        out_specs=pl.BlockSpec((tb, width), lambda b, p, t: (b * nt + t, p)),
        out_shape=jax.ShapeDtypeStruct((m, HGRN_HEADS * HGRN_DH), BF16),
        scratch_shapes=[pltpu.VMEM((hp, HGRN_DH, HGRN_DH), F32)],
        compiler_params=_cparams(("parallel", "parallel", "arbitrary")),
        name="hgrn",
    )(proj, proj, proj, proj, lbc, gn)


def _xattn_kernel(q_ref, k_ref, v_ref, o_ref):
    for h in range(XA_HEADS):
        cols = slice(h * XA_DH, (h + 1) * XA_DH)
        s = lax.dot_general(q_ref[:, cols], k_ref[:, cols], (((1,), (1,)), ((), ())),
                            preferred_element_type=F32) * (XA_DH ** -0.5)
        p = jnp.exp(s - jnp.max(s, axis=-1, keepdims=True))
        p = p / jnp.sum(p, axis=-1, keepdims=True)
        o_ref[:, cols] = jnp.dot(p.astype(BF16), v_ref[:, cols], preferred_element_type=F32).astype(o_ref.dtype)


def _xattn(q, kv, *, batch, seq, n_mem, tq):
    m, d = q.shape
    nt = seq // tq
    return pl.pallas_call(
        _xattn_kernel,
        grid=(batch, nt),
        in_specs=[
            pl.BlockSpec((tq, d), lambda b, t: (b * nt + t, 0)),
            pl.BlockSpec((n_mem, d), lambda b, t: (b, 0)),
            pl.BlockSpec((n_mem, d), lambda b, t: (b, 1)),
        ],
        out_specs=pl.BlockSpec((tq, d), lambda b, t: (b * nt + t, 0)),
        out_shape=jax.ShapeDtypeStruct((m, d), BF16),
        compiler_params=_cparams(("parallel", "arbitrary")),
        name="xattn",
    )(q, kv, kv)


def _router_kernel(h_ref, g_ref, wr_ref, br_ref, a_ref, meta_ref, cnt_ref, carry_ref):
    tm = h_ref.shape[0]

    @pl.when(pl.program_id(0) == 0)
    def _():
        carry_ref[...] = jnp.zeros_like(carry_ref)

    a = _rms(h_ref[...], g_ref[...])
    a_ref[...] = a.astype(BF16).reshape(a_ref.shape)
    a_hi = a.astype(BF16)
    a_lo = (a - a_hi.astype(F32)).astype(BF16)
    w = wr_ref[...]
    w_hi = w.astype(BF16)
    w_lo = (w - w_hi.astype(F32)).astype(BF16)
    logits = (jnp.dot(a_hi, w_hi, preferred_element_type=F32) + jnp.dot(a_lo, w_hi, preferred_element_type=F32)
              + jnp.dot(a_hi, w_lo, preferred_element_type=F32)) + br_ref[...]
    lane = lax.broadcasted_iota(jnp.int32, (tm, LANES), 1)
    lane_f = lane.astype(F32)
    neg = -jnp.inf

    gl = jnp.where(lane < N_GROUPS, logits, neg)
    gmax = jnp.max(gl, axis=1, keepdims=True)
    gidx = jnp.min(jnp.where(gl == gmax, lane_f, float(LANES)), axis=1, keepdims=True)
    p_group = 1.0 / jnp.sum(jnp.exp(gl - gmax), axis=1, keepdims=True)
    lo = float(N_GROUPS) + gidx * float(EXPERTS_PER_GROUP)
    el = jnp.where((lane_f >= lo) & (lane_f < lo + float(EXPERTS_PER_GROUP)), logits, neg)
    v1 = jnp.max(el, axis=1, keepdims=True)
    i1 = jnp.min(jnp.where(el == v1, lane_f, float(LANES)), axis=1, keepdims=True)
    el2 = jnp.where(lane_f == i1, neg, el)
    v2 = jnp.max(el2, axis=1, keepdims=True)
    i2 = jnp.min(jnp.where(el2 == v2, lane_f, float(LANES)), axis=1, keepdims=True)
    t = jnp.exp(v2 - v1)
    g1 = p_group / (1.0 + t)
    g2 = p_group * t / (1.0 + t)

    hit1 = lane_f == i1
    hit2 = lane_f == i2
    onehot = jnp.where(hit1 | hit2, 1.0, 0.0)
    row = lax.broadcasted_iota(jnp.int32, (tm, tm), 0)
    col = lax.broadcasted_iota(jnp.int32, (tm, tm), 1)
    strict = jnp.where(col < row, 1.0, 0.0).astype(BF16)
    before = jnp.dot(strict, onehot.astype(BF16), preferred_element_type=F32) + carry_ref[0:1, :]
    r1 = jnp.sum(jnp.where(hit1, before, 0.0), axis=1, keepdims=True)
    r2 = jnp.sum(jnp.where(hit2, before, 0.0), axis=1, keepdims=True)
    carry_ref[...] = carry_ref[...] + jnp.sum(onehot, axis=0, keepdims=True)
    cnt_ref[...] = carry_ref[...]

    meta = jnp.zeros((tm, LANES), F32)
    for idx, val in enumerate((i1 - float(N_GROUPS), i2 - float(N_GROUPS), r1, r2, g1, g2)):
        meta = jnp.where(lane == idx, val, meta)
    meta_ref[...] = meta


def _router(h, g, wr, br, *, tm):
    m, d = h.shape
    slab = d // LANES
    return pl.pallas_call(
        _router_kernel,
        grid=(m // tm,),
        in_specs=[
            pl.BlockSpec((tm, d), lambda i: (i, 0)),
            pl.BlockSpec((1, d), lambda i: (0, 0)),
            pl.BlockSpec((d, LANES), lambda i: (0, 0)),
            pl.BlockSpec((1, LANES), lambda i: (0, 0)),
        ],
        out_specs=[
            pl.BlockSpec((tm, slab, LANES), lambda i: (i, 0, 0)),
            pl.BlockSpec((tm, LANES), lambda i: (i, 0)),
            pl.BlockSpec((8, LANES), lambda i: (0, 0)),
        ],
        out_shape=[jax.ShapeDtypeStruct((m, slab, LANES), BF16), jax.ShapeDtypeStruct((m, LANES), F32),
                   jax.ShapeDtypeStruct((8, LANES), F32)],
        scratch_shapes=[pltpu.VMEM((8, LANES), F32)],
        compiler_params=_cparams(("arbitrary",)),
        name="router",
    )(h, g, wr, br)


PAD_PIECES = tuple(EXPERT_BLOCK >> (b + 1) for b in range(EXPERT_BLOCK.bit_length() - 1))


def _dispatch_kernel(d1_ref, d2_ref, fill_ref, npad_ref, nb_ref, a_ref, xs_hbm, slot_ref, st0, st1, zbuf, sem, zsem):
    i = pl.program_id(0)
    n_steps = pl.num_programs(0)
    tm = a_ref.shape[0]
    bufs = (st0, st1)
    rb = zbuf.shape[0]
    n_blocks = xs_hbm.shape[0] // rb
    min_blocks = (2 * tm * n_steps) // rb

    def zero_copies():
        out = []
        for e in range(N_EXPERTS):
            p = npad_ref[e]
            for piece in PAD_PIECES:
                out.append((p & piece != 0, pltpu.make_async_copy(
                    zbuf.at[pl.ds(0, piece)], xs_hbm.at[pl.ds(fill_ref[e] + (p & -(2 * piece)), piece)], zsem.at[0])))
        for b in range(min_blocks, n_blocks):
            out.append((b >= nb_ref[0], pltpu.make_async_copy(zbuf, xs_hbm.at[pl.ds(b * rb, rb)], zsem.at[0])))
        return out

    def wait_rows(s):
        for _ in range(2):
            pltpu.make_async_copy(bufs[s], xs_hbm.at[pl.ds(0, tm)], sem.at[s]).wait()

    @pl.when(i == 0)
    def _():
        def mark_unused(j, carry):
            slot_ref[j] = -1
            return carry
        for e in range(N_EXPERTS):
            lax.fori_loop(fill_ref[e], fill_ref[e] + npad_ref[e], mark_unused, 0)
        lax.fori_loop(nb_ref[0] * rb, slot_ref.shape[0], mark_unused, 0)
        zbuf[...] = jnp.zeros_like(zbuf)
        for cond, cp in zero_copies():
            @pl.when(cond)
            def _():
                cp.start()

    for s in (0, 1):
        @pl.when(lax.rem(i, 2) == s)
        def _():
            @pl.when(i >= 2)
            def _():
                wait_rows(s)

            bufs[s][...] = a_ref[...]
            for r in range(tm):
                tok = i * tm + r
                d1 = d1_ref[tok]
                d2 = d2_ref[tok]
                slot_ref[d1] = 2 * tok
                slot_ref[d2] = 2 * tok + 1
                pltpu.make_async_copy(bufs[s].at[r], xs_hbm.at[d1], sem.at[s]).start(priority=0)
                pltpu.make_async_copy(bufs[s].at[r], xs_hbm.at[d2], sem.at[s]).start(priority=1)

    @pl.when(i == n_steps - 1)
    def _():
        for s in (0, 1):
            @pl.when((lax.rem(n_steps - 1, 2) == s) | ((n_steps >= 2) & (lax.rem(n_steps, 2) == s)))
            def _():
                wait_rows(s)
        for cond, cp in zero_copies():
            @pl.when(cond)
            def _():
                cp.wait()


def _dispatch(dest1, dest2, fill, npad, n_blk, a, *, n_rows, tm):
    m, slab, _ = a.shape
    grid_spec = pltpu.PrefetchScalarGridSpec(
        num_scalar_prefetch=5,
        grid=(m // tm,),
        in_specs=[pl.BlockSpec((tm, slab, LANES), lambda i, *_: (i, 0, 0))],
        out_specs=[pl.BlockSpec(memory_space=pl.ANY), pl.BlockSpec(memory_space=pltpu.SMEM)],
        scratch_shapes=[pltpu.VMEM((tm, slab, LANES), a.dtype), pltpu.VMEM((tm, slab, LANES), a.dtype),
                        pltpu.VMEM((EXPERT_BLOCK, slab, LANES), a.dtype),
                        pltpu.SemaphoreType.DMA((2,)), pltpu.SemaphoreType.DMA((1,))],
    )
    return pl.pallas_call(
        _dispatch_kernel,
        grid_spec=grid_spec,
        out_shape=[jax.ShapeDtypeStruct((n_rows, slab, LANES), a.dtype), jax.ShapeDtypeStruct((n_rows,), jnp.int32)],
        compiler_params=_cparams(("arbitrary",)),
        name="dispatch",
    )(dest1, dest2, fill, npad, n_blk, a)


def _weight_copies(w_hbms, e, wst_refs, wsem):
    copies = []
    for w_hbm, wst_ref in zip(w_hbms, wst_refs):
        rows_per = w_hbm.shape[1] // WEIGHT_CHUNKS
        for c in range(WEIGHT_CHUNKS):
            rows = pl.ds(c * rows_per, rows_per)
            copies.append(pltpu.make_async_copy(w_hbm.at[e, rows, :], wst_ref.at[rows, :], wsem.at[0]))
    return copies


def _load_expert_weights(i, be_ref, nx_ref, w_hbms, wst_refs, wb_refs, wsem):
    def start(e):
        for cp in _weight_copies(w_hbms, e, wst_refs, wsem):
            cp.start(priority=WEIGHT_DMA_PRIORITY)

    @pl.when(i == 0)
    def _():
        start(be_ref[0])

    @pl.when((i == 0) | (be_ref[i] != be_ref[jnp.maximum(i - 1, 0)]))
    def _():
        for cp in _weight_copies(w_hbms, be_ref[i], wst_refs, wsem):
            cp.wait()
        for wst_ref, wb_ref in zip(wst_refs, wb_refs):
            _for_row_chunks(wst_ref.shape[0], lambda rows: wb_ref.__setitem__((rows, slice(None)),
                                                                              wst_ref[rows, :].astype(BF16)))

        @pl.when(nx_ref[i] >= 0)
        def _():
            start(nx_ref[i])


def _expert_kernel(slot_ref, be_ref, nx_ref, nb_ref, x_ref, wg_hbm, wu_hbm, wd_hbm, y_hbm, wsg, wsu, wsd, wgb, wub, wdb,
                   ys0, ys1, sem, wsem, *, n_tok):
    i = pl.program_id(0)
    n_steps = pl.num_programs(0)
    nb = nb_ref[0]
    rb = x_ref.shape[0]
    bufs = (ys0, ys1)

    def wait_block(buf, s):
        pltpu.make_async_copy(buf, y_hbm.at[pl.ds(0, rb)], sem.at[s]).wait()

    def scatter_previous(s):
        o = 1 - s
        for r in range(rb):
            v = jnp.where(i >= 1, slot_ref[jnp.maximum((i - 1) * rb + r, 0)], -1)
            dst = jnp.where(v >= 0, (v & 1) * n_tok + (v >> 1), 2 * n_tok + o * rb + r)
            pltpu.make_async_copy(bufs[o].at[r], y_hbm.at[dst], sem.at[o]).start()

    @pl.when(i == 0)
    def _():
        ys0[...] = jnp.zeros_like(ys0)
        ys1[...] = jnp.zeros_like(ys1)
        spare = pltpu.make_async_copy(ys0, y_hbm.at[pl.ds(2 * n_tok, rb)], sem.at[0])
        spare.start()
        spare.wait()

    for s in (0, 1):
        @pl.when((i < nb) & (lax.rem(i, 2) == s))
        def _():
            _load_expert_weights(i, be_ref, nx_ref, (wg_hbm, wu_hbm, wd_hbm), (wsg, wsu, wsd), (wgb, wub, wdb), wsem)

            @pl.when(i >= 1)
            def _():
                wait_block(bufs[s], s)

            scatter_previous(s)
            x = x_ref[...].reshape(rb, -1)
            hg = jnp.dot(x, wgb[...], preferred_element_type=F32)
            hu = jnp.dot(x, wub[...], preferred_element_type=F32)
            hb = (hg * _sigmoid(hg) * hu).astype(BF16)
            y = jnp.dot(hb, wdb[...], preferred_element_type=F32)
            bufs[s][...] = y.reshape(bufs[s].shape)

        @pl.when((i == nb) & (lax.rem(i, 2) == s))
        def _():
            scatter_previous(s)

    @pl.when(i == n_steps - 1)
    def _():
        for s in (0, 1):
            wait_block(bufs[s], s)


def _experts(slot, block_e, next_e, n_blk, xs, wg, wu, wd, *, rb, n_tok):
    n_rows, slab, _ = xs.shape
    d = slab * LANES
    de = wg.shape[2]
    grid_spec = pltpu.PrefetchScalarGridSpec(
        num_scalar_prefetch=4,
        grid=(n_rows // rb + 1,),
        in_specs=[pl.BlockSpec((rb, slab, LANES), lambda i, sl, be, nx, nb: (jnp.minimum(i, nb[0] - 1), 0, 0))]
        + [pl.BlockSpec(memory_space=pl.ANY)] * 3,
        out_specs=pl.BlockSpec(memory_space=pl.ANY),
        scratch_shapes=[pltpu.VMEM((d, de), F32), pltpu.VMEM((d, de), F32), pltpu.VMEM((de, d), F32),
                        pltpu.VMEM((d, de), BF16), pltpu.VMEM((d, de), BF16), pltpu.VMEM((de, d), BF16),
                        pltpu.VMEM((rb, slab, LANES), F32), pltpu.VMEM((rb, slab, LANES), F32),
                        pltpu.SemaphoreType.DMA((2,)), pltpu.SemaphoreType.DMA((1,))],
    )
    return pl.pallas_call(
        functools.partial(_expert_kernel, n_tok=n_tok),
        grid_spec=grid_spec,
        out_shape=jax.ShapeDtypeStruct((2 * n_tok + 2 * rb, slab, LANES), F32),
        compiler_params=_cparams(("arbitrary",), vmem=EXPERT_VMEM_LIMIT),
        name="experts",
    )(slot, block_e, next_e, n_blk, xs, wg, wu, wd)


def _combine_kernel(y0_ref, y1_ref, h_ref, meta_ref, g_ref, o_ref):
    tm, d = h_ref.shape
    meta = meta_ref[...]
    out = h_ref[...] + meta[:, 4:5] * y0_ref[...].reshape(tm, d) + meta[:, 5:6] * y1_ref[...].reshape(tm, d)
    o_ref[...] = _rms(out, g_ref[...])


def _combine(y, h, meta, g, *, tm):
    m, d = h.shape
    slab = d // LANES
    nt = m // tm
    return pl.pallas_call(
        _combine_kernel,
        grid=(nt,),
        in_specs=[
            pl.BlockSpec((tm, slab, LANES), lambda i: (i, 0, 0)),
            pl.BlockSpec((tm, slab, LANES), lambda i: (nt + i, 0, 0)),
            pl.BlockSpec((tm, d), lambda i: (i, 0)),
            pl.BlockSpec((tm, LANES), lambda i: (i, 0)),
            pl.BlockSpec((1, d), lambda i: (0, 0)),
        ],
        out_specs=pl.BlockSpec((tm, d), lambda i: (i, 0)),
        out_shape=jax.ShapeDtypeStruct((m, d), F32),
        compiler_params=_cparams(("parallel",)),
        name="combine",
    )(y, y, h, meta, g)


def kernel(x, mem, norm_mix_g, w_in, w_gla_alpha_up, b_gla_alpha, gla_out_norm_g, hgrn_lb_logits, hgrn_out_norm_g, w_mix_out, norm_xattn_g, norm_mem_g, w_xattn_q, w_xattn_kv, w_xattn_out, norm_ffn_g, w_router_group, b_router_group, w_router_expert, b_router_expert, w_expert_gate, w_expert_up, w_expert_down, norm_final_g):
    batch, seq, d = x.shape
    n_mem = mem.shape[1]
    m = batch * seq
    depth = norm_mix_g.shape[0]
    h = x.reshape(m, d)
    lb_all = jnp.cumsum(jax.nn.softmax(hgrn_lb_logits.astype(F32), axis=0), axis=0)
    gla_cols = 2 * GLA_HEADS * GLA_DK + 2 * GLA_HEADS * GLA_DV
    lr_rank = w_gla_alpha_up.shape[1]

    for l in range(depth):
        w_t = jnp.swapaxes(w_in[l], 0, 1)
        w_lr_t = jnp.pad(w_t[gla_cols:gla_cols + lr_rank], ((0, LANES - lr_rank), (0, 0))).astype(BF16)
        w_up = jnp.pad(w_gla_alpha_up[l], ((0, LANES - lr_rank), (0, 0))).astype(BF16)
        a_mix, la = _norm_gate(h, norm_mix_g[l][None, :], w_lr_t, w_up, b_gla_alpha[l][None, :], tm=min(m, 512))
        proj = _inproj(a_mix, w_t, tm=min(m, 4096), tn=512, skip_from=gla_cols, skip=lr_rank)
        lb = lb_all[l]
        lbc = jnp.zeros((8, lb.shape[0]), F32).at[0].set(jnp.log(lb)).at[1].set(jnp.log1p(-lb)).at[2].set(1.0 - lb)
        o_gla = _gla(proj.reshape(batch, seq, -1), la.reshape(batch, seq, -1), gla_out_norm_g[l][None, :],
                     tb=min(seq, 256), hp=4).reshape(m, -1)
        o_h = _hgrn(proj, lbc, hgrn_out_norm_g[l][None, :], batch=batch, seq=seq, tb=min(seq, 512), hp=8)
        h = _mm_res([o_gla, o_h], w_mix_out[l], h, tm=min(m, 512), tn=d, name="mix_out")

        kv = _norm_mm(mem.reshape(batch * n_mem, d), norm_mem_g[l][None, :], w_xattn_kv[l],
                      tm=batch * n_mem, tn=1024, out_dtype=BF16, name="mem_kv")
        q = _norm_mm(h, norm_xattn_g[l][None, :], w_xattn_q[l], tm=min(m, 512), tn=d, out_dtype=BF16,
                     name="xattn_q")
        o = _xattn(q, kv, batch=batch, seq=seq, n_mem=n_mem, tq=min(seq, 512))
        h = _mm_res([o], w_xattn_out[l], h, tm=min(m, 512), tn=d, name="xattn_out")

        wr = jnp.pad(jnp.concatenate([w_router_group[l], w_router_expert[l]], axis=1),
                     ((0, 0), (0, LANES - N_GROUPS - N_EXPERTS)))
        br = jnp.pad(jnp.concatenate([b_router_group[l], b_router_expert[l]]), (0, LANES - N_GROUPS - N_EXPERTS))
        a, meta, cnt = _router(h, norm_ffn_g[l][None, :], wr, br[None, :], tm=min(m, 512))

        e_idx = meta[:, 0:2].astype(jnp.int32)
        rank = meta[:, 2:4].astype(jnp.int32)
        counts = cnt[0, N_GROUPS:N_GROUPS + N_EXPERTS].astype(jnp.int32)
        padded = ((counts + EXPERT_BLOCK - 1) // EXPERT_BLOCK) * EXPERT_BLOCK
        pad_end = jnp.cumsum(padded)
        pad_start = pad_end - padded
        n_rows = 2 * m + N_EXPERTS * EXPERT_BLOCK
        n_blocks = n_rows // EXPERT_BLOCK
        n_blk = (pad_end[-1:] // EXPERT_BLOCK).astype(jnp.int32)
        blk_first = jnp.arange(n_blocks, dtype=jnp.int32) * EXPERT_BLOCK
        block_e = jnp.minimum(jnp.sum((pad_end[None, :] <= blk_first[:, None]).astype(jnp.int32), axis=1),
                              N_EXPERTS - 1)
        after = (pad_end // EXPERT_BLOCK)[block_e]
        next_e = jnp.where(after < n_blk[0], block_e[jnp.minimum(after, n_blocks - 1)], -1).astype(jnp.int32)

        experts = jnp.arange(N_EXPERTS, dtype=jnp.int32)
        dest = [jnp.sum(jnp.where(e_idx[:, k, None] == experts[None, :], pad_start[None, :], 0), axis=1) + rank[:, k]
                for k in range(2)]
        xs, slot = _dispatch(dest[0], dest[1], pad_start + counts, padded - counts, n_blk, a, n_rows=n_rows,
                             tm=min(m, 256))
        y = _experts(slot, block_e, next_e, n_blk, xs, w_expert_gate[l], w_expert_up[l], w_expert_down[l],
                     rb=EXPERT_BLOCK, n_tok=m)
        last = l == depth - 1
        g_fin = norm_final_g[None, :] if last else jnp.ones((1, d), F32)
        h = _combine(y, h, meta, g_fin, tm=min(m, 256))
        assert last, "the combine kernel fuses the final rmsnorm; deeper stacks need an un-normalised variant"

    return h.reshape(batch, seq, d)
```

```python
import functools

import jax
import jax.numpy as jnp
from jax import lax
from jax.experimental import pallas as pl
from jax.experimental.pallas import tpu as pltpu

F32 = jnp.float32
BF16 = jnp.bfloat16

EPS = 1e-6
CHUNK = 64
LANES = 128
GLA_HEADS, GLA_DK, GLA_DV = 4, 128, 256
HGRN_HEADS, HGRN_DH = 8, 128
XA_HEADS, XA_DH = 4, 512
N_GROUPS, EXPERTS_PER_GROUP, N_EXPERTS = 4, 8, 32
EXPERT_BLOCK = 256
WEIGHT_CHUNKS = 8
WEIGHT_DMA_PRIORITY = 1
VMEM_LIMIT = 56 * 1024 * 1024
EXPERT_VMEM_LIMIT = 60 * 1024 * 1024


def _cparams(sem, vmem=VMEM_LIMIT):
    return pltpu.CompilerParams(dimension_semantics=sem, vmem_limit_bytes=vmem)


def _log1pexp_neg(t):
    return jnp.log(1.0 + jnp.exp(-t))


def _log_sigmoid(z):
    return jnp.minimum(z, 0.0) - _log1pexp_neg(jnp.abs(z))


def _sigmoid(z):
    return 1.0 / (1.0 + jnp.exp(-z))


def _rms(x, g):
    return x * lax.rsqrt(jnp.mean(x * x, axis=-1, keepdims=True) + EPS) * g


def _nt_dot(x, y):
    return lax.dot_general(x, y, (((1,), (1,)), ((), ())), preferred_element_type=F32)


NORM_ROWS = 256


def _for_row_chunks(n_rows, fn):
    step = min(NORM_ROWS, n_rows)

    def body(ci, carry):
        fn(pl.ds(pl.multiple_of(ci * step, step), step))
        return carry

    lax.fori_loop(0, n_rows // step, body, 0)


def _norm_gate_kernel(x_ref, g_ref, wlr_ref, wup_ref, bal_ref, a_ref, la_ref):
    a = _rms(x_ref[...], g_ref[...]).astype(BF16)
    a_ref[...] = a
    lr = _nt_dot(a, wlr_ref[...])
    z = jnp.dot(lr.astype(BF16), wup_ref[...], preferred_element_type=F32) + bal_ref[...]
    la_ref[...] = _log_sigmoid(z) * (1.0 / 16.0)


def _norm_gate(x, g, w_lr_t, w_up, b_al, *, tm):
    m, d = x.shape
    nk = w_up.shape[1]
    return pl.pallas_call(
        _norm_gate_kernel,
        grid=(m // tm,),
        in_specs=[
            pl.BlockSpec((tm, d), lambda i: (i, 0)),
            pl.BlockSpec((1, d), lambda i: (0, 0)),
            pl.BlockSpec((LANES, d), lambda i: (0, 0)),
            pl.BlockSpec((LANES, nk), lambda i: (0, 0)),
            pl.BlockSpec((1, nk), lambda i: (0, 0)),
        ],
        out_specs=[pl.BlockSpec((tm, d), lambda i: (i, 0)), pl.BlockSpec((tm, nk), lambda i: (i, 0))],
        out_shape=[jax.ShapeDtypeStruct((m, d), BF16), jax.ShapeDtypeStruct((m, nk), F32)],
        compiler_params=_cparams(("parallel",)),
        name="norm_gate",
    )(x, g, w_lr_t, w_up, b_al)


def _inproj_kernel(a_ref, w_ref, proj_ref):
    proj_ref[...] = _nt_dot(a_ref[...], w_ref[...].astype(BF16))


def _inproj(a, w_t, *, tm, tn, skip_from, skip):
    m, d = a.shape
    n = w_t.shape[0] - skip
    first_after = skip_from // tn

    def w_rows(i, j):
        return (pl.multiple_of(j * tn + jnp.where(j >= first_after, skip, 0), 8), 0)

    return pl.pallas_call(
        _inproj_kernel,
        grid=(m // tm, n // tn),
        in_specs=[
            pl.BlockSpec((tm, d), lambda i, j: (i, 0), pipeline_mode=pl.Buffered(1)),
            pl.BlockSpec((pl.Element(tn), pl.Element(d)), w_rows),
        ],
        out_specs=pl.BlockSpec((tm, tn), lambda i, j: (i, j)),
        out_shape=jax.ShapeDtypeStruct((m, n), F32),
        compiler_params=_cparams(("parallel", "arbitrary")),
        name="inproj",
    )(a, w_t)


def _weight_spec(k, n, tn):
    if tn == n:
        return pl.BlockSpec((k, n), lambda i, j: (0, 0), pipeline_mode=pl.Buffered(1))
    return pl.BlockSpec((k, tn), lambda i, j: (0, j))


def _cast_weight(w_ref, wb_ref, resident):
    if resident:
        @pl.when((pl.program_id(0) == 0) & (pl.program_id(1) == 0))
        def _():
            wb_ref[...] = w_ref[...].astype(BF16)
    else:
        wb_ref[...] = w_ref[...].astype(BF16)


def _norm_mm_kernel(x_ref, g_ref, w_ref, o_ref, a_ref, wb_ref, *, resident):
    @pl.when(pl.program_id(1) == 0)
    def _():
        def rows_fn(rows):
            a_ref[rows, :] = _rms(x_ref[rows, :], g_ref[...]).astype(BF16)

        _for_row_chunks(x_ref.shape[0], rows_fn)

    _cast_weight(w_ref, wb_ref, resident)
    o_ref[...] = jnp.dot(a_ref[...], wb_ref[...], preferred_element_type=F32).astype(o_ref.dtype)


def _norm_mm(x, g, w, *, tm, tn, out_dtype, name):
    m, d = x.shape
    n = w.shape[1]
    return pl.pallas_call(
        functools.partial(_norm_mm_kernel, resident=tn == n),
        grid=(m // tm, n // tn),
        in_specs=[
            pl.BlockSpec((tm, d), lambda i, j: (i, 0)),
            pl.BlockSpec((1, d), lambda i, j: (0, 0)),
            _weight_spec(d, n, tn),
        ],
        out_specs=pl.BlockSpec((tm, tn), lambda i, j: (i, j)),
        out_shape=jax.ShapeDtypeStruct((m, n), out_dtype),
        scratch_shapes=[pltpu.VMEM((tm, d), BF16), pltpu.VMEM((d, tn), BF16)],
        compiler_params=_cparams(("arbitrary", "arbitrary")),
        name=name,
    )(x, g, w)


def _mm_res_kernel(*refs, n_lhs, resident):
    lhs = refs[:n_lhs]
    w_ref, res_ref, o_ref, wb_ref = refs[n_lhs:]
    _cast_weight(w_ref, wb_ref, resident)
    acc = res_ref[...]
    k0 = 0
    for l_ref in lhs:
        kp = l_ref.shape[1]
        acc = acc + jnp.dot(l_ref[...], wb_ref[k0:k0 + kp, :], preferred_element_type=F32)
        k0 += kp
    o_ref[...] = acc


def _mm_res(lhs_parts, w, res, *, tm, tn, name):
    m, n = res.shape
    k = w.shape[0]
    n_lhs = len(lhs_parts)
    in_specs = [pl.BlockSpec((tm, p.shape[1]), lambda i, j: (i, 0)) for p in lhs_parts]
    in_specs += [_weight_spec(k, n, tn), pl.BlockSpec((tm, tn), lambda i, j: (i, j))]
    return pl.pallas_call(
        functools.partial(_mm_res_kernel, n_lhs=n_lhs, resident=tn == n),
        grid=(m // tm, n // tn),
        in_specs=in_specs,
        out_specs=pl.BlockSpec((tm, tn), lambda i, j: (i, j)),
        out_shape=jax.ShapeDtypeStruct((m, n), F32),
        scratch_shapes=[pltpu.VMEM((k, tn), BF16)],
        compiler_params=_cparams(("arbitrary", "arbitrary")),
        name=name,
    )(*lhs_parts, w, res)


LEVELS = (32, 16, 8, 4, 2, 1)
LOG2E = 1.4426950408889634


def _split3_bf16(x):
    def top(v):
        bits = lax.bitcast_convert_type(v, jnp.uint32) & jnp.uint32(0xFFFF0000)
        return lax.bitcast_convert_type(bits, F32)
    hi = top(x)
    r1 = x - hi
    mid = top(r1)
    lo = r1 - mid
    return hi.astype(BF16), mid.astype(BF16), lo.astype(BF16)


def _mix_chunks(qs, ks, vs, las, st_refs):
    heads = range(len(qs))
    c, dk = qs[0].shape
    row = lax.broadcasted_iota(jnp.int32, (c, c), 0)
    col = lax.broadcasted_iota(jnp.int32, (c, c), 1)
    rowk = lax.broadcasted_iota(jnp.int32, (c, dk), 0)
    xor = jnp.bitwise_xor(row, col)

    las = [la * LOG2E for la in las]

    tri = jnp.where(col <= row, 1.0, 0.0).astype(BF16)
    b3 = [jnp.dot(tri, jnp.concatenate(_split3_bf16(la), axis=1), preferred_element_type=F32) for la in las]
    bs = [(t[:, :dk] + t[:, dk:2 * dk]) + t[:, 2 * dk:] for t in b3]

    def neg_dist(w, b, la):
        if w >= 4:
            parts = [jnp.broadcast_to(b[base + w - 1:base + w, :], (2 * w, dk)) for base in range(0, c, 2 * w)]
            m = parts[0] if len(parts) == 1 else jnp.concatenate(parts, axis=0)
            return -jnp.abs(b - m)
        if w == 2:
            r4 = jnp.bitwise_and(rowk, 3)
            nxt = pltpu.roll(la, c - 1, 0)
            prv = pltpu.roll(la, 1, 0)
            return jnp.where(r4 == 0, nxt, jnp.where(r4 == 1, 0.0, jnp.where(r4 == 2, la, la + prv)))
        return jnp.where(jnp.bitwise_and(rowk, 1) == 1, la, 0.0)

    acc = [_nt_dot(qs[h].astype(BF16), ks[h].astype(BF16)) for h in heads]
    for w in reversed(LEVELS):
        upper = jnp.bitwise_and(rowk, w) != 0
        xs = [(jnp.where(upper, qs[h], ks[h]) * jnp.exp2(neg_dist(w, bs[h], las[h]))).astype(BF16) for h in heads]
        gs = [_nt_dot(x, x) for x in xs]
        acc = [jnp.where(xor >= w, gs[h], acc[h]) for h in heads]
    ab = [jnp.where(col <= row, a, 0.0).astype(BF16) for a in acc]

    sts = [st_refs[h][...] for h in heads]
    qx = [(qs[h] * jnp.exp2(bs[h])).astype(BF16) for h in heads]
    b_last = [b[c - 1:c, :] for b in bs]
    kx = [(ks[h] * jnp.exp2(b_last[h] - bs[h])).astype(BF16) for h in heads]
    vb = [v.astype(BF16) for v in vs]
    outs = [jnp.dot(ab[h], vb[h], preferred_element_type=F32) + _nt_dot(qx[h], sts[h].astype(BF16)) for h in heads]
    for h in heads:
        st_refs[h][...] = jnp.exp2(b_last[h]) * sts[h] + lax.dot_general(
            vb[h], kx[h], (((0,), (0,)), ((), ())), preferred_element_type=F32)
    return outs


def _gated_norm(o, g, gn):
    return _rms(o, gn) * (g * _sigmoid(g))


def _gla_kernel(q_ref, k_ref, v_ref, g_ref, la_ref, gn_ref, o_ref, st_ref, *, n_chunk):
    @pl.when(pl.program_id(1) == 0)
    def _():
        st_ref[...] = jnp.zeros_like(st_ref)

    dk, dv = GLA_DK, GLA_DV
    n_b = q_ref.shape[0]
    n_h = st_ref.shape[0] // n_b
    streams = [(b, h) for b in range(n_b) for h in range(n_h)]
    kc = [slice(h * dk, (h + 1) * dk) for h in range(n_h)]
    vc = [slice(h * dv, (h + 1) * dv) for h in range(n_h)]

    def body(ci, carry):
        rows = pl.ds(pl.multiple_of(ci * CHUNK, CHUNK), CHUNK)
        outs = _mix_chunks([q_ref[b, rows, kc[h]] * (dk ** -0.5) for b, h in streams],
                           [k_ref[b, rows, kc[h]] for b, h in streams], [v_ref[b, rows, vc[h]] for b, h in streams],
                           [la_ref[b, rows, kc[h]] for b, h in streams],
                           [st_ref.at[n] for n in range(len(streams))])
        for n, (b, h) in enumerate(streams):
            o_ref[b, rows, vc[h]] = _gated_norm(outs[n], g_ref[b, rows, vc[h]], gn_ref[...]).astype(o_ref.dtype)
        return carry

    lax.fori_loop(0, n_chunk, body, 0)


def _gla(proj, la, gn, *, tb, hp):
    batch, seq, _ = proj.shape
    wk, wv = hp * GLA_DK, hp * GLA_DV
    n_grp = GLA_HEADS // hp
    spec = lambda width, first: pl.BlockSpec((batch, tb, width), lambda p, t: (0, t, first + p))
    return pl.pallas_call(
        functools.partial(_gla_kernel, n_chunk=tb // CHUNK),
        grid=(n_grp, seq // tb),
        in_specs=[spec(wk, 0), spec(wk, n_grp), spec(wv, n_grp), spec(wv, 2 * n_grp), spec(wk, 0),
                  pl.BlockSpec((1, GLA_DV), lambda p, t: (0, 0))],
        out_specs=spec(wv, 0),
        out_shape=jax.ShapeDtypeStruct((batch, seq, GLA_HEADS * GLA_DV), BF16),
        scratch_shapes=[pltpu.VMEM((batch * hp, GLA_DV, GLA_DK), F32)],
        compiler_params=_cparams(("parallel", "arbitrary")),
        name="gla",
    )(proj, proj, proj, proj, la, gn)


def _hgrn_kernel(q_ref, f_ref, i_ref, g_ref, lb_ref, gn_ref, o_ref, st_ref, *, n_chunk):
    @pl.when(pl.program_id(2) == 0)
    def _():
        st_ref[...] = jnp.zeros_like(st_ref)

    dh = HGRN_DH
    heads = range(st_ref.shape[0])

    def body(ci, carry):
        rows = pl.ds(pl.multiple_of(ci * CHUNK, CHUNK), CHUNK)
        hc = [slice(h * dh, (h + 1) * dh) for h in heads]
        qs, ks, las = [], [], []
        for h in heads:
            log_lb, log_1mlb, one_m_lb = lb_ref[0:1, hc[h]], lb_ref[1:2, hc[h]], lb_ref[2:3, hc[h]]
            hq = q_ref[rows, hc[h]]
            z = f_ref[rows, hc[h]]
            x1 = log_1mlb + _log_sigmoid(z)
            las.append(jnp.maximum(log_lb, x1) + _log1pexp_neg(jnp.abs(log_lb - x1)))
            qs.append(hq * _sigmoid(hq))
            ks.append(one_m_lb * _sigmoid(-z))
        outs = _mix_chunks(qs, ks, [i_ref[rows, hc[h]] for h in heads], las, [st_ref.at[h] for h in heads])
        for h in heads:
            o_ref[rows, hc[h]] = _gated_norm(outs[h], g_ref[rows, hc[h]], gn_ref[...]).astype(o_ref.dtype)
        return carry

    lax.fori_loop(0, n_chunk, body, 0)


def _hgrn(proj, lbc, gn, *, batch, seq, tb, hp):
    m = proj.shape[0]
    nt = seq // tb
    width = hp * HGRN_DH
    n_grp = HGRN_HEADS // hp
    first = 3072 // width
    spec = lambda seg: pl.BlockSpec((tb, width), lambda b, p, t: (b * nt + t, first + seg * n_grp + p))
    return pl.pallas_call(
        functools.partial(_hgrn_kernel, n_chunk=tb // CHUNK),
        grid=(batch, n_grp, nt),
        in_specs=[spec(0), spec(1), spec(2), spec(3),
                  pl.BlockSpec((8, width), lambda b, p, t: (0, p)),
                  pl.BlockSpec((1, HGRN_DH), lambda b, p, t: (0, 0))],
        out_specs=pl.BlockSpec((tb, width), lambda b, p, t: (b * nt + t, p)),
        out_shape=jax.ShapeDtypeStruct((m, HGRN_HEADS * HGRN_DH), BF16),
        scratch_shapes=[pltpu.VMEM((hp, HGRN_DH, HGRN_DH), F32)],
        compiler_params=_cparams(("parallel", "parallel", "arbitrary")),
        name="hgrn",
    )(proj, proj, proj, proj, lbc, gn)


def _xattn_kernel(q_ref, k_ref, v_ref, o_ref):
    cols = [slice(h * XA_DH, (h + 1) * XA_DH) for h in range(XA_HEADS)]
    s = [_nt_dot(q_ref[:, c], k_ref[:, c]) * (XA_DH ** -0.5) for c in cols]
    p = [jnp.exp(x - jnp.max(x, axis=-1, keepdims=True)) for x in s]
    p = [(x / jnp.sum(x, axis=-1, keepdims=True)).astype(BF16) for x in p]
    for c, x in zip(cols, p):
        o_ref[:, c] = jnp.dot(x, v_ref[:, c], preferred_element_type=F32).astype(o_ref.dtype)


def _xattn(q, kv, *, batch, seq, n_mem, tq):
    m, d = q.shape
    nt = seq // tq
    return pl.pallas_call(
        _xattn_kernel,
        grid=(batch, nt),
        in_specs=[
            pl.BlockSpec((tq, d), lambda b, t: (b * nt + t, 0)),
            pl.BlockSpec((n_mem, d), lambda b, t: (b, 0)),
            pl.BlockSpec((n_mem, d), lambda b, t: (b, 1)),
        ],
        out_specs=pl.BlockSpec((tq, d), lambda b, t: (b * nt + t, 0)),
        out_shape=jax.ShapeDtypeStruct((m, d), BF16),
        compiler_params=_cparams(("parallel", "arbitrary")),
        name="xattn",
    )(q, kv, kv)


def _router_kernel(h_ref, g_ref, wr_ref, br_ref, a_ref, meta_ref, cnt_ref, carry_ref):
    tm = h_ref.shape[0]

    @pl.when(pl.program_id(0) == 0)
    def _():
        carry_ref[...] = jnp.zeros_like(carry_ref)

    a = _rms(h_ref[...], g_ref[...])
    a_ref[...] = a.astype(BF16).reshape(a_ref.shape)
    a_hi = a.astype(BF16)
    a_lo = (a - a_hi.astype(F32)).astype(BF16)
    w = wr_ref[...]
    w_hi = w.astype(BF16)
    w_lo = (w - w_hi.astype(F32)).astype(BF16)
    logits = (jnp.dot(a_hi, w_hi, preferred_element_type=F32) + jnp.dot(a_lo, w_hi, preferred_element_type=F32)
              + jnp.dot(a_hi, w_lo, preferred_element_type=F32)) + br_ref[...]
    lane = lax.broadcasted_iota(jnp.int32, (tm, LANES), 1)
    lane_f = lane.astype(F32)
    neg = -jnp.inf

    gl = jnp.where(lane < N_GROUPS, logits, neg)
    gmax = jnp.max(gl, axis=1, keepdims=True)
    gidx = jnp.min(jnp.where(gl == gmax, lane_f, float(LANES)), axis=1, keepdims=True)
    p_group = 1.0 / jnp.sum(jnp.exp(gl - gmax), axis=1, keepdims=True)
    lo = float(N_GROUPS) + gidx * float(EXPERTS_PER_GROUP)
    el = jnp.where((lane_f >= lo) & (lane_f < lo + float(EXPERTS_PER_GROUP)), logits, neg)
    v1 = jnp.max(el, axis=1, keepdims=True)
    i1 = jnp.min(jnp.where(el == v1, lane_f, float(LANES)), axis=1, keepdims=True)
    el2 = jnp.where(lane_f == i1, neg, el)
    v2 = jnp.max(el2, axis=1, keepdims=True)
    i2 = jnp.min(jnp.where(el2 == v2, lane_f, float(LANES)), axis=1, keepdims=True)
    t = jnp.exp(v2 - v1)
    g1 = p_group / (1.0 + t)
    g2 = p_group * t / (1.0 + t)

    hit1 = lane_f == i1
    hit2 = lane_f == i2
    onehot = jnp.where(hit1 | hit2, 1.0, 0.0)
    row = lax.broadcasted_iota(jnp.int32, (tm, tm), 0)
    col = lax.broadcasted_iota(jnp.int32, (tm, tm), 1)
    strict = jnp.where(col < row, 1.0, 0.0).astype(BF16)
    before = jnp.dot(strict, onehot.astype(BF16), preferred_element_type=F32) + carry_ref[0:1, :]
    r1 = jnp.sum(jnp.where(hit1, before, 0.0), axis=1, keepdims=True)
    r2 = jnp.sum(jnp.where(hit2, before, 0.0), axis=1, keepdims=True)
    carry_ref[...] = carry_ref[...] + jnp.sum(onehot, axis=0, keepdims=True)
    cnt_ref[...] = carry_ref[...]

    meta = jnp.zeros((tm, LANES), F32)
    for idx, val in enumerate((i1 - float(N_GROUPS), i2 - float(N_GROUPS), r1, r2, g1, g2)):
        meta = jnp.where(lane == idx, val, meta)
    meta_ref[...] = meta


def _router(h, g, wr, br, *, tm):
    m, d = h.shape
    slab = d // LANES
    return pl.pallas_call(
        _router_kernel,
        grid=(m // tm,),
        in_specs=[
            pl.BlockSpec((tm, d), lambda i: (i, 0)),
            pl.BlockSpec((1, d), lambda i: (0, 0)),
            pl.BlockSpec((d, LANES), lambda i: (0, 0)),
            pl.BlockSpec((1, LANES), lambda i: (0, 0)),
        ],
        out_specs=[
            pl.BlockSpec((tm, slab, LANES), lambda i: (i, 0, 0)),
            pl.BlockSpec((tm, LANES), lambda i: (i, 0)),
            pl.BlockSpec((8, LANES), lambda i: (0, 0)),
        ],
        out_shape=[jax.ShapeDtypeStruct((m, slab, LANES), BF16), jax.ShapeDtypeStruct((m, LANES), F32),
                   jax.ShapeDtypeStruct((8, LANES), F32)],
        scratch_shapes=[pltpu.VMEM((8, LANES), F32)],
        compiler_params=_cparams(("arbitrary",)),
        name="router",
    )(h, g, wr, br)


PAD_PIECES = tuple(EXPERT_BLOCK >> (b + 1) for b in range(EXPERT_BLOCK.bit_length() - 1))


def _dispatch_kernel(d1_ref, d2_ref, fill_ref, npad_ref, nb_ref, a_ref, xs_hbm, slot_ref, st0, st1, zbuf, sem, zsem):
    i = pl.program_id(0)
    n_steps = pl.num_programs(0)
    tm = a_ref.shape[0]
    bufs = (st0, st1)
    rb = zbuf.shape[0]
    n_blocks = xs_hbm.shape[0] // rb
    min_blocks = (2 * tm * n_steps) // rb

    def zero_copies():
        out = []
        for e in range(N_EXPERTS):
            p = npad_ref[e]
            for piece in PAD_PIECES:
                out.append((p & piece != 0, pltpu.make_async_copy(
                    zbuf.at[pl.ds(0, piece)], xs_hbm.at[pl.ds(fill_ref[e] + (p & -(2 * piece)), piece)], zsem.at[0])))
        for b in range(min_blocks, n_blocks):
            out.append((b >= nb_ref[0], pltpu.make_async_copy(zbuf, xs_hbm.at[pl.ds(b * rb, rb)], zsem.at[0])))
        return out

    def wait_rows(s):
        for _ in range(2):
            pltpu.make_async_copy(bufs[s], xs_hbm.at[pl.ds(0, tm)], sem.at[s]).wait()

    @pl.when(i == 0)
    def _():
        def mark_unused(j, carry):
            slot_ref[j] = -1
            return carry
        for e in range(N_EXPERTS):
            lax.fori_loop(fill_ref[e], fill_ref[e] + npad_ref[e], mark_unused, 0)
        lax.fori_loop(nb_ref[0] * rb, slot_ref.shape[0], mark_unused, 0)
        zbuf[...] = jnp.zeros_like(zbuf)
        for cond, cp in zero_copies():
            @pl.when(cond)
            def _():
                cp.start()

    for s in (0, 1):
        @pl.when(lax.rem(i, 2) == s)
        def _():
            @pl.when(i >= 2)
            def _():
                wait_rows(s)

            bufs[s][...] = a_ref[...]
            for r in range(tm):
                tok = i * tm + r
                d1 = d1_ref[tok]
                d2 = d2_ref[tok]
                slot_ref[d1] = 2 * tok
                slot_ref[d2] = 2 * tok + 1
                pltpu.make_async_copy(bufs[s].at[r], xs_hbm.at[d1], sem.at[s]).start(priority=0)
                pltpu.make_async_copy(bufs[s].at[r], xs_hbm.at[d2], sem.at[s]).start(priority=1)

    @pl.when(i == n_steps - 1)
    def _():
        for s in (0, 1):
            @pl.when((lax.rem(n_steps - 1, 2) == s) | ((n_steps >= 2) & (lax.rem(n_steps, 2) == s)))
            def _():
                wait_rows(s)
        for cond, cp in zero_copies():
            @pl.when(cond)
            def _():
                cp.wait()


def _dispatch(dest1, dest2, fill, npad, n_blk, a, *, n_rows, tm):
    m, slab, _ = a.shape
    grid_spec = pltpu.PrefetchScalarGridSpec(
        num_scalar_prefetch=5,
        grid=(m // tm,),
        in_specs=[pl.BlockSpec((tm, slab, LANES), lambda i, *_: (i, 0, 0))],
        out_specs=[pl.BlockSpec(memory_space=pl.ANY), pl.BlockSpec(memory_space=pltpu.SMEM)],
        scratch_shapes=[pltpu.VMEM((tm, slab, LANES), a.dtype), pltpu.VMEM((tm, slab, LANES), a.dtype),
                        pltpu.VMEM((EXPERT_BLOCK, slab, LANES), a.dtype),
                        pltpu.SemaphoreType.DMA((2,)), pltpu.SemaphoreType.DMA((1,))],
    )
    return pl.pallas_call(
        _dispatch_kernel,
        grid_spec=grid_spec,
        out_shape=[jax.ShapeDtypeStruct((n_rows, slab, LANES), a.dtype), jax.ShapeDtypeStruct((n_rows,), jnp.int32)],
        compiler_params=_cparams(("arbitrary",)),
        name="dispatch",
    )(dest1, dest2, fill, npad, n_blk, a)


def _weight_copies(w_hbms, e, wst_refs, wsem):
    copies = []
    for w_hbm, wst_ref in zip(w_hbms, wst_refs):
        rows_per = w_hbm.shape[1] // WEIGHT_CHUNKS
        for c in range(WEIGHT_CHUNKS):
            rows = pl.ds(c * rows_per, rows_per)
            copies.append(pltpu.make_async_copy(w_hbm.at[e, rows, :], wst_ref.at[rows, :], wsem.at[0]))
    return copies


def _load_expert_weights(i, be_ref, nx_ref, w_hbms, wst_refs, wb_refs, wsem):
    def start(e):
        for cp in _weight_copies(w_hbms, e, wst_refs, wsem):
            cp.start(priority=WEIGHT_DMA_PRIORITY)

    @pl.when(i == 0)
    def _():
        start(be_ref[0])

    @pl.when((i == 0) | (be_ref[i] != be_ref[jnp.maximum(i - 1, 0)]))
    def _():
        for cp in _weight_copies(w_hbms, be_ref[i], wst_refs, wsem):
            cp.wait()
        for wst_ref, wb_ref in zip(wst_refs, wb_refs):
            _for_row_chunks(wst_ref.shape[0], lambda rows: wb_ref.__setitem__((rows, slice(None)),
                                                                              wst_ref[rows, :].astype(BF16)))

        @pl.when(nx_ref[i] >= 0)
        def _():
            start(nx_ref[i])


def _expert_kernel(slot_ref, be_ref, nx_ref, nb_ref, x_ref, wg_hbm, wu_hbm, wd_hbm, y_hbm, wsg, wsu, wsd, wgb, wub, wdb,
                   ys0, ys1, sem, wsem, *, n_tok):
    i = pl.program_id(0)
    n_steps = pl.num_programs(0)
    nb = nb_ref[0]
    rb = x_ref.shape[0]
    bufs = (ys0, ys1)

    def wait_block(buf, s):
        pltpu.make_async_copy(buf, y_hbm.at[pl.ds(0, rb)], sem.at[s]).wait()

    def scatter_previous(s):
        o = 1 - s
        for r in range(rb):
            v = jnp.where(i >= 1, slot_ref[jnp.maximum((i - 1) * rb + r, 0)], -1)
            dst = jnp.where(v >= 0, (v & 1) * n_tok + (v >> 1), 2 * n_tok + o * rb + r)
            pltpu.make_async_copy(bufs[o].at[r], y_hbm.at[dst], sem.at[o]).start()

    @pl.when(i == 0)
    def _():
        ys0[...] = jnp.zeros_like(ys0)
        ys1[...] = jnp.zeros_like(ys1)
        spare = pltpu.make_async_copy(ys0, y_hbm.at[pl.ds(2 * n_tok, rb)], sem.at[0])
        spare.start()
        spare.wait()

    for s in (0, 1):
        @pl.when((i < nb) & (lax.rem(i, 2) == s))
        def _():
            _load_expert_weights(i, be_ref, nx_ref, (wg_hbm, wu_hbm, wd_hbm), (wsg, wsu, wsd), (wgb, wub, wdb), wsem)

            @pl.when(i >= 1)
            def _():
                wait_block(bufs[s], s)

            scatter_previous(s)
            x = x_ref[...].reshape(rb, -1)
            hg = jnp.dot(x, wgb[...], preferred_element_type=F32)
            hu = jnp.dot(x, wub[...], preferred_element_type=F32)
            hb = (hg * _sigmoid(hg) * hu).astype(BF16)
            y = jnp.dot(hb, wdb[...], preferred_element_type=F32)
            bufs[s][...] = y.astype(bufs[s].dtype).reshape(bufs[s].shape)

        @pl.when((i == nb) & (lax.rem(i, 2) == s))
        def _():
            scatter_previous(s)

    @pl.when(i == n_steps - 1)
    def _():
        for s in (0, 1):
            wait_block(bufs[s], s)


def _experts(slot, block_e, next_e, n_blk, xs, wg, wu, wd, *, rb, n_tok):
    n_rows, slab, _ = xs.shape
    d = slab * LANES
    de = wg.shape[2]
    grid_spec = pltpu.PrefetchScalarGridSpec(
        num_scalar_prefetch=4,
        grid=(n_rows // rb + 1,),
        in_specs=[pl.BlockSpec((rb, slab, LANES), lambda i, sl, be, nx, nb: (jnp.minimum(i, nb[0] - 1), 0, 0))]
        + [pl.BlockSpec(memory_space=pl.ANY)] * 3,
        out_specs=pl.BlockSpec(memory_space=pl.ANY),
        scratch_shapes=[pltpu.VMEM((d, de), F32), pltpu.VMEM((d, de), F32), pltpu.VMEM((de, d), F32),
                        pltpu.VMEM((d, de), BF16), pltpu.VMEM((d, de), BF16), pltpu.VMEM((de, d), BF16),
                        pltpu.VMEM((rb, slab, LANES), BF16), pltpu.VMEM((rb, slab, LANES), BF16),
                        pltpu.SemaphoreType.DMA((2,)), pltpu.SemaphoreType.DMA((1,))],
    )
    return pl.pallas_call(
        functools.partial(_expert_kernel, n_tok=n_tok),
        grid_spec=grid_spec,
        out_shape=jax.ShapeDtypeStruct((2 * n_tok + 2 * rb, slab, LANES), BF16),
        compiler_params=_cparams(("arbitrary",), vmem=EXPERT_VMEM_LIMIT),
        name="experts",
    )(slot, block_e, next_e, n_blk, xs, wg, wu, wd)


def _combine_kernel(y0_ref, y1_ref, h_ref, meta_ref, g_ref, o_ref):
    tm, d = h_ref.shape
    meta = meta_ref[...]
    y0 = y0_ref[...].reshape(tm, d).astype(F32)
    y1 = y1_ref[...].reshape(tm, d).astype(F32)
    out = h_ref[...] + meta[:, 4:5] * y0 + meta[:, 5:6] * y1
    o_ref[...] = _rms(out, g_ref[...])


def _combine(y, h, meta, g, *, tm):
    m, d = h.shape
    slab = d // LANES
    nt = m // tm
    return pl.pallas_call(
        _combine_kernel,
        grid=(nt,),
        in_specs=[
            pl.BlockSpec((tm, slab, LANES), lambda i: (i, 0, 0)),
            pl.BlockSpec((tm, slab, LANES), lambda i: (nt + i, 0, 0)),
            pl.BlockSpec((tm, d), lambda i: (i, 0)),
            pl.BlockSpec((tm, LANES), lambda i: (i, 0)),
            pl.BlockSpec((1, d), lambda i: (0, 0)),
        ],
        out_specs=pl.BlockSpec((tm, d), lambda i: (i, 0)),
        out_shape=jax.ShapeDtypeStruct((m, d), F32),
        compiler_params=_cparams(("parallel",)),
        name="combine",
    )(y, y, h, meta, g)


def kernel(x, mem, norm_mix_g, w_in, w_gla_alpha_up, b_gla_alpha, gla_out_norm_g, hgrn_lb_logits, hgrn_out_norm_g, w_mix_out, norm_xattn_g, norm_mem_g, w_xattn_q, w_xattn_kv, w_xattn_out, norm_ffn_g, w_router_group, b_router_group, w_router_expert, b_router_expert, w_expert_gate, w_expert_up, w_expert_down, norm_final_g):
    batch, seq, d = x.shape
    n_mem = mem.shape[1]
    m = batch * seq
    depth = norm_mix_g.shape[0]
    h = x.reshape(m, d)
    lb_all = jnp.cumsum(jax.nn.softmax(hgrn_lb_logits.astype(F32), axis=0), axis=0)
    gla_cols = 2 * GLA_HEADS * GLA_DK + 2 * GLA_HEADS * GLA_DV
    lr_rank = w_gla_alpha_up.shape[1]

    for l in range(depth):
        w_t = jnp.swapaxes(w_in[l], 0, 1)
        w_lr_t = jnp.pad(w_t[gla_cols:gla_cols + lr_rank], ((0, LANES - lr_rank), (0, 0))).astype(BF16)
        w_up = jnp.pad(w_gla_alpha_up[l], ((0, LANES - lr_rank), (0, 0))).astype(BF16)
        a_mix, la = _norm_gate(h, norm_mix_g[l][None, :], w_lr_t, w_up, b_gla_alpha[l][None, :], tm=min(m, 512))
        proj = _inproj(a_mix, w_t, tm=min(m, 4096), tn=512, skip_from=gla_cols, skip=lr_rank)
        lb = lb_all[l]
        lbc = jnp.zeros((8, lb.shape[0]), F32).at[0].set(jnp.log(lb)).at[1].set(jnp.log1p(-lb)).at[2].set(1.0 - lb)
        o_gla = _gla(proj.reshape(batch, seq, -1), la.reshape(batch, seq, -1), gla_out_norm_g[l][None, :],
                     tb=min(seq, 256), hp=4).reshape(m, -1)
        o_h = _hgrn(proj, lbc, hgrn_out_norm_g[l][None, :], batch=batch, seq=seq, tb=min(seq, 512), hp=8)
        h = _mm_res([o_gla, o_h], w_mix_out[l], h, tm=min(m, 512), tn=d, name="mix_out")

        kv = _norm_mm(mem.reshape(batch * n_mem, d), norm_mem_g[l][None, :], w_xattn_kv[l],
                      tm=batch * n_mem, tn=1024, out_dtype=BF16, name="mem_kv")
        q = _norm_mm(h, norm_xattn_g[l][None, :], w_xattn_q[l], tm=min(m, 512), tn=d, out_dtype=BF16,
                     name="xattn_q")
        o = _xattn(q, kv, batch=batch, seq=seq, n_mem=n_mem, tq=min(seq, 512))
        h = _mm_res([o], w_xattn_out[l], h, tm=min(m, 512), tn=d, name="xattn_out")

        wr = jnp.pad(jnp.concatenate([w_router_group[l], w_router_expert[l]], axis=1),
                     ((0, 0), (0, LANES - N_GROUPS - N_EXPERTS)))
        br = jnp.pad(jnp.concatenate([b_router_group[l], b_router_expert[l]]), (0, LANES - N_GROUPS - N_EXPERTS))
        a, meta, cnt = _router(h, norm_ffn_g[l][None, :], wr, br[None, :], tm=min(m, 512))

        e_idx = meta[:, 0:2].astype(jnp.int32)
        rank = meta[:, 2:4].astype(jnp.int32)
        counts = cnt[0, N_GROUPS:N_GROUPS + N_EXPERTS].astype(jnp.int32)
        padded = ((counts + EXPERT_BLOCK - 1) // EXPERT_BLOCK) * EXPERT_BLOCK
        pad_end = jnp.cumsum(padded)
        pad_start = pad_end - padded
        n_rows = 2 * m + N_EXPERTS * EXPERT_BLOCK
        n_blocks = n_rows // EXPERT_BLOCK
        n_blk = (pad_end[-1:] // EXPERT_BLOCK).astype(jnp.int32)
        blk_first = jnp.arange(n_blocks, dtype=jnp.int32) * EXPERT_BLOCK
        block_e = jnp.minimum(jnp.sum((pad_end[None, :] <= blk_first[:, None]).astype(jnp.int32), axis=1),
                              N_EXPERTS - 1)
        after = (pad_end // EXPERT_BLOCK)[block_e]
        next_e = jnp.where(after < n_blk[0], block_e[jnp.minimum(after, n_blocks - 1)], -1).astype(jnp.int32)

        experts = jnp.arange(N_EXPERTS, dtype=jnp.int32)
        dest = [jnp.sum(jnp.where(e_idx[:, k, None] == experts[None, :], pad_start[None, :], 0), axis=1) + rank[:, k]
                for k in range(2)]
        xs, slot = _dispatch(dest[0], dest[1], pad_start + counts, padded - counts, n_blk, a, n_rows=n_rows,
                             tm=min(m, 256))
        y = _experts(slot, block_e, next_e, n_blk, xs, w_expert_gate[l], w_expert_up[l], w_expert_down[l],
                     rb=EXPERT_BLOCK, n_tok=m)
        last = l == depth - 1
        g_fin = norm_final_g[None, :] if last else jnp.ones((1, d), F32)
        h = _combine(y, h, meta, g_fin, tm=min(m, 256))
        assert last, "the combine kernel fuses the final rmsnorm; deeper stacks need an un-normalised variant"

    return h.reshape(batch, seq, d)
```

```python
import functools

import jax
import jax.numpy as jnp
from jax import lax
from jax.experimental import pallas as pl
from jax.experimental.pallas import tpu as pltpu

F32 = jnp.float32
BF16 = jnp.bfloat16

EPS = 1e-6
CHUNK = 64
LANES = 128
GLA_HEADS, GLA_DK, GLA_DV = 4, 128, 256
HGRN_HEADS, HGRN_DH = 8, 128
HGRN_HP = 8
XA_HEADS, XA_DH = 4, 512
N_GROUPS, EXPERTS_PER_GROUP, N_EXPERTS = 4, 8, 32
EXPERT_BLOCK = 256
WEIGHT_CHUNKS = 8
WEIGHT_DMA_PRIORITY = 1
VMEM_LIMIT = 56 * 1024 * 1024
EXPERT_VMEM_LIMIT = 60 * 1024 * 1024


def _cparams(sem, vmem=VMEM_LIMIT):
    return pltpu.CompilerParams(dimension_semantics=sem, vmem_limit_bytes=vmem)


def _log1pexp_neg(t):
    return jnp.log(1.0 + jnp.exp(-t))


def _log_sigmoid(z):
    return jnp.minimum(z, 0.0) - _log1pexp_neg(jnp.abs(z))


def _sigmoid(z):
    return 1.0 / (1.0 + jnp.exp(-z))


def _rms(x, g):
    return x * lax.rsqrt(jnp.mean(x * x, axis=-1, keepdims=True) + EPS) * g


def _nt_dot(x, y):
    return lax.dot_general(x, y, (((1,), (1,)), ((), ())), preferred_element_type=F32)


NORM_ROWS = 256


def _for_row_chunks(n_rows, fn):
    step = min(NORM_ROWS, n_rows)

    def body(ci, carry):
        fn(pl.ds(pl.multiple_of(ci * step, step), step))
        return carry

    lax.fori_loop(0, n_rows // step, body, 0)


def _norm_gate_kernel(x_ref, g_ref, wlr_ref, wup_ref, bal_ref, a_ref, la_ref):
    a = _rms(x_ref[...], g_ref[...]).astype(BF16)
    a_ref[...] = a
    lr = _nt_dot(a, wlr_ref[...])
    z = jnp.dot(lr.astype(BF16), wup_ref[...], preferred_element_type=F32) + bal_ref[...]
    la_ref[...] = _log_sigmoid(z) * (1.0 / 16.0)


def _norm_gate(x, g, w_lr_t, w_up, b_al, *, tm):
    m, d = x.shape
    nk = w_up.shape[1]
    return pl.pallas_call(
        _norm_gate_kernel,
        grid=(m // tm,),
        in_specs=[
            pl.BlockSpec((tm, d), lambda i: (i, 0)),
            pl.BlockSpec((1, d), lambda i: (0, 0)),
            pl.BlockSpec((LANES, d), lambda i: (0, 0)),
            pl.BlockSpec((LANES, nk), lambda i: (0, 0)),
            pl.BlockSpec((1, nk), lambda i: (0, 0)),
        ],
        out_specs=[pl.BlockSpec((tm, d), lambda i: (i, 0)), pl.BlockSpec((tm, nk), lambda i: (i, 0))],
        out_shape=[jax.ShapeDtypeStruct((m, d), BF16), jax.ShapeDtypeStruct((m, nk), F32)],
        compiler_params=_cparams(("parallel",)),
        name="norm_gate",
    )(x, g, w_lr_t, w_up, b_al)


def _inproj_kernel(a_ref, w_ref, proj_ref):
    proj_ref[...] = _nt_dot(a_ref[...], w_ref[...].astype(BF16))


def _inproj(a, w_t, *, tm, tn, skip_from, skip):
    m, d = a.shape
    n = w_t.shape[0] - skip
    first_after = skip_from // tn

    def w_rows(i, j):
        return (pl.multiple_of(j * tn + jnp.where(j >= first_after, skip, 0), 8), 0)

    return pl.pallas_call(
        _inproj_kernel,
        grid=(m // tm, n // tn),
        in_specs=[
            pl.BlockSpec((tm, d), lambda i, j: (i, 0), pipeline_mode=pl.Buffered(1)),
            pl.BlockSpec((pl.Element(tn), pl.Element(d)), w_rows),
        ],
        out_specs=pl.BlockSpec((tm, tn), lambda i, j: (i, j)),
        out_shape=jax.ShapeDtypeStruct((m, n), F32),
        compiler_params=_cparams(("parallel", "arbitrary")),
        name="inproj",
    )(a, w_t)


def _weight_spec(k, n, tn):
    if tn == n:
        return pl.BlockSpec((k, n), lambda i, j: (0, 0), pipeline_mode=pl.Buffered(1))
    return pl.BlockSpec((k, tn), lambda i, j: (0, j))


def _cast_weight(w_ref, wb_ref, resident):
    if resident:
        @pl.when((pl.program_id(0) == 0) & (pl.program_id(1) == 0))
        def _():
            wb_ref[...] = w_ref[...].astype(BF16)
    else:
        wb_ref[...] = w_ref[...].astype(BF16)


def _norm_mm_kernel(x_ref, g_ref, w_ref, o_ref, a_ref, wb_ref, *, resident):
    @pl.when(pl.program_id(1) == 0)
    def _():
        def rows_fn(rows):
            a_ref[rows, :] = _rms(x_ref[rows, :], g_ref[...]).astype(BF16)

        _for_row_chunks(x_ref.shape[0], rows_fn)

    _cast_weight(w_ref, wb_ref, resident)
    o_ref[...] = jnp.dot(a_ref[...], wb_ref[...], preferred_element_type=F32).astype(o_ref.dtype)


def _norm_mm(x, g, w, *, tm, tn, out_dtype, name):
    m, d = x.shape
    n = w.shape[1]
    return pl.pallas_call(
        functools.partial(_norm_mm_kernel, resident=tn == n),
        grid=(m // tm, n // tn),
        in_specs=[
            pl.BlockSpec((tm, d), lambda i, j: (i, 0)),
            pl.BlockSpec((1, d), lambda i, j: (0, 0)),
            _weight_spec(d, n, tn),
        ],
        out_specs=pl.BlockSpec((tm, tn), lambda i, j: (i, j)),
        out_shape=jax.ShapeDtypeStruct((m, n), out_dtype),
        scratch_shapes=[pltpu.VMEM((tm, d), BF16), pltpu.VMEM((d, tn), BF16)],
        compiler_params=_cparams(("arbitrary", "arbitrary")),
        name=name,
    )(x, g, w)


def _mm_res_kernel(*refs, n_lhs, resident):
    lhs = refs[:n_lhs]
    w_ref, res_ref, o_ref, wb_ref = refs[n_lhs:]
    _cast_weight(w_ref, wb_ref, resident)
    acc = res_ref[...]
    k0 = 0
    for l_ref in lhs:
        kp = l_ref.shape[1]
        acc = acc + jnp.dot(l_ref[...], wb_ref[k0:k0 + kp, :], preferred_element_type=F32)
        k0 += kp
    o_ref[...] = acc


def _mm_res(lhs_parts, w, res, *, tm, tn, name):
    m, n = res.shape
    k = w.shape[0]
    n_lhs = len(lhs_parts)
    in_specs = [pl.BlockSpec((tm, p.shape[1]), lambda i, j: (i, 0)) for p in lhs_parts]
    in_specs += [_weight_spec(k, n, tn), pl.BlockSpec((tm, tn), lambda i, j: (i, j))]
    return pl.pallas_call(
        functools.partial(_mm_res_kernel, n_lhs=n_lhs, resident=tn == n),
        grid=(m // tm, n // tn),
        in_specs=in_specs,
        out_specs=pl.BlockSpec((tm, tn), lambda i, j: (i, j)),
        out_shape=jax.ShapeDtypeStruct((m, n), F32),
        scratch_shapes=[pltpu.VMEM((k, tn), BF16)],
        compiler_params=_cparams(("arbitrary", "arbitrary")),
        name=name,
    )(*lhs_parts, w, res)


LEVELS = (32, 16, 8, 4, 2, 1)
LOG2E = 1.4426950408889634


def _split3_bf16(x):
    def top(v):
        bits = lax.bitcast_convert_type(v, jnp.uint32) & jnp.uint32(0xFFFF0000)
        return lax.bitcast_convert_type(bits, F32)
    hi = top(x)
    r1 = x - hi
    mid = top(r1)
    lo = r1 - mid
    return hi.astype(BF16), mid.astype(BF16), lo.astype(BF16)


def _mix_chunks(qs, ks, vs, las, st_refs):
    heads = range(len(qs))
    c, dk = qs[0].shape
    row = lax.broadcasted_iota(jnp.int32, (c, c), 0)
    col = lax.broadcasted_iota(jnp.int32, (c, c), 1)
    rowk = lax.broadcasted_iota(jnp.int32, (c, dk), 0)
    xor = jnp.bitwise_xor(row, col)

    las = [la * LOG2E for la in las]

    tri = jnp.where(col <= row, 1.0, 0.0).astype(BF16)
    b3 = [jnp.dot(tri, jnp.concatenate(_split3_bf16(la), axis=1), preferred_element_type=F32) for la in las]
    bs = [(t[:, :dk] + t[:, dk:2 * dk]) + t[:, 2 * dk:] for t in b3]

    def neg_dist(w, b, la):
        if w >= 4:
            parts = [jnp.broadcast_to(b[base + w - 1:base + w, :], (2 * w, dk)) for base in range(0, c, 2 * w)]
            m = parts[0] if len(parts) == 1 else jnp.concatenate(parts, axis=0)
            return -jnp.abs(b - m)
        if w == 2:
            r4 = jnp.bitwise_and(rowk, 3)
            nxt = pltpu.roll(la, c - 1, 0)
            prv = pltpu.roll(la, 1, 0)
            return jnp.where(r4 == 0, nxt, jnp.where(r4 == 1, 0.0, jnp.where(r4 == 2, la, la + prv)))
        return jnp.where(jnp.bitwise_and(rowk, 1) == 1, la, 0.0)

    acc = [_nt_dot(qs[h].astype(BF16), ks[h].astype(BF16)) for h in heads]
    for w in reversed(LEVELS):
        upper = jnp.bitwise_and(rowk, w) != 0
        xs = [(jnp.where(upper, qs[h], ks[h]) * jnp.exp2(neg_dist(w, bs[h], las[h]))).astype(BF16) for h in heads]
        gs = [_nt_dot(x, x) for x in xs]
        acc = [jnp.where(xor >= w, gs[h], acc[h]) for h in heads]
    ab = [jnp.where(col <= row, a, 0.0).astype(BF16) for a in acc]

    sts = [st_refs[h][...] for h in heads]
    qx = [(qs[h] * jnp.exp2(bs[h])).astype(BF16) for h in heads]
    b_last = [b[c - 1:c, :] for b in bs]
    kx = [(ks[h] * jnp.exp2(b_last[h] - bs[h])).astype(BF16) for h in heads]
    vb = [v.astype(BF16) for v in vs]
    outs = [jnp.dot(ab[h], vb[h], preferred_element_type=F32) + _nt_dot(qx[h], sts[h].astype(BF16)) for h in heads]
    for h in heads:
        st_refs[h][...] = jnp.exp2(b_last[h]) * sts[h] + lax.dot_general(
            vb[h], kx[h], (((0,), (0,)), ((), ())), preferred_element_type=F32)
    return outs


def _gated_norm(o, g, gn):
    return _rms(o, gn) * (g * _sigmoid(g))


def _gla_kernel(q_ref, k_ref, v_ref, g_ref, la_ref, gn_ref, o_ref, st_ref, *, n_chunk):
    @pl.when(pl.program_id(1) == 0)
    def _():
        st_ref[...] = jnp.zeros_like(st_ref)

    dk, dv = GLA_DK, GLA_DV
    n_b = q_ref.shape[0]
    n_h = st_ref.shape[0] // n_b
    streams = [(b, h) for b in range(n_b) for h in range(n_h)]
    kc = [slice(h * dk, (h + 1) * dk) for h in range(n_h)]
    vc = [slice(h * dv, (h + 1) * dv) for h in range(n_h)]

    def body(ci, carry):
        rows = pl.ds(pl.multiple_of(ci * CHUNK, CHUNK), CHUNK)
        outs = _mix_chunks([q_ref[b, rows, kc[h]] * (dk ** -0.5) for b, h in streams],
                           [k_ref[b, rows, kc[h]] for b, h in streams], [v_ref[b, rows, vc[h]] for b, h in streams],
                           [la_ref[b, rows, kc[h]] for b, h in streams],
                           [st_ref.at[n] for n in range(len(streams))])
        for n, (b, h) in enumerate(streams):
            o_ref[b, rows, vc[h]] = _gated_norm(outs[n], g_ref[b, rows, vc[h]], gn_ref[...]).astype(o_ref.dtype)
        return carry

    lax.fori_loop(0, n_chunk, body, 0)


def _gla(proj, la, gn, *, tb, hp):
    batch, seq, _ = proj.shape
    wk, wv = hp * GLA_DK, hp * GLA_DV
    n_grp = GLA_HEADS // hp
    spec = lambda width, first: pl.BlockSpec((batch, tb, width), lambda p, t: (0, t, first + p))
    return pl.pallas_call(
        functools.partial(_gla_kernel, n_chunk=tb // CHUNK),
        grid=(n_grp, seq // tb),
        in_specs=[spec(wk, 0), spec(wk, n_grp), spec(wv, n_grp), spec(wv, 2 * n_grp), spec(wk, 0),
                  pl.BlockSpec((1, GLA_DV), lambda p, t: (0, 0))],
        out_specs=spec(wv, 0),
        out_shape=jax.ShapeDtypeStruct((batch, seq, GLA_HEADS * GLA_DV), BF16),
        scratch_shapes=[pltpu.VMEM((batch * hp, GLA_DV, GLA_DK), F32)],
        compiler_params=_cparams(("parallel", "arbitrary")),
        name="gla",
    )(proj, proj, proj, proj, la, gn)


def _hgrn_kernel(q_ref, f_ref, i_ref, g_ref, lb_ref, gn_ref, o_ref, st_ref, *, n_chunk):
    @pl.when(pl.program_id(1) == 0)
    def _():
        st_ref[...] = jnp.zeros_like(st_ref)

    dh = HGRN_DH
    n_b = q_ref.shape[0]
    n_h = st_ref.shape[0] // n_b
    streams = [(b, h) for b in range(n_b) for h in range(n_h)]
    hc = [slice(h * dh, (h + 1) * dh) for h in range(n_h)]

    def body(ci, carry):
        rows = pl.ds(pl.multiple_of(ci * CHUNK, CHUNK), CHUNK)
        qs, ks, las = [], [], []
        for b, h in streams:
            log_lb, log_1mlb, one_m_lb = lb_ref[0:1, hc[h]], lb_ref[1:2, hc[h]], lb_ref[2:3, hc[h]]
            hq = q_ref[b, rows, hc[h]]
            z = f_ref[b, rows, hc[h]]
            x1 = log_1mlb + _log_sigmoid(z)
            las.append(jnp.maximum(log_lb, x1) + _log1pexp_neg(jnp.abs(log_lb - x1)))
            qs.append(hq * _sigmoid(hq))
            ks.append(one_m_lb * _sigmoid(-z))
        outs = _mix_chunks(qs, ks, [i_ref[b, rows, hc[h]] for b, h in streams], las,
                           [st_ref.at[n] for n in range(len(streams))])
        for n, (b, h) in enumerate(streams):
            o_ref[b, rows, hc[h]] = _gated_norm(outs[n], g_ref[b, rows, hc[h]], gn_ref[...]).astype(o_ref.dtype)
        return carry

    lax.fori_loop(0, n_chunk, body, 0)


def _hgrn(proj, lbc, gn, *, tb, hp):
    batch, seq, _ = proj.shape
    width = hp * HGRN_DH
    n_grp = HGRN_HEADS // hp
    first = 3072 // width
    spec = lambda seg: pl.BlockSpec((batch, tb, width), lambda p, t: (0, t, first + seg * n_grp + p))
    return pl.pallas_call(
        functools.partial(_hgrn_kernel, n_chunk=tb // CHUNK),
        grid=(n_grp, seq // tb),
        in_specs=[spec(0), spec(1), spec(2), spec(3),
                  pl.BlockSpec((8, width), lambda p, t: (0, p)),
                  pl.BlockSpec((1, HGRN_DH), lambda p, t: (0, 0))],
        out_specs=pl.BlockSpec((batch, tb, width), lambda p, t: (0, t, p)),
        out_shape=jax.ShapeDtypeStruct((batch, seq, HGRN_HEADS * HGRN_DH), BF16),
        scratch_shapes=[pltpu.VMEM((batch * hp, HGRN_DH, HGRN_DH), F32)],
        compiler_params=_cparams(("parallel", "arbitrary")),
        name="hgrn",
    )(proj, proj, proj, proj, lbc, gn)


def _xattn_kernel(q_ref, k_ref, v_ref, o_ref):
    cols = [slice(h * XA_DH, (h + 1) * XA_DH) for h in range(XA_HEADS)]
    s = [_nt_dot(q_ref[:, c], k_ref[:, c]) * (XA_DH ** -0.5) for c in cols]
    p = [jnp.exp(x - jnp.max(x, axis=-1, keepdims=True)) for x in s]
    p = [(x / jnp.sum(x, axis=-1, keepdims=True)).astype(BF16) for x in p]
    for c, x in zip(cols, p):
        o_ref[:, c] = jnp.dot(x, v_ref[:, c], preferred_element_type=F32).astype(o_ref.dtype)


def _xattn(q, kv, *, batch, seq, n_mem, tq):
    m, d = q.shape
    nt = seq // tq
    return pl.pallas_call(
        _xattn_kernel,
        grid=(batch, nt),
        in_specs=[
            pl.BlockSpec((tq, d), lambda b, t: (b * nt + t, 0)),
            pl.BlockSpec((n_mem, d), lambda b, t: (b, 0)),
            pl.BlockSpec((n_mem, d), lambda b, t: (b, 1)),
        ],
        out_specs=pl.BlockSpec((tq, d), lambda b, t: (b * nt + t, 0)),
        out_shape=jax.ShapeDtypeStruct((m, d), BF16),
        compiler_params=_cparams(("parallel", "arbitrary")),
        name="xattn",
    )(q, kv, kv)


ROUTER_ROWS = 40


def _router_kernel(h_ref, g_ref, wr_ref, br_ref, a_ref, meta_ref, cnt_ref, carry_ref):
    tm = h_ref.shape[0]
    nr = wr_ref.shape[0]

    @pl.when(pl.program_id(0) == 0)
    def _():
        carry_ref[...] = jnp.zeros_like(carry_ref)

    a = _rms(h_ref[...], g_ref[...])
    a_ref[...] = a.astype(BF16).reshape(a_ref.shape)
    a_hi = a.astype(BF16)
    a_lo = (a - a_hi.astype(F32)).astype(BF16)
    w = wr_ref[...]
    w_hi = w.astype(BF16)
    w_lo = (w - w_hi.astype(F32)).astype(BF16)
    logits = (_nt_dot(w_hi, a_hi) + _nt_dot(w_hi, a_lo) + _nt_dot(w_lo, a_hi)) + br_ref[:, 0:1]
    row = lax.broadcasted_iota(jnp.int32, (nr, tm), 0)
    row_f = row.astype(F32)
    neg = -jnp.inf

    def first_max(x):
        v = jnp.max(x, axis=0, keepdims=True)
        return v, jnp.min(jnp.where(x == v, row_f, float(nr)), axis=0, keepdims=True)

    gl = jnp.where(row < N_GROUPS, logits, neg)
    gmax, gidx = first_max(gl)
    p_group = 1.0 / jnp.sum(jnp.exp(gl - gmax), axis=0, keepdims=True)
    lo = float(N_GROUPS) + gidx * float(EXPERTS_PER_GROUP)
    el = jnp.where((row_f >= lo) & (row_f < lo + float(EXPERTS_PER_GROUP)), logits, neg)
    v1, i1 = first_max(el)
    v2, i2 = first_max(jnp.where(row_f == i1, neg, el))
    t = jnp.exp(v2 - v1)
    g1 = p_group / (1.0 + t)
    g2 = p_group * t / (1.0 + t)

    hit1 = row_f == i1
    hit2 = row_f == i2
    onehot = jnp.where(hit1 | hit2, 1.0, 0.0)
    src = lax.broadcasted_iota(jnp.int32, (tm, tm), 0)
    dst = lax.broadcasted_iota(jnp.int32, (tm, tm), 1)
    earlier = jnp.where(src < dst, 1.0, 0.0).astype(BF16)
    before = jnp.dot(onehot.astype(BF16), earlier, preferred_element_type=F32) + carry_ref[:, 0:1]
    r1 = jnp.sum(jnp.where(hit1, before, 0.0), axis=0, keepdims=True)
    r2 = jnp.sum(jnp.where(hit2, before, 0.0), axis=0, keepdims=True)
    carry_ref[...] = carry_ref[...] + jnp.sum(onehot, axis=1, keepdims=True)
    cnt_ref[...] = carry_ref[...]

    out_row = lax.broadcasted_iota(jnp.int32, (LANES, tm), 0)
    meta_t = jnp.zeros((LANES, tm), F32)
    for idx, val in enumerate((i1 - float(N_GROUPS), i2 - float(N_GROUPS), r1, r2, g1, g2)):
        meta_t = jnp.where(out_row == idx, val, meta_t)
    meta_ref[...] = meta_t.T


def _router(h, g, wr, br, *, tm):
    m, d = h.shape
    slab = d // LANES
    return pl.pallas_call(
        _router_kernel,
        grid=(m // tm,),
        in_specs=[
            pl.BlockSpec((tm, d), lambda i: (i, 0)),
            pl.BlockSpec((1, d), lambda i: (0, 0)),
            pl.BlockSpec((ROUTER_ROWS, d), lambda i: (0, 0)),
            pl.BlockSpec((ROUTER_ROWS, LANES), lambda i: (0, 0)),
        ],
        out_specs=[
            pl.BlockSpec((tm, slab, LANES), lambda i: (i, 0, 0)),
            pl.BlockSpec((tm, LANES), lambda i: (i, 0)),
            pl.BlockSpec((ROUTER_ROWS, LANES), lambda i: (0, 0)),
        ],
        out_shape=[jax.ShapeDtypeStruct((m, slab, LANES), BF16), jax.ShapeDtypeStruct((m, LANES), F32),
                   jax.ShapeDtypeStruct((ROUTER_ROWS, LANES), F32)],
        scratch_shapes=[pltpu.VMEM((ROUTER_ROWS, LANES), F32)],
        compiler_params=_cparams(("arbitrary",)),
        name="router",
    )(h, g, wr, br)


PAD_PIECES = tuple(EXPERT_BLOCK >> (b + 1) for b in range(EXPERT_BLOCK.bit_length() - 1))


def _dispatch_kernel(d1_ref, d2_ref, fill_ref, npad_ref, nb_ref, a_ref, xs_hbm, slot_ref, st0, st1, zbuf, sem, zsem):
    i = pl.program_id(0)
    n_steps = pl.num_programs(0)
    tm = a_ref.shape[0]
    bufs = (st0, st1)
    rb = zbuf.shape[0]
    n_blocks = xs_hbm.shape[0] // rb
    min_blocks = (2 * tm * n_steps) // rb

    def zero_copies():
        out = []
        for e in range(N_EXPERTS):
            p = npad_ref[e]
            for piece in PAD_PIECES:
                out.append((p & piece != 0, pltpu.make_async_copy(
                    zbuf.at[pl.ds(0, piece)], xs_hbm.at[pl.ds(fill_ref[e] + (p & -(2 * piece)), piece)], zsem.at[0])))
        for b in range(min_blocks, n_blocks):
            out.append((b >= nb_ref[0], pltpu.make_async_copy(zbuf, xs_hbm.at[pl.ds(b * rb, rb)], zsem.at[0])))
        return out

    def wait_rows(s):
        for _ in range(2):
            pltpu.make_async_copy(bufs[s], xs_hbm.at[pl.ds(0, tm)], sem.at[s]).wait()

    @pl.when(i == 0)
    def _():
        def mark_unused(j, carry):
            slot_ref[j] = -1
            return carry
        for e in range(N_EXPERTS):
            lax.fori_loop(fill_ref[e], fill_ref[e] + npad_ref[e], mark_unused, 0)
        lax.fori_loop(nb_ref[0] * rb, slot_ref.shape[0], mark_unused, 0)
        zbuf[...] = jnp.zeros_like(zbuf)
        for cond, cp in zero_copies():
            @pl.when(cond)
            def _():
                cp.start()

    for s in (0, 1):
        @pl.when(lax.rem(i, 2) == s)
        def _():
            @pl.when(i >= 2)
            def _():
                wait_rows(s)

            bufs[s][...] = a_ref[...]
            for r in range(tm):
                tok = i * tm + r
                d1 = d1_ref[tok]
                d2 = d2_ref[tok]
                slot_ref[d1] = 2 * tok
                slot_ref[d2] = 2 * tok + 1
                pltpu.make_async_copy(bufs[s].at[r], xs_hbm.at[d1], sem.at[s]).start(priority=0)
                pltpu.make_async_copy(bufs[s].at[r], xs_hbm.at[d2], sem.at[s]).start(priority=1)

    @pl.when(i == n_steps - 1)
    def _():
        for s in (0, 1):
            @pl.when((lax.rem(n_steps - 1, 2) == s) | ((n_steps >= 2) & (lax.rem(n_steps, 2) == s)))
            def _():
                wait_rows(s)
        for cond, cp in zero_copies():
            @pl.when(cond)
            def _():
                cp.wait()


def _dispatch(dest1, dest2, fill, npad, n_blk, a, *, n_rows, tm):
    m, slab, _ = a.shape
    grid_spec = pltpu.PrefetchScalarGridSpec(
        num_scalar_prefetch=5,
        grid=(m // tm,),
        in_specs=[pl.BlockSpec((tm, slab, LANES), lambda i, *_: (i, 0, 0))],
        out_specs=[pl.BlockSpec(memory_space=pl.ANY), pl.BlockSpec(memory_space=pltpu.SMEM)],
        scratch_shapes=[pltpu.VMEM((tm, slab, LANES), a.dtype), pltpu.VMEM((tm, slab, LANES), a.dtype),
                        pltpu.VMEM((EXPERT_BLOCK, slab, LANES), a.dtype),
                        pltpu.SemaphoreType.DMA((2,)), pltpu.SemaphoreType.DMA((1,))],
    )
    return pl.pallas_call(
        _dispatch_kernel,
        grid_spec=grid_spec,
        out_shape=[jax.ShapeDtypeStruct((n_rows, slab, LANES), a.dtype), jax.ShapeDtypeStruct((n_rows,), jnp.int32)],
        compiler_params=_cparams(("arbitrary",)),
        name="dispatch",
    )(dest1, dest2, fill, npad, n_blk, a)


def _weight_copies(w_hbms, e, wst_refs, wsem):
    copies = []
    for w_hbm, wst_ref in zip(w_hbms, wst_refs):
        rows_per = w_hbm.shape[1] // WEIGHT_CHUNKS
        for c in range(WEIGHT_CHUNKS):
            rows = pl.ds(c * rows_per, rows_per)
            copies.append(pltpu.make_async_copy(w_hbm.at[e, rows, :], wst_ref.at[rows, :], wsem.at[0]))
    return copies


def _load_expert_weights(i, be_ref, nx_ref, w_hbms, wst_refs, wb_refs, wsem):
    def start(e):
        for cp in _weight_copies(w_hbms, e, wst_refs, wsem):
            cp.start(priority=WEIGHT_DMA_PRIORITY)

    @pl.when(i == 0)
    def _():
        start(be_ref[0])

    @pl.when((i == 0) | (be_ref[i] != be_ref[jnp.maximum(i - 1, 0)]))
    def _():
        for cp in _weight_copies(w_hbms, be_ref[i], wst_refs, wsem):
            cp.wait()
        for wst_ref, wb_ref in zip(wst_refs, wb_refs):
            _for_row_chunks(wst_ref.shape[0], lambda rows: wb_ref.__setitem__((rows, slice(None)),
                                                                              wst_ref[rows, :].astype(BF16)))

        @pl.when(nx_ref[i] >= 0)
        def _():
            start(nx_ref[i])


def _expert_kernel(slot_ref, be_ref, nx_ref, nb_ref, x_ref, wg_hbm, wu_hbm, wd_hbm, y_hbm, wsg, wsu, wsd, wgb, wub, wdb,
                   ys0, ys1, sem, wsem, *, n_tok):
    i = pl.program_id(0)
    n_steps = pl.num_programs(0)
    nb = nb_ref[0]
    rb = x_ref.shape[0]
    bufs = (ys0, ys1)

    def wait_block(buf, s):
        pltpu.make_async_copy(buf, y_hbm.at[pl.ds(0, rb)], sem.at[s]).wait()

    def scatter_previous(s):
        o = 1 - s
        for r in range(rb):
            v = jnp.where(i >= 1, slot_ref[jnp.maximum((i - 1) * rb + r, 0)], -1)
            dst = jnp.where(v >= 0, (v & 1) * n_tok + (v >> 1), 2 * n_tok + o * rb + r)
            pltpu.make_async_copy(bufs[o].at[r], y_hbm.at[dst], sem.at[o]).start()

    @pl.when(i == 0)
    def _():
        ys0[...] = jnp.zeros_like(ys0)
        ys1[...] = jnp.zeros_like(ys1)
        spare = pltpu.make_async_copy(ys0, y_hbm.at[pl.ds(2 * n_tok, rb)], sem.at[0])
        spare.start()
        spare.wait()

    for s in (0, 1):
        @pl.when((i < nb) & (lax.rem(i, 2) == s))
        def _():
            _load_expert_weights(i, be_ref, nx_ref, (wg_hbm, wu_hbm, wd_hbm), (wsg, wsu, wsd), (wgb, wub, wdb), wsem)

            @pl.when(i >= 1)
            def _():
                wait_block(bufs[s], s)

            scatter_previous(s)
            x = x_ref[...].reshape(rb, -1)
            hg = jnp.dot(x, wgb[...], preferred_element_type=F32)
            hu = jnp.dot(x, wub[...], preferred_element_type=F32)
            hb = (hg * _sigmoid(hg) * hu).astype(BF16)
            y = jnp.dot(hb, wdb[...], preferred_element_type=F32)
            bufs[s][...] = y.astype(bufs[s].dtype).reshape(bufs[s].shape)

        @pl.when((i == nb) & (lax.rem(i, 2) == s))
        def _():
            scatter_previous(s)

    @pl.when(i == n_steps - 1)
    def _():
        for s in (0, 1):
            wait_block(bufs[s], s)


def _experts(slot, block_e, next_e, n_blk, xs, wg, wu, wd, *, rb, n_tok):
    n_rows, slab, _ = xs.shape
    d = slab * LANES
    de = wg.shape[2]
    grid_spec = pltpu.PrefetchScalarGridSpec(
        num_scalar_prefetch=4,
        grid=(n_rows // rb + 1,),
        in_specs=[pl.BlockSpec((rb, slab, LANES), lambda i, sl, be, nx, nb: (jnp.minimum(i, nb[0] - 1), 0, 0))]
        + [pl.BlockSpec(memory_space=pl.ANY)] * 3,
        out_specs=pl.BlockSpec(memory_space=pl.ANY),
        scratch_shapes=[pltpu.VMEM((d, de), F32), pltpu.VMEM((d, de), F32), pltpu.VMEM((de, d), F32),
                        pltpu.VMEM((d, de), BF16), pltpu.VMEM((d, de), BF16), pltpu.VMEM((de, d), BF16),
                        pltpu.VMEM((rb, slab, LANES), BF16), pltpu.VMEM((rb, slab, LANES), BF16),
                        pltpu.SemaphoreType.DMA((2,)), pltpu.SemaphoreType.DMA((1,))],
    )
    return pl.pallas_call(
        functools.partial(_expert_kernel, n_tok=n_tok),
        grid_spec=grid_spec,
        out_shape=jax.ShapeDtypeStruct((2 * n_tok + 2 * rb, slab, LANES), BF16),
        compiler_params=_cparams(("arbitrary",), vmem=EXPERT_VMEM_LIMIT),
        name="experts",
    )(slot, block_e, next_e, n_blk, xs, wg, wu, wd)


def _combine_kernel(y0_ref, y1_ref, h_ref, meta_ref, g_ref, o_ref):
    tm, d = h_ref.shape
    meta = meta_ref[...]
    y0 = y0_ref[...].reshape(tm, d).astype(F32)
    y1 = y1_ref[...].reshape(tm, d).astype(F32)
    out = h_ref[...] + meta[:, 4:5] * y0 + meta[:, 5:6] * y1
    o_ref[...] = _rms(out, g_ref[...])


def _combine(y, h, meta, g, *, tm):
    m, d = h.shape
    slab = d // LANES
    nt = m // tm
    return pl.pallas_call(
        _combine_kernel,
        grid=(nt,),
        in_specs=[
            pl.BlockSpec((tm, slab, LANES), lambda i: (i, 0, 0)),
            pl.BlockSpec((tm, slab, LANES), lambda i: (nt + i, 0, 0)),
            pl.BlockSpec((tm, d), lambda i: (i, 0)),
            pl.BlockSpec((tm, LANES), lambda i: (i, 0)),
            pl.BlockSpec((1, d), lambda i: (0, 0)),
        ],
        out_specs=pl.BlockSpec((tm, d), lambda i: (i, 0)),
        out_shape=jax.ShapeDtypeStruct((m, d), F32),
        compiler_params=_cparams(("parallel",)),
        name="combine",
    )(y, y, h, meta, g)


def kernel(x, mem, norm_mix_g, w_in, w_gla_alpha_up, b_gla_alpha, gla_out_norm_g, hgrn_lb_logits, hgrn_out_norm_g, w_mix_out, norm_xattn_g, norm_mem_g, w_xattn_q, w_xattn_kv, w_xattn_out, norm_ffn_g, w_router_group, b_router_group, w_router_expert, b_router_expert, w_expert_gate, w_expert_up, w_expert_down, norm_final_g):
    batch, seq, d = x.shape
    n_mem = mem.shape[1]
    m = batch * seq
    depth = norm_mix_g.shape[0]
    h = x.reshape(m, d)
    lb_all = jnp.cumsum(jax.nn.softmax(hgrn_lb_logits.astype(F32), axis=0), axis=0)
    gla_cols = 2 * GLA_HEADS * GLA_DK + 2 * GLA_HEADS * GLA_DV
    lr_rank = w_gla_alpha_up.shape[1]

    for l in range(depth):
        w_t = jnp.swapaxes(w_in[l], 0, 1)
        w_lr_t = jnp.pad(w_t[gla_cols:gla_cols + lr_rank], ((0, LANES - lr_rank), (0, 0))).astype(BF16)
        w_up = jnp.pad(w_gla_alpha_up[l], ((0, LANES - lr_rank), (0, 0))).astype(BF16)
        a_mix, la = _norm_gate(h, norm_mix_g[l][None, :], w_lr_t, w_up, b_gla_alpha[l][None, :], tm=min(m, 512))
        proj = _inproj(a_mix, w_t, tm=min(m, 4096), tn=512, skip_from=gla_cols, skip=lr_rank)
        lb = lb_all[l]
        lbc = jnp.zeros((8, lb.shape[0]), F32).at[0].set(jnp.log(lb)).at[1].set(jnp.log1p(-lb)).at[2].set(1.0 - lb)
        proj3 = proj.reshape(batch, seq, -1)
        o_gla = _gla(proj3, la.reshape(batch, seq, -1), gla_out_norm_g[l][None, :], tb=min(seq, 256), hp=4)
        o_h = _hgrn(proj3, lbc, hgrn_out_norm_g[l][None, :], tb=min(seq, 256), hp=HGRN_HP)
        o_gla, o_h = o_gla.reshape(m, -1), o_h.reshape(m, -1)
        h = _mm_res([o_gla, o_h], w_mix_out[l], h, tm=min(m, 512), tn=d, name="mix_out")

        kv = _norm_mm(mem.reshape(batch * n_mem, d), norm_mem_g[l][None, :], w_xattn_kv[l],
                      tm=batch * n_mem, tn=1024, out_dtype=BF16, name="mem_kv")
        q = _norm_mm(h, norm_xattn_g[l][None, :], w_xattn_q[l], tm=min(m, 512), tn=d, out_dtype=BF16,
                     name="xattn_q")
        o = _xattn(q, kv, batch=batch, seq=seq, n_mem=n_mem, tq=min(seq, 512))
        h = _mm_res([o], w_xattn_out[l], h, tm=min(m, 512), tn=d, name="xattn_out")

        n_logits = N_GROUPS + N_EXPERTS
        wr = jnp.pad(jnp.concatenate([w_router_group[l], w_router_expert[l]], axis=1).T,
                     ((0, ROUTER_ROWS - n_logits), (0, 0)))
        br = jnp.pad(jnp.concatenate([b_router_group[l], b_router_expert[l]]), (0, ROUTER_ROWS - n_logits))
        a, meta, cnt = _router(h, norm_ffn_g[l][None, :], wr, jnp.broadcast_to(br[:, None], (ROUTER_ROWS, LANES)),
                               tm=min(m, 512))

        e_idx = meta[:, 0:2].astype(jnp.int32)
        rank = meta[:, 2:4].astype(jnp.int32)
        counts = cnt[N_GROUPS:N_GROUPS + N_EXPERTS, 0].astype(jnp.int32)
        padded = ((counts + EXPERT_BLOCK - 1) // EXPERT_BLOCK) * EXPERT_BLOCK
        pad_end = jnp.cumsum(padded)
        pad_start = pad_end - padded
        n_rows = 2 * m + N_EXPERTS * EXPERT_BLOCK
        n_blocks = n_rows // EXPERT_BLOCK
        n_blk = (pad_end[-1:] // EXPERT_BLOCK).astype(jnp.int32)
        blk_first = jnp.arange(n_blocks, dtype=jnp.int32) * EXPERT_BLOCK
        block_e = jnp.minimum(jnp.sum((pad_end[None, :] <= blk_first[:, None]).astype(jnp.int32), axis=1),
                              N_EXPERTS - 1)
        after = (pad_end // EXPERT_BLOCK)[block_e]
        next_e = jnp.where(after < n_blk[0], block_e[jnp.minimum(after, n_blocks - 1)], -1).astype(jnp.int32)

        experts = jnp.arange(N_EXPERTS, dtype=jnp.int32)
        dest = [jnp.sum(jnp.where(e_idx[:, k, None] == experts[None, :], pad_start[None, :], 0), axis=1) + rank[:, k]
                for k in range(2)]
        xs, slot = _dispatch(dest[0], dest[1], pad_start + counts, padded - counts, n_blk, a, n_rows=n_rows,
                             tm=min(m, 256))
        y = _experts(slot, block_e, next_e, n_blk, xs, w_expert_gate[l], w_expert_up[l], w_expert_down[l],
                     rb=EXPERT_BLOCK, n_tok=m)
        last = l == depth - 1
        g_fin = norm_final_g[None, :] if last else jnp.ones((1, d), F32)
        h = _combine(y, h, meta, g_fin, tm=min(m, 256))
        assert last, "the combine kernel fuses the final rmsnorm; deeper stacks need an un-normalised variant"

    return h.reshape(batch, seq, d)
```

```python
import functools

import jax
import jax.numpy as jnp
from jax import lax
from jax.experimental import pallas as pl
from jax.experimental.pallas import tpu as pltpu

F32 = jnp.float32
BF16 = jnp.bfloat16

EPS = 1e-6
CHUNK = 64
LANES = 128
GLA_HEADS, GLA_DK, GLA_DV = 4, 128, 256
HGRN_HEADS, HGRN_DH = 8, 128
HGRN_HP = 8
XA_HEADS, XA_DH = 4, 512
N_GROUPS, EXPERTS_PER_GROUP, N_EXPERTS = 4, 8, 32
EXPERT_BLOCK = 256
PAD_UNIT = 128
WEIGHT_CHUNKS = 8
WEIGHT_DMA_PRIORITY = 1
VMEM_LIMIT = 56 * 1024 * 1024
EXPERT_VMEM_LIMIT = 60 * 1024 * 1024


def _cparams(sem, vmem=VMEM_LIMIT):
    return pltpu.CompilerParams(dimension_semantics=sem, vmem_limit_bytes=vmem)


def _log1pexp_neg(t):
    return jnp.log(1.0 + jnp.exp(-t))


def _log_sigmoid(z):
    return jnp.minimum(z, 0.0) - _log1pexp_neg(jnp.abs(z))


def _sigmoid(z):
    return 1.0 / (1.0 + jnp.exp(-z))


def _rms(x, g):
    return x * lax.rsqrt(jnp.mean(x * x, axis=-1, keepdims=True) + EPS) * g


def _nt_dot(x, y):
    return lax.dot_general(x, y, (((1,), (1,)), ((), ())), preferred_element_type=F32)


NORM_ROWS = 256


def _for_row_chunks(n_rows, fn):
    step = min(NORM_ROWS, n_rows)

    def body(ci, carry):
        fn(pl.ds(pl.multiple_of(ci * step, step), step))
        return carry

    lax.fori_loop(0, n_rows // step, body, 0)


def _norm_gate_kernel(x_ref, g_ref, wlr_ref, wup_ref, bal_ref, a_ref, la_ref):
    a = _rms(x_ref[...], g_ref[...]).astype(BF16)
    a_ref[...] = a
    lr = _nt_dot(a, wlr_ref[...])
    z = jnp.dot(lr.astype(BF16), wup_ref[...], preferred_element_type=F32) + bal_ref[...]
    la_ref[...] = _log_sigmoid(z) * (1.0 / 16.0)


def _norm_gate(x, g, w_lr_t, w_up, b_al, *, tm):
    m, d = x.shape
    nk = w_up.shape[1]
    return pl.pallas_call(
        _norm_gate_kernel,
        grid=(m // tm,),
        in_specs=[
            pl.BlockSpec((tm, d), lambda i: (i, 0)),
            pl.BlockSpec((1, d), lambda i: (0, 0)),
            pl.BlockSpec((LANES, d), lambda i: (0, 0)),
            pl.BlockSpec((LANES, nk), lambda i: (0, 0)),
            pl.BlockSpec((1, nk), lambda i: (0, 0)),
        ],
        out_specs=[pl.BlockSpec((tm, d), lambda i: (i, 0)), pl.BlockSpec((tm, nk), lambda i: (i, 0))],
        out_shape=[jax.ShapeDtypeStruct((m, d), BF16), jax.ShapeDtypeStruct((m, nk), F32)],
        compiler_params=_cparams(("parallel",)),
        name="norm_gate",
    )(x, g, w_lr_t, w_up, b_al)


def _inproj_kernel(a_ref, w_ref, proj_ref):
    proj_ref[...] = _nt_dot(a_ref[...], w_ref[...].astype(BF16))


def _inproj(a, w_t, *, tm, tn, skip_from, skip):
    m, d = a.shape
    n = w_t.shape[0] - skip
    first_after = skip_from // tn

    def w_rows(i, j):
        return (pl.multiple_of(j * tn + jnp.where(j >= first_after, skip, 0), 8), 0)

    return pl.pallas_call(
        _inproj_kernel,
        grid=(m // tm, n // tn),
        in_specs=[
            pl.BlockSpec((tm, d), lambda i, j: (i, 0), pipeline_mode=pl.Buffered(1)),
            pl.BlockSpec((pl.Element(tn), pl.Element(d)), w_rows),
        ],
        out_specs=pl.BlockSpec((tm, tn), lambda i, j: (i, j)),
        out_shape=jax.ShapeDtypeStruct((m, n), F32),
        compiler_params=_cparams(("parallel", "arbitrary")),
        name="inproj",
    )(a, w_t)


def _weight_spec(k, n, tn):
    if tn == n:
        return pl.BlockSpec((k, n), lambda i, j: (0, 0), pipeline_mode=pl.Buffered(1))
    return pl.BlockSpec((k, tn), lambda i, j: (0, j))


def _cast_weight(w_ref, wb_ref, resident):
    if resident:
        @pl.when((pl.program_id(0) == 0) & (pl.program_id(1) == 0))
        def _():
            wb_ref[...] = w_ref[...].astype(BF16)
    else:
        wb_ref[...] = w_ref[...].astype(BF16)


def _norm_mm_kernel(x_ref, g_ref, w_ref, o_ref, a_ref, wb_ref, *, resident):
    @pl.when(pl.program_id(1) == 0)
    def _():
        def rows_fn(rows):
            a_ref[rows, :] = _rms(x_ref[rows, :], g_ref[...]).astype(BF16)

        _for_row_chunks(x_ref.shape[0], rows_fn)

    _cast_weight(w_ref, wb_ref, resident)
    o_ref[...] = jnp.dot(a_ref[...], wb_ref[...], preferred_element_type=F32).astype(o_ref.dtype)


def _norm_mm(x, g, w, *, tm, tn, out_dtype, name):
    m, d = x.shape
    n = w.shape[1]
    return pl.pallas_call(
        functools.partial(_norm_mm_kernel, resident=tn == n),
        grid=(m // tm, n // tn),
        in_specs=[
            pl.BlockSpec((tm, d), lambda i, j: (i, 0)),
            pl.BlockSpec((1, d), lambda i, j: (0, 0)),
            _weight_spec(d, n, tn),
        ],
        out_specs=pl.BlockSpec((tm, tn), lambda i, j: (i, j)),
        out_shape=jax.ShapeDtypeStruct((m, n), out_dtype),
        scratch_shapes=[pltpu.VMEM((tm, d), BF16), pltpu.VMEM((d, tn), BF16)],
        compiler_params=_cparams(("arbitrary", "arbitrary")),
        name=name,
    )(x, g, w)


def _mm_res_kernel(*refs, n_lhs, resident):
    lhs = refs[:n_lhs]
    w_ref, res_ref, o_ref, wb_ref = refs[n_lhs:]
    _cast_weight(w_ref, wb_ref, resident)
    acc = res_ref[...]
    k0 = 0
    for l_ref in lhs:
        kp = l_ref.shape[1]
        acc = acc + jnp.dot(l_ref[...], wb_ref[k0:k0 + kp, :], preferred_element_type=F32)
        k0 += kp
    o_ref[...] = acc


def _mm_res(lhs_parts, w, res, *, tm, tn, name):
    m, n = res.shape
    k = w.shape[0]
    n_lhs = len(lhs_parts)
    in_specs = [pl.BlockSpec((tm, p.shape[1]), lambda i, j: (i, 0)) for p in lhs_parts]
    in_specs += [_weight_spec(k, n, tn), pl.BlockSpec((tm, tn), lambda i, j: (i, j))]
    return pl.pallas_call(
        functools.partial(_mm_res_kernel, n_lhs=n_lhs, resident=tn == n),
        grid=(m // tm, n // tn),
        in_specs=in_specs,
        out_specs=pl.BlockSpec((tm, tn), lambda i, j: (i, j)),
        out_shape=jax.ShapeDtypeStruct((m, n), F32),
        scratch_shapes=[pltpu.VMEM((k, tn), BF16)],
        compiler_params=_cparams(("arbitrary", "arbitrary")),
        name=name,
    )(*lhs_parts, w, res)


LEVELS = (32, 16, 8, 4, 2, 1)
LOG2E = 1.4426950408889634


def _split3_bf16(x):
    def top(v):
        bits = lax.bitcast_convert_type(v, jnp.uint32) & jnp.uint32(0xFFFF0000)
        return lax.bitcast_convert_type(bits, F32)
    hi = top(x)
    r1 = x - hi
    mid = top(r1)
    lo = r1 - mid
    return hi.astype(BF16), mid.astype(BF16), lo.astype(BF16)


def _mix_chunks(qs, ks, vs, las, st_refs):
    heads = range(len(qs))
    c, dk = qs[0].shape
    row = lax.broadcasted_iota(jnp.int32, (c, c), 0)
    col = lax.broadcasted_iota(jnp.int32, (c, c), 1)
    rowk = lax.broadcasted_iota(jnp.int32, (c, dk), 0)
    xor = jnp.bitwise_xor(row, col)

    las = [la * LOG2E for la in las]

    tri = jnp.where(col <= row, 1.0, 0.0).astype(BF16)
    b3 = [jnp.dot(tri, jnp.concatenate(_split3_bf16(la), axis=1), preferred_element_type=F32) for la in las]
    bs = [(t[:, :dk] + t[:, dk:2 * dk]) + t[:, 2 * dk:] for t in b3]

    def neg_dist(w, b, la):
        if w >= 4:
            parts = [jnp.broadcast_to(b[base + w - 1:base + w, :], (2 * w, dk)) for base in range(0, c, 2 * w)]
            m = parts[0] if len(parts) == 1 else jnp.concatenate(parts, axis=0)
            return -jnp.abs(b - m)
        if w == 2:
            r4 = jnp.bitwise_and(rowk, 3)
            nxt = pltpu.roll(la, c - 1, 0)
            prv = pltpu.roll(la, 1, 0)
            return jnp.where(r4 == 0, nxt, jnp.where(r4 == 1, 0.0, jnp.where(r4 == 2, la, la + prv)))
        return jnp.where(jnp.bitwise_and(rowk, 1) == 1, la, 0.0)

    acc = [_nt_dot(qs[h].astype(BF16), ks[h].astype(BF16)) for h in heads]
    for w in reversed(LEVELS):
        upper = jnp.bitwise_and(rowk, w) != 0
        xs = [(jnp.where(upper, qs[h], ks[h]) * jnp.exp2(neg_dist(w, bs[h], las[h]))).astype(BF16) for h in heads]
        gs = [_nt_dot(x, x) for x in xs]
        acc = [jnp.where(xor >= w, gs[h], acc[h]) for h in heads]
    ab = [jnp.where(col <= row, a, 0.0).astype(BF16) for a in acc]

    sts = [st_refs[h][...] for h in heads]
    qx = [(qs[h] * jnp.exp2(bs[h])).astype(BF16) for h in heads]
    b_last = [b[c - 1:c, :] for b in bs]
    kx = [(ks[h] * jnp.exp2(b_last[h] - bs[h])).astype(BF16) for h in heads]
    vb = [v.astype(BF16) for v in vs]
    outs = [jnp.dot(ab[h], vb[h], preferred_element_type=F32) + _nt_dot(qx[h], sts[h].astype(BF16)) for h in heads]
    for h in heads:
        st_refs[h][...] = jnp.exp2(b_last[h]) * sts[h] + lax.dot_general(
            vb[h], kx[h], (((0,), (0,)), ((), ())), preferred_element_type=F32)
    return outs


def _gated_norm(o, g, gn):
    return _rms(o, gn) * (g * _sigmoid(g))


def _gla_kernel(q_ref, k_ref, v_ref, g_ref, la_ref, gn_ref, o_ref, st_ref, *, n_chunk):
    @pl.when(pl.program_id(1) == 0)
    def _():
        st_ref[...] = jnp.zeros_like(st_ref)

    dk, dv = GLA_DK, GLA_DV
    n_b = q_ref.shape[0]
    n_h = st_ref.shape[0] // n_b
    streams = [(b, h) for b in range(n_b) for h in range(n_h)]
    kc = [slice(h * dk, (h + 1) * dk) for h in range(n_h)]
    vc = [slice(h * dv, (h + 1) * dv) for h in range(n_h)]

    def body(ci, carry):
        rows = pl.ds(pl.multiple_of(ci * CHUNK, CHUNK), CHUNK)
        outs = _mix_chunks([q_ref[b, rows, kc[h]] * (dk ** -0.5) for b, h in streams],
                           [k_ref[b, rows, kc[h]] for b, h in streams], [v_ref[b, rows, vc[h]] for b, h in streams],
                           [la_ref[b, rows, kc[h]] for b, h in streams],
                           [st_ref.at[n] for n in range(len(streams))])
        for n, (b, h) in enumerate(streams):
            o_ref[b, rows, vc[h]] = _gated_norm(outs[n], g_ref[b, rows, vc[h]], gn_ref[...]).astype(o_ref.dtype)
        return carry

    lax.fori_loop(0, n_chunk, body, 0)


def _gla(proj, la, gn, *, tb, hp):
    batch, seq, _ = proj.shape
    wk, wv = hp * GLA_DK, hp * GLA_DV
    n_grp = GLA_HEADS // hp
    spec = lambda width, first: pl.BlockSpec((batch, tb, width), lambda p, t: (0, t, first + p))
    return pl.pallas_call(
        functools.partial(_gla_kernel, n_chunk=tb // CHUNK),
        grid=(n_grp, seq // tb),
        in_specs=[spec(wk, 0), spec(wk, n_grp), spec(wv, n_grp), spec(wv, 2 * n_grp), spec(wk, 0),
                  pl.BlockSpec((1, GLA_DV), lambda p, t: (0, 0))],
        out_specs=spec(wv, 0),
        out_shape=jax.ShapeDtypeStruct((batch, seq, GLA_HEADS * GLA_DV), BF16),
        scratch_shapes=[pltpu.VMEM((batch * hp, GLA_DV, GLA_DK), F32)],
        compiler_params=_cparams(("parallel", "arbitrary")),
        name="gla",
    )(proj, proj, proj, proj, la, gn)


def _hgrn_kernel(q_ref, f_ref, i_ref, g_ref, lb_ref, gn_ref, o_ref, st_ref, *, n_chunk):
    @pl.when(pl.program_id(1) == 0)
    def _():
        st_ref[...] = jnp.zeros_like(st_ref)

    dh = HGRN_DH
    n_b = q_ref.shape[0]
    n_h = st_ref.shape[0] // n_b
    streams = [(b, h) for b in range(n_b) for h in range(n_h)]
    hc = [slice(h * dh, (h + 1) * dh) for h in range(n_h)]

    def body(ci, carry):
        rows = pl.ds(pl.multiple_of(ci * CHUNK, CHUNK), CHUNK)
        qs, ks, las = [], [], []
        for b, h in streams:
            log_lb, log_1mlb, one_m_lb = lb_ref[0:1, hc[h]], lb_ref[1:2, hc[h]], lb_ref[2:3, hc[h]]
            hq = q_ref[b, rows, hc[h]]
            z = f_ref[b, rows, hc[h]]
            x1 = log_1mlb + _log_sigmoid(z)
            las.append(jnp.maximum(log_lb, x1) + _log1pexp_neg(jnp.abs(log_lb - x1)))
            qs.append(hq * _sigmoid(hq))
            ks.append(one_m_lb * _sigmoid(-z))
        outs = _mix_chunks(qs, ks, [i_ref[b, rows, hc[h]] for b, h in streams], las,
                           [st_ref.at[n] for n in range(len(streams))])
        for n, (b, h) in enumerate(streams):
            o_ref[b, rows, hc[h]] = _gated_norm(outs[n], g_ref[b, rows, hc[h]], gn_ref[...]).astype(o_ref.dtype)
        return carry

    lax.fori_loop(0, n_chunk, body, 0)


def _hgrn(proj, lbc, gn, *, tb, hp):
    batch, seq, _ = proj.shape
    width = hp * HGRN_DH
    n_grp = HGRN_HEADS // hp
    first = 3072 // width
    spec = lambda seg: pl.BlockSpec((batch, tb, width), lambda p, t: (0, t, first + seg * n_grp + p))
    return pl.pallas_call(
        functools.partial(_hgrn_kernel, n_chunk=tb // CHUNK),
        grid=(n_grp, seq // tb),
        in_specs=[spec(0), spec(1), spec(2), spec(3),
                  pl.BlockSpec((8, width), lambda p, t: (0, p)),
                  pl.BlockSpec((1, HGRN_DH), lambda p, t: (0, 0))],
        out_specs=pl.BlockSpec((batch, tb, width), lambda p, t: (0, t, p)),
        out_shape=jax.ShapeDtypeStruct((batch, seq, HGRN_HEADS * HGRN_DH), BF16),
        scratch_shapes=[pltpu.VMEM((batch * hp, HGRN_DH, HGRN_DH), F32)],
        compiler_params=_cparams(("parallel", "arbitrary")),
        name="hgrn",
    )(proj, proj, proj, proj, lbc, gn)


def _xattn_kernel(q_ref, k_ref, v_ref, o_ref):
    cols = [slice(h * XA_DH, (h + 1) * XA_DH) for h in range(XA_HEADS)]
    s = [_nt_dot(q_ref[:, c], k_ref[:, c]) * (XA_DH ** -0.5) for c in cols]
    p = [jnp.exp(x - jnp.max(x, axis=-1, keepdims=True)) for x in s]
    p = [(x / jnp.sum(x, axis=-1, keepdims=True)).astype(BF16) for x in p]
    for c, x in zip(cols, p):
        o_ref[:, c] = jnp.dot(x, v_ref[:, c], preferred_element_type=F32).astype(o_ref.dtype)


def _xattn(q, kv, *, batch, seq, n_mem, tq):
    m, d = q.shape
    nt = seq // tq
    return pl.pallas_call(
        _xattn_kernel,
        grid=(batch, nt),
        in_specs=[
            pl.BlockSpec((tq, d), lambda b, t: (b * nt + t, 0)),
            pl.BlockSpec((n_mem, d), lambda b, t: (b, 0)),
            pl.BlockSpec((n_mem, d), lambda b, t: (b, 1)),
        ],
        out_specs=pl.BlockSpec((tq, d), lambda b, t: (b * nt + t, 0)),
        out_shape=jax.ShapeDtypeStruct((m, d), BF16),
        compiler_params=_cparams(("parallel", "arbitrary")),
        name="xattn",
    )(q, kv, kv)


ROUTER_ROWS = 40


def _router_kernel(h_ref, g_ref, wr_ref, br_ref, a_ref, meta_ref, cnt_ref, carry_ref):
    tm = h_ref.shape[0]
    nr = wr_ref.shape[0]

    @pl.when(pl.program_id(0) == 0)
    def _():
        carry_ref[...] = jnp.zeros_like(carry_ref)

    a = _rms(h_ref[...], g_ref[...])
    a_ref[...] = a.astype(BF16).reshape(a_ref.shape)
    a_hi = a.astype(BF16)
    a_lo = (a - a_hi.astype(F32)).astype(BF16)
    w = wr_ref[...]
    w_hi = w.astype(BF16)
    w_lo = (w - w_hi.astype(F32)).astype(BF16)
    logits = (_nt_dot(w_hi, a_hi) + _nt_dot(w_hi, a_lo) + _nt_dot(w_lo, a_hi)) + br_ref[:, 0:1]
    row = lax.broadcasted_iota(jnp.int32, (nr, tm), 0)
    row_f = row.astype(F32)
    neg = -jnp.inf

    def first_max(x):
        v = jnp.max(x, axis=0, keepdims=True)
        return v, jnp.min(jnp.where(x == v, row_f, float(nr)), axis=0, keepdims=True)

    gl = jnp.where(row < N_GROUPS, logits, neg)
    gmax, gidx = first_max(gl)
    p_group = 1.0 / jnp.sum(jnp.exp(gl - gmax), axis=0, keepdims=True)
    lo = float(N_GROUPS) + gidx * float(EXPERTS_PER_GROUP)
    el = jnp.where((row_f >= lo) & (row_f < lo + float(EXPERTS_PER_GROUP)), logits, neg)
    v1, i1 = first_max(el)
    v2, i2 = first_max(jnp.where(row_f == i1, neg, el))
    t = jnp.exp(v2 - v1)
    g1 = p_group / (1.0 + t)
    g2 = p_group * t / (1.0 + t)

    hit1 = row_f == i1
    hit2 = row_f == i2
    onehot = jnp.where(hit1 | hit2, 1.0, 0.0)
    src = lax.broadcasted_iota(jnp.int32, (tm, tm), 0)
    dst = lax.broadcasted_iota(jnp.int32, (tm, tm), 1)
    earlier = jnp.where(src < dst, 1.0, 0.0).astype(BF16)
    before = jnp.dot(onehot.astype(BF16), earlier, preferred_element_type=F32) + carry_ref[:, 0:1]
    r1 = jnp.sum(jnp.where(hit1, before, 0.0), axis=0, keepdims=True)
    r2 = jnp.sum(jnp.where(hit2, before, 0.0), axis=0, keepdims=True)
    carry_ref[...] = carry_ref[...] + jnp.sum(onehot, axis=1, keepdims=True)
    cnt_ref[...] = carry_ref[...]

    out_row = lax.broadcasted_iota(jnp.int32, (LANES, tm), 0)
    meta_t = jnp.zeros((LANES, tm), F32)
    for idx, val in enumerate((i1 - float(N_GROUPS), i2 - float(N_GROUPS), r1, r2, g1, g2)):
        meta_t = jnp.where(out_row == idx, val, meta_t)
    meta_ref[...] = meta_t.T


def _router(h, g, wr, br, *, tm):
    m, d = h.shape
    slab = d // LANES
    return pl.pallas_call(
        _router_kernel,
        grid=(m // tm,),
        in_specs=[
            pl.BlockSpec((tm, d), lambda i: (i, 0)),
            pl.BlockSpec((1, d), lambda i: (0, 0)),
            pl.BlockSpec((ROUTER_ROWS, d), lambda i: (0, 0)),
            pl.BlockSpec((ROUTER_ROWS, LANES), lambda i: (0, 0)),
        ],
        out_specs=[
            pl.BlockSpec((tm, slab, LANES), lambda i: (i, 0, 0)),
            pl.BlockSpec((tm, LANES), lambda i: (i, 0)),
            pl.BlockSpec((ROUTER_ROWS, LANES), lambda i: (0, 0)),
        ],
        out_shape=[jax.ShapeDtypeStruct((m, slab, LANES), BF16), jax.ShapeDtypeStruct((m, LANES), F32),
                   jax.ShapeDtypeStruct((ROUTER_ROWS, LANES), F32)],
        scratch_shapes=[pltpu.VMEM((ROUTER_ROWS, LANES), F32)],
        compiler_params=_cparams(("arbitrary",)),
        name="router",
    )(h, g, wr, br)


PAD_PIECES = tuple(PAD_UNIT >> (b + 1) for b in range(PAD_UNIT.bit_length() - 1))


def _dispatch_kernel(d1_ref, d2_ref, fill_ref, npad_ref, nb_ref, a_ref, xs_hbm, slot_ref, st0, st1, zbuf, sem, zsem):
    i = pl.program_id(0)
    n_steps = pl.num_programs(0)
    tm = a_ref.shape[0]
    bufs = (st0, st1)
    rb = zbuf.shape[0]
    n_blocks = xs_hbm.shape[0] // rb
    min_blocks = (2 * tm * n_steps) // rb

    def zero_copies():
        out = []
        for e in range(N_EXPERTS):
            p = npad_ref[e]
            for piece in PAD_PIECES:
                out.append((p & piece != 0, pltpu.make_async_copy(
                    zbuf.at[pl.ds(0, piece)], xs_hbm.at[pl.ds(fill_ref[e] + (p & -(2 * piece)), piece)], zsem.at[0])))
        for b in range(min_blocks, n_blocks):
            out.append((b >= nb_ref[0], pltpu.make_async_copy(zbuf, xs_hbm.at[pl.ds(b * rb, rb)], zsem.at[0])))
        return out

    def wait_rows(s):
        for _ in range(2):
            pltpu.make_async_copy(bufs[s], xs_hbm.at[pl.ds(0, tm)], sem.at[s]).wait()

    @pl.when(i == 0)
    def _():
        def mark_unused(j, carry):
            slot_ref[j] = -1
            return carry
        for e in range(N_EXPERTS):
            lax.fori_loop(fill_ref[e], fill_ref[e] + npad_ref[e], mark_unused, 0)
        lax.fori_loop(nb_ref[0] * rb, slot_ref.shape[0], mark_unused, 0)
        zbuf[...] = jnp.zeros_like(zbuf)
        for cond, cp in zero_copies():
            @pl.when(cond)
            def _():
                cp.start()

    for s in (0, 1):
        @pl.when(lax.rem(i, 2) == s)
        def _():
            @pl.when(i >= 2)
            def _():
                wait_rows(s)

            bufs[s][...] = a_ref[...]
            for r in range(tm):
                tok = i * tm + r
                d1 = d1_ref[tok]
                d2 = d2_ref[tok]
                slot_ref[d1] = 2 * tok
                slot_ref[d2] = 2 * tok + 1
                pltpu.make_async_copy(bufs[s].at[r], xs_hbm.at[d1], sem.at[s]).start(priority=0)
                pltpu.make_async_copy(bufs[s].at[r], xs_hbm.at[d2], sem.at[s]).start(priority=1)

    @pl.when(i == n_steps - 1)
    def _():
        for s in (0, 1):
            @pl.when((lax.rem(n_steps - 1, 2) == s) | ((n_steps >= 2) & (lax.rem(n_steps, 2) == s)))
            def _():
                wait_rows(s)
        for cond, cp in zero_copies():
            @pl.when(cond)
            def _():
                cp.wait()


def _dispatch(dest1, dest2, fill, npad, n_blk, a, *, n_rows, tm):
    m, slab, _ = a.shape
    grid_spec = pltpu.PrefetchScalarGridSpec(
        num_scalar_prefetch=5,
        grid=(m // tm,),
        in_specs=[pl.BlockSpec((tm, slab, LANES), lambda i, *_: (i, 0, 0))],
        out_specs=[pl.BlockSpec(memory_space=pl.ANY), pl.BlockSpec(memory_space=pltpu.SMEM)],
        scratch_shapes=[pltpu.VMEM((tm, slab, LANES), a.dtype), pltpu.VMEM((tm, slab, LANES), a.dtype),
                        pltpu.VMEM((PAD_UNIT, slab, LANES), a.dtype),
                        pltpu.SemaphoreType.DMA((2,)), pltpu.SemaphoreType.DMA((1,))],
    )
    return pl.pallas_call(
        _dispatch_kernel,
        grid_spec=grid_spec,
        out_shape=[jax.ShapeDtypeStruct((n_rows, slab, LANES), a.dtype), jax.ShapeDtypeStruct((n_rows,), jnp.int32)],
        compiler_params=_cparams(("arbitrary",)),
        name="dispatch",
    )(dest1, dest2, fill, npad, n_blk, a)


def _weight_copies(w_hbms, e, wst_refs, wsem):
    copies = []
    for w_hbm, wst_ref in zip(w_hbms, wst_refs):
        rows_per = w_hbm.shape[1] // WEIGHT_CHUNKS
        for c in range(WEIGHT_CHUNKS):
            rows = pl.ds(c * rows_per, rows_per)
            copies.append(pltpu.make_async_copy(w_hbm.at[e, rows, :], wst_ref.at[rows, :], wsem.at[0]))
    return copies


def _load_expert_weights(i, be_ref, nx_ref, w_hbms, wst_refs, wb_refs, wsem):
    def start(e):
        for cp in _weight_copies(w_hbms, e, wst_refs, wsem):
            cp.start(priority=WEIGHT_DMA_PRIORITY)

    @pl.when(i == 0)
    def _():
        start(be_ref[0])

    @pl.when((i == 0) | (be_ref[i] != be_ref[jnp.maximum(i - 1, 0)]))
    def _():
        for cp in _weight_copies(w_hbms, be_ref[i], wst_refs, wsem):
            cp.wait()
        for wst_ref, wb_ref in zip(wst_refs, wb_refs):
            _for_row_chunks(wst_ref.shape[0], lambda rows: wb_ref.__setitem__((rows, slice(None)),
                                                                              wst_ref[rows, :].astype(BF16)))

        @pl.when(nx_ref[i] >= 0)
        def _():
            start(nx_ref[i])


def _expert_kernel(slot_ref, be_ref, nx_ref, off_ref, half_ref, nb_ref, x_ref, wg_hbm, wu_hbm, wd_hbm, y_hbm, wsg, wsu, wsd,
                   wgb, wub, wdb, ys0, ys1, sem, wsem, *, n_tok):
    i = pl.program_id(0)
    n_steps = pl.num_programs(0)
    nb = nb_ref[0]
    rb = x_ref.shape[0]
    bufs = (ys0, ys1)

    def wait_block(buf, s):
        pltpu.make_async_copy(buf, y_hbm.at[pl.ds(0, rb)], sem.at[s]).wait()

    def scatter_previous(s):
        o = 1 - s
        prev = jnp.maximum(i - 1, 0)
        used = jnp.where(i >= 1, rb - half_ref[prev] * (rb // 2), 0)
        for r in range(rb):
            v = jnp.where(r < used, slot_ref[off_ref[prev] + r], -1)
            dst = jnp.where(v >= 0, (v & 1) * n_tok + (v >> 1), 2 * n_tok + o * rb + r)
            pltpu.make_async_copy(bufs[o].at[r], y_hbm.at[dst], sem.at[o]).start()

    @pl.when(i == 0)
    def _():
        ys0[...] = jnp.zeros_like(ys0)
        ys1[...] = jnp.zeros_like(ys1)
        spare = pltpu.make_async_copy(ys0, y_hbm.at[pl.ds(2 * n_tok, rb)], sem.at[0])
        spare.start()
        spare.wait()

    for s in (0, 1):
        for rows in (rb, rb // 2):
            @pl.when((i < nb) & (lax.rem(i, 2) == s) & (half_ref[jnp.minimum(i, n_steps - 2)] == (rows != rb)))
            def _():
                _load_expert_weights(i, be_ref, nx_ref, (wg_hbm, wu_hbm, wd_hbm), (wsg, wsu, wsd), (wgb, wub, wdb),
                                     wsem)

                @pl.when(i >= 1)
                def _():
                    wait_block(bufs[s], s)

                scatter_previous(s)
                x = x_ref[0:rows].reshape(rows, -1)
                hg = jnp.dot(x, wgb[...], preferred_element_type=F32)
                hu = jnp.dot(x, wub[...], preferred_element_type=F32)
                hb = (hg * _sigmoid(hg) * hu).astype(BF16)
                y = jnp.dot(hb, wdb[...], preferred_element_type=F32)
                bufs[s][0:rows] = y.astype(bufs[s].dtype).reshape((rows,) + bufs[s].shape[1:])

        @pl.when((i == nb) & (lax.rem(i, 2) == s))
        def _():
            scatter_previous(s)

    @pl.when(i == n_steps - 1)
    def _():
        for s in (0, 1):
            wait_block(bufs[s], s)


def _experts(slot, block_e, next_e, block_off, block_half, n_blk, xs, wg, wu, wd, *, rb, n_tok):
    _, slab, _ = xs.shape
    d = slab * LANES
    de = wg.shape[2]
    n_blocks = block_e.shape[0]

    def x_rows(i, sl, be, nx, off, hf, nb):
        return (pl.multiple_of(off[jnp.minimum(i, nb[0] - 1)], PAD_UNIT), 0, 0)

    grid_spec = pltpu.PrefetchScalarGridSpec(
        num_scalar_prefetch=6,
        grid=(n_blocks + 1,),
        in_specs=[pl.BlockSpec((pl.Element(rb), pl.Element(slab), pl.Element(LANES)), x_rows)]
        + [pl.BlockSpec(memory_space=pl.ANY)] * 3,
        out_specs=pl.BlockSpec(memory_space=pl.ANY),
        scratch_shapes=[pltpu.VMEM((d, de), F32), pltpu.VMEM((d, de), F32), pltpu.VMEM((de, d), F32),
                        pltpu.VMEM((d, de), BF16), pltpu.VMEM((d, de), BF16), pltpu.VMEM((de, d), BF16),
                        pltpu.VMEM((rb, slab, LANES), BF16), pltpu.VMEM((rb, slab, LANES), BF16),
                        pltpu.SemaphoreType.DMA((2,)), pltpu.SemaphoreType.DMA((1,))],
    )
    return pl.pallas_call(
        functools.partial(_expert_kernel, n_tok=n_tok),
        grid_spec=grid_spec,
        out_shape=jax.ShapeDtypeStruct((2 * n_tok + 2 * rb, slab, LANES), BF16),
        compiler_params=_cparams(("arbitrary",), vmem=EXPERT_VMEM_LIMIT),
        name="experts",
    )(slot, block_e, next_e, block_off, block_half, n_blk, xs, wg, wu, wd)


def _combine_kernel(y0_ref, y1_ref, h_ref, meta_ref, g_ref, o_ref):
    tm, d = h_ref.shape
    meta = meta_ref[...]
    y0 = y0_ref[...].reshape(tm, d).astype(F32)
    y1 = y1_ref[...].reshape(tm, d).astype(F32)
    out = h_ref[...] + meta[:, 4:5] * y0 + meta[:, 5:6] * y1
    o_ref[...] = _rms(out, g_ref[...])


def _combine(y, h, meta, g, *, tm):
    m, d = h.shape
    slab = d // LANES
    nt = m // tm
    return pl.pallas_call(
        _combine_kernel,
        grid=(nt,),
        in_specs=[
            pl.BlockSpec((tm, slab, LANES), lambda i: (i, 0, 0)),
            pl.BlockSpec((tm, slab, LANES), lambda i: (nt + i, 0, 0)),
            pl.BlockSpec((tm, d), lambda i: (i, 0)),
            pl.BlockSpec((tm, LANES), lambda i: (i, 0)),
            pl.BlockSpec((1, d), lambda i: (0, 0)),
        ],
        out_specs=pl.BlockSpec((tm, d), lambda i: (i, 0)),
        out_shape=jax.ShapeDtypeStruct((m, d), F32),
        compiler_params=_cparams(("parallel",)),
        name="combine",
    )(y, y, h, meta, g)


def kernel(x, mem, norm_mix_g, w_in, w_gla_alpha_up, b_gla_alpha, gla_out_norm_g, hgrn_lb_logits, hgrn_out_norm_g, w_mix_out, norm_xattn_g, norm_mem_g, w_xattn_q, w_xattn_kv, w_xattn_out, norm_ffn_g, w_router_group, b_router_group, w_router_expert, b_router_expert, w_expert_gate, w_expert_up, w_expert_down, norm_final_g):
    batch, seq, d = x.shape
    n_mem = mem.shape[1]
    m = batch * seq
    depth = norm_mix_g.shape[0]
    h = x.reshape(m, d)
    lb_all = jnp.cumsum(jax.nn.softmax(hgrn_lb_logits.astype(F32), axis=0), axis=0)
    gla_cols = 2 * GLA_HEADS * GLA_DK + 2 * GLA_HEADS * GLA_DV
    lr_rank = w_gla_alpha_up.shape[1]

    for l in range(depth):
        w_t = jnp.swapaxes(w_in[l], 0, 1)
        w_lr_t = jnp.pad(w_t[gla_cols:gla_cols + lr_rank], ((0, LANES - lr_rank), (0, 0))).astype(BF16)
        w_up = jnp.pad(w_gla_alpha_up[l], ((0, LANES - lr_rank), (0, 0))).astype(BF16)
        a_mix, la = _norm_gate(h, norm_mix_g[l][None, :], w_lr_t, w_up, b_gla_alpha[l][None, :], tm=min(m, 512))
        proj = _inproj(a_mix, w_t, tm=min(m, 4096), tn=512, skip_from=gla_cols, skip=lr_rank)
        lb = lb_all[l]
        lbc = jnp.zeros((8, lb.shape[0]), F32).at[0].set(jnp.log(lb)).at[1].set(jnp.log1p(-lb)).at[2].set(1.0 - lb)
        proj3 = proj.reshape(batch, seq, -1)
        o_gla = _gla(proj3, la.reshape(batch, seq, -1), gla_out_norm_g[l][None, :], tb=min(seq, 256), hp=4)
        o_h = _hgrn(proj3, lbc, hgrn_out_norm_g[l][None, :], tb=min(seq, 256), hp=HGRN_HP)
        o_gla, o_h = o_gla.reshape(m, -1), o_h.reshape(m, -1)
        h = _mm_res([o_gla, o_h], w_mix_out[l], h, tm=min(m, 512), tn=d, name="mix_out")

        kv = _norm_mm(mem.reshape(batch * n_mem, d), norm_mem_g[l][None, :], w_xattn_kv[l],
                      tm=batch * n_mem, tn=1024, out_dtype=BF16, name="mem_kv")
        q = _norm_mm(h, norm_xattn_g[l][None, :], w_xattn_q[l], tm=min(m, 512), tn=d, out_dtype=BF16,
                     name="xattn_q")
        o = _xattn(q, kv, batch=batch, seq=seq, n_mem=n_mem, tq=min(seq, 512))
        h = _mm_res([o], w_xattn_out[l], h, tm=min(m, 512), tn=d, name="xattn_out")

        n_logits = N_GROUPS + N_EXPERTS
        wr = jnp.pad(jnp.concatenate([w_router_group[l], w_router_expert[l]], axis=1).T,
                     ((0, ROUTER_ROWS - n_logits), (0, 0)))
        br = jnp.pad(jnp.concatenate([b_router_group[l], b_router_expert[l]]), (0, ROUTER_ROWS - n_logits))
        a, meta, cnt = _router(h, norm_ffn_g[l][None, :], wr, jnp.broadcast_to(br[:, None], (ROUTER_ROWS, LANES)),
                               tm=min(m, 512))

        e_idx = meta[:, 0:2].astype(jnp.int32)
        rank = meta[:, 2:4].astype(jnp.int32)
        counts = cnt[N_GROUPS:N_GROUPS + N_EXPERTS, 0].astype(jnp.int32)
        padded = ((counts + PAD_UNIT - 1) // PAD_UNIT) * PAD_UNIT
        pad_end = jnp.cumsum(padded)
        pad_start = pad_end - padded
        n_rows = 2 * m + N_EXPERTS * PAD_UNIT + PAD_UNIT
        n_units = (pad_end[-1:] // PAD_UNIT).astype(jnp.int32)
        n_full = padded // EXPERT_BLOCK
        n_blk_e = n_full + (padded % EXPERT_BLOCK) // PAD_UNIT
        blk_end = jnp.cumsum(n_blk_e)
        blk_start = blk_end - n_blk_e
        n_blocks = (2 * m) // EXPERT_BLOCK + N_EXPERTS
        n_blk = blk_end[-1:].astype(jnp.int32)
        blk = jnp.arange(n_blocks, dtype=jnp.int32)
        block_e = jnp.minimum(jnp.sum((blk_end[None, :] <= blk[:, None]).astype(jnp.int32), axis=1), N_EXPERTS - 1)
        blk_local = blk - blk_start[block_e]
        block_off = (pad_start[block_e] + EXPERT_BLOCK * blk_local).astype(jnp.int32)
        block_off = jnp.where(blk < n_blk[0], block_off, 0)
        block_half = (blk_local >= n_full[block_e]).astype(jnp.int32)
        after = blk_end[block_e]
        next_e = jnp.where(after < n_blk[0], block_e[jnp.minimum(after, n_blocks - 1)], -1).astype(jnp.int32)

        experts = jnp.arange(N_EXPERTS, dtype=jnp.int32)
        dest = [jnp.sum(jnp.where(e_idx[:, k, None] == experts[None, :], pad_start[None, :], 0), axis=1) + rank[:, k]
                for k in range(2)]
        xs, slot = _dispatch(dest[0], dest[1], pad_start + counts, padded - counts, n_units, a, n_rows=n_rows,
                             tm=min(m, 256))
        y = _experts(slot, block_e, next_e, block_off, block_half, n_blk, xs, w_expert_gate[l], w_expert_up[l],
                     w_expert_down[l], rb=EXPERT_BLOCK, n_tok=m)
        last = l == depth - 1
        g_fin = norm_final_g[None, :] if last else jnp.ones((1, d), F32)
        h = _combine(y, h, meta, g_fin, tm=min(m, 256))
        assert last, "the combine kernel fuses the final rmsnorm; deeper stacks need an un-normalised variant"

    return h.reshape(batch, seq, d)
```

```python
import functools
from typing import NamedTuple

import jax
import jax.numpy as jnp
from jax import lax
from jax.experimental import pallas as pl
from jax.experimental.pallas import tpu as pltpu

F32 = jnp.float32
BF16 = jnp.bfloat16

EPS = 1e-6
CHUNK = 64
LANES = 128
GLA_HEADS, GLA_DK, GLA_DV = 4, 128, 256
HGRN_HEADS, HGRN_DH = 8, 128
XA_HEADS, XA_DH = 4, 512
N_GROUPS, EXPERTS_PER_GROUP, N_EXPERTS = 4, 8, 32
EXPERT_BLOCK = 256
PAD_UNIT = 128
WEIGHT_CHUNKS = 8
WEIGHT_DMA_PRIORITY = 1
VMEM_LIMIT = 56 * 1024 * 1024
EXPERT_VMEM_LIMIT = 60 * 1024 * 1024


class Tiles(NamedTuple):
    norm_gate_rows: int
    inproj_rows: int
    inproj_cols: int
    mixer_rows: int
    gla_heads: int
    hgrn_heads: int
    square_rows: int
    mem_kv_cols: int
    xattn_rows: int
    router_rows: int
    dispatch_rows: int
    combine_rows: int


def _tiles(m, seq):
    return Tiles(norm_gate_rows=min(m, 1024), inproj_rows=min(m, 4096), inproj_cols=512, mixer_rows=min(seq, 256),
                 gla_heads=GLA_HEADS, hgrn_heads=HGRN_HEADS, square_rows=min(m, 512), mem_kv_cols=1024,
                 xattn_rows=min(seq, 512), router_rows=min(m, 512), dispatch_rows=min(m, 256),
                 combine_rows=min(m, 512))


def _cparams(sem, vmem=VMEM_LIMIT):
    return pltpu.CompilerParams(dimension_semantics=sem, vmem_limit_bytes=vmem)


def _log1pexp_neg(t):
    return jnp.log(1.0 + jnp.exp(-t))


def _log_sigmoid(z):
    return jnp.minimum(z, 0.0) - _log1pexp_neg(jnp.abs(z))


def _sigmoid(z):
    return 1.0 / (1.0 + jnp.exp(-z))


def _rms(x, g):
    return x * lax.rsqrt(jnp.mean(x * x, axis=-1, keepdims=True) + EPS) * g


def _nt_dot(x, y):
    return lax.dot_general(x, y, (((1,), (1,)), ((), ())), preferred_element_type=F32)


NORM_ROWS = 256


def _for_row_chunks(n_rows, fn):
    step = min(NORM_ROWS, n_rows)

    def body(ci, carry):
        fn(pl.ds(pl.multiple_of(ci * step, step), step))
        return carry

    lax.fori_loop(0, n_rows // step, body, 0)


def _norm_gate_kernel(x_ref, g_ref, wlr_ref, wup_ref, bal_ref, a_ref, la_ref):
    a = _rms(x_ref[...], g_ref[...]).astype(BF16)
    a_ref[...] = a
    lr = _nt_dot(a, wlr_ref[...])
    z = jnp.dot(lr.astype(BF16), wup_ref[...], preferred_element_type=F32) + bal_ref[...]
    la_ref[...] = _log_sigmoid(z) * (1.0 / 16.0)


def _norm_gate(x, g, w_lr_t, w_up, b_al, *, tm):
    m, d = x.shape
    nk = w_up.shape[1]
    return pl.pallas_call(
        _norm_gate_kernel,
        grid=(m // tm,),
        in_specs=[
            pl.BlockSpec((tm, d), lambda i: (i, 0)),
            pl.BlockSpec((1, d), lambda i: (0, 0)),
            pl.BlockSpec((LANES, d), lambda i: (0, 0)),
            pl.BlockSpec((LANES, nk), lambda i: (0, 0)),
            pl.BlockSpec((1, nk), lambda i: (0, 0)),
        ],
        out_specs=[pl.BlockSpec((tm, d), lambda i: (i, 0)), pl.BlockSpec((tm, nk), lambda i: (i, 0))],
        out_shape=[jax.ShapeDtypeStruct((m, d), BF16), jax.ShapeDtypeStruct((m, nk), F32)],
        compiler_params=_cparams(("parallel",)),
        name="norm_gate",
    )(x, g, w_lr_t, w_up, b_al)


def _inproj_kernel(a_ref, w_ref, proj_ref):
    proj_ref[...] = _nt_dot(a_ref[...], w_ref[...].astype(BF16))


def _inproj(a, w_t, *, tm, tn, skip_from, skip):
    m, d = a.shape
    n = w_t.shape[0] - skip
    first_after = skip_from // tn

    def w_rows(i, j):
        return (pl.multiple_of(j * tn + jnp.where(j >= first_after, skip, 0), 8), 0)

    return pl.pallas_call(
        _inproj_kernel,
        grid=(m // tm, n // tn),
        in_specs=[
            pl.BlockSpec((tm, d), lambda i, j: (i, 0), pipeline_mode=pl.Buffered(1)),
            pl.BlockSpec((pl.Element(tn), pl.Element(d)), w_rows),
        ],
        out_specs=pl.BlockSpec((tm, tn), lambda i, j: (i, j)),
        out_shape=jax.ShapeDtypeStruct((m, n), F32),
        compiler_params=_cparams(("parallel", "arbitrary")),
        name="inproj",
    )(a, w_t)


def _weight_spec(k, n, tn):
    if tn == n:
        return pl.BlockSpec((k, n), lambda i, j: (0, 0), pipeline_mode=pl.Buffered(1))
    return pl.BlockSpec((k, tn), lambda i, j: (0, j))


def _cast_weight(w_ref, wb_ref, resident):
    if resident:
        @pl.when((pl.program_id(0) == 0) & (pl.program_id(1) == 0))
        def _():
            wb_ref[...] = w_ref[...].astype(BF16)
    else:
        wb_ref[...] = w_ref[...].astype(BF16)


def _norm_mm_kernel(x_ref, g_ref, w_ref, o_ref, a_ref, wb_ref, *, resident):
    @pl.when(pl.program_id(1) == 0)
    def _():
        def rows_fn(rows):
            a_ref[rows, :] = _rms(x_ref[rows, :], g_ref[...]).astype(BF16)

        _for_row_chunks(x_ref.shape[0], rows_fn)

    _cast_weight(w_ref, wb_ref, resident)
    o_ref[...] = jnp.dot(a_ref[...], wb_ref[...], preferred_element_type=F32).astype(o_ref.dtype)


def _norm_mm(x, g, w, *, tm, tn, out_dtype, name):
    m, d = x.shape
    n = w.shape[1]
    return pl.pallas_call(
        functools.partial(_norm_mm_kernel, resident=tn == n),
        grid=(m // tm, n // tn),
        in_specs=[
            pl.BlockSpec((tm, d), lambda i, j: (i, 0)),
            pl.BlockSpec((1, d), lambda i, j: (0, 0)),
            _weight_spec(d, n, tn),
        ],
        out_specs=pl.BlockSpec((tm, tn), lambda i, j: (i, j)),
        out_shape=jax.ShapeDtypeStruct((m, n), out_dtype),
        scratch_shapes=[pltpu.VMEM((tm, d), BF16), pltpu.VMEM((d, tn), BF16)],
        compiler_params=_cparams(("arbitrary", "arbitrary")),
        name=name,
    )(x, g, w)


def _mm_res_kernel(*refs, n_lhs, resident):
    lhs = refs[:n_lhs]
    w_ref, res_ref, o_ref, wb_ref = refs[n_lhs:]
    _cast_weight(w_ref, wb_ref, resident)
    acc = res_ref[...]
    k0 = 0
    for l_ref in lhs:
        kp = l_ref.shape[1]
        acc = acc + jnp.dot(l_ref[...], wb_ref[k0:k0 + kp, :], preferred_element_type=F32)
        k0 += kp
    o_ref[...] = acc


def _mm_res(lhs_parts, w, res, *, tm, tn, name):
    m, n = res.shape
    k = w.shape[0]
    n_lhs = len(lhs_parts)
    in_specs = [pl.BlockSpec((tm, p.shape[1]), lambda i, j: (i, 0)) for p in lhs_parts]
    in_specs += [_weight_spec(k, n, tn), pl.BlockSpec((tm, tn), lambda i, j: (i, j))]
    return pl.pallas_call(
        functools.partial(_mm_res_kernel, n_lhs=n_lhs, resident=tn == n),
        grid=(m // tm, n // tn),
        in_specs=in_specs,
        out_specs=pl.BlockSpec((tm, tn), lambda i, j: (i, j)),
        out_shape=jax.ShapeDtypeStruct((m, n), F32),
        scratch_shapes=[pltpu.VMEM((k, tn), BF16)],
        compiler_params=_cparams(("arbitrary", "arbitrary")),
        name=name,
    )(*lhs_parts, w, res)


LEVELS = (32, 16, 8, 4, 2, 1)
LOG2E = 1.4426950408889634


def _split3_bf16(x):
    def top(v):
        bits = lax.bitcast_convert_type(v, jnp.uint32) & jnp.uint32(0xFFFF0000)
        return lax.bitcast_convert_type(bits, F32)
    hi = top(x)
    r1 = x - hi
    mid = top(r1)
    lo = r1 - mid
    return hi.astype(BF16), mid.astype(BF16), lo.astype(BF16)


def _mix_chunks(qs, ks, vs, las, st_refs):
    heads = range(len(qs))
    c, dk = qs[0].shape
    row = lax.broadcasted_iota(jnp.int32, (c, c), 0)
    col = lax.broadcasted_iota(jnp.int32, (c, c), 1)
    rowk = lax.broadcasted_iota(jnp.int32, (c, dk), 0)
    xor = jnp.bitwise_xor(row, col)

    las = [la * LOG2E for la in las]

    tri = jnp.where(col <= row, 1.0, 0.0).astype(BF16)
    b3 = [jnp.dot(tri, jnp.concatenate(_split3_bf16(la), axis=1), preferred_element_type=F32) for la in las]
    bs = [(t[:, :dk] + t[:, dk:2 * dk]) + t[:, 2 * dk:] for t in b3]

    def neg_dist(w, b, la):
        if w >= 4:
            parts = [jnp.broadcast_to(b[base + w - 1:base + w, :], (2 * w, dk)) for base in range(0, c, 2 * w)]
            m = parts[0] if len(parts) == 1 else jnp.concatenate(parts, axis=0)
            return -jnp.abs(b - m)
        if w == 2:
            r4 = jnp.bitwise_and(rowk, 3)
            nxt = pltpu.roll(la, c - 1, 0)
            prv = pltpu.roll(la, 1, 0)
            return jnp.where(r4 == 0, nxt, jnp.where(r4 == 1, 0.0, jnp.where(r4 == 2, la, la + prv)))
        return jnp.where(jnp.bitwise_and(rowk, 1) == 1, la, 0.0)

    acc = [_nt_dot(qs[h].astype(BF16), ks[h].astype(BF16)) for h in heads]
    for w in reversed(LEVELS):
        upper = jnp.bitwise_and(rowk, w) != 0
        xs = [(jnp.where(upper, qs[h], ks[h]) * jnp.exp2(neg_dist(w, bs[h], las[h]))).astype(BF16) for h in heads]
        gs = [_nt_dot(x, x) for x in xs]
        acc = [jnp.where(xor >= w, gs[h], acc[h]) for h in heads]
    ab = [jnp.where(col <= row, a, 0.0).astype(BF16) for a in acc]

    sts = [st_refs[h][...] for h in heads]
    qx = [(qs[h] * jnp.exp2(bs[h])).astype(BF16) for h in heads]
    b_last = [b[c - 1:c, :] for b in bs]
    kx = [(ks[h] * jnp.exp2(b_last[h] - bs[h])).astype(BF16) for h in heads]
    vb = [v.astype(BF16) for v in vs]
    outs = [jnp.dot(ab[h], vb[h], preferred_element_type=F32) + _nt_dot(qx[h], sts[h].astype(BF16)) for h in heads]
    for h in heads:
        st_refs[h][...] = jnp.exp2(b_last[h]) * sts[h] + lax.dot_general(
            vb[h], kx[h], (((0,), (0,)), ((), ())), preferred_element_type=F32)
    return outs


def _gated_norm(o, g, gn):
    return _rms(o, gn) * (g * _sigmoid(g))


def _gla_kernel(q_ref, k_ref, v_ref, g_ref, la_ref, gn_ref, o_ref, st_ref, *, n_chunk):
    @pl.when(pl.program_id(1) == 0)
    def _():
        st_ref[...] = jnp.zeros_like(st_ref)

    dk, dv = GLA_DK, GLA_DV
    n_b = q_ref.shape[0]
    n_h = st_ref.shape[0] // n_b
    streams = [(b, h) for b in range(n_b) for h in range(n_h)]
    kc = [slice(h * dk, (h + 1) * dk) for h in range(n_h)]
    vc = [slice(h * dv, (h + 1) * dv) for h in range(n_h)]

    def body(ci, carry):
        rows = pl.ds(pl.multiple_of(ci * CHUNK, CHUNK), CHUNK)
        outs = _mix_chunks([q_ref[b, rows, kc[h]] * (dk ** -0.5) for b, h in streams],
                           [k_ref[b, rows, kc[h]] for b, h in streams], [v_ref[b, rows, vc[h]] for b, h in streams],
                           [la_ref[b, rows, kc[h]] for b, h in streams],
                           [st_ref.at[n] for n in range(len(streams))])
        for n, (b, h) in enumerate(streams):
            o_ref[b, rows, vc[h]] = _gated_norm(outs[n], g_ref[b, rows, vc[h]], gn_ref[...]).astype(o_ref.dtype)
        return carry

    lax.fori_loop(0, n_chunk, body, 0)


def _gla(proj, la, gn, *, tb, hp):
    batch, seq, _ = proj.shape
    wk, wv = hp * GLA_DK, hp * GLA_DV
    n_grp = GLA_HEADS // hp
    spec = lambda width, first: pl.BlockSpec((batch, tb, width), lambda p, t: (0, t, first + p))
    return pl.pallas_call(
        functools.partial(_gla_kernel, n_chunk=tb // CHUNK),
        grid=(n_grp, seq // tb),
        in_specs=[spec(wk, 0), spec(wk, n_grp), spec(wv, n_grp), spec(wv, 2 * n_grp), spec(wk, 0),
                  pl.BlockSpec((1, GLA_DV), lambda p, t: (0, 0))],
        out_specs=spec(wv, 0),
        out_shape=jax.ShapeDtypeStruct((batch, seq, GLA_HEADS * GLA_DV), BF16),
        scratch_shapes=[pltpu.VMEM((batch * hp, GLA_DV, GLA_DK), F32)],
        compiler_params=_cparams(("parallel", "arbitrary")),
        name="gla",
    )(proj, proj, proj, proj, la, gn)


def _hgrn_kernel(q_ref, f_ref, i_ref, g_ref, lb_ref, gn_ref, o_ref, st_ref, *, n_chunk):
    @pl.when(pl.program_id(1) == 0)
    def _():
        st_ref[...] = jnp.zeros_like(st_ref)

    dh = HGRN_DH
    n_b = q_ref.shape[0]
    n_h = st_ref.shape[0] // n_b
    streams = [(b, h) for b in range(n_b) for h in range(n_h)]
    hc = [slice(h * dh, (h + 1) * dh) for h in range(n_h)]

    def body(ci, carry):
        rows = pl.ds(pl.multiple_of(ci * CHUNK, CHUNK), CHUNK)
        qs, ks, las = [], [], []
        for b, h in streams:
            log_lb, log_1mlb, one_m_lb = lb_ref[0:1, hc[h]], lb_ref[1:2, hc[h]], lb_ref[2:3, hc[h]]
            hq = q_ref[b, rows, hc[h]]
            z = f_ref[b, rows, hc[h]]
            x1 = log_1mlb + _log_sigmoid(z)
            las.append(jnp.maximum(log_lb, x1) + _log1pexp_neg(jnp.abs(log_lb - x1)))
            qs.append(hq * _sigmoid(hq))
            ks.append(one_m_lb * _sigmoid(-z))
        outs = _mix_chunks(qs, ks, [i_ref[b, rows, hc[h]] for b, h in streams], las,
                           [st_ref.at[n] for n in range(len(streams))])
        for n, (b, h) in enumerate(streams):
            o_ref[b, rows, hc[h]] = _gated_norm(outs[n], g_ref[b, rows, hc[h]], gn_ref[...]).astype(o_ref.dtype)
        return carry

    lax.fori_loop(0, n_chunk, body, 0)


def _hgrn(proj, lbc, gn, *, tb, hp):
    batch, seq, _ = proj.shape
    width = hp * HGRN_DH
    n_grp = HGRN_HEADS // hp
    first = 3072 // width
    spec = lambda seg: pl.BlockSpec((batch, tb, width), lambda p, t: (0, t, first + seg * n_grp + p))
    return pl.pallas_call(
        functools.partial(_hgrn_kernel, n_chunk=tb // CHUNK),
        grid=(n_grp, seq // tb),
        in_specs=[spec(0), spec(1), spec(2), spec(3),
                  pl.BlockSpec((8, width), lambda p, t: (0, p)),
                  pl.BlockSpec((1, HGRN_DH), lambda p, t: (0, 0))],
        out_specs=pl.BlockSpec((batch, tb, width), lambda p, t: (0, t, p)),
        out_shape=jax.ShapeDtypeStruct((batch, seq, HGRN_HEADS * HGRN_DH), BF16),
        scratch_shapes=[pltpu.VMEM((batch * hp, HGRN_DH, HGRN_DH), F32)],
        compiler_params=_cparams(("parallel", "arbitrary")),
        name="hgrn",
    )(proj, proj, proj, proj, lbc, gn)


def _xattn_kernel(q_ref, k_ref, v_ref, o_ref):
    cols = [slice(h * XA_DH, (h + 1) * XA_DH) for h in range(XA_HEADS)]
    s = [_nt_dot(q_ref[:, c], k_ref[:, c]) * (XA_DH ** -0.5) for c in cols]
    p = [jnp.exp(x - jnp.max(x, axis=-1, keepdims=True)) for x in s]
    p = [(x / jnp.sum(x, axis=-1, keepdims=True)).astype(BF16) for x in p]
    for c, x in zip(cols, p):
        o_ref[:, c] = jnp.dot(x, v_ref[:, c], preferred_element_type=F32).astype(o_ref.dtype)


def _xattn(q, kv, *, batch, seq, n_mem, tq):
    m, d = q.shape
    nt = seq // tq
    return pl.pallas_call(
        _xattn_kernel,
        grid=(batch, nt),
        in_specs=[
            pl.BlockSpec((tq, d), lambda b, t: (b * nt + t, 0)),
            pl.BlockSpec((n_mem, d), lambda b, t: (b, 0)),
            pl.BlockSpec((n_mem, d), lambda b, t: (b, 1)),
        ],
        out_specs=pl.BlockSpec((tq, d), lambda b, t: (b * nt + t, 0)),
        out_shape=jax.ShapeDtypeStruct((m, d), BF16),
        compiler_params=_cparams(("parallel", "arbitrary")),
        name="xattn",
    )(q, kv, kv)


ROUTER_ROWS = 40


def _router_kernel(h_ref, g_ref, wr_ref, br_ref, a_ref, meta_ref, cnt_ref, carry_ref):
    tm = h_ref.shape[0]
    nr = wr_ref.shape[0]

    @pl.when(pl.program_id(0) == 0)
    def _():
        carry_ref[...] = jnp.zeros_like(carry_ref)

    a = _rms(h_ref[...], g_ref[...])
    a_ref[...] = a.astype(BF16).reshape(a_ref.shape)
    a_hi = a.astype(BF16)
    a_lo = (a - a_hi.astype(F32)).astype(BF16)
    w = wr_ref[...]
    w_hi = w.astype(BF16)
    w_lo = (w - w_hi.astype(F32)).astype(BF16)
    logits = (_nt_dot(w_hi, a_hi) + _nt_dot(w_hi, a_lo) + _nt_dot(w_lo, a_hi)) + br_ref[:, 0:1]
    row = lax.broadcasted_iota(jnp.int32, (nr, tm), 0)
    row_f = row.astype(F32)
    neg = -jnp.inf

    def first_max(x):
        v = jnp.max(x, axis=0, keepdims=True)
        return v, jnp.min(jnp.where(x == v, row_f, float(nr)), axis=0, keepdims=True)

    gl = jnp.where(row < N_GROUPS, logits, neg)
    gmax, gidx = first_max(gl)
    p_group = 1.0 / jnp.sum(jnp.exp(gl - gmax), axis=0, keepdims=True)
    lo = float(N_GROUPS) + gidx * float(EXPERTS_PER_GROUP)
    el = jnp.where((row_f >= lo) & (row_f < lo + float(EXPERTS_PER_GROUP)), logits, neg)
    v1, i1 = first_max(el)
    v2, i2 = first_max(jnp.where(row_f == i1, neg, el))
    t = jnp.exp(v2 - v1)
    g1 = p_group / (1.0 + t)
    g2 = p_group * t / (1.0 + t)

    hit1 = row_f == i1
    hit2 = row_f == i2
    onehot = jnp.where(hit1 | hit2, 1.0, 0.0)
    src = lax.broadcasted_iota(jnp.int32, (tm, tm), 0)
    dst = lax.broadcasted_iota(jnp.int32, (tm, tm), 1)
    earlier = jnp.where(src < dst, 1.0, 0.0).astype(BF16)
    before = jnp.dot(onehot.astype(BF16), earlier, preferred_element_type=F32) + carry_ref[:, 0:1]
    r1 = jnp.sum(jnp.where(hit1, before, 0.0), axis=0, keepdims=True)
    r2 = jnp.sum(jnp.where(hit2, before, 0.0), axis=0, keepdims=True)
    carry_ref[...] = carry_ref[...] + jnp.sum(onehot, axis=1, keepdims=True)
    cnt_ref[...] = carry_ref[...]

    out_row = lax.broadcasted_iota(jnp.int32, (LANES, tm), 0)
    meta_t = jnp.zeros((LANES, tm), F32)
    for idx, val in enumerate((i1 - float(N_GROUPS), i2 - float(N_GROUPS), r1, r2, g1, g2)):
        meta_t = jnp.where(out_row == idx, val, meta_t)
    meta_ref[...] = meta_t.T


def _router(h, g, wr, br, *, tm):
    m, d = h.shape
    slab = d // LANES
    return pl.pallas_call(
        _router_kernel,
        grid=(m // tm,),
        in_specs=[
            pl.BlockSpec((tm, d), lambda i: (i, 0)),
            pl.BlockSpec((1, d), lambda i: (0, 0)),
            pl.BlockSpec((ROUTER_ROWS, d), lambda i: (0, 0)),
            pl.BlockSpec((ROUTER_ROWS, LANES), lambda i: (0, 0)),
        ],
        out_specs=[
            pl.BlockSpec((tm, slab, LANES), lambda i: (i, 0, 0)),
            pl.BlockSpec((tm, LANES), lambda i: (i, 0)),
            pl.BlockSpec((ROUTER_ROWS, LANES), lambda i: (0, 0)),
        ],
        out_shape=[jax.ShapeDtypeStruct((m, slab, LANES), BF16), jax.ShapeDtypeStruct((m, LANES), F32),
                   jax.ShapeDtypeStruct((ROUTER_ROWS, LANES), F32)],
        scratch_shapes=[pltpu.VMEM((ROUTER_ROWS, LANES), F32)],
        compiler_params=_cparams(("arbitrary",)),
        name="router",
    )(h, g, wr, br)


PAD_PIECES = tuple(PAD_UNIT >> (b + 1) for b in range(PAD_UNIT.bit_length() - 1))


def _dispatch_kernel(d1_ref, d2_ref, fill_ref, npad_ref, nb_ref, a_ref, xs_hbm, slot_ref, st0, st1, zbuf, sem, zsem):
    i = pl.program_id(0)
    n_steps = pl.num_programs(0)
    tm = a_ref.shape[0]
    bufs = (st0, st1)
    rb = zbuf.shape[0]
    n_blocks = xs_hbm.shape[0] // rb
    min_blocks = (2 * tm * n_steps) // rb

    def zero_copies():
        out = []
        for e in range(N_EXPERTS):
            p = npad_ref[e]
            for piece in PAD_PIECES:
                out.append((p & piece != 0, pltpu.make_async_copy(
                    zbuf.at[pl.ds(0, piece)], xs_hbm.at[pl.ds(fill_ref[e] + (p & -(2 * piece)), piece)], zsem.at[0])))
        for b in range(min_blocks, n_blocks):
            out.append((b >= nb_ref[0], pltpu.make_async_copy(zbuf, xs_hbm.at[pl.ds(b * rb, rb)], zsem.at[0])))
        return out

    def wait_rows(s):
        for _ in range(2):
            pltpu.make_async_copy(bufs[s], xs_hbm.at[pl.ds(0, tm)], sem.at[s]).wait()

    @pl.when(i == 0)
    def _():
        def mark_unused(j, carry):
            slot_ref[j] = -1
            return carry
        for e in range(N_EXPERTS):
            lax.fori_loop(fill_ref[e], fill_ref[e] + npad_ref[e], mark_unused, 0)
        lax.fori_loop(nb_ref[0] * rb, slot_ref.shape[0], mark_unused, 0)
        zbuf[...] = jnp.zeros_like(zbuf)
        for cond, cp in zero_copies():
            @pl.when(cond)
            def _():
                cp.start()

    for s in (0, 1):
        @pl.when(lax.rem(i, 2) == s)
        def _():
            @pl.when(i >= 2)
            def _():
                wait_rows(s)

            bufs[s][...] = a_ref[...]
            for r in range(tm):
                tok = i * tm + r
                d1 = d1_ref[tok]
                d2 = d2_ref[tok]
                slot_ref[d1] = 2 * tok
                slot_ref[d2] = 2 * tok + 1
                pltpu.make_async_copy(bufs[s].at[r], xs_hbm.at[d1], sem.at[s]).start(priority=0)
                pltpu.make_async_copy(bufs[s].at[r], xs_hbm.at[d2], sem.at[s]).start(priority=1)

    @pl.when(i == n_steps - 1)
    def _():
        for s in (0, 1):
            @pl.when((lax.rem(n_steps - 1, 2) == s) | ((n_steps >= 2) & (lax.rem(n_steps, 2) == s)))
            def _():
                wait_rows(s)
        for cond, cp in zero_copies():
            @pl.when(cond)
            def _():
                cp.wait()


def _dispatch(dest1, dest2, fill, npad, n_blk, a, *, n_rows, tm):
    m, slab, _ = a.shape
    grid_spec = pltpu.PrefetchScalarGridSpec(
        num_scalar_prefetch=5,
        grid=(m // tm,),
        in_specs=[pl.BlockSpec((tm, slab, LANES), lambda i, *_: (i, 0, 0))],
        out_specs=[pl.BlockSpec(memory_space=pl.ANY), pl.BlockSpec(memory_space=pltpu.SMEM)],
        scratch_shapes=[pltpu.VMEM((tm, slab, LANES), a.dtype), pltpu.VMEM((tm, slab, LANES), a.dtype),
                        pltpu.VMEM((PAD_UNIT, slab, LANES), a.dtype),
                        pltpu.SemaphoreType.DMA((2,)), pltpu.SemaphoreType.DMA((1,))],
    )
    return pl.pallas_call(
        _dispatch_kernel,
        grid_spec=grid_spec,
        out_shape=[jax.ShapeDtypeStruct((n_rows, slab, LANES), a.dtype), jax.ShapeDtypeStruct((n_rows,), jnp.int32)],
        compiler_params=_cparams(("arbitrary",)),
        name="dispatch",
    )(dest1, dest2, fill, npad, n_blk, a)


def _weight_copies(w_hbms, e, wst_refs, wsem):
    copies = []
    for w_hbm, wst_ref in zip(w_hbms, wst_refs):
        rows_per = w_hbm.shape[1] // WEIGHT_CHUNKS
        for c in range(WEIGHT_CHUNKS):
            rows = pl.ds(c * rows_per, rows_per)
            copies.append(pltpu.make_async_copy(w_hbm.at[e, rows, :], wst_ref.at[rows, :], wsem.at[0]))
    return copies


def _load_expert_weights(i, be_ref, nx_ref, w_hbms, wst_refs, wb_refs, wsem):
    def start(e):
        for cp in _weight_copies(w_hbms, e, wst_refs, wsem):
            cp.start(priority=WEIGHT_DMA_PRIORITY)

    @pl.when(i == 0)
    def _():
        start(be_ref[0])

    @pl.when((i == 0) | (be_ref[i] != be_ref[jnp.maximum(i - 1, 0)]))
    def _():
        for cp in _weight_copies(w_hbms, be_ref[i], wst_refs, wsem):
            cp.wait()
        for wst_ref, wb_ref in zip(wst_refs, wb_refs):
            _for_row_chunks(wst_ref.shape[0], lambda rows: wb_ref.__setitem__((rows, slice(None)),
                                                                              wst_ref[rows, :].astype(BF16)))

        @pl.when(nx_ref[i] >= 0)
        def _():
            start(nx_ref[i])


def _expert_kernel(slot_ref, be_ref, nx_ref, off_ref, half_ref, nb_ref, x_ref, wg_hbm, wu_hbm, wd_hbm, y_hbm, wsg, wsu, wsd,
                   wgb, wub, wdb, ys0, ys1, sem, wsem, *, n_tok):
    i = pl.program_id(0)
    n_steps = pl.num_programs(0)
    nb = nb_ref[0]
    rb = x_ref.shape[0]
    bufs = (ys0, ys1)

    def wait_block(buf, s):
        pltpu.make_async_copy(buf, y_hbm.at[pl.ds(0, rb)], sem.at[s]).wait()

    def scatter_previous(s):
        o = 1 - s
        prev = jnp.maximum(i - 1, 0)
        used = jnp.where(i >= 1, rb - half_ref[prev] * (rb // 2), 0)
        for r in range(rb):
            v = jnp.where(r < used, slot_ref[off_ref[prev] + r], -1)
            dst = jnp.where(v >= 0, (v & 1) * n_tok + (v >> 1), 2 * n_tok + o * rb + r)
            pltpu.make_async_copy(bufs[o].at[r], y_hbm.at[dst], sem.at[o]).start()

    @pl.when(i == 0)
    def _():
        ys0[...] = jnp.zeros_like(ys0)
        ys1[...] = jnp.zeros_like(ys1)
        spare = pltpu.make_async_copy(ys0, y_hbm.at[pl.ds(2 * n_tok, rb)], sem.at[0])
        spare.start()
        spare.wait()

    for s in (0, 1):
        for rows in (rb, rb // 2):
            @pl.when((i < nb) & (lax.rem(i, 2) == s) & (half_ref[jnp.minimum(i, n_steps - 2)] == (rows != rb)))
            def _():
                _load_expert_weights(i, be_ref, nx_ref, (wg_hbm, wu_hbm, wd_hbm), (wsg, wsu, wsd), (wgb, wub, wdb),
                                     wsem)

                @pl.when(i >= 1)
                def _():
                    wait_block(bufs[s], s)

                scatter_previous(s)
                x = x_ref[0:rows].reshape(rows, -1)
                hg = jnp.dot(x, wgb[...], preferred_element_type=F32)
                hu = jnp.dot(x, wub[...], preferred_element_type=F32)
                hb = (hg * _sigmoid(hg) * hu).astype(BF16)
                y = jnp.dot(hb, wdb[...], preferred_element_type=F32)
                bufs[s][0:rows] = y.astype(bufs[s].dtype).reshape((rows,) + bufs[s].shape[1:])

        @pl.when((i == nb) & (lax.rem(i, 2) == s))
        def _():
            scatter_previous(s)

    @pl.when(i == n_steps - 1)
    def _():
        for s in (0, 1):
            wait_block(bufs[s], s)


def _experts(slot, block_e, next_e, block_off, block_half, n_blk, xs, wg, wu, wd, *, rb, n_tok):
    _, slab, _ = xs.shape
    d = slab * LANES
    de = wg.shape[2]
    n_blocks = block_e.shape[0]

    def x_rows(i, sl, be, nx, off, hf, nb):
        return (pl.multiple_of(off[jnp.minimum(i, nb[0] - 1)], PAD_UNIT), 0, 0)

    grid_spec = pltpu.PrefetchScalarGridSpec(
        num_scalar_prefetch=6,
        grid=(n_blocks + 1,),
        in_specs=[pl.BlockSpec((pl.Element(rb), pl.Element(slab), pl.Element(LANES)), x_rows)]
        + [pl.BlockSpec(memory_space=pl.ANY)] * 3,
        out_specs=pl.BlockSpec(memory_space=pl.ANY),
        scratch_shapes=[pltpu.VMEM((d, de), F32), pltpu.VMEM((d, de), F32), pltpu.VMEM((de, d), F32),
                        pltpu.VMEM((d, de), BF16), pltpu.VMEM((d, de), BF16), pltpu.VMEM((de, d), BF16),
                        pltpu.VMEM((rb, slab, LANES), BF16), pltpu.VMEM((rb, slab, LANES), BF16),
                        pltpu.SemaphoreType.DMA((2,)), pltpu.SemaphoreType.DMA((1,))],
    )
    return pl.pallas_call(
        functools.partial(_expert_kernel, n_tok=n_tok),
        grid_spec=grid_spec,
        out_shape=jax.ShapeDtypeStruct((2 * n_tok + 2 * rb, slab, LANES), BF16),
        compiler_params=_cparams(("arbitrary",), vmem=EXPERT_VMEM_LIMIT),
        name="experts",
    )(slot, block_e, next_e, block_off, block_half, n_blk, xs, wg, wu, wd)


def _combine_kernel(y0_ref, y1_ref, h_ref, meta_ref, g_ref, o_ref):
    tm, d = h_ref.shape
    meta = meta_ref[...]
    y0 = y0_ref[...].reshape(tm, d).astype(F32)
    y1 = y1_ref[...].reshape(tm, d).astype(F32)
    out = h_ref[...] + meta[:, 4:5] * y0 + meta[:, 5:6] * y1
    o_ref[...] = _rms(out, g_ref[...])


def _combine(y, h, meta, g, *, tm):
    m, d = h.shape
    slab = d // LANES
    nt = m // tm
    return pl.pallas_call(
        _combine_kernel,
        grid=(nt,),
        in_specs=[
            pl.BlockSpec((tm, slab, LANES), lambda i: (i, 0, 0)),
            pl.BlockSpec((tm, slab, LANES), lambda i: (nt + i, 0, 0)),
            pl.BlockSpec((tm, d), lambda i: (i, 0)),
            pl.BlockSpec((tm, LANES), lambda i: (i, 0)),
            pl.BlockSpec((1, d), lambda i: (0, 0)),
        ],
        out_specs=pl.BlockSpec((tm, d), lambda i: (i, 0)),
        out_shape=jax.ShapeDtypeStruct((m, d), F32),
        compiler_params=_cparams(("parallel",)),
        name="combine",
    )(y, y, h, meta, g)


def kernel(x, mem, norm_mix_g, w_in, w_gla_alpha_up, b_gla_alpha, gla_out_norm_g, hgrn_lb_logits, hgrn_out_norm_g, w_mix_out, norm_xattn_g, norm_mem_g, w_xattn_q, w_xattn_kv, w_xattn_out, norm_ffn_g, w_router_group, b_router_group, w_router_expert, b_router_expert, w_expert_gate, w_expert_up, w_expert_down, norm_final_g):
    batch, seq, d = x.shape
    n_mem = mem.shape[1]
    m = batch * seq
    depth = norm_mix_g.shape[0]
    t = _tiles(m, seq)
    h = x.reshape(m, d)
    lb_all = jnp.cumsum(jax.nn.softmax(hgrn_lb_logits.astype(F32), axis=0), axis=0)
    gla_cols = 2 * GLA_HEADS * GLA_DK + 2 * GLA_HEADS * GLA_DV
    lr_rank = w_gla_alpha_up.shape[1]

    for l in range(depth):
        w_t = jnp.swapaxes(w_in[l], 0, 1)
        w_lr_t = jnp.pad(w_t[gla_cols:gla_cols + lr_rank], ((0, LANES - lr_rank), (0, 0))).astype(BF16)
        w_up = jnp.pad(w_gla_alpha_up[l], ((0, LANES - lr_rank), (0, 0))).astype(BF16)
        a_mix, la = _norm_gate(h, norm_mix_g[l][None, :], w_lr_t, w_up, b_gla_alpha[l][None, :], tm=t.norm_gate_rows)
        proj = _inproj(a_mix, w_t, tm=t.inproj_rows, tn=t.inproj_cols, skip_from=gla_cols, skip=lr_rank)
        lb = lb_all[l]
        lbc = jnp.zeros((8, lb.shape[0]), F32).at[0].set(jnp.log(lb)).at[1].set(jnp.log1p(-lb)).at[2].set(1.0 - lb)
        proj3 = proj.reshape(batch, seq, -1)
        o_gla = _gla(proj3, la.reshape(batch, seq, -1), gla_out_norm_g[l][None, :], tb=t.mixer_rows, hp=t.gla_heads)
        o_h = _hgrn(proj3, lbc, hgrn_out_norm_g[l][None, :], tb=t.mixer_rows, hp=t.hgrn_heads)
        o_gla, o_h = o_gla.reshape(m, -1), o_h.reshape(m, -1)
        h = _mm_res([o_gla, o_h], w_mix_out[l], h, tm=t.square_rows, tn=d, name="mix_out")

        kv = _norm_mm(mem.reshape(batch * n_mem, d), norm_mem_g[l][None, :], w_xattn_kv[l],
                      tm=batch * n_mem, tn=t.mem_kv_cols, out_dtype=BF16, name="mem_kv")
        q = _norm_mm(h, norm_xattn_g[l][None, :], w_xattn_q[l], tm=t.square_rows, tn=d, out_dtype=BF16,
                     name="xattn_q")
        o = _xattn(q, kv, batch=batch, seq=seq, n_mem=n_mem, tq=t.xattn_rows)
        h = _mm_res([o], w_xattn_out[l], h, tm=t.square_rows, tn=d, name="xattn_out")

        n_logits = N_GROUPS + N_EXPERTS
        wr = jnp.pad(jnp.concatenate([w_router_group[l], w_router_expert[l]], axis=1).T,
                     ((0, ROUTER_ROWS - n_logits), (0, 0)))
        br = jnp.pad(jnp.concatenate([b_router_group[l], b_router_expert[l]]), (0, ROUTER_ROWS - n_logits))
        a, meta, cnt = _router(h, norm_ffn_g[l][None, :], wr, jnp.broadcast_to(br[:, None], (ROUTER_ROWS, LANES)),
                               tm=t.router_rows)

        e_idx = meta[:, 0:2].astype(jnp.int32)
        rank = meta[:, 2:4].astype(jnp.int32)
        counts = cnt[N_GROUPS:N_GROUPS + N_EXPERTS, 0].astype(jnp.int32)
        padded = ((counts + PAD_UNIT - 1) // PAD_UNIT) * PAD_UNIT
        pad_end = jnp.cumsum(padded)
        pad_start = pad_end - padded
        n_rows = 2 * m + N_EXPERTS * PAD_UNIT + PAD_UNIT
        n_units = (pad_end[-1:] // PAD_UNIT).astype(jnp.int32)
        n_full = padded // EXPERT_BLOCK
        n_blk_e = n_full + (padded % EXPERT_BLOCK) // PAD_UNIT
        blk_end = jnp.cumsum(n_blk_e)
        blk_start = blk_end - n_blk_e
        n_blocks = (2 * m) // EXPERT_BLOCK + N_EXPERTS
        n_blk = blk_end[-1:].astype(jnp.int32)
        blk = jnp.arange(n_blocks, dtype=jnp.int32)
        block_e = jnp.minimum(jnp.sum((blk_end[None, :] <= blk[:, None]).astype(jnp.int32), axis=1), N_EXPERTS - 1)
        blk_local = blk - blk_start[block_e]
        block_off = (pad_start[block_e] + EXPERT_BLOCK * blk_local).astype(jnp.int32)
        block_off = jnp.where(blk < n_blk[0], block_off, 0)
        block_half = (blk_local >= n_full[block_e]).astype(jnp.int32)
        after = blk_end[block_e]
        next_e = jnp.where(after < n_blk[0], block_e[jnp.minimum(after, n_blocks - 1)], -1).astype(jnp.int32)

        experts = jnp.arange(N_EXPERTS, dtype=jnp.int32)
        dest = [jnp.sum(jnp.where(e_idx[:, k, None] == experts[None, :], pad_start[None, :], 0), axis=1) + rank[:, k]
                for k in range(2)]
        xs, slot = _dispatch(dest[0], dest[1], pad_start + counts, padded - counts, n_units, a, n_rows=n_rows,
                             tm=t.dispatch_rows)
        y = _experts(slot, block_e, next_e, block_off, block_half, n_blk, xs, w_expert_gate[l], w_expert_up[l],
                     w_expert_down[l], rb=EXPERT_BLOCK, n_tok=m)
        last = l == depth - 1
        g_fin = norm_final_g[None, :] if last else jnp.ones((1, d), F32)
        h = _combine(y, h, meta, g_fin, tm=t.combine_rows)
        assert last, "the combine kernel fuses the final rmsnorm; deeper stacks need an un-normalised variant"

    return h.reshape(batch, seq, d)
```

```python
import functools
from typing import NamedTuple

import jax
import jax.numpy as jnp
from jax import lax
from jax.experimental import pallas as pl
from jax.experimental.pallas import tpu as pltpu

F32 = jnp.float32
BF16 = jnp.bfloat16

EPS = 1e-6
CHUNK = 64
LANES = 128
GLA_HEADS, GLA_DK, GLA_DV = 4, 128, 256
HGRN_HEADS, HGRN_DH = 8, 128
XA_HEADS, XA_DH = 4, 512
N_GROUPS, EXPERTS_PER_GROUP, N_EXPERTS = 4, 8, 32
EXPERT_BLOCK = 256
PAD_UNIT = 128
WEIGHT_CHUNKS = 8
WEIGHT_DMA_PRIORITY = 1
VMEM_LIMIT = 56 * 1024 * 1024
EXPERT_VMEM_LIMIT = 60 * 1024 * 1024


class Tiles(NamedTuple):
    norm_gate_rows: int
    inproj_rows: int
    inproj_cols: int
    mixer_rows: int
    gla_heads: int
    hgrn_heads: int
    square_rows: int
    mem_kv_cols: int
    xattn_rows: int
    router_rows: int
    dispatch_rows: int
    combine_rows: int


def _tiles(m, seq):
    return Tiles(norm_gate_rows=min(m, 1024), inproj_rows=min(m, 4096), inproj_cols=512, mixer_rows=min(seq, 256),
                 gla_heads=GLA_HEADS, hgrn_heads=HGRN_HEADS, square_rows=min(m, 512), mem_kv_cols=1024,
                 xattn_rows=min(seq, 512), router_rows=min(m, 512), dispatch_rows=min(m, 256),
                 combine_rows=min(m, 512))


def _cparams(sem, vmem=VMEM_LIMIT):
    return pltpu.CompilerParams(dimension_semantics=sem, vmem_limit_bytes=vmem)


def _log1pexp_neg(t):
    return jnp.log(1.0 + jnp.exp(-t))


def _log_sigmoid(z):
    return jnp.minimum(z, 0.0) - _log1pexp_neg(jnp.abs(z))


def _sigmoid(z):
    return 1.0 / (1.0 + jnp.exp(-z))


def _rms(x, g):
    return x * lax.rsqrt(jnp.mean(x * x, axis=-1, keepdims=True) + EPS) * g


def _nt_dot(x, y):
    return lax.dot_general(x, y, (((1,), (1,)), ((), ())), preferred_element_type=F32)


NORM_ROWS = 256


def _for_row_chunks(n_rows, fn):
    step = min(NORM_ROWS, n_rows)

    def body(ci, carry):
        fn(pl.ds(pl.multiple_of(ci * step, step), step))
        return carry

    lax.fori_loop(0, n_rows // step, body, 0)


def _norm_gate_kernel(x_ref, g_ref, wlr_ref, wup_ref, bal_ref, a_ref, la_ref):
    a = _rms(x_ref[...], g_ref[...]).astype(BF16)
    a_ref[...] = a
    lr = _nt_dot(a, wlr_ref[...])
    z = jnp.dot(lr.astype(BF16), wup_ref[...], preferred_element_type=F32) + bal_ref[...]
    la_ref[...] = _log_sigmoid(z) * (1.0 / 16.0)


def _norm_gate(x, g, w_lr_t, w_up, b_al, *, tm):
    m, d = x.shape
    nk = w_up.shape[1]
    return pl.pallas_call(
        _norm_gate_kernel,
        grid=(m // tm,),
        in_specs=[
            pl.BlockSpec((tm, d), lambda i: (i, 0)),
            pl.BlockSpec((1, d), lambda i: (0, 0)),
            pl.BlockSpec((LANES, d), lambda i: (0, 0)),
            pl.BlockSpec((LANES, nk), lambda i: (0, 0)),
            pl.BlockSpec((1, nk), lambda i: (0, 0)),
        ],
        out_specs=[pl.BlockSpec((tm, d), lambda i: (i, 0)), pl.BlockSpec((tm, nk), lambda i: (i, 0))],
        out_shape=[jax.ShapeDtypeStruct((m, d), BF16), jax.ShapeDtypeStruct((m, nk), F32)],
        compiler_params=_cparams(("parallel",)),
        name="norm_gate",
    )(x, g, w_lr_t, w_up, b_al)


def _inproj_kernel(a_ref, w_ref, proj_ref):
    proj_ref[...] = _nt_dot(a_ref[...], w_ref[...].astype(BF16))


def _inproj(a, w_t, *, tm, tn, skip_from, skip):
    m, d = a.shape
    n = w_t.shape[0] - skip
    first_after = skip_from // tn

    def w_rows(i, j):
        return (pl.multiple_of(j * tn + jnp.where(j >= first_after, skip, 0), 8), 0)

    return pl.pallas_call(
        _inproj_kernel,
        grid=(m // tm, n // tn),
        in_specs=[
            pl.BlockSpec((tm, d), lambda i, j: (i, 0), pipeline_mode=pl.Buffered(1)),
            pl.BlockSpec((pl.Element(tn), pl.Element(d)), w_rows),
        ],
        out_specs=pl.BlockSpec((tm, tn), lambda i, j: (i, j)),
        out_shape=jax.ShapeDtypeStruct((m, n), F32),
        compiler_params=_cparams(("parallel", "arbitrary")),
        name="inproj",
    )(a, w_t)


def _weight_spec(k, n, tn):
    if tn == n:
        return pl.BlockSpec((k, n), lambda i, j: (0, 0), pipeline_mode=pl.Buffered(1))
    return pl.BlockSpec((k, tn), lambda i, j: (0, j))


def _cast_weight(w_ref, wb_ref, resident):
    if resident:
        @pl.when((pl.program_id(0) == 0) & (pl.program_id(1) == 0))
        def _():
            wb_ref[...] = w_ref[...].astype(BF16)
    else:
        wb_ref[...] = w_ref[...].astype(BF16)


def _norm_mm_kernel(x_ref, g_ref, w_ref, o_ref, a_ref, wb_ref, *, resident):
    @pl.when(pl.program_id(1) == 0)
    def _():
        def rows_fn(rows):
            a_ref[rows, :] = _rms(x_ref[rows, :], g_ref[...]).astype(BF16)

        _for_row_chunks(x_ref.shape[0], rows_fn)

    _cast_weight(w_ref, wb_ref, resident)
    o_ref[...] = jnp.dot(a_ref[...], wb_ref[...], preferred_element_type=F32).astype(o_ref.dtype)


def _norm_mm(x, g, w, *, tm, tn, out_dtype, name):
    m, d = x.shape
    n = w.shape[1]
    return pl.pallas_call(
        functools.partial(_norm_mm_kernel, resident=tn == n),
        grid=(m // tm, n // tn),
        in_specs=[
            pl.BlockSpec((tm, d), lambda i, j: (i, 0)),
            pl.BlockSpec((1, d), lambda i, j: (0, 0)),
            _weight_spec(d, n, tn),
        ],
        out_specs=pl.BlockSpec((tm, tn), lambda i, j: (i, j)),
        out_shape=jax.ShapeDtypeStruct((m, n), out_dtype),
        scratch_shapes=[pltpu.VMEM((tm, d), BF16), pltpu.VMEM((d, tn), BF16)],
        compiler_params=_cparams(("arbitrary", "arbitrary")),
        name=name,
    )(x, g, w)


def _mm_res_kernel(*refs, n_lhs, resident):
    lhs = refs[:n_lhs]
    w_ref, res_ref, o_ref, wb_ref = refs[n_lhs:]
    _cast_weight(w_ref, wb_ref, resident)
    acc = res_ref[...]
    k0 = 0
    for l_ref in lhs:
        kp = l_ref.shape[1]
        acc = acc + jnp.dot(l_ref[...], wb_ref[k0:k0 + kp, :], preferred_element_type=F32)
        k0 += kp
    o_ref[...] = acc


def _mm_res(lhs_parts, w, res, *, tm, tn, name):
    m, n = res.shape
    k = w.shape[0]
    n_lhs = len(lhs_parts)
    in_specs = [pl.BlockSpec((tm, p.shape[1]), lambda i, j: (i, 0)) for p in lhs_parts]
    in_specs += [_weight_spec(k, n, tn), pl.BlockSpec((tm, tn), lambda i, j: (i, j))]
    return pl.pallas_call(
        functools.partial(_mm_res_kernel, n_lhs=n_lhs, resident=tn == n),
        grid=(m // tm, n // tn),
        in_specs=in_specs,
        out_specs=pl.BlockSpec((tm, tn), lambda i, j: (i, j)),
        out_shape=jax.ShapeDtypeStruct((m, n), F32),
        scratch_shapes=[pltpu.VMEM((k, tn), BF16)],
        compiler_params=_cparams(("arbitrary", "arbitrary")),
        name=name,
    )(*lhs_parts, w, res)


LEVELS = (32, 16, 8, 4, 2, 1)
LOG2E = 1.4426950408889634


def _split3_bf16(x):
    def top(v):
        bits = lax.bitcast_convert_type(v, jnp.uint32) & jnp.uint32(0xFFFF0000)
        return lax.bitcast_convert_type(bits, F32)
    hi = top(x)
    r1 = x - hi
    mid = top(r1)
    lo = r1 - mid
    return hi.astype(BF16), mid.astype(BF16), lo.astype(BF16)


def _mix_chunks(qs, ks, vs, las, st_refs):
    heads = range(len(qs))
    c, dk = qs[0].shape
    row = lax.broadcasted_iota(jnp.int32, (c, c), 0)
    col = lax.broadcasted_iota(jnp.int32, (c, c), 1)
    rowk = lax.broadcasted_iota(jnp.int32, (c, dk), 0)
    xor = jnp.bitwise_xor(row, col)

    las = [la * LOG2E for la in las]

    tri = jnp.where(col <= row, 1.0, 0.0).astype(BF16)
    b3 = [jnp.dot(tri, jnp.concatenate(_split3_bf16(la), axis=1), preferred_element_type=F32) for la in las]
    bs = [(t[:, :dk] + t[:, dk:2 * dk]) + t[:, 2 * dk:] for t in b3]

    def neg_dist(w, b, la):
        if w >= 4:
            parts = [jnp.broadcast_to(b[base + w - 1:base + w, :], (2 * w, dk)) for base in range(0, c, 2 * w)]
            m = parts[0] if len(parts) == 1 else jnp.concatenate(parts, axis=0)
            return -jnp.abs(b - m)
        if w == 2:
            r4 = jnp.bitwise_and(rowk, 3)
            nxt = pltpu.roll(la, c - 1, 0)
            prv = pltpu.roll(la, 1, 0)
            return jnp.where(r4 == 0, nxt, jnp.where(r4 == 1, 0.0, jnp.where(r4 == 2, la, la + prv)))
        return jnp.where(jnp.bitwise_and(rowk, 1) == 1, la, 0.0)

    acc = [_nt_dot(qs[h].astype(BF16), ks[h].astype(BF16)) for h in heads]
    for w in reversed(LEVELS):
        upper = jnp.bitwise_and(rowk, w) != 0
        xs = [(jnp.where(upper, qs[h], ks[h]) * jnp.exp2(neg_dist(w, bs[h], las[h]))).astype(BF16) for h in heads]
        gs = [_nt_dot(x, x) for x in xs]
        acc = [jnp.where(xor >= w, gs[h], acc[h]) for h in heads]
    ab = [jnp.where(col <= row, a, 0.0).astype(BF16) for a in acc]

    sts = [st_refs[h][...] for h in heads]
    qx = [(qs[h] * jnp.exp2(bs[h])).astype(BF16) for h in heads]
    b_last = [b[c - 1:c, :] for b in bs]
    kx = [(ks[h] * jnp.exp2(b_last[h] - bs[h])).astype(BF16) for h in heads]
    vb = [v.astype(BF16) for v in vs]
    outs = [jnp.dot(ab[h], vb[h], preferred_element_type=F32) + _nt_dot(qx[h], sts[h].astype(BF16)) for h in heads]
    for h in heads:
        st_refs[h][...] = jnp.exp2(b_last[h]) * sts[h] + lax.dot_general(
            vb[h], kx[h], (((0,), (0,)), ((), ())), preferred_element_type=F32)
    return outs


def _gated_norm(o, g, gn):
    return _rms(o, gn) * (g * _sigmoid(g))


def _gla_kernel(q_ref, k_ref, v_ref, g_ref, la_ref, gn_ref, o_ref, st_ref, *, n_chunk):
    @pl.when(pl.program_id(1) == 0)
    def _():
        st_ref[...] = jnp.zeros_like(st_ref)

    dk, dv = GLA_DK, GLA_DV
    n_b = q_ref.shape[0]
    n_h = st_ref.shape[0] // n_b
    streams = [(b, h) for b in range(n_b) for h in range(n_h)]
    kc = [slice(h * dk, (h + 1) * dk) for h in range(n_h)]
    vc = [slice(h * dv, (h + 1) * dv) for h in range(n_h)]

    def body(ci, carry):
        rows = pl.ds(pl.multiple_of(ci * CHUNK, CHUNK), CHUNK)
        outs = _mix_chunks([q_ref[b, rows, kc[h]] * (dk ** -0.5) for b, h in streams],
                           [k_ref[b, rows, kc[h]] for b, h in streams], [v_ref[b, rows, vc[h]] for b, h in streams],
                           [la_ref[b, rows, kc[h]] for b, h in streams],
                           [st_ref.at[n] for n in range(len(streams))])
        for n, (b, h) in enumerate(streams):
            o_ref[b, rows, vc[h]] = _gated_norm(outs[n], g_ref[b, rows, vc[h]], gn_ref[...]).astype(o_ref.dtype)
        return carry

    lax.fori_loop(0, n_chunk, body, 0)


def _gla(proj, la, gn, *, tb, hp):
    batch, seq, _ = proj.shape
    wk, wv = hp * GLA_DK, hp * GLA_DV
    n_grp = GLA_HEADS // hp
    spec = lambda width, first: pl.BlockSpec((batch, tb, width), lambda p, t: (0, t, first + p))
    return pl.pallas_call(
        functools.partial(_gla_kernel, n_chunk=tb // CHUNK),
        grid=(n_grp, seq // tb),
        in_specs=[spec(wk, 0), spec(wk, n_grp), spec(wv, n_grp), spec(wv, 2 * n_grp), spec(wk, 0),
                  pl.BlockSpec((1, GLA_DV), lambda p, t: (0, 0))],
        out_specs=spec(wv, 0),
        out_shape=jax.ShapeDtypeStruct((batch, seq, GLA_HEADS * GLA_DV), BF16),
        scratch_shapes=[pltpu.VMEM((batch * hp, GLA_DV, GLA_DK), F32)],
        compiler_params=_cparams(("parallel", "arbitrary")),
        name="gla",
    )(proj, proj, proj, proj, la, gn)


def _hgrn_kernel(q_ref, f_ref, i_ref, g_ref, lb_ref, gn_ref, o_ref, st_ref, *, n_chunk):
    @pl.when(pl.program_id(1) == 0)
    def _():
        st_ref[...] = jnp.zeros_like(st_ref)

    dh = HGRN_DH
    n_b = q_ref.shape[0]
    n_h = st_ref.shape[0] // n_b
    streams = [(b, h) for b in range(n_b) for h in range(n_h)]
    hc = [slice(h * dh, (h + 1) * dh) for h in range(n_h)]

    def body(ci, carry):
        rows = pl.ds(pl.multiple_of(ci * CHUNK, CHUNK), CHUNK)
        qs, ks, las = [], [], []
        for b, h in streams:
            log_lb, log_1mlb, one_m_lb = lb_ref[0:1, hc[h]], lb_ref[1:2, hc[h]], lb_ref[2:3, hc[h]]
            hq = q_ref[b, rows, hc[h]]
            z = f_ref[b, rows, hc[h]]
            x1 = log_1mlb + _log_sigmoid(z)
            las.append(jnp.maximum(log_lb, x1) + _log1pexp_neg(jnp.abs(log_lb - x1)))
            qs.append(hq * _sigmoid(hq))
            ks.append(one_m_lb * _sigmoid(-z))
        outs = _mix_chunks(qs, ks, [i_ref[b, rows, hc[h]] for b, h in streams], las,
                           [st_ref.at[n] for n in range(len(streams))])
        for n, (b, h) in enumerate(streams):
            o_ref[b, rows, hc[h]] = _gated_norm(outs[n], g_ref[b, rows, hc[h]], gn_ref[...]).astype(o_ref.dtype)
        return carry

    lax.fori_loop(0, n_chunk, body, 0)


def _hgrn(proj, lbc, gn, *, tb, hp):
    batch, seq, _ = proj.shape
    width = hp * HGRN_DH
    n_grp = HGRN_HEADS // hp
    first = 3072 // width
    spec = lambda seg: pl.BlockSpec((batch, tb, width), lambda p, t: (0, t, first + seg * n_grp + p))
    return pl.pallas_call(
        functools.partial(_hgrn_kernel, n_chunk=tb // CHUNK),
        grid=(n_grp, seq // tb),
        in_specs=[spec(0), spec(1), spec(2), spec(3),
                  pl.BlockSpec((8, width), lambda p, t: (0, p)),
                  pl.BlockSpec((1, HGRN_DH), lambda p, t: (0, 0))],
        out_specs=pl.BlockSpec((batch, tb, width), lambda p, t: (0, t, p)),
        out_shape=jax.ShapeDtypeStruct((batch, seq, HGRN_HEADS * HGRN_DH), BF16),
        scratch_shapes=[pltpu.VMEM((batch * hp, HGRN_DH, HGRN_DH), F32)],
        compiler_params=_cparams(("parallel", "arbitrary")),
        name="hgrn",
    )(proj, proj, proj, proj, lbc, gn)


def _xattn_kernel(x_ref, g_ref, w_ref, k_ref, v_ref, o_ref, wb_ref):
    @pl.when(pl.program_id(0) == 0)
    def _():
        _for_row_chunks(w_ref.shape[0], lambda rows: wb_ref.__setitem__((rows, slice(None)),
                                                                        w_ref[rows, :].astype(BF16)))

    a = _rms(x_ref[...], g_ref[...]).astype(BF16)
    q = jnp.dot(a, wb_ref[...], preferred_element_type=F32).astype(BF16)
    cols = [slice(h * XA_DH, (h + 1) * XA_DH) for h in range(XA_HEADS)]
    s = [_nt_dot(q[:, c], k_ref[:, c]) * (XA_DH ** -0.5) for c in cols]
    p = [jnp.exp(x - jnp.max(x, axis=-1, keepdims=True)) for x in s]
    p = [(x / jnp.sum(x, axis=-1, keepdims=True)).astype(BF16) for x in p]
    for c, x in zip(cols, p):
        o_ref[:, c] = jnp.dot(x, v_ref[:, c], preferred_element_type=F32).astype(o_ref.dtype)


def _xattn(x, g, w_q, kv, *, seq, n_mem, tq):
    m, d = x.shape
    nt = seq // tq
    return pl.pallas_call(
        _xattn_kernel,
        grid=(m // tq,),
        in_specs=[
            pl.BlockSpec((tq, d), lambda i: (i, 0)),
            pl.BlockSpec((1, d), lambda i: (0, 0)),
            pl.BlockSpec((d, d), lambda i: (0, 0), pipeline_mode=pl.Buffered(1)),
            pl.BlockSpec((n_mem, d), lambda i: (i // nt, 0)),
            pl.BlockSpec((n_mem, d), lambda i: (i // nt, 1)),
        ],
        out_specs=pl.BlockSpec((tq, d), lambda i: (i, 0)),
        out_shape=jax.ShapeDtypeStruct((m, d), BF16),
        scratch_shapes=[pltpu.VMEM((d, d), BF16)],
        compiler_params=_cparams(("arbitrary",)),
        name="xattn",
    )(x, g, w_q, kv, kv)


ROUTER_ROWS = 40


def _router_kernel(h_ref, g_ref, wr_ref, br_ref, a_ref, meta_ref, cnt_ref, carry_ref):
    tm = h_ref.shape[0]
    nr = wr_ref.shape[0]

    @pl.when(pl.program_id(0) == 0)
    def _():
        carry_ref[...] = jnp.zeros_like(carry_ref)

    a = _rms(h_ref[...], g_ref[...])
    a_ref[...] = a.astype(BF16).reshape(a_ref.shape)
    a_hi = a.astype(BF16)
    a_lo = (a - a_hi.astype(F32)).astype(BF16)
    w = wr_ref[...]
    w_hi = w.astype(BF16)
    w_lo = (w - w_hi.astype(F32)).astype(BF16)
    logits = (_nt_dot(w_hi, a_hi) + _nt_dot(w_hi, a_lo) + _nt_dot(w_lo, a_hi)) + br_ref[:, 0:1]
    row = lax.broadcasted_iota(jnp.int32, (nr, tm), 0)
    row_f = row.astype(F32)
    neg = -jnp.inf

    def first_max(x):
        v = jnp.max(x, axis=0, keepdims=True)
        return v, jnp.min(jnp.where(x == v, row_f, float(nr)), axis=0, keepdims=True)

    gl = jnp.where(row < N_GROUPS, logits, neg)
    gmax, gidx = first_max(gl)
    p_group = 1.0 / jnp.sum(jnp.exp(gl - gmax), axis=0, keepdims=True)
    lo = float(N_GROUPS) + gidx * float(EXPERTS_PER_GROUP)
    el = jnp.where((row_f >= lo) & (row_f < lo + float(EXPERTS_PER_GROUP)), logits, neg)
    v1, i1 = first_max(el)
    v2, i2 = first_max(jnp.where(row_f == i1, neg, el))
    t = jnp.exp(v2 - v1)
    g1 = p_group / (1.0 + t)
    g2 = p_group * t / (1.0 + t)

    hit1 = row_f == i1
    hit2 = row_f == i2
    onehot = jnp.where(hit1 | hit2, 1.0, 0.0)
    src = lax.broadcasted_iota(jnp.int32, (tm, tm), 0)
    dst = lax.broadcasted_iota(jnp.int32, (tm, tm), 1)
    earlier = jnp.where(src < dst, 1.0, 0.0).astype(BF16)
    before = jnp.dot(onehot.astype(BF16), earlier, preferred_element_type=F32) + carry_ref[:, 0:1]
    r1 = jnp.sum(jnp.where(hit1, before, 0.0), axis=0, keepdims=True)
    r2 = jnp.sum(jnp.where(hit2, before, 0.0), axis=0, keepdims=True)
    carry_ref[...] = carry_ref[...] + jnp.sum(onehot, axis=1, keepdims=True)
    cnt_ref[...] = carry_ref[...]

    out_row = lax.broadcasted_iota(jnp.int32, (LANES, tm), 0)
    meta_t = jnp.zeros((LANES, tm), F32)
    for idx, val in enumerate((i1 - float(N_GROUPS), i2 - float(N_GROUPS), r1, r2, g1, g2)):
        meta_t = jnp.where(out_row == idx, val, meta_t)
    meta_ref[...] = meta_t.T


def _router(h, g, wr, br, *, tm):
    m, d = h.shape
    slab = d // LANES
    return pl.pallas_call(
        _router_kernel,
        grid=(m // tm,),
        in_specs=[
            pl.BlockSpec((tm, d), lambda i: (i, 0)),
            pl.BlockSpec((1, d), lambda i: (0, 0)),
            pl.BlockSpec((ROUTER_ROWS, d), lambda i: (0, 0)),
            pl.BlockSpec((ROUTER_ROWS, LANES), lambda i: (0, 0)),
        ],
        out_specs=[
            pl.BlockSpec((tm, slab, LANES), lambda i: (i, 0, 0)),
            pl.BlockSpec((tm, LANES), lambda i: (i, 0)),
            pl.BlockSpec((ROUTER_ROWS, LANES), lambda i: (0, 0)),
        ],
        out_shape=[jax.ShapeDtypeStruct((m, slab, LANES), BF16), jax.ShapeDtypeStruct((m, LANES), F32),
                   jax.ShapeDtypeStruct((ROUTER_ROWS, LANES), F32)],
        scratch_shapes=[pltpu.VMEM((ROUTER_ROWS, LANES), F32)],
        compiler_params=_cparams(("arbitrary",)),
        name="router",
    )(h, g, wr, br)


PAD_PIECES = tuple(PAD_UNIT >> (b + 1) for b in range(PAD_UNIT.bit_length() - 1))


def _dispatch_kernel(d1_ref, d2_ref, fill_ref, npad_ref, nb_ref, a_ref, xs_hbm, slot_ref, st0, st1, zbuf, sem, zsem):
    i = pl.program_id(0)
    n_steps = pl.num_programs(0)
    tm = a_ref.shape[0]
    bufs = (st0, st1)
    rb = zbuf.shape[0]
    n_blocks = xs_hbm.shape[0] // rb
    min_blocks = (2 * tm * n_steps) // rb

    def zero_copies():
        out = []
        for e in range(N_EXPERTS):
            p = npad_ref[e]
            for piece in PAD_PIECES:
                out.append((p & piece != 0, pltpu.make_async_copy(
                    zbuf.at[pl.ds(0, piece)], xs_hbm.at[pl.ds(fill_ref[e] + (p & -(2 * piece)), piece)], zsem.at[0])))
        for b in range(min_blocks, n_blocks):
            out.append((b >= nb_ref[0], pltpu.make_async_copy(zbuf, xs_hbm.at[pl.ds(b * rb, rb)], zsem.at[0])))
        return out

    def wait_rows(s):
        for _ in range(2):
            pltpu.make_async_copy(bufs[s], xs_hbm.at[pl.ds(0, tm)], sem.at[s]).wait()

    @pl.when(i == 0)
    def _():
        def mark_unused(j, carry):
            slot_ref[j] = -1
            return carry
        for e in range(N_EXPERTS):
            lax.fori_loop(fill_ref[e], fill_ref[e] + npad_ref[e], mark_unused, 0)
        lax.fori_loop(nb_ref[0] * rb, slot_ref.shape[0], mark_unused, 0)
        zbuf[...] = jnp.zeros_like(zbuf)
        for cond, cp in zero_copies():
            @pl.when(cond)
            def _():
                cp.start()

    for s in (0, 1):
        @pl.when(lax.rem(i, 2) == s)
        def _():
            @pl.when(i >= 2)
            def _():
                wait_rows(s)

            bufs[s][...] = a_ref[...]
            for r in range(tm):
                tok = i * tm + r
                d1 = d1_ref[tok]
                d2 = d2_ref[tok]
                slot_ref[d1] = 2 * tok
                slot_ref[d2] = 2 * tok + 1
                pltpu.make_async_copy(bufs[s].at[r], xs_hbm.at[d1], sem.at[s]).start(priority=0)
                pltpu.make_async_copy(bufs[s].at[r], xs_hbm.at[d2], sem.at[s]).start(priority=1)

    @pl.when(i == n_steps - 1)
    def _():
        for s in (0, 1):
            @pl.when((lax.rem(n_steps - 1, 2) == s) | ((n_steps >= 2) & (lax.rem(n_steps, 2) == s)))
            def _():
                wait_rows(s)
        for cond, cp in zero_copies():
            @pl.when(cond)
            def _():
                cp.wait()


def _dispatch(dest1, dest2, fill, npad, n_blk, a, *, n_rows, tm):
    m, slab, _ = a.shape
    grid_spec = pltpu.PrefetchScalarGridSpec(
        num_scalar_prefetch=5,
        grid=(m // tm,),
        in_specs=[pl.BlockSpec((tm, slab, LANES), lambda i, *_: (i, 0, 0))],
        out_specs=[pl.BlockSpec(memory_space=pl.ANY), pl.BlockSpec(memory_space=pltpu.SMEM)],
        scratch_shapes=[pltpu.VMEM((tm, slab, LANES), a.dtype), pltpu.VMEM((tm, slab, LANES), a.dtype),
                        pltpu.VMEM((PAD_UNIT, slab, LANES), a.dtype),
                        pltpu.SemaphoreType.DMA((2,)), pltpu.SemaphoreType.DMA((1,))],
    )
    return pl.pallas_call(
        _dispatch_kernel,
        grid_spec=grid_spec,
        out_shape=[jax.ShapeDtypeStruct((n_rows, slab, LANES), a.dtype), jax.ShapeDtypeStruct((n_rows,), jnp.int32)],
        compiler_params=_cparams(("arbitrary",)),
        name="dispatch",
    )(dest1, dest2, fill, npad, n_blk, a)


def _weight_copies(w_hbms, e, wst_refs, wsem):
    copies = []
    for w_hbm, wst_ref in zip(w_hbms, wst_refs):
        rows_per = w_hbm.shape[1] // WEIGHT_CHUNKS
        for c in range(WEIGHT_CHUNKS):
            rows = pl.ds(c * rows_per, rows_per)
            copies.append(pltpu.make_async_copy(w_hbm.at[e, rows, :], wst_ref.at[rows, :], wsem.at[0]))
    return copies


def _load_expert_weights(i, be_ref, nx_ref, w_hbms, wst_refs, wb_refs, wsem):
    def start(e):
        for cp in _weight_copies(w_hbms, e, wst_refs, wsem):
            cp.start(priority=WEIGHT_DMA_PRIORITY)

    @pl.when(i == 0)
    def _():
        start(be_ref[0])

    @pl.when((i == 0) | (be_ref[i] != be_ref[jnp.maximum(i - 1, 0)]))
    def _():
        for cp in _weight_copies(w_hbms, be_ref[i], wst_refs, wsem):
            cp.wait()
        for wst_ref, wb_ref in zip(wst_refs, wb_refs):
            _for_row_chunks(wst_ref.shape[0], lambda rows: wb_ref.__setitem__((rows, slice(None)),
                                                                              wst_ref[rows, :].astype(BF16)))

        @pl.when(nx_ref[i] >= 0)
        def _():
            start(nx_ref[i])


def _expert_kernel(slot_ref, be_ref, nx_ref, off_ref, half_ref, nb_ref, x_ref, wg_hbm, wu_hbm, wd_hbm, y_hbm, wsg, wsu, wsd,
                   wgb, wub, wdb, ys0, ys1, sem, wsem, *, n_tok):
    i = pl.program_id(0)
    n_steps = pl.num_programs(0)
    nb = nb_ref[0]
    rb = x_ref.shape[0]
    bufs = (ys0, ys1)

    def wait_block(buf, s):
        pltpu.make_async_copy(buf, y_hbm.at[pl.ds(0, rb)], sem.at[s]).wait()

    def scatter_previous(s):
        o = 1 - s
        prev = jnp.maximum(i - 1, 0)
        used = jnp.where(i >= 1, rb - half_ref[prev] * (rb // 2), 0)
        for r in range(rb):
            v = jnp.where(r < used, slot_ref[off_ref[prev] + r], -1)
            dst = jnp.where(v >= 0, (v & 1) * n_tok + (v >> 1), 2 * n_tok + o * rb + r)
            pltpu.make_async_copy(bufs[o].at[r], y_hbm.at[dst], sem.at[o]).start()

    @pl.when(i == 0)
    def _():
        ys0[...] = jnp.zeros_like(ys0)
        ys1[...] = jnp.zeros_like(ys1)
        spare = pltpu.make_async_copy(ys0, y_hbm.at[pl.ds(2 * n_tok, rb)], sem.at[0])
        spare.start()
        spare.wait()

    for s in (0, 1):
        for rows in (rb, rb // 2):
            @pl.when((i < nb) & (lax.rem(i, 2) == s) & (half_ref[jnp.minimum(i, n_steps - 2)] == (rows != rb)))
            def _():
                _load_expert_weights(i, be_ref, nx_ref, (wg_hbm, wu_hbm, wd_hbm), (wsg, wsu, wsd), (wgb, wub, wdb),
                                     wsem)

                @pl.when(i >= 1)
                def _():
                    wait_block(bufs[s], s)

                scatter_previous(s)
                x = x_ref[0:rows].reshape(rows, -1)
                hg = jnp.dot(x, wgb[...], preferred_element_type=F32)
                hu = jnp.dot(x, wub[...], preferred_element_type=F32)
                hb = (hg * _sigmoid(hg) * hu).astype(BF16)
                y = jnp.dot(hb, wdb[...], preferred_element_type=F32)
                bufs[s][0:rows] = y.astype(bufs[s].dtype).reshape((rows,) + bufs[s].shape[1:])

        @pl.when((i == nb) & (lax.rem(i, 2) == s))
        def _():
            scatter_previous(s)

    @pl.when(i == n_steps - 1)
    def _():
        for s in (0, 1):
            wait_block(bufs[s], s)


def _experts(slot, block_e, next_e, block_off, block_half, n_blk, xs, wg, wu, wd, *, rb, n_tok):
    _, slab, _ = xs.shape
    d = slab * LANES
    de = wg.shape[2]
    n_blocks = block_e.shape[0]

    def x_rows(i, sl, be, nx, off, hf, nb):
        return (pl.multiple_of(off[jnp.minimum(i, nb[0] - 1)], PAD_UNIT), 0, 0)

    grid_spec = pltpu.PrefetchScalarGridSpec(
        num_scalar_prefetch=6,
        grid=(n_blocks + 1,),
        in_specs=[pl.BlockSpec((pl.Element(rb), pl.Element(slab), pl.Element(LANES)), x_rows)]
        + [pl.BlockSpec(memory_space=pl.ANY)] * 3,
        out_specs=pl.BlockSpec(memory_space=pl.ANY),
        scratch_shapes=[pltpu.VMEM((d, de), F32), pltpu.VMEM((d, de), F32), pltpu.VMEM((de, d), F32),
                        pltpu.VMEM((d, de), BF16), pltpu.VMEM((d, de), BF16), pltpu.VMEM((de, d), BF16),
                        pltpu.VMEM((rb, slab, LANES), BF16), pltpu.VMEM((rb, slab, LANES), BF16),
                        pltpu.SemaphoreType.DMA((2,)), pltpu.SemaphoreType.DMA((1,))],
    )
    return pl.pallas_call(
        functools.partial(_expert_kernel, n_tok=n_tok),
        grid_spec=grid_spec,
        out_shape=jax.ShapeDtypeStruct((2 * n_tok + 2 * rb, slab, LANES), BF16),
        compiler_params=_cparams(("arbitrary",), vmem=EXPERT_VMEM_LIMIT),
        name="experts",
    )(slot, block_e, next_e, block_off, block_half, n_blk, xs, wg, wu, wd)


def _combine_kernel(y0_ref, y1_ref, h_ref, meta_ref, g_ref, o_ref):
    tm, d = h_ref.shape
    meta = meta_ref[...]
    y0 = y0_ref[...].reshape(tm, d).astype(F32)
    y1 = y1_ref[...].reshape(tm, d).astype(F32)
    out = h_ref[...] + meta[:, 4:5] * y0 + meta[:, 5:6] * y1
    o_ref[...] = _rms(out, g_ref[...])


def _combine(y, h, meta, g, *, tm):
    m, d = h.shape
    slab = d // LANES
    nt = m // tm
    return pl.pallas_call(
        _combine_kernel,
        grid=(nt,),
        in_specs=[
            pl.BlockSpec((tm, slab, LANES), lambda i: (i, 0, 0)),
            pl.BlockSpec((tm, slab, LANES), lambda i: (nt + i, 0, 0)),
            pl.BlockSpec((tm, d), lambda i: (i, 0)),
            pl.BlockSpec((tm, LANES), lambda i: (i, 0)),
            pl.BlockSpec((1, d), lambda i: (0, 0)),
        ],
        out_specs=pl.BlockSpec((tm, d), lambda i: (i, 0)),
        out_shape=jax.ShapeDtypeStruct((m, d), F32),
        compiler_params=_cparams(("parallel",)),
        name="combine",
    )(y, y, h, meta, g)


def kernel(x, mem, norm_mix_g, w_in, w_gla_alpha_up, b_gla_alpha, gla_out_norm_g, hgrn_lb_logits, hgrn_out_norm_g, w_mix_out, norm_xattn_g, norm_mem_g, w_xattn_q, w_xattn_kv, w_xattn_out, norm_ffn_g, w_router_group, b_router_group, w_router_expert, b_router_expert, w_expert_gate, w_expert_up, w_expert_down, norm_final_g):
    batch, seq, d = x.shape
    n_mem = mem.shape[1]
    m = batch * seq
    depth = norm_mix_g.shape[0]
    t = _tiles(m, seq)
    h = x.reshape(m, d)
    lb_all = jnp.cumsum(jax.nn.softmax(hgrn_lb_logits.astype(F32), axis=0), axis=0)
    gla_cols = 2 * GLA_HEADS * GLA_DK + 2 * GLA_HEADS * GLA_DV
    lr_rank = w_gla_alpha_up.shape[1]

    for l in range(depth):
        w_t = jnp.swapaxes(w_in[l], 0, 1)
        w_lr_t = jnp.pad(w_t[gla_cols:gla_cols + lr_rank], ((0, LANES - lr_rank), (0, 0))).astype(BF16)
        w_up = jnp.pad(w_gla_alpha_up[l], ((0, LANES - lr_rank), (0, 0))).astype(BF16)
        a_mix, la = _norm_gate(h, norm_mix_g[l][None, :], w_lr_t, w_up, b_gla_alpha[l][None, :], tm=t.norm_gate_rows)
        proj = _inproj(a_mix, w_t, tm=t.inproj_rows, tn=t.inproj_cols, skip_from=gla_cols, skip=lr_rank)
        lb = lb_all[l]
        lbc = jnp.zeros((8, lb.shape[0]), F32).at[0].set(jnp.log(lb)).at[1].set(jnp.log1p(-lb)).at[2].set(1.0 - lb)
        proj3 = proj.reshape(batch, seq, -1)
        o_gla = _gla(proj3, la.reshape(batch, seq, -1), gla_out_norm_g[l][None, :], tb=t.mixer_rows, hp=t.gla_heads)
        o_h = _hgrn(proj3, lbc, hgrn_out_norm_g[l][None, :], tb=t.mixer_rows, hp=t.hgrn_heads)
        o_gla, o_h = o_gla.reshape(m, -1), o_h.reshape(m, -1)
        h = _mm_res([o_gla, o_h], w_mix_out[l], h, tm=t.square_rows, tn=d, name="mix_out")

        kv = _norm_mm(mem.reshape(batch * n_mem, d), norm_mem_g[l][None, :], w_xattn_kv[l],
                      tm=batch * n_mem, tn=t.mem_kv_cols, out_dtype=BF16, name="mem_kv")
        o = _xattn(h, norm_xattn_g[l][None, :], w_xattn_q[l], kv, seq=seq, n_mem=n_mem, tq=t.xattn_rows)
        h = _mm_res([o], w_xattn_out[l], h, tm=t.square_rows, tn=d, name="xattn_out")

        n_logits = N_GROUPS + N_EXPERTS
        wr = jnp.pad(jnp.concatenate([w_router_group[l], w_router_expert[l]], axis=1).T,
                     ((0, ROUTER_ROWS - n_logits), (0, 0)))
        br = jnp.pad(jnp.concatenate([b_router_group[l], b_router_expert[l]]), (0, ROUTER_ROWS - n_logits))
        a, meta, cnt = _router(h, norm_ffn_g[l][None, :], wr, jnp.broadcast_to(br[:, None], (ROUTER_ROWS, LANES)),
                               tm=t.router_rows)

        e_idx = meta[:, 0:2].astype(jnp.int32)
        rank = meta[:, 2:4].astype(jnp.int32)
        counts = cnt[N_GROUPS:N_GROUPS + N_EXPERTS, 0].astype(jnp.int32)
        padded = ((counts + PAD_UNIT - 1) // PAD_UNIT) * PAD_UNIT
        pad_end = jnp.cumsum(padded)
        pad_start = pad_end - padded
        n_rows = 2 * m + N_EXPERTS * PAD_UNIT + PAD_UNIT
        n_units = (pad_end[-1:] // PAD_UNIT).astype(jnp.int32)
        n_full = padded // EXPERT_BLOCK
        n_blk_e = n_full + (padded % EXPERT_BLOCK) // PAD_UNIT
        blk_end = jnp.cumsum(n_blk_e)
        blk_start = blk_end - n_blk_e
        n_blocks = (2 * m) // EXPERT_BLOCK + N_EXPERTS
        n_blk = blk_end[-1:].astype(jnp.int32)
        blk = jnp.arange(n_blocks, dtype=jnp.int32)
        block_e = jnp.minimum(jnp.sum((blk_end[None, :] <= blk[:, None]).astype(jnp.int32), axis=1), N_EXPERTS - 1)
        blk_local = blk - blk_start[block_e]
        block_off = (pad_start[block_e] + EXPERT_BLOCK * blk_local).astype(jnp.int32)
        block_off = jnp.where(blk < n_blk[0], block_off, 0)
        block_half = (blk_local >= n_full[block_e]).astype(jnp.int32)
        after = blk_end[block_e]
        next_e = jnp.where(after < n_blk[0], block_e[jnp.minimum(after, n_blocks - 1)], -1).astype(jnp.int32)

        experts = jnp.arange(N_EXPERTS, dtype=jnp.int32)
        dest = [jnp.sum(jnp.where(e_idx[:, k, None] == experts[None, :], pad_start[None, :], 0), axis=1) + rank[:, k]
                for k in range(2)]
        xs, slot = _dispatch(dest[0], dest[1], pad_start + counts, padded - counts, n_units, a, n_rows=n_rows,
                             tm=t.dispatch_rows)
        y = _experts(slot, block_e, next_e, block_off, block_half, n_blk, xs, w_expert_gate[l], w_expert_up[l],
                     w_expert_down[l], rb=EXPERT_BLOCK, n_tok=m)
        last = l == depth - 1
        g_fin = norm_final_g[None, :] if last else jnp.ones((1, d), F32)
        h = _combine(y, h, meta, g_fin, tm=t.combine_rows)
        assert last, "the combine kernel fuses the final rmsnorm; deeper stacks need an un-normalised variant"

    return h.reshape(batch, seq, d)
```

```python
import functools
from typing import NamedTuple

import jax
import jax.numpy as jnp
from jax import lax
from jax.experimental import pallas as pl
from jax.experimental.pallas import tpu as pltpu

F32 = jnp.float32
BF16 = jnp.bfloat16

EPS = 1e-6
CHUNK = 64
LANES = 128
GLA_HEADS, GLA_DK, GLA_DV = 4, 128, 256
HGRN_HEADS, HGRN_DH = 8, 128
XA_HEADS, XA_DH = 4, 512
N_GROUPS, EXPERTS_PER_GROUP, N_EXPERTS = 4, 8, 32
EXPERT_BLOCK = 256
PAD_UNIT = 128
WEIGHT_CHUNKS = 8
WEIGHT_DMA_PRIORITY = 1
VMEM_LIMIT = 56 * 1024 * 1024
EXPERT_VMEM_LIMIT = 60 * 1024 * 1024


class Tiles(NamedTuple):
    norm_gate_rows: int
    inproj_rows: int
    inproj_cols: int
    mixer_rows: int
    gla_heads: int
    hgrn_heads: int
    square_rows: int
    mem_kv_cols: int
    xattn_rows: int
    router_rows: int
    dispatch_rows: int
    combine_rows: int


def _tiles(m, seq):
    return Tiles(norm_gate_rows=min(m, 1024), inproj_rows=min(m, 4096), inproj_cols=512, mixer_rows=min(seq, 256),
                 gla_heads=GLA_HEADS, hgrn_heads=HGRN_HEADS, square_rows=min(m, 512), mem_kv_cols=1024,
                 xattn_rows=min(seq, 512), router_rows=min(m, 512), dispatch_rows=min(m, 256),
                 combine_rows=min(m, 512))


def _cparams(sem, vmem=VMEM_LIMIT):
    return pltpu.CompilerParams(dimension_semantics=sem, vmem_limit_bytes=vmem)


def _log1pexp_neg(t):
    return jnp.log(1.0 + jnp.exp(-t))


def _log_sigmoid(z):
    return jnp.minimum(z, 0.0) - _log1pexp_neg(jnp.abs(z))


def _sigmoid(z):
    return 1.0 / (1.0 + jnp.exp(-z))


def _rms(x, g):
    return x * lax.rsqrt(jnp.mean(x * x, axis=-1, keepdims=True) + EPS) * g


def _nt_dot(x, y):
    return lax.dot_general(x, y, (((1,), (1,)), ((), ())), preferred_element_type=F32)


NORM_ROWS = 256


def _for_row_chunks(n_rows, fn):
    step = min(NORM_ROWS, n_rows)

    def body(ci, carry):
        fn(pl.ds(pl.multiple_of(ci * step, step), step))
        return carry

    lax.fori_loop(0, n_rows // step, body, 0)


def _norm_gate_kernel(x_ref, g_ref, wlr_ref, wup_ref, bal_ref, a_ref, la_ref):
    a = _rms(x_ref[...], g_ref[...]).astype(BF16)
    a_ref[...] = a
    lr = _nt_dot(a, wlr_ref[...])
    z = jnp.dot(lr.astype(BF16), wup_ref[...], preferred_element_type=F32) + bal_ref[...]
    la_ref[...] = _log_sigmoid(z) * (1.0 / 16.0)


def _norm_gate(x, g, w_lr_t, w_up, b_al, *, tm):
    m, d = x.shape
    nk = w_up.shape[1]
    return pl.pallas_call(
        _norm_gate_kernel,
        grid=(m // tm,),
        in_specs=[
            pl.BlockSpec((tm, d), lambda i: (i, 0)),
            pl.BlockSpec((1, d), lambda i: (0, 0)),
            pl.BlockSpec((LANES, d), lambda i: (0, 0)),
            pl.BlockSpec((LANES, nk), lambda i: (0, 0)),
            pl.BlockSpec((1, nk), lambda i: (0, 0)),
        ],
        out_specs=[pl.BlockSpec((tm, d), lambda i: (i, 0)), pl.BlockSpec((tm, nk), lambda i: (i, 0))],
        out_shape=[jax.ShapeDtypeStruct((m, d), BF16), jax.ShapeDtypeStruct((m, nk), F32)],
        compiler_params=_cparams(("parallel",)),
        name="norm_gate",
    )(x, g, w_lr_t, w_up, b_al)


def _inproj_kernel(a_ref, w_ref, proj_ref):
    proj_ref[...] = _nt_dot(a_ref[...], w_ref[...].astype(BF16))


def _inproj(a, w_t, *, tm, tn, skip_from, skip):
    m, d = a.shape
    n = w_t.shape[0] - skip
    first_after = skip_from // tn

    def w_rows(i, j):
        return (pl.multiple_of(j * tn + jnp.where(j >= first_after, skip, 0), 8), 0)

    return pl.pallas_call(
        _inproj_kernel,
        grid=(m // tm, n // tn),
        in_specs=[
            pl.BlockSpec((tm, d), lambda i, j: (i, 0), pipeline_mode=pl.Buffered(1)),
            pl.BlockSpec((pl.Element(tn), pl.Element(d)), w_rows),
        ],
        out_specs=pl.BlockSpec((tm, tn), lambda i, j: (i, j)),
        out_shape=jax.ShapeDtypeStruct((m, n), F32),
        compiler_params=_cparams(("parallel", "arbitrary")),
        name="inproj",
    )(a, w_t)


def _weight_spec(k, n, tn):
    if tn == n:
        return pl.BlockSpec((k, n), lambda i, j: (0, 0), pipeline_mode=pl.Buffered(1))
    return pl.BlockSpec((k, tn), lambda i, j: (0, j))


def _cast_weight(w_ref, wb_ref, resident):
    if resident:
        @pl.when((pl.program_id(0) == 0) & (pl.program_id(1) == 0))
        def _():
            wb_ref[...] = w_ref[...].astype(BF16)
    else:
        wb_ref[...] = w_ref[...].astype(BF16)


def _norm_mm_kernel(x_ref, g_ref, w_ref, o_ref, a_ref, wb_ref, *, resident):
    @pl.when(pl.program_id(1) == 0)
    def _():
        def rows_fn(rows):
            a_ref[rows, :] = _rms(x_ref[rows, :], g_ref[...]).astype(BF16)

        _for_row_chunks(x_ref.shape[0], rows_fn)

    _cast_weight(w_ref, wb_ref, resident)
    o_ref[...] = jnp.dot(a_ref[...], wb_ref[...], preferred_element_type=F32).astype(o_ref.dtype)


def _norm_mm(x, g, w, *, tm, tn, out_dtype, name):
    m, d = x.shape
    n = w.shape[1]
    return pl.pallas_call(
        functools.partial(_norm_mm_kernel, resident=tn == n),
        grid=(m // tm, n // tn),
        in_specs=[
            pl.BlockSpec((tm, d), lambda i, j: (i, 0)),
            pl.BlockSpec((1, d), lambda i, j: (0, 0)),
            _weight_spec(d, n, tn),
        ],
        out_specs=pl.BlockSpec((tm, tn), lambda i, j: (i, j)),
        out_shape=jax.ShapeDtypeStruct((m, n), out_dtype),
        scratch_shapes=[pltpu.VMEM((tm, d), BF16), pltpu.VMEM((d, tn), BF16)],
        compiler_params=_cparams(("arbitrary", "arbitrary")),
        name=name,
    )(x, g, w)


def _mm_res_kernel(*refs, n_lhs, resident):
    lhs = refs[:n_lhs]
    w_ref, res_ref, o_ref, wb_ref = refs[n_lhs:]
    _cast_weight(w_ref, wb_ref, resident)
    acc = res_ref[...]
    k0 = 0
    for l_ref in lhs:
        kp = l_ref.shape[1]
        acc = acc + jnp.dot(l_ref[...], wb_ref[k0:k0 + kp, :], preferred_element_type=F32)
        k0 += kp
    o_ref[...] = acc


def _mm_res(lhs_parts, w, res, *, tm, tn, name):
    m, n = res.shape
    k = w.shape[0]
    n_lhs = len(lhs_parts)
    in_specs = [pl.BlockSpec((tm, p.shape[1]), lambda i, j: (i, 0)) for p in lhs_parts]
    in_specs += [_weight_spec(k, n, tn), pl.BlockSpec((tm, tn), lambda i, j: (i, j))]
    return pl.pallas_call(
        functools.partial(_mm_res_kernel, n_lhs=n_lhs, resident=tn == n),
        grid=(m // tm, n // tn),
        in_specs=in_specs,
        out_specs=pl.BlockSpec((tm, tn), lambda i, j: (i, j)),
        out_shape=jax.ShapeDtypeStruct((m, n), F32),
        scratch_shapes=[pltpu.VMEM((k, tn), BF16)],
        compiler_params=_cparams(("arbitrary", "arbitrary")),
        name=name,
    )(*lhs_parts, w, res)


LEVELS = (32, 16, 8, 4, 2, 1)
LOG2E = 1.4426950408889634


def _split3_bf16(x):
    def top(v):
        bits = lax.bitcast_convert_type(v, jnp.uint32) & jnp.uint32(0xFFFF0000)
        return lax.bitcast_convert_type(bits, F32)
    hi = top(x)
    r1 = x - hi
    mid = top(r1)
    lo = r1 - mid
    return hi.astype(BF16), mid.astype(BF16), lo.astype(BF16)


def _mix_chunks(qs, ks, vs, las, st_refs):
    heads = range(len(qs))
    c, dk = qs[0].shape
    row = lax.broadcasted_iota(jnp.int32, (c, c), 0)
    col = lax.broadcasted_iota(jnp.int32, (c, c), 1)
    rowk = lax.broadcasted_iota(jnp.int32, (c, dk), 0)
    xor = jnp.bitwise_xor(row, col)

    las = [la * LOG2E for la in las]

    tri = jnp.where(col <= row, 1.0, 0.0).astype(BF16)
    b3 = [jnp.dot(tri, jnp.concatenate(_split3_bf16(la), axis=1), preferred_element_type=F32) for la in las]
    bs = [(t[:, :dk] + t[:, dk:2 * dk]) + t[:, 2 * dk:] for t in b3]

    def neg_dist(w, b, la):
        if w >= 4:
            parts = [jnp.broadcast_to(b[base + w - 1:base + w, :], (2 * w, dk)) for base in range(0, c, 2 * w)]
            m = parts[0] if len(parts) == 1 else jnp.concatenate(parts, axis=0)
            return -jnp.abs(b - m)
        if w == 2:
            r4 = jnp.bitwise_and(rowk, 3)
            nxt = pltpu.roll(la, c - 1, 0)
            prv = pltpu.roll(la, 1, 0)
            return jnp.where(r4 == 0, nxt, jnp.where(r4 == 1, 0.0, jnp.where(r4 == 2, la, la + prv)))
        return jnp.where(jnp.bitwise_and(rowk, 1) == 1, la, 0.0)

    acc = [_nt_dot(qs[h].astype(BF16), ks[h].astype(BF16)) for h in heads]
    for w in reversed(LEVELS):
        upper = jnp.bitwise_and(rowk, w) != 0
        xs = [(jnp.where(upper, qs[h], ks[h]) * jnp.exp2(neg_dist(w, bs[h], las[h]))).astype(BF16) for h in heads]
        gs = [_nt_dot(x, x) for x in xs]
        acc = [jnp.where(xor >= w, gs[h], acc[h]) for h in heads]
    ab = [jnp.where(col <= row, a, 0.0).astype(BF16) for a in acc]

    sts = [st_refs[h][...] for h in heads]
    qx = [(qs[h] * jnp.exp2(bs[h])).astype(BF16) for h in heads]
    b_last = [b[c - 1:c, :] for b in bs]
    kx = [(ks[h] * jnp.exp2(b_last[h] - bs[h])).astype(BF16) for h in heads]
    vb = [v.astype(BF16) for v in vs]
    outs = [jnp.dot(ab[h], vb[h], preferred_element_type=F32) + _nt_dot(qx[h], sts[h].astype(BF16)) for h in heads]
    for h in heads:
        st_refs[h][...] = jnp.exp2(b_last[h]) * sts[h] + lax.dot_general(
            vb[h], kx[h], (((0,), (0,)), ((), ())), preferred_element_type=F32)
    return outs


def _gated_norm(o, g, gn):
    return _rms(o, gn) * (g * _sigmoid(g))


def _gla_kernel(q_ref, k_ref, v_ref, g_ref, la_ref, gn_ref, o_ref, st_ref, *, n_chunk):
    @pl.when(pl.program_id(1) == 0)
    def _():
        st_ref[...] = jnp.zeros_like(st_ref)

    dk, dv = GLA_DK, GLA_DV
    n_b = q_ref.shape[0]
    n_h = st_ref.shape[0] // n_b
    streams = [(b, h) for b in range(n_b) for h in range(n_h)]
    kc = [slice(h * dk, (h + 1) * dk) for h in range(n_h)]
    vc = [slice(h * dv, (h + 1) * dv) for h in range(n_h)]

    def body(ci, carry):
        rows = pl.ds(pl.multiple_of(ci * CHUNK, CHUNK), CHUNK)
        outs = _mix_chunks([q_ref[b, rows, kc[h]] * (dk ** -0.5) for b, h in streams],
                           [k_ref[b, rows, kc[h]] for b, h in streams], [v_ref[b, rows, vc[h]] for b, h in streams],
                           [la_ref[b, rows, kc[h]] for b, h in streams],
                           [st_ref.at[n] for n in range(len(streams))])
        for n, (b, h) in enumerate(streams):
            o_ref[b, rows, vc[h]] = _gated_norm(outs[n], g_ref[b, rows, vc[h]], gn_ref[...]).astype(o_ref.dtype)
        return carry

    lax.fori_loop(0, n_chunk, body, 0)


def _gla(proj, la, gn, *, tb, hp):
    batch, seq, _ = proj.shape
    wk, wv = hp * GLA_DK, hp * GLA_DV
    n_grp = GLA_HEADS // hp
    spec = lambda width, first: pl.BlockSpec((batch, tb, width), lambda p, t: (0, t, first + p))
    return pl.pallas_call(
        functools.partial(_gla_kernel, n_chunk=tb // CHUNK),
        grid=(n_grp, seq // tb),
        in_specs=[spec(wk, 0), spec(wk, n_grp), spec(wv, n_grp), spec(wv, 2 * n_grp), spec(wk, 0),
                  pl.BlockSpec((1, GLA_DV), lambda p, t: (0, 0))],
        out_specs=spec(wv, 0),
        out_shape=jax.ShapeDtypeStruct((batch, seq, GLA_HEADS * GLA_DV), BF16),
        scratch_shapes=[pltpu.VMEM((batch * hp, GLA_DV, GLA_DK), F32)],
        compiler_params=_cparams(("parallel", "arbitrary")),
        name="gla",
    )(proj, proj, proj, proj, la, gn)


def _hgrn_kernel(q_ref, f_ref, i_ref, g_ref, lb_ref, gn_ref, o_ref, st_ref, *, n_chunk):
    @pl.when(pl.program_id(1) == 0)
    def _():
        st_ref[...] = jnp.zeros_like(st_ref)

    dh = HGRN_DH
    n_b = q_ref.shape[0]
    n_h = st_ref.shape[0] // n_b
    streams = [(b, h) for b in range(n_b) for h in range(n_h)]
    hc = [slice(h * dh, (h + 1) * dh) for h in range(n_h)]

    def body(ci, carry):
        rows = pl.ds(pl.multiple_of(ci * CHUNK, CHUNK), CHUNK)
        qs, ks, las = [], [], []
        for b, h in streams:
            log_lb, log_1mlb, one_m_lb = lb_ref[0:1, hc[h]], lb_ref[1:2, hc[h]], lb_ref[2:3, hc[h]]
            hq = q_ref[b, rows, hc[h]]
            z = f_ref[b, rows, hc[h]]
            x1 = log_1mlb + _log_sigmoid(z)
            las.append(jnp.maximum(log_lb, x1) + _log1pexp_neg(jnp.abs(log_lb - x1)))
            qs.append(hq * _sigmoid(hq))
            ks.append(one_m_lb * _sigmoid(-z))
        outs = _mix_chunks(qs, ks, [i_ref[b, rows, hc[h]] for b, h in streams], las,
                           [st_ref.at[n] for n in range(len(streams))])
        for n, (b, h) in enumerate(streams):
            o_ref[b, rows, hc[h]] = _gated_norm(outs[n], g_ref[b, rows, hc[h]], gn_ref[...]).astype(o_ref.dtype)
        return carry

    lax.fori_loop(0, n_chunk, body, 0)


def _hgrn(proj, lbc, gn, *, tb, hp):
    batch, seq, _ = proj.shape
    width = hp * HGRN_DH
    n_grp = HGRN_HEADS // hp
    first = 3072 // width
    spec = lambda seg: pl.BlockSpec((batch, tb, width), lambda p, t: (0, t, first + seg * n_grp + p))
    return pl.pallas_call(
        functools.partial(_hgrn_kernel, n_chunk=tb // CHUNK),
        grid=(n_grp, seq // tb),
        in_specs=[spec(0), spec(1), spec(2), spec(3),
                  pl.BlockSpec((8, width), lambda p, t: (0, p)),
                  pl.BlockSpec((1, HGRN_DH), lambda p, t: (0, 0))],
        out_specs=pl.BlockSpec((batch, tb, width), lambda p, t: (0, t, p)),
        out_shape=jax.ShapeDtypeStruct((batch, seq, HGRN_HEADS * HGRN_DH), BF16),
        scratch_shapes=[pltpu.VMEM((batch * hp, HGRN_DH, HGRN_DH), F32)],
        compiler_params=_cparams(("parallel", "arbitrary")),
        name="hgrn",
    )(proj, proj, proj, proj, lbc, gn)


def _xattn_kernel(x_ref, g_ref, w_ref, k_ref, v_ref, o_ref, wb_ref):
    @pl.when(pl.program_id(0) == 0)
    def _():
        _for_row_chunks(w_ref.shape[0], lambda rows: wb_ref.__setitem__((rows, slice(None)),
                                                                        w_ref[rows, :].astype(BF16)))

    a = _rms(x_ref[...], g_ref[...]).astype(BF16)
    q = jnp.dot(a, wb_ref[...], preferred_element_type=F32).astype(BF16)
    cols = [slice(h * XA_DH, (h + 1) * XA_DH) for h in range(XA_HEADS)]
    s = [_nt_dot(q[:, c], k_ref[:, c]) * (XA_DH ** -0.5) for c in cols]
    p = [jnp.exp(x - jnp.max(x, axis=-1, keepdims=True)) for x in s]
    p = [(x / jnp.sum(x, axis=-1, keepdims=True)).astype(BF16) for x in p]
    for c, x in zip(cols, p):
        o_ref[:, c] = jnp.dot(x, v_ref[:, c], preferred_element_type=F32).astype(o_ref.dtype)


def _xattn(x, g, w_q, kv, *, seq, n_mem, tq):
    m, d = x.shape
    nt = seq // tq
    return pl.pallas_call(
        _xattn_kernel,
        grid=(m // tq,),
        in_specs=[
            pl.BlockSpec((tq, d), lambda i: (i, 0)),
            pl.BlockSpec((1, d), lambda i: (0, 0)),
            pl.BlockSpec((d, d), lambda i: (0, 0), pipeline_mode=pl.Buffered(1)),
            pl.BlockSpec((n_mem, d), lambda i: (i // nt, 0)),
            pl.BlockSpec((n_mem, d), lambda i: (i // nt, 1)),
        ],
        out_specs=pl.BlockSpec((tq, d), lambda i: (i, 0)),
        out_shape=jax.ShapeDtypeStruct((m, d), BF16),
        scratch_shapes=[pltpu.VMEM((d, d), BF16)],
        compiler_params=_cparams(("arbitrary",)),
        name="xattn",
    )(x, g, w_q, kv, kv)


ROUTER_ROWS = 40
META_ROWS = 8


def _router_kernel(h_ref, g_ref, wr_ref, br_ref, a_ref, meta_ref, cnt_ref, carry_ref):
    tm = h_ref.shape[0]
    nr = wr_ref.shape[0]

    @pl.when(pl.program_id(0) == 0)
    def _():
        carry_ref[...] = jnp.zeros_like(carry_ref)

    a = _rms(h_ref[...], g_ref[...])
    a_ref[...] = a.astype(BF16).reshape(a_ref.shape)
    a_hi = a.astype(BF16)
    a_lo = (a - a_hi.astype(F32)).astype(BF16)
    w = wr_ref[...]
    w_hi = w.astype(BF16)
    w_lo = (w - w_hi.astype(F32)).astype(BF16)
    logits = (_nt_dot(w_hi, a_hi) + _nt_dot(w_hi, a_lo) + _nt_dot(w_lo, a_hi)) + br_ref[:, 0:1]
    row = lax.broadcasted_iota(jnp.int32, (nr, tm), 0)
    row_f = row.astype(F32)
    neg = -jnp.inf

    def first_max(x):
        v = jnp.max(x, axis=0, keepdims=True)
        return v, jnp.min(jnp.where(x == v, row_f, float(nr)), axis=0, keepdims=True)

    gl = jnp.where(row < N_GROUPS, logits, neg)
    gmax, gidx = first_max(gl)
    p_group = 1.0 / jnp.sum(jnp.exp(gl - gmax), axis=0, keepdims=True)
    lo = float(N_GROUPS) + gidx * float(EXPERTS_PER_GROUP)
    el = jnp.where((row_f >= lo) & (row_f < lo + float(EXPERTS_PER_GROUP)), logits, neg)
    v1, i1 = first_max(el)
    v2, i2 = first_max(jnp.where(row_f == i1, neg, el))
    t = jnp.exp(v2 - v1)
    g1 = p_group / (1.0 + t)
    g2 = p_group * t / (1.0 + t)

    hit1 = row_f == i1
    hit2 = row_f == i2
    onehot = jnp.where(hit1 | hit2, 1.0, 0.0)
    src = lax.broadcasted_iota(jnp.int32, (tm, tm), 0)
    dst = lax.broadcasted_iota(jnp.int32, (tm, tm), 1)
    earlier = jnp.where(src < dst, 1.0, 0.0).astype(BF16)
    before = jnp.dot(onehot.astype(BF16), earlier, preferred_element_type=F32) + carry_ref[:, 0:1]
    r1 = jnp.sum(jnp.where(hit1, before, 0.0), axis=0, keepdims=True)
    r2 = jnp.sum(jnp.where(hit2, before, 0.0), axis=0, keepdims=True)
    carry_ref[...] = carry_ref[...] + jnp.sum(onehot, axis=1, keepdims=True)
    cnt_ref[...] = carry_ref[...]

    out_row = lax.broadcasted_iota(jnp.int32, meta_ref.shape, 0)
    meta = jnp.zeros(meta_ref.shape, F32)
    for idx, val in enumerate((i1 - float(N_GROUPS), i2 - float(N_GROUPS), r1, r2, g1, g2)):
        meta = jnp.where(out_row == idx, val, meta)
    meta_ref[...] = meta


def _router(h, g, wr, br, *, tm):
    m, d = h.shape
    slab = d // LANES
    return pl.pallas_call(
        _router_kernel,
        grid=(m // tm,),
        in_specs=[
            pl.BlockSpec((tm, d), lambda i: (i, 0)),
            pl.BlockSpec((1, d), lambda i: (0, 0)),
            pl.BlockSpec((ROUTER_ROWS, d), lambda i: (0, 0)),
            pl.BlockSpec((ROUTER_ROWS, LANES), lambda i: (0, 0)),
        ],
        out_specs=[
            pl.BlockSpec((tm, slab, LANES), lambda i: (i, 0, 0)),
            pl.BlockSpec((META_ROWS, tm), lambda i: (0, i)),
            pl.BlockSpec((ROUTER_ROWS, LANES), lambda i: (0, 0)),
        ],
        out_shape=[jax.ShapeDtypeStruct((m, slab, LANES), BF16), jax.ShapeDtypeStruct((META_ROWS, m), F32),
                   jax.ShapeDtypeStruct((ROUTER_ROWS, LANES), F32)],
        scratch_shapes=[pltpu.VMEM((ROUTER_ROWS, LANES), F32)],
        compiler_params=_cparams(("arbitrary",)),
        name="router",
    )(h, g, wr, br)


PAD_PIECES = tuple(PAD_UNIT >> (b + 1) for b in range(PAD_UNIT.bit_length() - 1))


def _dispatch_kernel(d1_ref, d2_ref, fill_ref, npad_ref, nb_ref, a_ref, xs_hbm, slot_ref, st0, st1, zbuf, sem, zsem):
    i = pl.program_id(0)
    n_steps = pl.num_programs(0)
    tm = a_ref.shape[0]
    bufs = (st0, st1)
    rb = zbuf.shape[0]
    n_blocks = xs_hbm.shape[0] // rb
    min_blocks = (2 * tm * n_steps) // rb

    def zero_copies():
        out = []
        for e in range(N_EXPERTS):
            p = npad_ref[e]
            for piece in PAD_PIECES:
                out.append((p & piece != 0, pltpu.make_async_copy(
                    zbuf.at[pl.ds(0, piece)], xs_hbm.at[pl.ds(fill_ref[e] + (p & -(2 * piece)), piece)], zsem.at[0])))
        for b in range(min_blocks, n_blocks):
            out.append((b >= nb_ref[0], pltpu.make_async_copy(zbuf, xs_hbm.at[pl.ds(b * rb, rb)], zsem.at[0])))
        return out

    def wait_rows(s):
        for _ in range(2):
            pltpu.make_async_copy(bufs[s], xs_hbm.at[pl.ds(0, tm)], sem.at[s]).wait()

    @pl.when(i == 0)
    def _():
        def mark_unused(j, carry):
            slot_ref[j] = -1
            return carry
        for e in range(N_EXPERTS):
            lax.fori_loop(fill_ref[e], fill_ref[e] + npad_ref[e], mark_unused, 0)
        lax.fori_loop(nb_ref[0] * rb, slot_ref.shape[0], mark_unused, 0)
        zbuf[...] = jnp.zeros_like(zbuf)
        for cond, cp in zero_copies():
            @pl.when(cond)
            def _():
                cp.start()

    for s in (0, 1):
        @pl.when(lax.rem(i, 2) == s)
        def _():
            @pl.when(i >= 2)
            def _():
                wait_rows(s)

            bufs[s][...] = a_ref[...]
            for r in range(tm):
                tok = i * tm + r
                d1 = d1_ref[tok]
                d2 = d2_ref[tok]
                slot_ref[d1] = 2 * tok
                slot_ref[d2] = 2 * tok + 1
                pltpu.make_async_copy(bufs[s].at[r], xs_hbm.at[d1], sem.at[s]).start(priority=0)
                pltpu.make_async_copy(bufs[s].at[r], xs_hbm.at[d2], sem.at[s]).start(priority=1)

    @pl.when(i == n_steps - 1)
    def _():
        for s in (0, 1):
            @pl.when((lax.rem(n_steps - 1, 2) == s) | ((n_steps >= 2) & (lax.rem(n_steps, 2) == s)))
            def _():
                wait_rows(s)
        for cond, cp in zero_copies():
            @pl.when(cond)
            def _():
                cp.wait()


def _dispatch(dest1, dest2, fill, npad, n_blk, a, *, n_rows, tm):
    m, slab, _ = a.shape
    grid_spec = pltpu.PrefetchScalarGridSpec(
        num_scalar_prefetch=5,
        grid=(m // tm,),
        in_specs=[pl.BlockSpec((tm, slab, LANES), lambda i, *_: (i, 0, 0))],
        out_specs=[pl.BlockSpec(memory_space=pl.ANY), pl.BlockSpec(memory_space=pltpu.SMEM)],
        scratch_shapes=[pltpu.VMEM((tm, slab, LANES), a.dtype), pltpu.VMEM((tm, slab, LANES), a.dtype),
                        pltpu.VMEM((PAD_UNIT, slab, LANES), a.dtype),
                        pltpu.SemaphoreType.DMA((2,)), pltpu.SemaphoreType.DMA((1,))],
    )
    return pl.pallas_call(
        _dispatch_kernel,
        grid_spec=grid_spec,
        out_shape=[jax.ShapeDtypeStruct((n_rows, slab, LANES), a.dtype), jax.ShapeDtypeStruct((n_rows,), jnp.int32)],
        compiler_params=_cparams(("arbitrary",)),
        name="dispatch",
    )(dest1, dest2, fill, npad, n_blk, a)


def _weight_copies(w_hbms, e, wst_refs, wsem):
    copies = []
    for w_hbm, wst_ref in zip(w_hbms, wst_refs):
        rows_per = w_hbm.shape[1] // WEIGHT_CHUNKS
        for c in range(WEIGHT_CHUNKS):
            rows = pl.ds(c * rows_per, rows_per)
            copies.append(pltpu.make_async_copy(w_hbm.at[e, rows, :], wst_ref.at[rows, :], wsem.at[0]))
    return copies


def _load_expert_weights(i, be_ref, nx_ref, w_hbms, wst_refs, wb_refs, wsem):
    def start(e):
        for cp in _weight_copies(w_hbms, e, wst_refs, wsem):
            cp.start(priority=WEIGHT_DMA_PRIORITY)

    @pl.when(i == 0)
    def _():
        start(be_ref[0])

    @pl.when((i == 0) | (be_ref[i] != be_ref[jnp.maximum(i - 1, 0)]))
    def _():
        for cp in _weight_copies(w_hbms, be_ref[i], wst_refs, wsem):
            cp.wait()
        for wst_ref, wb_ref in zip(wst_refs, wb_refs):
            _for_row_chunks(wst_ref.shape[0], lambda rows: wb_ref.__setitem__((rows, slice(None)),
                                                                              wst_ref[rows, :].astype(BF16)))

        @pl.when(nx_ref[i] >= 0)
        def _():
            start(nx_ref[i])


def _expert_kernel(slot_ref, be_ref, nx_ref, off_ref, half_ref, nb_ref, x_ref, wg_hbm, wu_hbm, wd_hbm, y_hbm, wsg, wsu, wsd,
                   wgb, wub, wdb, ys0, ys1, sem, wsem, *, n_tok):
    i = pl.program_id(0)
    n_steps = pl.num_programs(0)
    nb = nb_ref[0]
    rb = x_ref.shape[0]
    bufs = (ys0, ys1)

    def wait_block(buf, s):
        pltpu.make_async_copy(buf, y_hbm.at[pl.ds(0, rb)], sem.at[s]).wait()

    def scatter_previous(s):
        o = 1 - s
        prev = jnp.maximum(i - 1, 0)
        used = jnp.where(i >= 1, rb - half_ref[prev] * (rb // 2), 0)
        for r in range(rb):
            v = jnp.where(r < used, slot_ref[off_ref[prev] + r], -1)
            dst = jnp.where(v >= 0, (v & 1) * n_tok + (v >> 1), 2 * n_tok + o * rb + r)
            pltpu.make_async_copy(bufs[o].at[r], y_hbm.at[dst], sem.at[o]).start()

    @pl.when(i == 0)
    def _():
        ys0[...] = jnp.zeros_like(ys0)
        ys1[...] = jnp.zeros_like(ys1)
        spare = pltpu.make_async_copy(ys0, y_hbm.at[pl.ds(2 * n_tok, rb)], sem.at[0])
        spare.start()
        spare.wait()

    for s in (0, 1):
        for rows in (rb, rb // 2):
            @pl.when((i < nb) & (lax.rem(i, 2) == s) & (half_ref[jnp.minimum(i, n_steps - 2)] == (rows != rb)))
            def _():
                _load_expert_weights(i, be_ref, nx_ref, (wg_hbm, wu_hbm, wd_hbm), (wsg, wsu, wsd), (wgb, wub, wdb),
                                     wsem)

                @pl.when(i >= 1)
                def _():
                    wait_block(bufs[s], s)

                scatter_previous(s)
                x = x_ref[0:rows].reshape(rows, -1)
                hg = jnp.dot(x, wgb[...], preferred_element_type=F32)
                hu = jnp.dot(x, wub[...], preferred_element_type=F32)
                hb = (hg * _sigmoid(hg) * hu).astype(BF16)
                y = jnp.dot(hb, wdb[...], preferred_element_type=F32)
                bufs[s][0:rows] = y.astype(bufs[s].dtype).reshape((rows,) + bufs[s].shape[1:])

        @pl.when((i == nb) & (lax.rem(i, 2) == s))
        def _():
            scatter_previous(s)

    @pl.when(i == n_steps - 1)
    def _():
        for s in (0, 1):
            wait_block(bufs[s], s)


def _experts(slot, block_e, next_e, block_off, block_half, n_blk, xs, wg, wu, wd, *, rb, n_tok):
    _, slab, _ = xs.shape
    d = slab * LANES
    de = wg.shape[2]
    n_blocks = block_e.shape[0]

    def x_rows(i, sl, be, nx, off, hf, nb):
        return (pl.multiple_of(off[jnp.minimum(i, nb[0] - 1)], PAD_UNIT), 0, 0)

    grid_spec = pltpu.PrefetchScalarGridSpec(
        num_scalar_prefetch=6,
        grid=(n_blocks + 1,),
        in_specs=[pl.BlockSpec((pl.Element(rb), pl.Element(slab), pl.Element(LANES)), x_rows)]
        + [pl.BlockSpec(memory_space=pl.ANY)] * 3,
        out_specs=pl.BlockSpec(memory_space=pl.ANY),
        scratch_shapes=[pltpu.VMEM((d, de), F32), pltpu.VMEM((d, de), F32), pltpu.VMEM((de, d), F32),
                        pltpu.VMEM((d, de), BF16), pltpu.VMEM((d, de), BF16), pltpu.VMEM((de, d), BF16),
                        pltpu.VMEM((rb, slab, LANES), BF16), pltpu.VMEM((rb, slab, LANES), BF16),
                        pltpu.SemaphoreType.DMA((2,)), pltpu.SemaphoreType.DMA((1,))],
    )
    return pl.pallas_call(
        functools.partial(_expert_kernel, n_tok=n_tok),
        grid_spec=grid_spec,
        out_shape=jax.ShapeDtypeStruct((2 * n_tok + 2 * rb, slab, LANES), BF16),
        compiler_params=_cparams(("arbitrary",), vmem=EXPERT_VMEM_LIMIT),
        name="experts",
    )(slot, block_e, next_e, block_off, block_half, n_blk, xs, wg, wu, wd)


def _combine_kernel(y0_ref, y1_ref, h_ref, meta_ref, g_ref, o_ref):
    tm, d = h_ref.shape
    gates = meta_ref[...].T
    y0 = y0_ref[...].reshape(tm, d).astype(F32)
    y1 = y1_ref[...].reshape(tm, d).astype(F32)
    out = h_ref[...] + gates[:, 4:5] * y0 + gates[:, 5:6] * y1
    o_ref[...] = _rms(out, g_ref[...])


def _combine(y, h, meta, g, *, tm):
    m, d = h.shape
    slab = d // LANES
    nt = m // tm
    return pl.pallas_call(
        _combine_kernel,
        grid=(nt,),
        in_specs=[
            pl.BlockSpec((tm, slab, LANES), lambda i: (i, 0, 0)),
            pl.BlockSpec((tm, slab, LANES), lambda i: (nt + i, 0, 0)),
            pl.BlockSpec((tm, d), lambda i: (i, 0)),
            pl.BlockSpec((META_ROWS, tm), lambda i: (0, i)),
            pl.BlockSpec((1, d), lambda i: (0, 0)),
        ],
        out_specs=pl.BlockSpec((tm, d), lambda i: (i, 0)),
        out_shape=jax.ShapeDtypeStruct((m, d), F32),
        compiler_params=_cparams(("parallel",)),
        name="combine",
    )(y, y, h, meta, g)


def kernel(x, mem, norm_mix_g, w_in, w_gla_alpha_up, b_gla_alpha, gla_out_norm_g, hgrn_lb_logits, hgrn_out_norm_g, w_mix_out, norm_xattn_g, norm_mem_g, w_xattn_q, w_xattn_kv, w_xattn_out, norm_ffn_g, w_router_group, b_router_group, w_router_expert, b_router_expert, w_expert_gate, w_expert_up, w_expert_down, norm_final_g):
    batch, seq, d = x.shape
    n_mem = mem.shape[1]
    m = batch * seq
    depth = norm_mix_g.shape[0]
    t = _tiles(m, seq)
    h = x.reshape(m, d)
    lb_all = jnp.cumsum(jax.nn.softmax(hgrn_lb_logits.astype(F32), axis=0), axis=0)
    gla_cols = 2 * GLA_HEADS * GLA_DK + 2 * GLA_HEADS * GLA_DV
    lr_rank = w_gla_alpha_up.shape[1]

    for l in range(depth):
        w_t = jnp.swapaxes(w_in[l], 0, 1)
        w_lr_t = jnp.pad(w_t[gla_cols:gla_cols + lr_rank], ((0, LANES - lr_rank), (0, 0))).astype(BF16)
        w_up = jnp.pad(w_gla_alpha_up[l], ((0, LANES - lr_rank), (0, 0))).astype(BF16)
        a_mix, la = _norm_gate(h, norm_mix_g[l][None, :], w_lr_t, w_up, b_gla_alpha[l][None, :], tm=t.norm_gate_rows)
        proj = _inproj(a_mix, w_t, tm=t.inproj_rows, tn=t.inproj_cols, skip_from=gla_cols, skip=lr_rank)
        lb = lb_all[l]
        lbc = jnp.zeros((8, lb.shape[0]), F32).at[0].set(jnp.log(lb)).at[1].set(jnp.log1p(-lb)).at[2].set(1.0 - lb)
        proj3 = proj.reshape(batch, seq, -1)
        o_gla = _gla(proj3, la.reshape(batch, seq, -1), gla_out_norm_g[l][None, :], tb=t.mixer_rows, hp=t.gla_heads)
        o_h = _hgrn(proj3, lbc, hgrn_out_norm_g[l][None, :], tb=t.mixer_rows, hp=t.hgrn_heads)
        o_gla, o_h = o_gla.reshape(m, -1), o_h.reshape(m, -1)
        h = _mm_res([o_gla, o_h], w_mix_out[l], h, tm=t.square_rows, tn=d, name="mix_out")

        kv = _norm_mm(mem.reshape(batch * n_mem, d), norm_mem_g[l][None, :], w_xattn_kv[l],
                      tm=batch * n_mem, tn=t.mem_kv_cols, out_dtype=BF16, name="mem_kv")
        o = _xattn(h, norm_xattn_g[l][None, :], w_xattn_q[l], kv, seq=seq, n_mem=n_mem, tq=t.xattn_rows)
        h = _mm_res([o], w_xattn_out[l], h, tm=t.square_rows, tn=d, name="xattn_out")

        n_logits = N_GROUPS + N_EXPERTS
        wr = jnp.pad(jnp.concatenate([w_router_group[l], w_router_expert[l]], axis=1).T,
                     ((0, ROUTER_ROWS - n_logits), (0, 0)))
        br = jnp.pad(jnp.concatenate([b_router_group[l], b_router_expert[l]]), (0, ROUTER_ROWS - n_logits))
        a, meta, cnt = _router(h, norm_ffn_g[l][None, :], wr, jnp.broadcast_to(br[:, None], (ROUTER_ROWS, LANES)),
                               tm=t.router_rows)

        e_idx = meta[0:2].astype(jnp.int32)
        rank = meta[2:4].astype(jnp.int32)
        counts = cnt[N_GROUPS:N_GROUPS + N_EXPERTS, 0].astype(jnp.int32)
        padded = ((counts + PAD_UNIT - 1) // PAD_UNIT) * PAD_UNIT
        pad_end = jnp.cumsum(padded)
        pad_start = pad_end - padded
        n_rows = 2 * m + N_EXPERTS * PAD_UNIT + PAD_UNIT
        n_units = (pad_end[-1:] // PAD_UNIT).astype(jnp.int32)
        n_full = padded // EXPERT_BLOCK
        n_blk_e = n_full + (padded % EXPERT_BLOCK) // PAD_UNIT
        blk_end = jnp.cumsum(n_blk_e)
        blk_start = blk_end - n_blk_e
        n_blocks = (2 * m) // EXPERT_BLOCK + N_EXPERTS
        n_blk = blk_end[-1:].astype(jnp.int32)
        blk = jnp.arange(n_blocks, dtype=jnp.int32)
        block_e = jnp.minimum(jnp.sum((blk_end[None, :] <= blk[:, None]).astype(jnp.int32), axis=1), N_EXPERTS - 1)
        blk_local = blk - blk_start[block_e]
        block_off = (pad_start[block_e] + EXPERT_BLOCK * blk_local).astype(jnp.int32)
        block_off = jnp.where(blk < n_blk[0], block_off, 0)
        block_half = (blk_local >= n_full[block_e]).astype(jnp.int32)
        after = blk_end[block_e]
        next_e = jnp.where(after < n_blk[0], block_e[jnp.minimum(after, n_blocks - 1)], -1).astype(jnp.int32)

        experts = jnp.arange(N_EXPERTS, dtype=jnp.int32)
        dest = [jnp.sum(jnp.where(e_idx[k][:, None] == experts[None, :], pad_start[None, :], 0), axis=1) + rank[k]
                for k in range(2)]
        xs, slot = _dispatch(dest[0], dest[1], pad_start + counts, padded - counts, n_units, a, n_rows=n_rows,
                             tm=t.dispatch_rows)
        y = _experts(slot, block_e, next_e, block_off, block_half, n_blk, xs, w_expert_gate[l], w_expert_up[l],
                     w_expert_down[l], rb=EXPERT_BLOCK, n_tok=m)
        last = l == depth - 1
        g_fin = norm_final_g[None, :] if last else jnp.ones((1, d), F32)
        h = _combine(y, h, meta, g_fin, tm=t.combine_rows)
        assert last, "the combine kernel fuses the final rmsnorm; deeper stacks need an un-normalised variant"

    return h.reshape(batch, seq, d)
```

```python
import functools
from typing import NamedTuple

import jax
import jax.numpy as jnp
from jax import lax
from jax.experimental import pallas as pl
from jax.experimental.pallas import tpu as pltpu

F32 = jnp.float32
BF16 = jnp.bfloat16

EPS = 1e-6
CHUNK = 64
LANES = 128
GLA_HEADS, GLA_DK, GLA_DV = 4, 128, 256
HGRN_HEADS, HGRN_DH = 8, 128
XA_HEADS, XA_DH = 4, 512
N_GROUPS, EXPERTS_PER_GROUP, N_EXPERTS = 4, 8, 32
EXPERT_BLOCK = 256
PAD_UNIT = 128
WEIGHT_CHUNKS = 8
WEIGHT_DMA_PRIORITY = 1
VMEM_LIMIT = 56 * 1024 * 1024
EXPERT_VMEM_LIMIT = 60 * 1024 * 1024


class Tiles(NamedTuple):
    norm_gate_rows: int
    inproj_rows: int
    inproj_cols: int
    mixer_rows: int
    gla_heads: int
    hgrn_heads: int
    square_rows: int
    mem_kv_cols: int
    xattn_rows: int
    router_rows: int
    dispatch_rows: int
    combine_rows: int


def _tiles(m, seq):
    return Tiles(norm_gate_rows=min(m, 1024), inproj_rows=min(m, 4096), inproj_cols=512, mixer_rows=min(seq, 256),
                 gla_heads=GLA_HEADS, hgrn_heads=HGRN_HEADS, square_rows=min(m, 512), mem_kv_cols=1024,
                 xattn_rows=min(seq, 512), router_rows=min(m, 512), dispatch_rows=min(m, 256),
                 combine_rows=min(m, 512))


def _cparams(sem, vmem=VMEM_LIMIT):
    return pltpu.CompilerParams(dimension_semantics=sem, vmem_limit_bytes=vmem)


def _log1pexp_neg(t):
    return jnp.log(1.0 + jnp.exp(-t))


def _log_sigmoid(z):
    return jnp.minimum(z, 0.0) - _log1pexp_neg(jnp.abs(z))


def _sigmoid(z):
    return 1.0 / (1.0 + jnp.exp(-z))


def _rms(x, g):
    return x * lax.rsqrt(jnp.mean(x * x, axis=-1, keepdims=True) + EPS) * g


def _nt_dot(x, y):
    return lax.dot_general(x, y, (((1,), (1,)), ((), ())), preferred_element_type=F32)


NORM_ROWS = 256


def _for_row_chunks(n_rows, fn):
    step = min(NORM_ROWS, n_rows)

    def body(ci, carry):
        fn(pl.ds(pl.multiple_of(ci * step, step), step))
        return carry

    lax.fori_loop(0, n_rows // step, body, 0)


def _norm_gate_kernel(x_ref, g_ref, wlr_ref, wup_ref, bal_ref, a_ref, la_ref):
    a = _rms(x_ref[...], g_ref[...]).astype(BF16)
    a_ref[...] = a
    lr = _nt_dot(a, wlr_ref[...])
    z = jnp.dot(lr.astype(BF16), wup_ref[...], preferred_element_type=F32) + bal_ref[...]
    la_ref[...] = _log_sigmoid(z) * (1.0 / 16.0)


def _norm_gate(x, g, w_lr_t, w_up, b_al, *, tm):
    m, d = x.shape
    nk = w_up.shape[1]
    return pl.pallas_call(
        _norm_gate_kernel,
        grid=(m // tm,),
        in_specs=[
            pl.BlockSpec((tm, d), lambda i: (i, 0)),
            pl.BlockSpec((1, d), lambda i: (0, 0)),
            pl.BlockSpec((LANES, d), lambda i: (0, 0)),
            pl.BlockSpec((LANES, nk), lambda i: (0, 0)),
            pl.BlockSpec((1, nk), lambda i: (0, 0)),
        ],
        out_specs=[pl.BlockSpec((tm, d), lambda i: (i, 0)), pl.BlockSpec((tm, nk), lambda i: (i, 0))],
        out_shape=[jax.ShapeDtypeStruct((m, d), BF16), jax.ShapeDtypeStruct((m, nk), F32)],
        compiler_params=_cparams(("parallel",)),
        name="norm_gate",
    )(x, g, w_lr_t, w_up, b_al)


def _inproj_kernel(a_ref, w_ref, proj_ref):
    proj_ref[...] = _nt_dot(a_ref[...], w_ref[...].astype(BF16))


def _inproj(a, w_t, *, tm, tn, skip_from, skip):
    m, d = a.shape
    n = w_t.shape[0] - skip
    first_after = skip_from // tn

    def w_rows(i, j):
        return (pl.multiple_of(j * tn + jnp.where(j >= first_after, skip, 0), 8), 0)

    return pl.pallas_call(
        _inproj_kernel,
        grid=(m // tm, n // tn),
        in_specs=[
            pl.BlockSpec((tm, d), lambda i, j: (i, 0), pipeline_mode=pl.Buffered(1)),
            pl.BlockSpec((pl.Element(tn), pl.Element(d)), w_rows),
        ],
        out_specs=pl.BlockSpec((tm, tn), lambda i, j: (i, j)),
        out_shape=jax.ShapeDtypeStruct((m, n), F32),
        compiler_params=_cparams(("parallel", "arbitrary")),
        name="inproj",
    )(a, w_t)


def _weight_spec(k, n, tn):
    if tn == n:
        return pl.BlockSpec((k, n), lambda i, j: (0, 0), pipeline_mode=pl.Buffered(1))
    return pl.BlockSpec((k, tn), lambda i, j: (0, j))


def _cast_weight(w_ref, wb_ref, resident):
    if resident:
        @pl.when((pl.program_id(0) == 0) & (pl.program_id(1) == 0))
        def _():
            wb_ref[...] = w_ref[...].astype(BF16)
    else:
        wb_ref[...] = w_ref[...].astype(BF16)


def _norm_mm_kernel(x_ref, g_ref, w_ref, o_ref, a_ref, wb_ref, *, resident):
    @pl.when(pl.program_id(1) == 0)
    def _():
        def rows_fn(rows):
            a_ref[rows, :] = _rms(x_ref[rows, :], g_ref[...]).astype(BF16)

        _for_row_chunks(x_ref.shape[0], rows_fn)

    _cast_weight(w_ref, wb_ref, resident)
    o_ref[...] = jnp.dot(a_ref[...], wb_ref[...], preferred_element_type=F32).astype(o_ref.dtype)


def _norm_mm(x, g, w, *, tm, tn, out_dtype, name):
    m, d = x.shape
    n = w.shape[1]
    return pl.pallas_call(
        functools.partial(_norm_mm_kernel, resident=tn == n),
        grid=(m // tm, n // tn),
        in_specs=[
            pl.BlockSpec((tm, d), lambda i, j: (i, 0)),
            pl.BlockSpec((1, d), lambda i, j: (0, 0)),
            _weight_spec(d, n, tn),
        ],
        out_specs=pl.BlockSpec((tm, tn), lambda i, j: (i, j)),
        out_shape=jax.ShapeDtypeStruct((m, n), out_dtype),
        scratch_shapes=[pltpu.VMEM((tm, d), BF16), pltpu.VMEM((d, tn), BF16)],
        compiler_params=_cparams(("arbitrary", "arbitrary")),
        name=name,
    )(x, g, w)


def _mm_res_kernel(*refs, n_lhs, resident):
    lhs = refs[:n_lhs]
    w_ref, res_ref, o_ref, wb_ref = refs[n_lhs:]
    _cast_weight(w_ref, wb_ref, resident)
    acc = res_ref[...]
    k0 = 0
    for l_ref in lhs:
        kp = l_ref.shape[1]
        acc = acc + jnp.dot(l_ref[...], wb_ref[k0:k0 + kp, :], preferred_element_type=F32)
        k0 += kp
    o_ref[...] = acc


def _mm_res(lhs_parts, w, res, *, tm, tn, name):
    m, n = res.shape
    k = w.shape[0]
    n_lhs = len(lhs_parts)
    in_specs = [pl.BlockSpec((tm, p.shape[1]), lambda i, j: (i, 0)) for p in lhs_parts]
    in_specs += [_weight_spec(k, n, tn), pl.BlockSpec((tm, tn), lambda i, j: (i, j))]
    return pl.pallas_call(
        functools.partial(_mm_res_kernel, n_lhs=n_lhs, resident=tn == n),
        grid=(m // tm, n // tn),
        in_specs=in_specs,
        out_specs=pl.BlockSpec((tm, tn), lambda i, j: (i, j)),
        out_shape=jax.ShapeDtypeStruct((m, n), F32),
        scratch_shapes=[pltpu.VMEM((k, tn), BF16)],
        compiler_params=_cparams(("arbitrary", "arbitrary")),
        name=name,
    )(*lhs_parts, w, res)


LEVELS = (32, 16, 8, 4, 2, 1)
LOG2E = 1.4426950408889634


def _split3_bf16(x):
    def top(v):
        bits = lax.bitcast_convert_type(v, jnp.uint32) & jnp.uint32(0xFFFF0000)
        return lax.bitcast_convert_type(bits, F32)
    hi = top(x)
    r1 = x - hi
    mid = top(r1)
    lo = r1 - mid
    return hi.astype(BF16), mid.astype(BF16), lo.astype(BF16)


def _mix_chunks(qs, ks, vs, las, st_refs):
    heads = range(len(qs))
    c, dk = qs[0].shape
    row = lax.broadcasted_iota(jnp.int32, (c, c), 0)
    col = lax.broadcasted_iota(jnp.int32, (c, c), 1)
    rowk = lax.broadcasted_iota(jnp.int32, (c, dk), 0)
    xor = jnp.bitwise_xor(row, col)

    las = [la * LOG2E for la in las]

    tri = jnp.where(col <= row, 1.0, 0.0).astype(BF16)
    b3 = [jnp.dot(tri, jnp.concatenate(_split3_bf16(la), axis=1), preferred_element_type=F32) for la in las]
    bs = [(t[:, :dk] + t[:, dk:2 * dk]) + t[:, 2 * dk:] for t in b3]

    def neg_dist(w, b, la):
        if w >= 4:
            parts = [jnp.broadcast_to(b[base + w - 1:base + w, :], (2 * w, dk)) for base in range(0, c, 2 * w)]
            m = parts[0] if len(parts) == 1 else jnp.concatenate(parts, axis=0)
            return -jnp.abs(b - m)
        if w == 2:
            r4 = jnp.bitwise_and(rowk, 3)
            nxt = pltpu.roll(la, c - 1, 0)
            prv = pltpu.roll(la, 1, 0)
            return jnp.where(r4 == 0, nxt, jnp.where(r4 == 1, 0.0, jnp.where(r4 == 2, la, la + prv)))
        return jnp.where(jnp.bitwise_and(rowk, 1) == 1, la, 0.0)

    acc = [_nt_dot(qs[h].astype(BF16), ks[h].astype(BF16)) for h in heads]
    for w in reversed(LEVELS):
        upper = jnp.bitwise_and(rowk, w) != 0
        xs = [(jnp.where(upper, qs[h], ks[h]) * jnp.exp2(neg_dist(w, bs[h], las[h]))).astype(BF16) for h in heads]
        gs = [_nt_dot(x, x) for x in xs]
        acc = [jnp.where(xor >= w, gs[h], acc[h]) for h in heads]
    ab = [jnp.where(col <= row, a, 0.0).astype(BF16) for a in acc]

    sts = [st_refs[h][...] for h in heads]
    qx = [(qs[h] * jnp.exp2(bs[h])).astype(BF16) for h in heads]
    b_last = [b[c - 1:c, :] for b in bs]
    kx = [(ks[h] * jnp.exp2(b_last[h] - bs[h])).astype(BF16) for h in heads]
    vb = [v.astype(BF16) for v in vs]
    outs = [jnp.dot(ab[h], vb[h], preferred_element_type=F32) + _nt_dot(qx[h], sts[h].astype(BF16)) for h in heads]
    for h in heads:
        st_refs[h][...] = jnp.exp2(b_last[h]) * sts[h] + lax.dot_general(
            vb[h], kx[h], (((0,), (0,)), ((), ())), preferred_element_type=F32)
    return outs


def _gated_norm(o, g, gn):
    return _rms(o, gn) * (g * _sigmoid(g))


def _gla_kernel(q_ref, k_ref, v_ref, g_ref, la_ref, gn_ref, o_ref, st_ref, *, n_chunk):
    @pl.when(pl.program_id(1) == 0)
    def _():
        st_ref[...] = jnp.zeros_like(st_ref)

    dk, dv = GLA_DK, GLA_DV
    n_b = q_ref.shape[0]
    n_h = st_ref.shape[0] // n_b
    streams = [(b, h) for b in range(n_b) for h in range(n_h)]
    kc = [slice(h * dk, (h + 1) * dk) for h in range(n_h)]
    vc = [slice(h * dv, (h + 1) * dv) for h in range(n_h)]

    def body(ci, carry):
        rows = pl.ds(pl.multiple_of(ci * CHUNK, CHUNK), CHUNK)
        outs = _mix_chunks([q_ref[b, rows, kc[h]] * (dk ** -0.5) for b, h in streams],
                           [k_ref[b, rows, kc[h]] for b, h in streams], [v_ref[b, rows, vc[h]] for b, h in streams],
                           [la_ref[b, rows, kc[h]] for b, h in streams],
                           [st_ref.at[n] for n in range(len(streams))])
        for n, (b, h) in enumerate(streams):
            o_ref[b, rows, vc[h]] = _gated_norm(outs[n], g_ref[b, rows, vc[h]], gn_ref[...]).astype(o_ref.dtype)
        return carry

    lax.fori_loop(0, n_chunk, body, 0)


def _gla(proj, la, gn, *, tb, hp):
    batch, seq, _ = proj.shape
    wk, wv = hp * GLA_DK, hp * GLA_DV
    n_grp = GLA_HEADS // hp
    spec = lambda width, first: pl.BlockSpec((batch, tb, width), lambda p, t: (0, t, first + p))
    return pl.pallas_call(
        functools.partial(_gla_kernel, n_chunk=tb // CHUNK),
        grid=(n_grp, seq // tb),
        in_specs=[spec(wk, 0), spec(wk, n_grp), spec(wv, n_grp), spec(wv, 2 * n_grp), spec(wk, 0),
                  pl.BlockSpec((1, GLA_DV), lambda p, t: (0, 0))],
        out_specs=spec(wv, 0),
        out_shape=jax.ShapeDtypeStruct((batch, seq, GLA_HEADS * GLA_DV), BF16),
        scratch_shapes=[pltpu.VMEM((batch * hp, GLA_DV, GLA_DK), F32)],
        compiler_params=_cparams(("parallel", "arbitrary")),
        name="gla",
    )(proj, proj, proj, proj, la, gn)


def _hgrn_kernel(q_ref, f_ref, i_ref, g_ref, lb_ref, gn_ref, o_ref, st_ref, *, n_chunk):
    @pl.when(pl.program_id(1) == 0)
    def _():
        st_ref[...] = jnp.zeros_like(st_ref)

    dh = HGRN_DH
    n_b = q_ref.shape[0]
    n_h = st_ref.shape[0] // n_b
    streams = [(b, h) for b in range(n_b) for h in range(n_h)]
    hc = [slice(h * dh, (h + 1) * dh) for h in range(n_h)]

    def body(ci, carry):
        rows = pl.ds(pl.multiple_of(ci * CHUNK, CHUNK), CHUNK)
        qs, ks, las = [], [], []
        for b, h in streams:
            log_lb, log_1mlb, one_m_lb = lb_ref[0:1, hc[h]], lb_ref[1:2, hc[h]], lb_ref[2:3, hc[h]]
            hq = q_ref[b, rows, hc[h]]
            z = f_ref[b, rows, hc[h]]
            x1 = log_1mlb + _log_sigmoid(z)
            las.append(jnp.maximum(log_lb, x1) + _log1pexp_neg(jnp.abs(log_lb - x1)))
            qs.append(hq * _sigmoid(hq))
            ks.append(one_m_lb * _sigmoid(-z))
        outs = _mix_chunks(qs, ks, [i_ref[b, rows, hc[h]] for b, h in streams], las,
                           [st_ref.at[n] for n in range(len(streams))])
        for n, (b, h) in enumerate(streams):
            o_ref[b, rows, hc[h]] = _gated_norm(outs[n], g_ref[b, rows, hc[h]], gn_ref[...]).astype(o_ref.dtype)
        return carry

    lax.fori_loop(0, n_chunk, body, 0)


def _hgrn(proj, lbc, gn, *, tb, hp):
    batch, seq, _ = proj.shape
    width = hp * HGRN_DH
    n_grp = HGRN_HEADS // hp
    first = 3072 // width
    spec = lambda seg: pl.BlockSpec((batch, tb, width), lambda p, t: (0, t, first + seg * n_grp + p))
    return pl.pallas_call(
        functools.partial(_hgrn_kernel, n_chunk=tb // CHUNK),
        grid=(n_grp, seq // tb),
        in_specs=[spec(0), spec(1), spec(2), spec(3),
                  pl.BlockSpec((8, width), lambda p, t: (0, p)),
                  pl.BlockSpec((1, HGRN_DH), lambda p, t: (0, 0))],
        out_specs=pl.BlockSpec((batch, tb, width), lambda p, t: (0, t, p)),
        out_shape=jax.ShapeDtypeStruct((batch, seq, HGRN_HEADS * HGRN_DH), BF16),
        scratch_shapes=[pltpu.VMEM((batch * hp, HGRN_DH, HGRN_DH), F32)],
        compiler_params=_cparams(("parallel", "arbitrary")),
        name="hgrn",
    )(proj, proj, proj, proj, lbc, gn)


def _xattn_kernel(x_ref, g_ref, w_ref, k_ref, v_ref, o_ref, wb_ref):
    @pl.when(pl.program_id(0) == 0)
    def _():
        _for_row_chunks(w_ref.shape[0], lambda rows: wb_ref.__setitem__((rows, slice(None)),
                                                                        w_ref[rows, :].astype(BF16)))

    a = _rms(x_ref[...], g_ref[...]).astype(BF16)
    q = jnp.dot(a, wb_ref[...], preferred_element_type=F32).astype(BF16)
    cols = [slice(h * XA_DH, (h + 1) * XA_DH) for h in range(XA_HEADS)]
    s = [_nt_dot(q[:, c], k_ref[:, c]) * (XA_DH ** -0.5) for c in cols]
    p = [jnp.exp(x - jnp.max(x, axis=-1, keepdims=True)) for x in s]
    p = [(x / jnp.sum(x, axis=-1, keepdims=True)).astype(BF16) for x in p]
    for c, x in zip(cols, p):
        o_ref[:, c] = jnp.dot(x, v_ref[:, c], preferred_element_type=F32).astype(o_ref.dtype)


def _xattn(x, g, w_q, kv, *, seq, n_mem, tq):
    m, d = x.shape
    nt = seq // tq
    return pl.pallas_call(
        _xattn_kernel,
        grid=(m // tq,),
        in_specs=[
            pl.BlockSpec((tq, d), lambda i: (i, 0)),
            pl.BlockSpec((1, d), lambda i: (0, 0)),
            pl.BlockSpec((d, d), lambda i: (0, 0), pipeline_mode=pl.Buffered(1)),
            pl.BlockSpec((n_mem, d), lambda i: (i // nt, 0)),
            pl.BlockSpec((n_mem, d), lambda i: (i // nt, 1)),
        ],
        out_specs=pl.BlockSpec((tq, d), lambda i: (i, 0)),
        out_shape=jax.ShapeDtypeStruct((m, d), BF16),
        scratch_shapes=[pltpu.VMEM((d, d), BF16)],
        compiler_params=_cparams(("arbitrary",)),
        name="xattn",
    )(x, g, w_q, kv, kv)


ROUTER_ROWS = 40
META_ROWS = 8


def _router_kernel(h_ref, g_ref, wr_ref, br_ref, a_ref, meta_ref, cnt_ref, carry_ref):
    tm = h_ref.shape[0]
    nr = wr_ref.shape[0]

    @pl.when(pl.program_id(0) == 0)
    def _():
        carry_ref[...] = jnp.zeros_like(carry_ref)

    a = _rms(h_ref[...], g_ref[...])
    a_ref[...] = a.astype(BF16).reshape(a_ref.shape)
    a_hi = a.astype(BF16)
    a_lo = (a - a_hi.astype(F32)).astype(BF16)
    w = wr_ref[...]
    w_hi = w.astype(BF16)
    w_lo = (w - w_hi.astype(F32)).astype(BF16)
    logits = (_nt_dot(w_hi, a_hi) + _nt_dot(w_hi, a_lo) + _nt_dot(w_lo, a_hi)) + br_ref[:, 0:1]
    row = lax.broadcasted_iota(jnp.int32, (nr, tm), 0)
    row_f = row.astype(F32)
    neg = -jnp.inf

    def first_max(x):
        v = jnp.max(x, axis=0, keepdims=True)
        return v, jnp.min(jnp.where(x == v, row_f, float(nr)), axis=0, keepdims=True)

    gl = jnp.where(row < N_GROUPS, logits, neg)
    gmax, gidx = first_max(gl)
    p_group = 1.0 / jnp.sum(jnp.exp(gl - gmax), axis=0, keepdims=True)
    lo = float(N_GROUPS) + gidx * float(EXPERTS_PER_GROUP)
    el = jnp.where((row_f >= lo) & (row_f < lo + float(EXPERTS_PER_GROUP)), logits, neg)
    v1, i1 = first_max(el)
    v2, i2 = first_max(jnp.where(row_f == i1, neg, el))
    t = jnp.exp(v2 - v1)
    g1 = p_group / (1.0 + t)
    g2 = p_group * t / (1.0 + t)

    hit1 = row_f == i1
    hit2 = row_f == i2
    onehot = jnp.where(hit1 | hit2, 1.0, 0.0)
    src = lax.broadcasted_iota(jnp.int32, (tm, tm), 0)
    dst = lax.broadcasted_iota(jnp.int32, (tm, tm), 1)
    earlier = jnp.where(src < dst, 1.0, 0.0).astype(BF16)
    before = jnp.dot(onehot.astype(BF16), earlier, preferred_element_type=F32) + carry_ref[:, 0:1]
    r1 = jnp.sum(jnp.where(hit1, before, 0.0), axis=0, keepdims=True)
    r2 = jnp.sum(jnp.where(hit2, before, 0.0), axis=0, keepdims=True)
    carry_ref[...] = carry_ref[...] + jnp.sum(onehot, axis=1, keepdims=True)
    cnt_ref[...] = carry_ref[...]

    out_row = lax.broadcasted_iota(jnp.int32, meta_ref.shape, 0)
    meta = jnp.zeros(meta_ref.shape, F32)
    for idx, val in enumerate((i1 - float(N_GROUPS), i2 - float(N_GROUPS), r1, r2, g1, g2)):
        meta = jnp.where(out_row == idx, val, meta)
    meta_ref[...] = meta


def _router(h, g, wr, br, *, tm):
    m, d = h.shape
    slab = d // LANES
    return pl.pallas_call(
        _router_kernel,
        grid=(m // tm,),
        in_specs=[
            pl.BlockSpec((tm, d), lambda i: (i, 0)),
            pl.BlockSpec((1, d), lambda i: (0, 0)),
            pl.BlockSpec((ROUTER_ROWS, d), lambda i: (0, 0)),
            pl.BlockSpec((ROUTER_ROWS, LANES), lambda i: (0, 0)),
        ],
        out_specs=[
            pl.BlockSpec((tm, slab, LANES), lambda i: (i, 0, 0)),
            pl.BlockSpec((META_ROWS, tm), lambda i: (0, i)),
            pl.BlockSpec((ROUTER_ROWS, LANES), lambda i: (0, 0)),
        ],
        out_shape=[jax.ShapeDtypeStruct((m, slab, LANES), BF16), jax.ShapeDtypeStruct((META_ROWS, m), F32),
                   jax.ShapeDtypeStruct((ROUTER_ROWS, LANES), F32)],
        scratch_shapes=[pltpu.VMEM((ROUTER_ROWS, LANES), F32)],
        compiler_params=_cparams(("arbitrary",)),
        name="router",
    )(h, g, wr, br)


PAD_PIECES = tuple(PAD_UNIT >> (b + 1) for b in range(PAD_UNIT.bit_length() - 1))


def _dispatch_kernel(d1_ref, d2_ref, fill_ref, npad_ref, nb_ref, a_ref, xs_hbm, slot_ref, st0, st1, zbuf, sem, zsem):
    i = pl.program_id(0)
    n_steps = pl.num_programs(0)
    tm = a_ref.shape[0]
    bufs = (st0, st1)
    rb = zbuf.shape[0]
    n_blocks = xs_hbm.shape[0] // rb
    min_blocks = (2 * tm * n_steps) // rb

    def zero_copies():
        out = []
        for e in range(N_EXPERTS):
            p = npad_ref[e]
            for piece in PAD_PIECES:
                out.append((p & piece != 0, pltpu.make_async_copy(
                    zbuf.at[pl.ds(0, piece)], xs_hbm.at[pl.ds(fill_ref[e] + (p & -(2 * piece)), piece)], zsem.at[0])))
        for b in range(min_blocks, n_blocks):
            out.append((b >= nb_ref[0], pltpu.make_async_copy(zbuf, xs_hbm.at[pl.ds(b * rb, rb)], zsem.at[0])))
        return out

    def wait_rows(s):
        for _ in range(2):
            pltpu.make_async_copy(bufs[s], xs_hbm.at[pl.ds(0, tm)], sem.at[s]).wait()

    @pl.when(i == 0)
    def _():
        def mark_unused(j, carry):
            slot_ref[j] = -1
            return carry
        for e in range(N_EXPERTS):
            lax.fori_loop(fill_ref[e], fill_ref[e] + npad_ref[e], mark_unused, 0)
        lax.fori_loop(nb_ref[0] * rb, slot_ref.shape[0], mark_unused, 0)
        zbuf[...] = jnp.zeros_like(zbuf)
        for cond, cp in zero_copies():
            @pl.when(cond)
            def _():
                cp.start()

    for s in (0, 1):
        @pl.when(lax.rem(i, 2) == s)
        def _():
            @pl.when(i >= 2)
            def _():
                wait_rows(s)

            bufs[s][...] = a_ref[...]
            for r in range(tm):
                tok = i * tm + r
                d1 = d1_ref[tok]
                d2 = d2_ref[tok]
                slot_ref[d1] = 2 * tok
                slot_ref[d2] = 2 * tok + 1
                pltpu.make_async_copy(bufs[s].at[r], xs_hbm.at[d1], sem.at[s]).start(priority=0)
                pltpu.make_async_copy(bufs[s].at[r], xs_hbm.at[d2], sem.at[s]).start(priority=1)

    @pl.when(i == n_steps - 1)
    def _():
        for s in (0, 1):
            @pl.when((lax.rem(n_steps - 1, 2) == s) | ((n_steps >= 2) & (lax.rem(n_steps, 2) == s)))
            def _():
                wait_rows(s)
        for cond, cp in zero_copies():
            @pl.when(cond)
            def _():
                cp.wait()


def _dispatch(dest1, dest2, fill, npad, n_blk, a, *, n_rows, tm):
    m, slab, _ = a.shape
    grid_spec = pltpu.PrefetchScalarGridSpec(
        num_scalar_prefetch=5,
        grid=(m // tm,),
        in_specs=[pl.BlockSpec((tm, slab, LANES), lambda i, *_: (i, 0, 0))],
        out_specs=[pl.BlockSpec(memory_space=pl.ANY), pl.BlockSpec(memory_space=pltpu.SMEM)],
        scratch_shapes=[pltpu.VMEM((tm, slab, LANES), a.dtype), pltpu.VMEM((tm, slab, LANES), a.dtype),
                        pltpu.VMEM((PAD_UNIT, slab, LANES), a.dtype),
                        pltpu.SemaphoreType.DMA((2,)), pltpu.SemaphoreType.DMA((1,))],
    )
    return pl.pallas_call(
        _dispatch_kernel,
        grid_spec=grid_spec,
        out_shape=[jax.ShapeDtypeStruct((n_rows, slab, LANES), a.dtype), jax.ShapeDtypeStruct((n_rows,), jnp.int32)],
        compiler_params=_cparams(("arbitrary",)),
        name="dispatch",
    )(dest1, dest2, fill, npad, n_blk, a)


def _weight_copies(w_hbm, e, wst_ref, sem):
    rows_per = w_hbm.shape[1] // WEIGHT_CHUNKS
    return [pltpu.make_async_copy(w_hbm.at[e, pl.ds(c * rows_per, rows_per), :],
                                  wst_ref.at[pl.ds(c * rows_per, rows_per), :], sem) for c in range(WEIGHT_CHUNKS)]


def _load_expert_weights(i, be_ref, nx_ref, w_hbms, wst_refs, wb_refs, wsem):
    def start(t, e):
        for cp in _weight_copies(w_hbms[t], e, wst_refs[t], wsem.at[t]):
            cp.start(priority=WEIGHT_DMA_PRIORITY)

    @pl.when(i == 0)
    def _():
        for t in range(len(w_hbms)):
            start(t, be_ref[0])

    @pl.when((i == 0) | (be_ref[i] != be_ref[jnp.maximum(i - 1, 0)]))
    def _():
        for t, (wst_ref, wb_ref) in enumerate(zip(wst_refs, wb_refs)):
            for cp in _weight_copies(w_hbms[t], be_ref[i], wst_ref, wsem.at[t]):
                cp.wait()
            _for_row_chunks(wst_ref.shape[0], lambda rows: wb_ref.__setitem__((rows, slice(None)),
                                                                              wst_ref[rows, :].astype(BF16)))

            @pl.when(nx_ref[i] >= 0)
            def _():
                start(t, nx_ref[i])


def _expert_kernel(slot_ref, be_ref, nx_ref, off_ref, half_ref, nb_ref, x_ref, wg_hbm, wu_hbm, wd_hbm, y_hbm, wsg, wsu, wsd,
                   wgb, wub, wdb, ys0, ys1, sem, wsem, *, n_tok):
    i = pl.program_id(0)
    n_steps = pl.num_programs(0)
    nb = nb_ref[0]
    rb = x_ref.shape[0]
    bufs = (ys0, ys1)

    def wait_block(buf, s):
        pltpu.make_async_copy(buf, y_hbm.at[pl.ds(0, rb)], sem.at[s]).wait()

    def scatter_previous(s):
        o = 1 - s
        prev = jnp.maximum(i - 1, 0)
        used = jnp.where(i >= 1, rb - half_ref[prev] * (rb // 2), 0)
        for r in range(rb):
            v = jnp.where(r < used, slot_ref[off_ref[prev] + r], -1)
            dst = jnp.where(v >= 0, (v & 1) * n_tok + (v >> 1), 2 * n_tok + o * rb + r)
            pltpu.make_async_copy(bufs[o].at[r], y_hbm.at[dst], sem.at[o]).start()

    @pl.when(i == 0)
    def _():
        ys0[...] = jnp.zeros_like(ys0)
        ys1[...] = jnp.zeros_like(ys1)
        spare = pltpu.make_async_copy(ys0, y_hbm.at[pl.ds(2 * n_tok, rb)], sem.at[0])
        spare.start()
        spare.wait()

    for s in (0, 1):
        for rows in (rb, rb // 2):
            @pl.when((i < nb) & (lax.rem(i, 2) == s) & (half_ref[jnp.minimum(i, n_steps - 2)] == (rows != rb)))
            def _():
                _load_expert_weights(i, be_ref, nx_ref, (wg_hbm, wu_hbm, wd_hbm), (wsg, wsu, wsd), (wgb, wub, wdb),
                                     wsem)

                @pl.when(i >= 1)
                def _():
                    wait_block(bufs[s], s)

                scatter_previous(s)
                x = x_ref[0:rows].reshape(rows, -1)
                hg = jnp.dot(x, wgb[...], preferred_element_type=F32)
                hu = jnp.dot(x, wub[...], preferred_element_type=F32)
                hb = (hg * _sigmoid(hg) * hu).astype(BF16)
                y = jnp.dot(hb, wdb[...], preferred_element_type=F32)
                bufs[s][0:rows] = y.astype(bufs[s].dtype).reshape((rows,) + bufs[s].shape[1:])

        @pl.when((i == nb) & (lax.rem(i, 2) == s))
        def _():
            scatter_previous(s)

    @pl.when(i == n_steps - 1)
    def _():
        for s in (0, 1):
            wait_block(bufs[s], s)


def _experts(slot, block_e, next_e, block_off, block_half, n_blk, xs, wg, wu, wd, *, rb, n_tok):
    _, slab, _ = xs.shape
    d = slab * LANES
    de = wg.shape[2]
    n_blocks = block_e.shape[0]

    def x_rows(i, sl, be, nx, off, hf, nb):
        return (pl.multiple_of(off[jnp.minimum(i, nb[0] - 1)], PAD_UNIT), 0, 0)

    grid_spec = pltpu.PrefetchScalarGridSpec(
        num_scalar_prefetch=6,
        grid=(n_blocks + 1,),
        in_specs=[pl.BlockSpec((pl.Element(rb), pl.Element(slab), pl.Element(LANES)), x_rows)]
        + [pl.BlockSpec(memory_space=pl.ANY)] * 3,
        out_specs=pl.BlockSpec(memory_space=pl.ANY),
        scratch_shapes=[pltpu.VMEM((d, de), F32), pltpu.VMEM((d, de), F32), pltpu.VMEM((de, d), F32),
                        pltpu.VMEM((d, de), BF16), pltpu.VMEM((d, de), BF16), pltpu.VMEM((de, d), BF16),
                        pltpu.VMEM((rb, slab, LANES), BF16), pltpu.VMEM((rb, slab, LANES), BF16),
                        pltpu.SemaphoreType.DMA((2,)), pltpu.SemaphoreType.DMA((3,))],
    )
    return pl.pallas_call(
        functools.partial(_expert_kernel, n_tok=n_tok),
        grid_spec=grid_spec,
        out_shape=jax.ShapeDtypeStruct((2 * n_tok + 2 * rb, slab, LANES), BF16),
        compiler_params=_cparams(("arbitrary",), vmem=EXPERT_VMEM_LIMIT),
        name="experts",
    )(slot, block_e, next_e, block_off, block_half, n_blk, xs, wg, wu, wd)


def _combine_kernel(y0_ref, y1_ref, h_ref, meta_ref, g_ref, o_ref):
    tm, d = h_ref.shape
    gates = meta_ref[...].T
    y0 = y0_ref[...].reshape(tm, d).astype(F32)
    y1 = y1_ref[...].reshape(tm, d).astype(F32)
    out = h_ref[...] + gates[:, 4:5] * y0 + gates[:, 5:6] * y1
    o_ref[...] = _rms(out, g_ref[...])


def _combine(y, h, meta, g, *, tm):
    m, d = h.shape
    slab = d // LANES
    nt = m // tm
    return pl.pallas_call(
        _combine_kernel,
        grid=(nt,),
        in_specs=[
            pl.BlockSpec((tm, slab, LANES), lambda i: (i, 0, 0)),
            pl.BlockSpec((tm, slab, LANES), lambda i: (nt + i, 0, 0)),
            pl.BlockSpec((tm, d), lambda i: (i, 0)),
            pl.BlockSpec((META_ROWS, tm), lambda i: (0, i)),
            pl.BlockSpec((1, d), lambda i: (0, 0)),
        ],
        out_specs=pl.BlockSpec((tm, d), lambda i: (i, 0)),
        out_shape=jax.ShapeDtypeStruct((m, d), F32),
        compiler_params=_cparams(("parallel",)),
        name="combine",
    )(y, y, h, meta, g)


def kernel(x, mem, norm_mix_g, w_in, w_gla_alpha_up, b_gla_alpha, gla_out_norm_g, hgrn_lb_logits, hgrn_out_norm_g, w_mix_out, norm_xattn_g, norm_mem_g, w_xattn_q, w_xattn_kv, w_xattn_out, norm_ffn_g, w_router_group, b_router_group, w_router_expert, b_router_expert, w_expert_gate, w_expert_up, w_expert_down, norm_final_g):
    batch, seq, d = x.shape
    n_mem = mem.shape[1]
    m = batch * seq
    depth = norm_mix_g.shape[0]
    t = _tiles(m, seq)
    h = x.reshape(m, d)
    lb_all = jnp.cumsum(jax.nn.softmax(hgrn_lb_logits.astype(F32), axis=0), axis=0)
    gla_cols = 2 * GLA_HEADS * GLA_DK + 2 * GLA_HEADS * GLA_DV
    lr_rank = w_gla_alpha_up.shape[1]

    for l in range(depth):
        w_t = jnp.swapaxes(w_in[l], 0, 1)
        w_lr_t = jnp.pad(w_t[gla_cols:gla_cols + lr_rank], ((0, LANES - lr_rank), (0, 0))).astype(BF16)
        w_up = jnp.pad(w_gla_alpha_up[l], ((0, LANES - lr_rank), (0, 0))).astype(BF16)
        a_mix, la = _norm_gate(h, norm_mix_g[l][None, :], w_lr_t, w_up, b_gla_alpha[l][None, :], tm=t.norm_gate_rows)
        proj = _inproj(a_mix, w_t, tm=t.inproj_rows, tn=t.inproj_cols, skip_from=gla_cols, skip=lr_rank)
        lb = lb_all[l]
        lbc = jnp.zeros((8, lb.shape[0]), F32).at[0].set(jnp.log(lb)).at[1].set(jnp.log1p(-lb)).at[2].set(1.0 - lb)
        proj3 = proj.reshape(batch, seq, -1)
        o_gla = _gla(proj3, la.reshape(batch, seq, -1), gla_out_norm_g[l][None, :], tb=t.mixer_rows, hp=t.gla_heads)
        o_h = _hgrn(proj3, lbc, hgrn_out_norm_g[l][None, :], tb=t.mixer_rows, hp=t.hgrn_heads)
        o_gla, o_h = o_gla.reshape(m, -1), o_h.reshape(m, -1)
        h = _mm_res([o_gla, o_h], w_mix_out[l], h, tm=t.square_rows, tn=d, name="mix_out")

        kv = _norm_mm(mem.reshape(batch * n_mem, d), norm_mem_g[l][None, :], w_xattn_kv[l],
                      tm=batch * n_mem, tn=t.mem_kv_cols, out_dtype=BF16, name="mem_kv")
        o = _xattn(h, norm_xattn_g[l][None, :], w_xattn_q[l], kv, seq=seq, n_mem=n_mem, tq=t.xattn_rows)
        h = _mm_res([o], w_xattn_out[l], h, tm=t.square_rows, tn=d, name="xattn_out")

        n_logits = N_GROUPS + N_EXPERTS
        wr = jnp.pad(jnp.concatenate([w_router_group[l], w_router_expert[l]], axis=1).T,
                     ((0, ROUTER_ROWS - n_logits), (0, 0)))
        br = jnp.pad(jnp.concatenate([b_router_group[l], b_router_expert[l]]), (0, ROUTER_ROWS - n_logits))
        a, meta, cnt = _router(h, norm_ffn_g[l][None, :], wr, jnp.broadcast_to(br[:, None], (ROUTER_ROWS, LANES)),
                               tm=t.router_rows)

        e_idx = meta[0:2].astype(jnp.int32)
        rank = meta[2:4].astype(jnp.int32)
        counts = cnt[N_GROUPS:N_GROUPS + N_EXPERTS, 0].astype(jnp.int32)
        padded = ((counts + PAD_UNIT - 1) // PAD_UNIT) * PAD_UNIT
        pad_end = jnp.cumsum(padded)
        pad_start = pad_end - padded
        n_rows = 2 * m + N_EXPERTS * PAD_UNIT + PAD_UNIT
        n_units = (pad_end[-1:] // PAD_UNIT).astype(jnp.int32)
        n_full = padded // EXPERT_BLOCK
        n_blk_e = n_full + (padded % EXPERT_BLOCK) // PAD_UNIT
        blk_end = jnp.cumsum(n_blk_e)
        blk_start = blk_end - n_blk_e
        n_blocks = (2 * m) // EXPERT_BLOCK + N_EXPERTS
        n_blk = blk_end[-1:].astype(jnp.int32)
        blk = jnp.arange(n_blocks, dtype=jnp.int32)
        block_e = jnp.minimum(jnp.sum((blk_end[None, :] <= blk[:, None]).astype(jnp.int32), axis=1), N_EXPERTS - 1)
        blk_local = blk - blk_start[block_e]
        block_off = (pad_start[block_e] + EXPERT_BLOCK * blk_local).astype(jnp.int32)
        block_off = jnp.where(blk < n_blk[0], block_off, 0)
        block_half = (blk_local >= n_full[block_e]).astype(jnp.int32)
        after = blk_end[block_e]
        next_e = jnp.where(after < n_blk[0], block_e[jnp.minimum(after, n_blocks - 1)], -1).astype(jnp.int32)

        experts = jnp.arange(N_EXPERTS, dtype=jnp.int32)
        dest = [jnp.sum(jnp.where(e_idx[k][:, None] == experts[None, :], pad_start[None, :], 0), axis=1) + rank[k]
                for k in range(2)]
        xs, slot = _dispatch(dest[0], dest[1], pad_start + counts, padded - counts, n_units, a, n_rows=n_rows,
                             tm=t.dispatch_rows)
        y = _experts(slot, block_e, next_e, block_off, block_half, n_blk, xs, w_expert_gate[l], w_expert_up[l],
                     w_expert_down[l], rb=EXPERT_BLOCK, n_tok=m)
        last = l == depth - 1
        g_fin = norm_final_g[None, :] if last else jnp.ones((1, d), F32)
        h = _combine(y, h, meta, g_fin, tm=t.combine_rows)
        assert last, "the combine kernel fuses the final rmsnorm; deeper stacks need an un-normalised variant"

    return h.reshape(batch, seq, d)
```

```python
import functools
from typing import NamedTuple

import jax
import jax.numpy as jnp
from jax import lax
from jax.experimental import pallas as pl
from jax.experimental.pallas import tpu as pltpu

F32 = jnp.float32
BF16 = jnp.bfloat16

EPS = 1e-6
CHUNK = 64
LANES = 128
GLA_HEADS, GLA_DK, GLA_DV = 4, 128, 256
HGRN_HEADS, HGRN_DH = 8, 128
XA_HEADS, XA_DH = 4, 512
N_GROUPS, EXPERTS_PER_GROUP, N_EXPERTS = 4, 8, 32
EXPERT_BLOCK = 256
PAD_UNIT = 128
WEIGHT_CHUNKS = 8
WEIGHT_DMA_PRIORITY = 1
VMEM_LIMIT = 56 * 1024 * 1024
EXPERT_VMEM_LIMIT = 60 * 1024 * 1024


class Tiles(NamedTuple):
    norm_gate_rows: int
    inproj_rows: int
    inproj_cols: int
    mixer_rows: int
    gla_heads: int
    hgrn_heads: int
    square_rows: int
    mem_kv_cols: int
    xattn_rows: int
    router_rows: int
    dispatch_rows: int
    combine_rows: int


def _tiles(m, seq):
    return Tiles(norm_gate_rows=min(m, 1024), inproj_rows=min(m, 4096), inproj_cols=512, mixer_rows=min(seq, 256),
                 gla_heads=GLA_HEADS, hgrn_heads=HGRN_HEADS, square_rows=min(m, 512), mem_kv_cols=1024,
                 xattn_rows=min(seq, 512), router_rows=min(m, 512), dispatch_rows=min(m, 256),
                 combine_rows=min(m, 512))


def _cparams(sem, vmem=VMEM_LIMIT):
    return pltpu.CompilerParams(dimension_semantics=sem, vmem_limit_bytes=vmem)


def _log1pexp_neg(t):
    return jnp.log(1.0 + jnp.exp(-t))


def _log_sigmoid(z):
    return jnp.minimum(z, 0.0) - _log1pexp_neg(jnp.abs(z))


def _sigmoid(z):
    return 1.0 / (1.0 + jnp.exp(-z))


def _rms(x, g):
    return x * lax.rsqrt(jnp.mean(x * x, axis=-1, keepdims=True) + EPS) * g


def _nt_dot(x, y):
    return lax.dot_general(x, y, (((1,), (1,)), ((), ())), preferred_element_type=F32)


NORM_ROWS = 256


def _for_row_chunks(n_rows, fn):
    step = min(NORM_ROWS, n_rows)

    def body(ci, carry):
        fn(pl.ds(pl.multiple_of(ci * step, step), step))
        return carry

    lax.fori_loop(0, n_rows // step, body, 0)


def _norm_gate_kernel(x_ref, g_ref, wlr_ref, wup_ref, bal_ref, a_ref, la_ref):
    a = _rms(x_ref[...], g_ref[...]).astype(BF16)
    a_ref[...] = a
    lr = _nt_dot(a, wlr_ref[...])
    z = jnp.dot(lr.astype(BF16), wup_ref[...], preferred_element_type=F32) + bal_ref[...]
    la_ref[...] = _log_sigmoid(z) * (1.0 / 16.0)


def _norm_gate(x, g, w_lr_t, w_up, b_al, *, tm):
    m, d = x.shape
    nk = w_up.shape[1]
    return pl.pallas_call(
        _norm_gate_kernel,
        grid=(m // tm,),
        in_specs=[
            pl.BlockSpec((tm, d), lambda i: (i, 0)),
            pl.BlockSpec((1, d), lambda i: (0, 0)),
            pl.BlockSpec((LANES, d), lambda i: (0, 0)),
            pl.BlockSpec((LANES, nk), lambda i: (0, 0)),
            pl.BlockSpec((1, nk), lambda i: (0, 0)),
        ],
        out_specs=[pl.BlockSpec((tm, d), lambda i: (i, 0)), pl.BlockSpec((tm, nk), lambda i: (i, 0))],
        out_shape=[jax.ShapeDtypeStruct((m, d), BF16), jax.ShapeDtypeStruct((m, nk), F32)],
        compiler_params=_cparams(("parallel",)),
        name="norm_gate",
    )(x, g, w_lr_t, w_up, b_al)


def _inproj_kernel(a_ref, w_ref, proj_ref):
    proj_ref[...] = _nt_dot(a_ref[...], w_ref[...].astype(BF16))


def _inproj(a, w_t, *, tm, tn, skip_from, skip):
    m, d = a.shape
    n = w_t.shape[0] - skip
    first_after = skip_from // tn

    def w_rows(i, j):
        return (pl.multiple_of(j * tn + jnp.where(j >= first_after, skip, 0), 8), 0)

    return pl.pallas_call(
        _inproj_kernel,
        grid=(m // tm, n // tn),
        in_specs=[
            pl.BlockSpec((tm, d), lambda i, j: (i, 0), pipeline_mode=pl.Buffered(1)),
            pl.BlockSpec((pl.Element(tn), pl.Element(d)), w_rows),
        ],
        out_specs=pl.BlockSpec((tm, tn), lambda i, j: (i, j)),
        out_shape=jax.ShapeDtypeStruct((m, n), F32),
        compiler_params=_cparams(("parallel", "arbitrary")),
        name="inproj",
    )(a, w_t)


def _weight_spec(k, n, tn):
    if tn == n:
        return pl.BlockSpec((k, n), lambda i, j: (0, 0), pipeline_mode=pl.Buffered(1))
    return pl.BlockSpec((k, tn), lambda i, j: (0, j))


def _cast_weight(w_ref, wb_ref, resident):
    if resident:
        @pl.when((pl.program_id(0) == 0) & (pl.program_id(1) == 0))
        def _():
            wb_ref[...] = w_ref[...].astype(BF16)
    else:
        wb_ref[...] = w_ref[...].astype(BF16)


def _norm_mm_kernel(x_ref, g_ref, w_ref, o_ref, a_ref, wb_ref, *, resident):
    @pl.when(pl.program_id(1) == 0)
    def _():
        def rows_fn(rows):
            a_ref[rows, :] = _rms(x_ref[rows, :], g_ref[...]).astype(BF16)

        _for_row_chunks(x_ref.shape[0], rows_fn)

    _cast_weight(w_ref, wb_ref, resident)
    o_ref[...] = jnp.dot(a_ref[...], wb_ref[...], preferred_element_type=F32).astype(o_ref.dtype)


def _norm_mm(x, g, w, *, tm, tn, out_dtype, name):
    m, d = x.shape
    n = w.shape[1]
    return pl.pallas_call(
        functools.partial(_norm_mm_kernel, resident=tn == n),
        grid=(m // tm, n // tn),
        in_specs=[
            pl.BlockSpec((tm, d), lambda i, j: (i, 0)),
            pl.BlockSpec((1, d), lambda i, j: (0, 0)),
            _weight_spec(d, n, tn),
        ],
        out_specs=pl.BlockSpec((tm, tn), lambda i, j: (i, j)),
        out_shape=jax.ShapeDtypeStruct((m, n), out_dtype),
        scratch_shapes=[pltpu.VMEM((tm, d), BF16), pltpu.VMEM((d, tn), BF16)],
        compiler_params=_cparams(("arbitrary", "arbitrary")),
        name=name,
    )(x, g, w)


def _mm_res_kernel(*refs, n_lhs, resident):
    lhs = refs[:n_lhs]
    w_ref, res_ref, o_ref, wb_ref = refs[n_lhs:]
    _cast_weight(w_ref, wb_ref, resident)
    acc = res_ref[...]
    k0 = 0
    for l_ref in lhs:
        kp = l_ref.shape[1]
        acc = acc + jnp.dot(l_ref[...], wb_ref[k0:k0 + kp, :], preferred_element_type=F32)
        k0 += kp
    o_ref[...] = acc


def _mm_res(lhs_parts, w, res, *, tm, tn, name):
    m, n = res.shape
    k = w.shape[0]
    n_lhs = len(lhs_parts)
    in_specs = [pl.BlockSpec((tm, p.shape[1]), lambda i, j: (i, 0)) for p in lhs_parts]
    in_specs += [_weight_spec(k, n, tn), pl.BlockSpec((tm, tn), lambda i, j: (i, j))]
    return pl.pallas_call(
        functools.partial(_mm_res_kernel, n_lhs=n_lhs, resident=tn == n),
        grid=(m // tm, n // tn),
        in_specs=in_specs,
        out_specs=pl.BlockSpec((tm, tn), lambda i, j: (i, j)),
        out_shape=jax.ShapeDtypeStruct((m, n), F32),
        scratch_shapes=[pltpu.VMEM((k, tn), BF16)],
        compiler_params=_cparams(("arbitrary", "arbitrary")),
        name=name,
    )(*lhs_parts, w, res)


LEVELS = (32, 16, 8, 4, 2, 1)
LOG2E = 1.4426950408889634


def _split3_bf16(x):
    def top(v):
        bits = lax.bitcast_convert_type(v, jnp.uint32) & jnp.uint32(0xFFFF0000)
        return lax.bitcast_convert_type(bits, F32)
    hi = top(x)
    r1 = x - hi
    mid = top(r1)
    lo = r1 - mid
    return hi.astype(BF16), mid.astype(BF16), lo.astype(BF16)


def _mix_chunks(qs, ks, vs, las, st_refs):
    heads = range(len(qs))
    c, dk = qs[0].shape
    row = lax.broadcasted_iota(jnp.int32, (c, c), 0)
    col = lax.broadcasted_iota(jnp.int32, (c, c), 1)
    rowk = lax.broadcasted_iota(jnp.int32, (c, dk), 0)
    xor = jnp.bitwise_xor(row, col)

    las = [la * LOG2E for la in las]

    tri = jnp.where(col <= row, 1.0, 0.0).astype(BF16)
    b3 = [jnp.dot(tri, jnp.concatenate(_split3_bf16(la), axis=1), preferred_element_type=F32) for la in las]
    bs = [(t[:, :dk] + t[:, dk:2 * dk]) + t[:, 2 * dk:] for t in b3]

    def neg_dist(w, b, la):
        if w >= 4:
            parts = [jnp.broadcast_to(b[base + w - 1:base + w, :], (2 * w, dk)) for base in range(0, c, 2 * w)]
            m = parts[0] if len(parts) == 1 else jnp.concatenate(parts, axis=0)
            return -jnp.abs(b - m)
        if w == 2:
            r4 = jnp.bitwise_and(rowk, 3)
            nxt = pltpu.roll(la, c - 1, 0)
            prv = pltpu.roll(la, 1, 0)
            return jnp.where(r4 == 0, nxt, jnp.where(r4 == 1, 0.0, jnp.where(r4 == 2, la, la + prv)))
        return jnp.where(jnp.bitwise_and(rowk, 1) == 1, la, 0.0)

    acc = [_nt_dot(qs[h].astype(BF16), ks[h].astype(BF16)) for h in heads]
    for w in reversed(LEVELS):
        upper = jnp.bitwise_and(rowk, w) != 0
        xs = [(jnp.where(upper, qs[h], ks[h]) * jnp.exp2(neg_dist(w, bs[h], las[h]))).astype(BF16) for h in heads]
        gs = [_nt_dot(x, x) for x in xs]
        acc = [jnp.where(xor >= w, gs[h], acc[h]) for h in heads]
    ab = [jnp.where(col <= row, a, 0.0).astype(BF16) for a in acc]

    sts = [st_refs[h][...] for h in heads]
    qx = [(qs[h] * jnp.exp2(bs[h])).astype(BF16) for h in heads]
    b_last = [b[c - 1:c, :] for b in bs]
    kx = [(ks[h] * jnp.exp2(b_last[h] - bs[h])).astype(BF16) for h in heads]
    vb = [v.astype(BF16) for v in vs]
    outs = [jnp.dot(ab[h], vb[h], preferred_element_type=F32) + _nt_dot(qx[h], sts[h].astype(BF16)) for h in heads]
    for h in heads:
        st_refs[h][...] = jnp.exp2(b_last[h]) * sts[h] + lax.dot_general(
            vb[h], kx[h], (((0,), (0,)), ((), ())), preferred_element_type=F32)
    return outs


def _gated_norm(o, g, gn):
    return _rms(o, gn) * (g * _sigmoid(g))


def _gla_kernel(q_ref, k_ref, v_ref, g_ref, la_ref, gn_ref, o_ref, st_ref, *, n_chunk):
    @pl.when(pl.program_id(1) == 0)
    def _():
        st_ref[...] = jnp.zeros_like(st_ref)

    dk, dv = GLA_DK, GLA_DV
    n_b = q_ref.shape[0]
    n_h = st_ref.shape[0] // n_b
    streams = [(b, h) for b in range(n_b) for h in range(n_h)]
    kc = [slice(h * dk, (h + 1) * dk) for h in range(n_h)]
    vc = [slice(h * dv, (h + 1) * dv) for h in range(n_h)]

    def body(ci, carry):
        rows = pl.ds(pl.multiple_of(ci * CHUNK, CHUNK), CHUNK)
        outs = _mix_chunks([q_ref[b, rows, kc[h]] * (dk ** -0.5) for b, h in streams],
                           [k_ref[b, rows, kc[h]] for b, h in streams], [v_ref[b, rows, vc[h]] for b, h in streams],
                           [la_ref[b, rows, kc[h]] for b, h in streams],
                           [st_ref.at[n] for n in range(len(streams))])
        for n, (b, h) in enumerate(streams):
            o_ref[b, rows, vc[h]] = _gated_norm(outs[n], g_ref[b, rows, vc[h]], gn_ref[...]).astype(o_ref.dtype)
        return carry

    lax.fori_loop(0, n_chunk, body, 0)


def _gla(proj, la, gn, *, tb, hp):
    batch, seq, _ = proj.shape
    wk, wv = hp * GLA_DK, hp * GLA_DV
    n_grp = GLA_HEADS // hp
    spec = lambda width, first: pl.BlockSpec((batch, tb, width), lambda p, t: (0, t, first + p))
    return pl.pallas_call(
        functools.partial(_gla_kernel, n_chunk=tb // CHUNK),
        grid=(n_grp, seq // tb),
        in_specs=[spec(wk, 0), spec(wk, n_grp), spec(wv, n_grp), spec(wv, 2 * n_grp), spec(wk, 0),
                  pl.BlockSpec((1, GLA_DV), lambda p, t: (0, 0))],
        out_specs=spec(wv, 0),
        out_shape=jax.ShapeDtypeStruct((batch, seq, GLA_HEADS * GLA_DV), BF16),
        scratch_shapes=[pltpu.VMEM((batch * hp, GLA_DV, GLA_DK), F32)],
        compiler_params=_cparams(("parallel", "arbitrary")),
        name="gla",
    )(proj, proj, proj, proj, la, gn)


def _hgrn_kernel(q_ref, f_ref, i_ref, g_ref, lb_ref, gn_ref, o_ref, st_ref, *, n_chunk):
    @pl.when(pl.program_id(1) == 0)
    def _():
        st_ref[...] = jnp.zeros_like(st_ref)

    dh = HGRN_DH
    n_b = q_ref.shape[0]
    n_h = st_ref.shape[0] // n_b
    streams = [(b, h) for b in range(n_b) for h in range(n_h)]
    hc = [slice(h * dh, (h + 1) * dh) for h in range(n_h)]

    def body(ci, carry):
        rows = pl.ds(pl.multiple_of(ci * CHUNK, CHUNK), CHUNK)
        qs, ks, las = [], [], []
        for b, h in streams:
            log_lb, log_1mlb, one_m_lb = lb_ref[0:1, hc[h]], lb_ref[1:2, hc[h]], lb_ref[2:3, hc[h]]
            hq = q_ref[b, rows, hc[h]]
            z = f_ref[b, rows, hc[h]]
            x1 = log_1mlb + _log_sigmoid(z)
            las.append(jnp.maximum(log_lb, x1) + _log1pexp_neg(jnp.abs(log_lb - x1)))
            qs.append(hq * _sigmoid(hq))
            ks.append(one_m_lb * _sigmoid(-z))
        outs = _mix_chunks(qs, ks, [i_ref[b, rows, hc[h]] for b, h in streams], las,
                           [st_ref.at[n] for n in range(len(streams))])
        for n, (b, h) in enumerate(streams):
            o_ref[b, rows, hc[h]] = _gated_norm(outs[n], g_ref[b, rows, hc[h]], gn_ref[...]).astype(o_ref.dtype)
        return carry

    lax.fori_loop(0, n_chunk, body, 0)


def _hgrn(proj, lbc, gn, *, tb, hp):
    batch, seq, _ = proj.shape
    width = hp * HGRN_DH
    n_grp = HGRN_HEADS // hp
    first = 3072 // width
    spec = lambda seg: pl.BlockSpec((batch, tb, width), lambda p, t: (0, t, first + seg * n_grp + p))
    return pl.pallas_call(
        functools.partial(_hgrn_kernel, n_chunk=tb // CHUNK),
        grid=(n_grp, seq // tb),
        in_specs=[spec(0), spec(1), spec(2), spec(3),
                  pl.BlockSpec((8, width), lambda p, t: (0, p)),
                  pl.BlockSpec((1, HGRN_DH), lambda p, t: (0, 0))],
        out_specs=pl.BlockSpec((batch, tb, width), lambda p, t: (0, t, p)),
        out_shape=jax.ShapeDtypeStruct((batch, seq, HGRN_HEADS * HGRN_DH), BF16),
        scratch_shapes=[pltpu.VMEM((batch * hp, HGRN_DH, HGRN_DH), F32)],
        compiler_params=_cparams(("parallel", "arbitrary")),
        name="hgrn",
    )(proj, proj, proj, proj, lbc, gn)


def _xattn_kernel(x_ref, g_ref, w_ref, k_ref, v_ref, o_ref, wb_ref):
    @pl.when(pl.program_id(0) == 0)
    def _():
        _for_row_chunks(w_ref.shape[0], lambda rows: wb_ref.__setitem__((rows, slice(None)),
                                                                        w_ref[rows, :].astype(BF16)))

    a = _rms(x_ref[...], g_ref[...]).astype(BF16)
    q = jnp.dot(a, wb_ref[...], preferred_element_type=F32).astype(BF16)
    cols = [slice(h * XA_DH, (h + 1) * XA_DH) for h in range(XA_HEADS)]
    s = [_nt_dot(q[:, c], k_ref[:, c]) * (XA_DH ** -0.5) for c in cols]
    p = [jnp.exp(x - jnp.max(x, axis=-1, keepdims=True)) for x in s]
    p = [(x / jnp.sum(x, axis=-1, keepdims=True)).astype(BF16) for x in p]
    for c, x in zip(cols, p):
        o_ref[:, c] = jnp.dot(x, v_ref[:, c], preferred_element_type=F32).astype(o_ref.dtype)


def _xattn(x, g, w_q, kv, *, seq, n_mem, tq):
    m, d = x.shape
    nt = seq // tq
    return pl.pallas_call(
        _xattn_kernel,
        grid=(m // tq,),
        in_specs=[
            pl.BlockSpec((tq, d), lambda i: (i, 0)),
            pl.BlockSpec((1, d), lambda i: (0, 0)),
            pl.BlockSpec((d, d), lambda i: (0, 0), pipeline_mode=pl.Buffered(1)),
            pl.BlockSpec((n_mem, d), lambda i: (i // nt, 0)),
            pl.BlockSpec((n_mem, d), lambda i: (i // nt, 1)),
        ],
        out_specs=pl.BlockSpec((tq, d), lambda i: (i, 0)),
        out_shape=jax.ShapeDtypeStruct((m, d), BF16),
        scratch_shapes=[pltpu.VMEM((d, d), BF16)],
        compiler_params=_cparams(("arbitrary",)),
        name="xattn",
    )(x, g, w_q, kv, kv)


ROUTER_ROWS = 40
META_ROWS = 8


def _router_kernel(h_ref, g_ref, wr_ref, br_ref, a_ref, meta_ref, cnt_ref, carry_ref):
    tm = h_ref.shape[0]
    nr = wr_ref.shape[0]

    @pl.when(pl.program_id(0) == 0)
    def _():
        carry_ref[...] = jnp.zeros_like(carry_ref)

    a = _rms(h_ref[...], g_ref[...])
    a_ref[...] = a.astype(BF16).reshape(a_ref.shape)
    a_hi = a.astype(BF16)
    a_lo = (a - a_hi.astype(F32)).astype(BF16)
    w = wr_ref[...]
    w_hi = w.astype(BF16)
    w_lo = (w - w_hi.astype(F32)).astype(BF16)
    logits = (_nt_dot(w_hi, a_hi) + _nt_dot(w_hi, a_lo) + _nt_dot(w_lo, a_hi)) + br_ref[:, 0:1]
    row = lax.broadcasted_iota(jnp.int32, (nr, tm), 0)
    row_f = row.astype(F32)
    neg = -jnp.inf

    def first_max(x):
        v = jnp.max(x, axis=0, keepdims=True)
        return v, jnp.min(jnp.where(x == v, row_f, float(nr)), axis=0, keepdims=True)

    gl = jnp.where(row < N_GROUPS, logits, neg)
    gmax, gidx = first_max(gl)
    p_group = 1.0 / jnp.sum(jnp.exp(gl - gmax), axis=0, keepdims=True)
    lo = float(N_GROUPS) + gidx * float(EXPERTS_PER_GROUP)
    el = jnp.where((row_f >= lo) & (row_f < lo + float(EXPERTS_PER_GROUP)), logits, neg)
    v1, i1 = first_max(el)
    v2, i2 = first_max(jnp.where(row_f == i1, neg, el))
    t = jnp.exp(v2 - v1)
    g1 = p_group / (1.0 + t)
    g2 = p_group * t / (1.0 + t)

    hit1 = row_f == i1
    hit2 = row_f == i2
    onehot = jnp.where(hit1 | hit2, 1.0, 0.0)
    src = lax.broadcasted_iota(jnp.int32, (tm, tm), 0)
    dst = lax.broadcasted_iota(jnp.int32, (tm, tm), 1)
    earlier = jnp.where(src < dst, 1.0, 0.0).astype(BF16)
    before = jnp.dot(onehot.astype(BF16), earlier, preferred_element_type=F32) + carry_ref[:, 0:1]
    r1 = jnp.sum(jnp.where(hit1, before, 0.0), axis=0, keepdims=True)
    r2 = jnp.sum(jnp.where(hit2, before, 0.0), axis=0, keepdims=True)
    carry_ref[...] = carry_ref[...] + jnp.sum(onehot, axis=1, keepdims=True)
    cnt_ref[...] = carry_ref[...]

    out_row = lax.broadcasted_iota(jnp.int32, meta_ref.shape, 0)
    meta = jnp.zeros(meta_ref.shape, F32)
    for idx, val in enumerate((i1 - float(N_GROUPS), i2 - float(N_GROUPS), r1, r2, g1, g2)):
        meta = jnp.where(out_row == idx, val, meta)
    meta_ref[...] = meta


def _router(h, g, wr, br, *, tm):
    m, d = h.shape
    slab = d // LANES
    return pl.pallas_call(
        _router_kernel,
        grid=(m // tm,),
        in_specs=[
            pl.BlockSpec((tm, d), lambda i: (i, 0)),
            pl.BlockSpec((1, d), lambda i: (0, 0)),
            pl.BlockSpec((ROUTER_ROWS, d), lambda i: (0, 0)),
            pl.BlockSpec((ROUTER_ROWS, LANES), lambda i: (0, 0)),
        ],
        out_specs=[
            pl.BlockSpec((tm, slab, LANES), lambda i: (i, 0, 0)),
            pl.BlockSpec((META_ROWS, tm), lambda i: (0, i)),
            pl.BlockSpec((ROUTER_ROWS, LANES), lambda i: (0, 0)),
        ],
        out_shape=[jax.ShapeDtypeStruct((m, slab, LANES), BF16), jax.ShapeDtypeStruct((META_ROWS, m), F32),
                   jax.ShapeDtypeStruct((ROUTER_ROWS, LANES), F32)],
        scratch_shapes=[pltpu.VMEM((ROUTER_ROWS, LANES), F32)],
        compiler_params=_cparams(("arbitrary",)),
        name="router",
    )(h, g, wr, br)


PAD_PIECES = tuple(PAD_UNIT >> (b + 1) for b in range(PAD_UNIT.bit_length() - 1))


def _dispatch_kernel(d1_ref, d2_ref, fill_ref, npad_ref, nb_ref, a_ref, xs_hbm, slot_ref, st0, st1, zbuf, sem, zsem):
    i = pl.program_id(0)
    n_steps = pl.num_programs(0)
    tm = a_ref.shape[0]
    bufs = (st0, st1)
    rb = zbuf.shape[0]
    n_blocks = xs_hbm.shape[0] // rb
    min_blocks = (2 * tm * n_steps) // rb

    def zero_copies():
        out = []
        for e in range(N_EXPERTS):
            p = npad_ref[e]
            for piece in PAD_PIECES:
                out.append((p & piece != 0, pltpu.make_async_copy(
                    zbuf.at[pl.ds(0, piece)], xs_hbm.at[pl.ds(fill_ref[e] + (p & -(2 * piece)), piece)], zsem.at[0])))
        for b in range(min_blocks, n_blocks):
            out.append((b >= nb_ref[0], pltpu.make_async_copy(zbuf, xs_hbm.at[pl.ds(b * rb, rb)], zsem.at[0])))
        return out

    def wait_rows(s):
        for _ in range(2):
            pltpu.make_async_copy(bufs[s], xs_hbm.at[pl.ds(0, tm)], sem.at[s]).wait()

    @pl.when(i == 0)
    def _():
        def mark_unused(j, carry):
            slot_ref[j] = -1
            return carry
        for e in range(N_EXPERTS):
            lax.fori_loop(fill_ref[e], fill_ref[e] + npad_ref[e], mark_unused, 0)
        lax.fori_loop(nb_ref[0] * rb, slot_ref.shape[0], mark_unused, 0)
        zbuf[...] = jnp.zeros_like(zbuf)
        for cond, cp in zero_copies():
            @pl.when(cond)
            def _():
                cp.start()

    for s in (0, 1):
        @pl.when(lax.rem(i, 2) == s)
        def _():
            @pl.when(i >= 2)
            def _():
                wait_rows(s)

            bufs[s][...] = a_ref[...]
            for r in range(tm):
                tok = i * tm + r
                d1 = d1_ref[tok]
                d2 = d2_ref[tok]
                slot_ref[d1] = 2 * tok
                slot_ref[d2] = 2 * tok + 1
                pltpu.make_async_copy(bufs[s].at[r], xs_hbm.at[d1], sem.at[s]).start(priority=0)
                pltpu.make_async_copy(bufs[s].at[r], xs_hbm.at[d2], sem.at[s]).start(priority=1)

    @pl.when(i == n_steps - 1)
    def _():
        for s in (0, 1):
            @pl.when((lax.rem(n_steps - 1, 2) == s) | ((n_steps >= 2) & (lax.rem(n_steps, 2) == s)))
            def _():
                wait_rows(s)
        for cond, cp in zero_copies():
            @pl.when(cond)
            def _():
                cp.wait()


def _dispatch(dest1, dest2, fill, npad, n_blk, a, *, n_rows, tm):
    m, slab, _ = a.shape
    grid_spec = pltpu.PrefetchScalarGridSpec(
        num_scalar_prefetch=5,
        grid=(m // tm,),
        in_specs=[pl.BlockSpec((tm, slab, LANES), lambda i, *_: (i, 0, 0))],
        out_specs=[pl.BlockSpec(memory_space=pl.ANY), pl.BlockSpec(memory_space=pltpu.SMEM)],
        scratch_shapes=[pltpu.VMEM((tm, slab, LANES), a.dtype), pltpu.VMEM((tm, slab, LANES), a.dtype),
                        pltpu.VMEM((PAD_UNIT, slab, LANES), a.dtype),
                        pltpu.SemaphoreType.DMA((2,)), pltpu.SemaphoreType.DMA((1,))],
    )
    return pl.pallas_call(
        _dispatch_kernel,
        grid_spec=grid_spec,
        out_shape=[jax.ShapeDtypeStruct((n_rows, slab, LANES), a.dtype), jax.ShapeDtypeStruct((n_rows,), jnp.int32)],
        compiler_params=_cparams(("arbitrary",)),
        name="dispatch",
    )(dest1, dest2, fill, npad, n_blk, a)


def _weight_copies(w_hbm, e, wst_ref, sem):
    rows_per = w_hbm.shape[1] // WEIGHT_CHUNKS
    return [pltpu.make_async_copy(w_hbm.at[e, pl.ds(c * rows_per, rows_per), :],
                                  wst_ref.at[pl.ds(c * rows_per, rows_per), :], sem) for c in range(WEIGHT_CHUNKS)]


def _load_expert_weights(i, be_ref, nx_ref, w_hbms, wst_refs, wb_refs, wsem, which):
    def start(t, e):
        for cp in _weight_copies(w_hbms[t], e, wst_refs[t], wsem.at[t]):
            cp.start(priority=WEIGHT_DMA_PRIORITY)

    @pl.when(i == 0)
    def _():
        for t in which:
            start(t, be_ref[0])

    @pl.when((i == 0) | (be_ref[i] != be_ref[jnp.maximum(i - 1, 0)]))
    def _():
        for t in which:
            wst_ref, wb_ref = wst_refs[t], wb_refs[t]
            for cp in _weight_copies(w_hbms[t], be_ref[i], wst_ref, wsem.at[t]):
                cp.wait()
            _for_row_chunks(wst_ref.shape[0], lambda rows: wb_ref.__setitem__((rows, slice(None)),
                                                                              wst_ref[rows, :].astype(BF16)))

            @pl.when(nx_ref[i] >= 0)
            def _():
                start(t, nx_ref[i])


def _expert_kernel(slot_ref, be_ref, nx_ref, off_ref, half_ref, nb_ref, x_ref, wg_hbm, wu_hbm, wd_hbm, y_hbm, wsg, wsu, wsd,
                   wgb, wub, wdb, ys0, ys1, sem, wsem, *, n_tok):
    i = pl.program_id(0)
    n_steps = pl.num_programs(0)
    nb = nb_ref[0]
    rb = x_ref.shape[0]
    bufs = (ys0, ys1)

    def wait_block(buf, s):
        pltpu.make_async_copy(buf, y_hbm.at[pl.ds(0, rb)], sem.at[s]).wait()

    def scatter_previous(s):
        o = 1 - s
        prev = jnp.maximum(i - 1, 0)
        used = jnp.where(i >= 1, rb - half_ref[prev] * (rb // 2), 0)
        for r in range(rb):
            v = jnp.where(r < used, slot_ref[off_ref[prev] + r], -1)
            dst = jnp.where(v >= 0, (v & 1) * n_tok + (v >> 1), 2 * n_tok + o * rb + r)
            pltpu.make_async_copy(bufs[o].at[r], y_hbm.at[dst], sem.at[o]).start()

    @pl.when(i == 0)
    def _():
        ys0[...] = jnp.zeros_like(ys0)
        ys1[...] = jnp.zeros_like(ys1)
        spare = pltpu.make_async_copy(ys0, y_hbm.at[pl.ds(2 * n_tok, rb)], sem.at[0])
        spare.start()
        spare.wait()

    for s in (0, 1):
        for rows in (rb, rb // 2):
            @pl.when((i < nb) & (lax.rem(i, 2) == s) & (half_ref[jnp.minimum(i, n_steps - 2)] == (rows != rb)))
            def _():
                weights = (i, be_ref, nx_ref, (wg_hbm, wu_hbm, wd_hbm), (wsg, wsu, wsd), (wgb, wub, wdb), wsem)
                _load_expert_weights(*weights, which=(0, 1))

                @pl.when(i >= 1)
                def _():
                    wait_block(bufs[s], s)

                scatter_previous(s)
                x = x_ref[0:rows].reshape(rows, -1)
                hg = jnp.dot(x, wgb[...], preferred_element_type=F32)
                hu = jnp.dot(x, wub[...], preferred_element_type=F32)
                hb = (hg * _sigmoid(hg) * hu).astype(BF16)
                _load_expert_weights(*weights, which=(2,))
                y = jnp.dot(hb, wdb[...], preferred_element_type=F32)
                bufs[s][0:rows] = y.astype(bufs[s].dtype).reshape((rows,) + bufs[s].shape[1:])

        @pl.when((i == nb) & (lax.rem(i, 2) == s))
        def _():
            scatter_previous(s)

    @pl.when(i == n_steps - 1)
    def _():
        for s in (0, 1):
            wait_block(bufs[s], s)


def _experts(slot, block_e, next_e, block_off, block_half, n_blk, xs, wg, wu, wd, *, rb, n_tok):
    _, slab, _ = xs.shape
    d = slab * LANES
    de = wg.shape[2]
    n_blocks = block_e.shape[0]

    def x_rows(i, sl, be, nx, off, hf, nb):
        return (pl.multiple_of(off[jnp.minimum(i, nb[0] - 1)], PAD_UNIT), 0, 0)

    grid_spec = pltpu.PrefetchScalarGridSpec(
        num_scalar_prefetch=6,
        grid=(n_blocks + 1,),
        in_specs=[pl.BlockSpec((pl.Element(rb), pl.Element(slab), pl.Element(LANES)), x_rows)]
        + [pl.BlockSpec(memory_space=pl.ANY)] * 3,
        out_specs=pl.BlockSpec(memory_space=pl.ANY),
        scratch_shapes=[pltpu.VMEM((d, de), F32), pltpu.VMEM((d, de), F32), pltpu.VMEM((de, d), F32),
                        pltpu.VMEM((d, de), BF16), pltpu.VMEM((d, de), BF16), pltpu.VMEM((de, d), BF16),
                        pltpu.VMEM((rb, slab, LANES), BF16), pltpu.VMEM((rb, slab, LANES), BF16),
                        pltpu.SemaphoreType.DMA((2,)), pltpu.SemaphoreType.DMA((3,))],
    )
    return pl.pallas_call(
        functools.partial(_expert_kernel, n_tok=n_tok),
        grid_spec=grid_spec,
        out_shape=jax.ShapeDtypeStruct((2 * n_tok + 2 * rb, slab, LANES), BF16),
        compiler_params=_cparams(("arbitrary",), vmem=EXPERT_VMEM_LIMIT),
        name="experts",
    )(slot, block_e, next_e, block_off, block_half, n_blk, xs, wg, wu, wd)


def _combine_kernel(y0_ref, y1_ref, h_ref, meta_ref, g_ref, o_ref):
    tm, d = h_ref.shape
    gates = meta_ref[...].T
    y0 = y0_ref[...].reshape(tm, d).astype(F32)
    y1 = y1_ref[...].reshape(tm, d).astype(F32)
    out = h_ref[...] + gates[:, 4:5] * y0 + gates[:, 5:6] * y1
    o_ref[...] = _rms(out, g_ref[...])


def _combine(y, h, meta, g, *, tm):
    m, d = h.shape
    slab = d // LANES
    nt = m // tm
    return pl.pallas_call(
        _combine_kernel,
        grid=(nt,),
        in_specs=[
            pl.BlockSpec((tm, slab, LANES), lambda i: (i, 0, 0)),
            pl.BlockSpec((tm, slab, LANES), lambda i: (nt + i, 0, 0)),
            pl.BlockSpec((tm, d), lambda i: (i, 0)),
            pl.BlockSpec((META_ROWS, tm), lambda i: (0, i)),
            pl.BlockSpec((1, d), lambda i: (0, 0)),
        ],
        out_specs=pl.BlockSpec((tm, d), lambda i: (i, 0)),
        out_shape=jax.ShapeDtypeStruct((m, d), F32),
        compiler_params=_cparams(("parallel",)),
        name="combine",
    )(y, y, h, meta, g)


def kernel(x, mem, norm_mix_g, w_in, w_gla_alpha_up, b_gla_alpha, gla_out_norm_g, hgrn_lb_logits, hgrn_out_norm_g, w_mix_out, norm_xattn_g, norm_mem_g, w_xattn_q, w_xattn_kv, w_xattn_out, norm_ffn_g, w_router_group, b_router_group, w_router_expert, b_router_expert, w_expert_gate, w_expert_up, w_expert_down, norm_final_g):
    batch, seq, d = x.shape
    n_mem = mem.shape[1]
    m = batch * seq
    depth = norm_mix_g.shape[0]
    t = _tiles(m, seq)
    h = x.reshape(m, d)
    lb_all = jnp.cumsum(jax.nn.softmax(hgrn_lb_logits.astype(F32), axis=0), axis=0)
    gla_cols = 2 * GLA_HEADS * GLA_DK + 2 * GLA_HEADS * GLA_DV
    lr_rank = w_gla_alpha_up.shape[1]

    for l in range(depth):
        w_t = jnp.swapaxes(w_in[l], 0, 1)
        w_lr_t = jnp.pad(w_t[gla_cols:gla_cols + lr_rank], ((0, LANES - lr_rank), (0, 0))).astype(BF16)
        w_up = jnp.pad(w_gla_alpha_up[l], ((0, LANES - lr_rank), (0, 0))).astype(BF16)
        a_mix, la = _norm_gate(h, norm_mix_g[l][None, :], w_lr_t, w_up, b_gla_alpha[l][None, :], tm=t.norm_gate_rows)
        proj = _inproj(a_mix, w_t, tm=t.inproj_rows, tn=t.inproj_cols, skip_from=gla_cols, skip=lr_rank)
        lb = lb_all[l]
        lbc = jnp.zeros((8, lb.shape[0]), F32).at[0].set(jnp.log(lb)).at[1].set(jnp.log1p(-lb)).at[2].set(1.0 - lb)
        proj3 = proj.reshape(batch, seq, -1)
        o_gla = _gla(proj3, la.reshape(batch, seq, -1), gla_out_norm_g[l][None, :], tb=t.mixer_rows, hp=t.gla_heads)
        o_h = _hgrn(proj3, lbc, hgrn_out_norm_g[l][None, :], tb=t.mixer_rows, hp=t.hgrn_heads)
        o_gla, o_h = o_gla.reshape(m, -1), o_h.reshape(m, -1)
        h = _mm_res([o_gla, o_h], w_mix_out[l], h, tm=t.square_rows, tn=d, name="mix_out")

        kv = _norm_mm(mem.reshape(batch * n_mem, d), norm_mem_g[l][None, :], w_xattn_kv[l],
                      tm=batch * n_mem, tn=t.mem_kv_cols, out_dtype=BF16, name="mem_kv")
        o = _xattn(h, norm_xattn_g[l][None, :], w_xattn_q[l], kv, seq=seq, n_mem=n_mem, tq=t.xattn_rows)
        h = _mm_res([o], w_xattn_out[l], h, tm=t.square_rows, tn=d, name="xattn_out")

        n_logits = N_GROUPS + N_EXPERTS
        wr = jnp.pad(jnp.concatenate([w_router_group[l], w_router_expert[l]], axis=1).T,
                     ((0, ROUTER_ROWS - n_logits), (0, 0)))
        br = jnp.pad(jnp.concatenate([b_router_group[l], b_router_expert[l]]), (0, ROUTER_ROWS - n_logits))
        a, meta, cnt = _router(h, norm_ffn_g[l][None, :], wr, jnp.broadcast_to(br[:, None], (ROUTER_ROWS, LANES)),
                               tm=t.router_rows)

        e_idx = meta[0:2].astype(jnp.int32)
        rank = meta[2:4].astype(jnp.int32)
        counts = cnt[N_GROUPS:N_GROUPS + N_EXPERTS, 0].astype(jnp.int32)
        padded = ((counts + PAD_UNIT - 1) // PAD_UNIT) * PAD_UNIT
        pad_end = jnp.cumsum(padded)
        pad_start = pad_end - padded
        n_rows = 2 * m + N_EXPERTS * PAD_UNIT + PAD_UNIT
        n_units = (pad_end[-1:] // PAD_UNIT).astype(jnp.int32)
        n_full = padded // EXPERT_BLOCK
        n_blk_e = n_full + (padded % EXPERT_BLOCK) // PAD_UNIT
        blk_end = jnp.cumsum(n_blk_e)
        blk_start = blk_end - n_blk_e
        n_blocks = (2 * m) // EXPERT_BLOCK + N_EXPERTS
        n_blk = blk_end[-1:].astype(jnp.int32)
        blk = jnp.arange(n_blocks, dtype=jnp.int32)
        block_e = jnp.minimum(jnp.sum((blk_end[None, :] <= blk[:, None]).astype(jnp.int32), axis=1), N_EXPERTS - 1)
        blk_local = blk - blk_start[block_e]
        block_off = (pad_start[block_e] + EXPERT_BLOCK * blk_local).astype(jnp.int32)
        block_off = jnp.where(blk < n_blk[0], block_off, 0)
        block_half = (blk_local >= n_full[block_e]).astype(jnp.int32)
        after = blk_end[block_e]
        next_e = jnp.where(after < n_blk[0], block_e[jnp.minimum(after, n_blocks - 1)], -1).astype(jnp.int32)

        experts = jnp.arange(N_EXPERTS, dtype=jnp.int32)
        dest = [jnp.sum(jnp.where(e_idx[k][:, None] == experts[None, :], pad_start[None, :], 0), axis=1) + rank[k]
                for k in range(2)]
        xs, slot = _dispatch(dest[0], dest[1], pad_start + counts, padded - counts, n_units, a, n_rows=n_rows,
                             tm=t.dispatch_rows)
        y = _experts(slot, block_e, next_e, block_off, block_half, n_blk, xs, w_expert_gate[l], w_expert_up[l],
                     w_expert_down[l], rb=EXPERT_BLOCK, n_tok=m)
        last = l == depth - 1
        g_fin = norm_final_g[None, :] if last else jnp.ones((1, d), F32)
        h = _combine(y, h, meta, g_fin, tm=t.combine_rows)
        assert last, "the combine kernel fuses the final rmsnorm; deeper stacks need an un-normalised variant"

    return h.reshape(batch, seq, d)
```

```python
import functools
from typing import NamedTuple

import jax
import jax.numpy as jnp
from jax import lax
from jax.experimental import pallas as pl
from jax.experimental.pallas import tpu as pltpu

F32 = jnp.float32
BF16 = jnp.bfloat16

EPS = 1e-6
CHUNK = 64
LANES = 128
GLA_HEADS, GLA_DK, GLA_DV = 4, 128, 256
HGRN_HEADS, HGRN_DH = 8, 128
XA_HEADS, XA_DH = 4, 512
N_GROUPS, EXPERTS_PER_GROUP, N_EXPERTS = 4, 8, 32
EXPERT_BLOCK = 256
PAD_UNIT = 128
WEIGHT_CHUNKS = 8
WEIGHT_DMA_PRIORITY = 1
VMEM_LIMIT = 56 * 1024 * 1024
EXPERT_VMEM_LIMIT = 60 * 1024 * 1024


class Tiles(NamedTuple):
    norm_gate_rows: int
    inproj_rows: int
    inproj_cols: int
    mixer_rows: int
    gla_heads: int
    hgrn_heads: int
    square_rows: int
    mem_kv_cols: int
    xattn_rows: int
    router_rows: int
    dispatch_rows: int
    combine_rows: int


def _tiles(m, seq):
    return Tiles(norm_gate_rows=min(m, 1024), inproj_rows=min(m, 4096), inproj_cols=512, mixer_rows=min(seq, 256),
                 gla_heads=GLA_HEADS, hgrn_heads=HGRN_HEADS, square_rows=min(m, 512), mem_kv_cols=1024,
                 xattn_rows=min(seq, 512), router_rows=min(m, 512), dispatch_rows=min(m, 256),
                 combine_rows=min(m, 512))


def _cparams(sem, vmem=VMEM_LIMIT):
    return pltpu.CompilerParams(dimension_semantics=sem, vmem_limit_bytes=vmem)


def _log1pexp_neg(t):
    return jnp.log(1.0 + jnp.exp(-t))


def _log_sigmoid(z):
    return jnp.minimum(z, 0.0) - _log1pexp_neg(jnp.abs(z))


def _sigmoid(z):
    return 1.0 / (1.0 + jnp.exp(-z))


def _rms(x, g):
    return x * lax.rsqrt(jnp.mean(x * x, axis=-1, keepdims=True) + EPS) * g


def _nt_dot(x, y):
    return lax.dot_general(x, y, (((1,), (1,)), ((), ())), preferred_element_type=F32)


NORM_ROWS = 256


def _for_row_chunks(n_rows, fn):
    step = min(NORM_ROWS, n_rows)

    def body(ci, carry):
        fn(pl.ds(pl.multiple_of(ci * step, step), step))
        return carry

    lax.fori_loop(0, n_rows // step, body, 0)


def _norm_gate_kernel(x_ref, g_ref, wlr_ref, wup_ref, bal_ref, a_ref, la_ref):
    a = _rms(x_ref[...], g_ref[...]).astype(BF16)
    a_ref[...] = a
    lr = _nt_dot(a, wlr_ref[...])
    z = jnp.dot(lr.astype(BF16), wup_ref[...], preferred_element_type=F32) + bal_ref[...]
    la_ref[...] = _log_sigmoid(z) * (1.0 / 16.0)


def _norm_gate(x, g, w_lr_t, w_up, b_al, *, tm):
    m, d = x.shape
    nk = w_up.shape[1]
    return pl.pallas_call(
        _norm_gate_kernel,
        grid=(m // tm,),
        in_specs=[
            pl.BlockSpec((tm, d), lambda i: (i, 0)),
            pl.BlockSpec((1, d), lambda i: (0, 0)),
            pl.BlockSpec((LANES, d), lambda i: (0, 0)),
            pl.BlockSpec((LANES, nk), lambda i: (0, 0)),
            pl.BlockSpec((1, nk), lambda i: (0, 0)),
        ],
        out_specs=[pl.BlockSpec((tm, d), lambda i: (i, 0)), pl.BlockSpec((tm, nk), lambda i: (i, 0))],
        out_shape=[jax.ShapeDtypeStruct((m, d), BF16), jax.ShapeDtypeStruct((m, nk), F32)],
        compiler_params=_cparams(("parallel",)),
        name="norm_gate",
    )(x, g, w_lr_t, w_up, b_al)


INPROJ_DOT_ROWS = 1024


def _inproj_kernel(a_ref, w_ref, proj_ref, wb_ref):
    wb_ref[...] = w_ref[...].astype(BF16)
    step = min(INPROJ_DOT_ROWS, a_ref.shape[0])

    def body(ci, carry):
        rows = pl.ds(pl.multiple_of(ci * step, step), step)
        proj_ref[rows, :] = _nt_dot(a_ref[rows, :], wb_ref[...])
        return carry

    lax.fori_loop(0, a_ref.shape[0] // step, body, 0)


def _inproj(a, w_t, *, tm, tn, skip_from, skip):
    m, d = a.shape
    n = w_t.shape[0] - skip
    first_after = skip_from // tn

    def w_rows(i, j):
        return (pl.multiple_of(j * tn + jnp.where(j >= first_after, skip, 0), 8), 0)

    return pl.pallas_call(
        _inproj_kernel,
        grid=(m // tm, n // tn),
        in_specs=[
            pl.BlockSpec((tm, d), lambda i, j: (i, 0), pipeline_mode=pl.Buffered(1)),
            pl.BlockSpec((pl.Element(tn), pl.Element(d)), w_rows),
        ],
        out_specs=pl.BlockSpec((tm, tn), lambda i, j: (i, j)),
        out_shape=jax.ShapeDtypeStruct((m, n), F32),
        scratch_shapes=[pltpu.VMEM((tn, d), BF16)],
        compiler_params=_cparams(("parallel", "arbitrary")),
        name="inproj",
    )(a, w_t)


def _weight_spec(k, n, tn):
    if tn == n:
        return pl.BlockSpec((k, n), lambda i, j: (0, 0), pipeline_mode=pl.Buffered(1))
    return pl.BlockSpec((k, tn), lambda i, j: (0, j))


def _cast_weight(w_ref, wb_ref, resident):
    if resident:
        @pl.when((pl.program_id(0) == 0) & (pl.program_id(1) == 0))
        def _():
            wb_ref[...] = w_ref[...].astype(BF16)
    else:
        wb_ref[...] = w_ref[...].astype(BF16)


def _norm_mm_kernel(x_ref, g_ref, w_ref, o_ref, a_ref, wb_ref, *, resident):
    @pl.when(pl.program_id(1) == 0)
    def _():
        def rows_fn(rows):
            a_ref[rows, :] = _rms(x_ref[rows, :], g_ref[...]).astype(BF16)

        _for_row_chunks(x_ref.shape[0], rows_fn)

    _cast_weight(w_ref, wb_ref, resident)
    o_ref[...] = jnp.dot(a_ref[...], wb_ref[...], preferred_element_type=F32).astype(o_ref.dtype)


def _norm_mm(x, g, w, *, tm, tn, out_dtype, name):
    m, d = x.shape
    n = w.shape[1]
    return pl.pallas_call(
        functools.partial(_norm_mm_kernel, resident=tn == n),
        grid=(m // tm, n // tn),
        in_specs=[
            pl.BlockSpec((tm, d), lambda i, j: (i, 0)),
            pl.BlockSpec((1, d), lambda i, j: (0, 0)),
            _weight_spec(d, n, tn),
        ],
        out_specs=pl.BlockSpec((tm, tn), lambda i, j: (i, j)),
        out_shape=jax.ShapeDtypeStruct((m, n), out_dtype),
        scratch_shapes=[pltpu.VMEM((tm, d), BF16), pltpu.VMEM((d, tn), BF16)],
        compiler_params=_cparams(("arbitrary", "arbitrary")),
        name=name,
    )(x, g, w)


def _mm_res_kernel(*refs, n_lhs, resident):
    lhs = refs[:n_lhs]
    w_ref, res_ref, o_ref, wb_ref = refs[n_lhs:]
    _cast_weight(w_ref, wb_ref, resident)
    acc = res_ref[...]
    k0 = 0
    for l_ref in lhs:
        kp = l_ref.shape[1]
        acc = acc + jnp.dot(l_ref[...], wb_ref[k0:k0 + kp, :], preferred_element_type=F32)
        k0 += kp
    o_ref[...] = acc


def _mm_res(lhs_parts, w, res, *, tm, tn, name):
    m, n = res.shape
    k = w.shape[0]
    n_lhs = len(lhs_parts)
    in_specs = [pl.BlockSpec((tm, p.shape[1]), lambda i, j: (i, 0)) for p in lhs_parts]
    in_specs += [_weight_spec(k, n, tn), pl.BlockSpec((tm, tn), lambda i, j: (i, j))]
    return pl.pallas_call(
        functools.partial(_mm_res_kernel, n_lhs=n_lhs, resident=tn == n),
        grid=(m // tm, n // tn),
        in_specs=in_specs,
        out_specs=pl.BlockSpec((tm, tn), lambda i, j: (i, j)),
        out_shape=jax.ShapeDtypeStruct((m, n), F32),
        scratch_shapes=[pltpu.VMEM((k, tn), BF16)],
        compiler_params=_cparams(("arbitrary", "arbitrary")),
        name=name,
    )(*lhs_parts, w, res)


LEVELS = (32, 16, 8, 4, 2, 1)
LOG2E = 1.4426950408889634


def _split3_bf16(x):
    def top(v):
        bits = lax.bitcast_convert_type(v, jnp.uint32) & jnp.uint32(0xFFFF0000)
        return lax.bitcast_convert_type(bits, F32)
    hi = top(x)
    r1 = x - hi
    mid = top(r1)
    lo = r1 - mid
    return hi.astype(BF16), mid.astype(BF16), lo.astype(BF16)


def _mix_chunks(qs, ks, vs, las, st_refs):
    heads = range(len(qs))
    c, dk = qs[0].shape
    row = lax.broadcasted_iota(jnp.int32, (c, c), 0)
    col = lax.broadcasted_iota(jnp.int32, (c, c), 1)
    rowk = lax.broadcasted_iota(jnp.int32, (c, dk), 0)
    xor = jnp.bitwise_xor(row, col)

    las = [la * LOG2E for la in las]

    tri = jnp.where(col <= row, 1.0, 0.0).astype(BF16)
    b3 = [jnp.dot(tri, jnp.concatenate(_split3_bf16(la), axis=1), preferred_element_type=F32) for la in las]
    bs = [(t[:, :dk] + t[:, dk:2 * dk]) + t[:, 2 * dk:] for t in b3]

    def neg_dist(w, b, la):
        if w >= 4:
            parts = [jnp.broadcast_to(b[base + w - 1:base + w, :], (2 * w, dk)) for base in range(0, c, 2 * w)]
            m = parts[0] if len(parts) == 1 else jnp.concatenate(parts, axis=0)
            return -jnp.abs(b - m)
        if w == 2:
            r4 = jnp.bitwise_and(rowk, 3)
            nxt = pltpu.roll(la, c - 1, 0)
            prv = pltpu.roll(la, 1, 0)
            return jnp.where(r4 == 0, nxt, jnp.where(r4 == 1, 0.0, jnp.where(r4 == 2, la, la + prv)))
        return jnp.where(jnp.bitwise_and(rowk, 1) == 1, la, 0.0)

    acc = [_nt_dot(qs[h].astype(BF16), ks[h].astype(BF16)) for h in heads]
    for w in reversed(LEVELS):
        upper = jnp.bitwise_and(rowk, w) != 0
        xs = [(jnp.where(upper, qs[h], ks[h]) * jnp.exp2(neg_dist(w, bs[h], las[h]))).astype(BF16) for h in heads]
        gs = [_nt_dot(x, x) for x in xs]
        acc = [jnp.where(xor >= w, gs[h], acc[h]) for h in heads]
    ab = [jnp.where(col <= row, a, 0.0).astype(BF16) for a in acc]

    sts = [st_refs[h][...] for h in heads]
    qx = [(qs[h] * jnp.exp2(bs[h])).astype(BF16) for h in heads]
    b_last = [b[c - 1:c, :] for b in bs]
    kx = [(ks[h] * jnp.exp2(b_last[h] - bs[h])).astype(BF16) for h in heads]
    vb = [v.astype(BF16) for v in vs]
    outs = [jnp.dot(ab[h], vb[h], preferred_element_type=F32) + _nt_dot(qx[h], sts[h].astype(BF16)) for h in heads]
    for h in heads:
        st_refs[h][...] = jnp.exp2(b_last[h]) * sts[h] + lax.dot_general(
            vb[h], kx[h], (((0,), (0,)), ((), ())), preferred_element_type=F32)
    return outs


def _gated_norm(o, g, gn):
    return _rms(o, gn) * (g * _sigmoid(g))


def _gla_kernel(q_ref, k_ref, v_ref, g_ref, la_ref, gn_ref, o_ref, st_ref, *, n_chunk):
    @pl.when(pl.program_id(1) == 0)
    def _():
        st_ref[...] = jnp.zeros_like(st_ref)

    dk, dv = GLA_DK, GLA_DV
    n_b = q_ref.shape[0]
    n_h = st_ref.shape[0] // n_b
    streams = [(b, h) for b in range(n_b) for h in range(n_h)]
    kc = [slice(h * dk, (h + 1) * dk) for h in range(n_h)]
    vc = [slice(h * dv, (h + 1) * dv) for h in range(n_h)]

    def body(ci, carry):
        rows = pl.ds(pl.multiple_of(ci * CHUNK, CHUNK), CHUNK)
        outs = _mix_chunks([q_ref[b, rows, kc[h]] * (dk ** -0.5) for b, h in streams],
                           [k_ref[b, rows, kc[h]] for b, h in streams], [v_ref[b, rows, vc[h]] for b, h in streams],
                           [la_ref[b, rows, kc[h]] for b, h in streams],
                           [st_ref.at[n] for n in range(len(streams))])
        for n, (b, h) in enumerate(streams):
            o_ref[b, rows, vc[h]] = _gated_norm(outs[n], g_ref[b, rows, vc[h]], gn_ref[...]).astype(o_ref.dtype)
        return carry

    lax.fori_loop(0, n_chunk, body, 0)


def _gla(proj, la, gn, *, tb, hp):
    batch, seq, _ = proj.shape
    wk, wv = hp * GLA_DK, hp * GLA_DV
    n_grp = GLA_HEADS // hp
    spec = lambda width, first: pl.BlockSpec((batch, tb, width), lambda p, t: (0, t, first + p))
    return pl.pallas_call(
        functools.partial(_gla_kernel, n_chunk=tb // CHUNK),
        grid=(n_grp, seq // tb),
        in_specs=[spec(wk, 0), spec(wk, n_grp), spec(wv, n_grp), spec(wv, 2 * n_grp), spec(wk, 0),
                  pl.BlockSpec((1, GLA_DV), lambda p, t: (0, 0))],
        out_specs=spec(wv, 0),
        out_shape=jax.ShapeDtypeStruct((batch, seq, GLA_HEADS * GLA_DV), BF16),
        scratch_shapes=[pltpu.VMEM((batch * hp, GLA_DV, GLA_DK), F32)],
        compiler_params=_cparams(("parallel", "arbitrary")),
        name="gla",
    )(proj, proj, proj, proj, la, gn)


def _hgrn_kernel(q_ref, f_ref, i_ref, g_ref, lb_ref, gn_ref, o_ref, st_ref, *, n_chunk):
    @pl.when(pl.program_id(1) == 0)
    def _():
        st_ref[...] = jnp.zeros_like(st_ref)

    dh = HGRN_DH
    n_b = q_ref.shape[0]
    n_h = st_ref.shape[0] // n_b
    streams = [(b, h) for b in range(n_b) for h in range(n_h)]
    hc = [slice(h * dh, (h + 1) * dh) for h in range(n_h)]

    def body(ci, carry):
        rows = pl.ds(pl.multiple_of(ci * CHUNK, CHUNK), CHUNK)
        qs, ks, las = [], [], []
        for b, h in streams:
            log_lb, log_1mlb, one_m_lb = lb_ref[0:1, hc[h]], lb_ref[1:2, hc[h]], lb_ref[2:3, hc[h]]
            hq = q_ref[b, rows, hc[h]]
            z = f_ref[b, rows, hc[h]]
            x1 = log_1mlb + _log_sigmoid(z)
            las.append(jnp.maximum(log_lb, x1) + _log1pexp_neg(jnp.abs(log_lb - x1)))
            qs.append(hq * _sigmoid(hq))
            ks.append(one_m_lb * _sigmoid(-z))
        outs = _mix_chunks(qs, ks, [i_ref[b, rows, hc[h]] for b, h in streams], las,
                           [st_ref.at[n] for n in range(len(streams))])
        for n, (b, h) in enumerate(streams):
            o_ref[b, rows, hc[h]] = _gated_norm(outs[n], g_ref[b, rows, hc[h]], gn_ref[...]).astype(o_ref.dtype)
        return carry

    lax.fori_loop(0, n_chunk, body, 0)


def _hgrn(proj, lbc, gn, *, tb, hp):
    batch, seq, _ = proj.shape
    width = hp * HGRN_DH
    n_grp = HGRN_HEADS // hp
    first = 3072 // width
    spec = lambda seg: pl.BlockSpec((batch, tb, width), lambda p, t: (0, t, first + seg * n_grp + p))
    return pl.pallas_call(
        functools.partial(_hgrn_kernel, n_chunk=tb // CHUNK),
        grid=(n_grp, seq // tb),
        in_specs=[spec(0), spec(1), spec(2), spec(3),
                  pl.BlockSpec((8, width), lambda p, t: (0, p)),
                  pl.BlockSpec((1, HGRN_DH), lambda p, t: (0, 0))],
        out_specs=pl.BlockSpec((batch, tb, width), lambda p, t: (0, t, p)),
        out_shape=jax.ShapeDtypeStruct((batch, seq, HGRN_HEADS * HGRN_DH), BF16),
        scratch_shapes=[pltpu.VMEM((batch * hp, HGRN_DH, HGRN_DH), F32)],
        compiler_params=_cparams(("parallel", "arbitrary")),
        name="hgrn",
    )(proj, proj, proj, proj, lbc, gn)


def _xattn_kernel(x_ref, g_ref, w_ref, k_ref, v_ref, o_ref, wb_ref):
    @pl.when(pl.program_id(0) == 0)
    def _():
        _for_row_chunks(w_ref.shape[0], lambda rows: wb_ref.__setitem__((rows, slice(None)),
                                                                        w_ref[rows, :].astype(BF16)))

    a = _rms(x_ref[...], g_ref[...]).astype(BF16)
    q = jnp.dot(a, wb_ref[...], preferred_element_type=F32).astype(BF16)
    cols = [slice(h * XA_DH, (h + 1) * XA_DH) for h in range(XA_HEADS)]
    s = [_nt_dot(q[:, c], k_ref[:, c]) * (XA_DH ** -0.5) for c in cols]
    p = [jnp.exp(x - jnp.max(x, axis=-1, keepdims=True)) for x in s]
    p = [(x / jnp.sum(x, axis=-1, keepdims=True)).astype(BF16) for x in p]
    for c, x in zip(cols, p):
        o_ref[:, c] = jnp.dot(x, v_ref[:, c], preferred_element_type=F32).astype(o_ref.dtype)


def _xattn(x, g, w_q, kv, *, seq, n_mem, tq):
    m, d = x.shape
    nt = seq // tq
    return pl.pallas_call(
        _xattn_kernel,
        grid=(m // tq,),
        in_specs=[
            pl.BlockSpec((tq, d), lambda i: (i, 0)),
            pl.BlockSpec((1, d), lambda i: (0, 0)),
            pl.BlockSpec((d, d), lambda i: (0, 0), pipeline_mode=pl.Buffered(1)),
            pl.BlockSpec((n_mem, d), lambda i: (i // nt, 0)),
            pl.BlockSpec((n_mem, d), lambda i: (i // nt, 1)),
        ],
        out_specs=pl.BlockSpec((tq, d), lambda i: (i, 0)),
        out_shape=jax.ShapeDtypeStruct((m, d), BF16),
        scratch_shapes=[pltpu.VMEM((d, d), BF16)],
        compiler_params=_cparams(("arbitrary",)),
        name="xattn",
    )(x, g, w_q, kv, kv)


ROUTER_ROWS = 40
META_ROWS = 8


def _router_kernel(h_ref, g_ref, wr_ref, br_ref, a_ref, meta_ref, cnt_ref, carry_ref):
    tm = h_ref.shape[0]
    nr = wr_ref.shape[0]

    @pl.when(pl.program_id(0) == 0)
    def _():
        carry_ref[...] = jnp.zeros_like(carry_ref)

    a = _rms(h_ref[...], g_ref[...])
    a_ref[...] = a.astype(BF16).reshape(a_ref.shape)
    a_hi = a.astype(BF16)
    a_lo = (a - a_hi.astype(F32)).astype(BF16)
    w = wr_ref[...]
    w_hi = w.astype(BF16)
    w_lo = (w - w_hi.astype(F32)).astype(BF16)
    logits = (_nt_dot(w_hi, a_hi) + _nt_dot(w_hi, a_lo) + _nt_dot(w_lo, a_hi)) + br_ref[:, 0:1]
    row = lax.broadcasted_iota(jnp.int32, (nr, tm), 0)
    row_f = row.astype(F32)
    neg = -jnp.inf

    def first_max(x):
        v = jnp.max(x, axis=0, keepdims=True)
        return v, jnp.min(jnp.where(x == v, row_f, float(nr)), axis=0, keepdims=True)

    gl = jnp.where(row < N_GROUPS, logits, neg)
    gmax, gidx = first_max(gl)
    p_group = 1.0 / jnp.sum(jnp.exp(gl - gmax), axis=0, keepdims=True)
    lo = float(N_GROUPS) + gidx * float(EXPERTS_PER_GROUP)
    el = jnp.where((row_f >= lo) & (row_f < lo + float(EXPERTS_PER_GROUP)), logits, neg)
    v1, i1 = first_max(el)
    v2, i2 = first_max(jnp.where(row_f == i1, neg, el))
    t = jnp.exp(v2 - v1)
    g1 = p_group / (1.0 + t)
    g2 = p_group * t / (1.0 + t)

    hit1 = row_f == i1
    hit2 = row_f == i2
    onehot = jnp.where(hit1 | hit2, 1.0, 0.0)
    src = lax.broadcasted_iota(jnp.int32, (tm, tm), 0)
    dst = lax.broadcasted_iota(jnp.int32, (tm, tm), 1)
    earlier = jnp.where(src < dst, 1.0, 0.0).astype(BF16)
    before = jnp.dot(onehot.astype(BF16), earlier, preferred_element_type=F32) + carry_ref[:, 0:1]
    r1 = jnp.sum(jnp.where(hit1, before, 0.0), axis=0, keepdims=True)
    r2 = jnp.sum(jnp.where(hit2, before, 0.0), axis=0, keepdims=True)
    carry_ref[...] = carry_ref[...] + jnp.sum(onehot, axis=1, keepdims=True)
    cnt_ref[...] = carry_ref[...]

    out_row = lax.broadcasted_iota(jnp.int32, meta_ref.shape, 0)
    meta = jnp.zeros(meta_ref.shape, F32)
    for idx, val in enumerate((i1 - float(N_GROUPS), i2 - float(N_GROUPS), r1, r2, g1, g2)):
        meta = jnp.where(out_row == idx, val, meta)
    meta_ref[...] = meta


def _router(h, g, wr, br, *, tm):
    m, d = h.shape
    slab = d // LANES
    return pl.pallas_call(
        _router_kernel,
        grid=(m // tm,),
        in_specs=[
            pl.BlockSpec((tm, d), lambda i: (i, 0)),
            pl.BlockSpec((1, d), lambda i: (0, 0)),
            pl.BlockSpec((ROUTER_ROWS, d), lambda i: (0, 0)),
            pl.BlockSpec((ROUTER_ROWS, LANES), lambda i: (0, 0)),
        ],
        out_specs=[
            pl.BlockSpec((tm, slab, LANES), lambda i: (i, 0, 0)),
            pl.BlockSpec((META_ROWS, tm), lambda i: (0, i)),
            pl.BlockSpec((ROUTER_ROWS, LANES), lambda i: (0, 0)),
        ],
        out_shape=[jax.ShapeDtypeStruct((m, slab, LANES), BF16), jax.ShapeDtypeStruct((META_ROWS, m), F32),
                   jax.ShapeDtypeStruct((ROUTER_ROWS, LANES), F32)],
        scratch_shapes=[pltpu.VMEM((ROUTER_ROWS, LANES), F32)],
        compiler_params=_cparams(("arbitrary",)),
        name="router",
    )(h, g, wr, br)


PAD_PIECES = tuple(PAD_UNIT >> (b + 1) for b in range(PAD_UNIT.bit_length() - 1))


def _dispatch_kernel(d1_ref, d2_ref, fill_ref, npad_ref, nb_ref, a_ref, xs_hbm, slot_ref, st0, st1, zbuf, sem, zsem):
    i = pl.program_id(0)
    n_steps = pl.num_programs(0)
    tm = a_ref.shape[0]
    bufs = (st0, st1)
    rb = zbuf.shape[0]
    n_blocks = xs_hbm.shape[0] // rb
    min_blocks = (2 * tm * n_steps) // rb

    def zero_copies():
        out = []
        for e in range(N_EXPERTS):
            p = npad_ref[e]
            for piece in PAD_PIECES:
                out.append((p & piece != 0, pltpu.make_async_copy(
                    zbuf.at[pl.ds(0, piece)], xs_hbm.at[pl.ds(fill_ref[e] + (p & -(2 * piece)), piece)], zsem.at[0])))
        for b in range(min_blocks, n_blocks):
            out.append((b >= nb_ref[0], pltpu.make_async_copy(zbuf, xs_hbm.at[pl.ds(b * rb, rb)], zsem.at[0])))
        return out

    def wait_rows(s):
        for _ in range(2):
            pltpu.make_async_copy(bufs[s], xs_hbm.at[pl.ds(0, tm)], sem.at[s]).wait()

    @pl.when(i == 0)
    def _():
        def mark_unused(j, carry):
            slot_ref[j] = -1
            return carry
        for e in range(N_EXPERTS):
            lax.fori_loop(fill_ref[e], fill_ref[e] + npad_ref[e], mark_unused, 0)
        lax.fori_loop(nb_ref[0] * rb, slot_ref.shape[0], mark_unused, 0)
        zbuf[...] = jnp.zeros_like(zbuf)
        for cond, cp in zero_copies():
            @pl.when(cond)
            def _():
                cp.start()

    for s in (0, 1):
        @pl.when(lax.rem(i, 2) == s)
        def _():
            @pl.when(i >= 2)
            def _():
                wait_rows(s)

            bufs[s][...] = a_ref[...]
            for r in range(tm):
                tok = i * tm + r
                d1 = d1_ref[tok]
                d2 = d2_ref[tok]
                slot_ref[d1] = 2 * tok
                slot_ref[d2] = 2 * tok + 1
                pltpu.make_async_copy(bufs[s].at[r], xs_hbm.at[d1], sem.at[s]).start(priority=0)
                pltpu.make_async_copy(bufs[s].at[r], xs_hbm.at[d2], sem.at[s]).start(priority=1)

    @pl.when(i == n_steps - 1)
    def _():
        for s in (0, 1):
            @pl.when((lax.rem(n_steps - 1, 2) == s) | ((n_steps >= 2) & (lax.rem(n_steps, 2) == s)))
            def _():
                wait_rows(s)
        for cond, cp in zero_copies():
            @pl.when(cond)
            def _():
                cp.wait()


def _dispatch(dest1, dest2, fill, npad, n_blk, a, *, n_rows, tm):
    m, slab, _ = a.shape
    grid_spec = pltpu.PrefetchScalarGridSpec(
        num_scalar_prefetch=5,
        grid=(m // tm,),
        in_specs=[pl.BlockSpec((tm, slab, LANES), lambda i, *_: (i, 0, 0))],
        out_specs=[pl.BlockSpec(memory_space=pl.ANY), pl.BlockSpec(memory_space=pltpu.SMEM)],
        scratch_shapes=[pltpu.VMEM((tm, slab, LANES), a.dtype), pltpu.VMEM((tm, slab, LANES), a.dtype),
                        pltpu.VMEM((PAD_UNIT, slab, LANES), a.dtype),
                        pltpu.SemaphoreType.DMA((2,)), pltpu.SemaphoreType.DMA((1,))],
    )
    return pl.pallas_call(
        _dispatch_kernel,
        grid_spec=grid_spec,
        out_shape=[jax.ShapeDtypeStruct((n_rows, slab, LANES), a.dtype), jax.ShapeDtypeStruct((n_rows,), jnp.int32)],
        compiler_params=_cparams(("arbitrary",)),
        name="dispatch",
    )(dest1, dest2, fill, npad, n_blk, a)


def _weight_copies(w_hbm, e, wst_ref, sem):
    rows_per = w_hbm.shape[1] // WEIGHT_CHUNKS
    return [pltpu.make_async_copy(w_hbm.at[e, pl.ds(c * rows_per, rows_per), :],
                                  wst_ref.at[pl.ds(c * rows_per, rows_per), :], sem) for c in range(WEIGHT_CHUNKS)]


def _load_expert_weights(i, be_ref, nx_ref, w_hbms, wst_refs, wb_refs, wsem):
    def start(t, e):
        for cp in _weight_copies(w_hbms[t], e, wst_refs[t], wsem.at[t]):
            cp.start(priority=WEIGHT_DMA_PRIORITY)

    @pl.when(i == 0)
    def _():
        for t in range(len(w_hbms)):
            start(t, be_ref[0])

    @pl.when((i == 0) | (be_ref[i] != be_ref[jnp.maximum(i - 1, 0)]))
    def _():
        for t, (wst_ref, wb_ref) in enumerate(zip(wst_refs, wb_refs)):
            for cp in _weight_copies(w_hbms[t], be_ref[i], wst_ref, wsem.at[t]):
                cp.wait()
            _for_row_chunks(wst_ref.shape[0], lambda rows: wb_ref.__setitem__((rows, slice(None)),
                                                                              wst_ref[rows, :].astype(BF16)))

            @pl.when(nx_ref[i] >= 0)
            def _():
                start(t, nx_ref[i])


def _expert_kernel(slot_ref, be_ref, nx_ref, off_ref, half_ref, nb_ref, x_ref, wg_hbm, wu_hbm, wd_hbm, y_hbm, wsg, wsu, wsd,
                   wgb, wub, wdb, ys0, ys1, sem, wsem, *, n_tok):
    i = pl.program_id(0)
    n_steps = pl.num_programs(0)
    nb = nb_ref[0]
    rb = x_ref.shape[0]
    bufs = (ys0, ys1)

    def wait_block(buf, s):
        pltpu.make_async_copy(buf, y_hbm.at[pl.ds(0, rb)], sem.at[s]).wait()

    def scatter_previous(s):
        o = 1 - s
        prev = jnp.maximum(i - 1, 0)
        used = jnp.where(i >= 1, rb - half_ref[prev] * (rb // 2), 0)
        for r in range(rb):
            v = jnp.where(r < used, slot_ref[off_ref[prev] + r], -1)
            dst = jnp.where(v >= 0, (v & 1) * n_tok + (v >> 1), 2 * n_tok + o * rb + r)
            pltpu.make_async_copy(bufs[o].at[r], y_hbm.at[dst], sem.at[o]).start()

    @pl.when(i == 0)
    def _():
        ys0[...] = jnp.zeros_like(ys0)
        ys1[...] = jnp.zeros_like(ys1)
        spare = pltpu.make_async_copy(ys0, y_hbm.at[pl.ds(2 * n_tok, rb)], sem.at[0])
        spare.start()
        spare.wait()

    for s in (0, 1):
        for rows in (rb, rb // 2):
            @pl.when((i < nb) & (lax.rem(i, 2) == s) & (half_ref[jnp.minimum(i, n_steps - 2)] == (rows != rb)))
            def _():
                _load_expert_weights(i, be_ref, nx_ref, (wg_hbm, wu_hbm, wd_hbm), (wsg, wsu, wsd), (wgb, wub, wdb),
                                     wsem)

                @pl.when(i >= 1)
                def _():
                    wait_block(bufs[s], s)

                scatter_previous(s)
                x = x_ref[0:rows].reshape(rows, -1)
                hg = jnp.dot(x, wgb[...], preferred_element_type=F32)
                hu = jnp.dot(x, wub[...], preferred_element_type=F32)
                hb = (hg * _sigmoid(hg) * hu).astype(BF16)
                y = jnp.dot(hb, wdb[...], preferred_element_type=F32)
                bufs[s][0:rows] = y.astype(bufs[s].dtype).reshape((rows,) + bufs[s].shape[1:])

        @pl.when((i == nb) & (lax.rem(i, 2) == s))
        def _():
            scatter_previous(s)

    @pl.when(i == n_steps - 1)
    def _():
        for s in (0, 1):
            wait_block(bufs[s], s)


def _experts(slot, block_e, next_e, block_off, block_half, n_blk, xs, wg, wu, wd, *, rb, n_tok):
    _, slab, _ = xs.shape
    d = slab * LANES
    de = wg.shape[2]
    n_blocks = block_e.shape[0]

    def x_rows(i, sl, be, nx, off, hf, nb):
        return (pl.multiple_of(off[jnp.minimum(i, nb[0] - 1)], PAD_UNIT), 0, 0)

    grid_spec = pltpu.PrefetchScalarGridSpec(
        num_scalar_prefetch=6,
        grid=(n_blocks + 1,),
        in_specs=[pl.BlockSpec((pl.Element(rb), pl.Element(slab), pl.Element(LANES)), x_rows)]
        + [pl.BlockSpec(memory_space=pl.ANY)] * 3,
        out_specs=pl.BlockSpec(memory_space=pl.ANY),
        scratch_shapes=[pltpu.VMEM((d, de), F32), pltpu.VMEM((d, de), F32), pltpu.VMEM((de, d), F32),
                        pltpu.VMEM((d, de), BF16), pltpu.VMEM((d, de), BF16), pltpu.VMEM((de, d), BF16),
                        pltpu.VMEM((rb, slab, LANES), BF16), pltpu.VMEM((rb, slab, LANES), BF16),
                        pltpu.SemaphoreType.DMA((2,)), pltpu.SemaphoreType.DMA((3,))],
    )
    return pl.pallas_call(
        functools.partial(_expert_kernel, n_tok=n_tok),
        grid_spec=grid_spec,
        out_shape=jax.ShapeDtypeStruct((2 * n_tok + 2 * rb, slab, LANES), BF16),
        compiler_params=_cparams(("arbitrary",), vmem=EXPERT_VMEM_LIMIT),
        name="experts",
    )(slot, block_e, next_e, block_off, block_half, n_blk, xs, wg, wu, wd)


def _combine_kernel(y0_ref, y1_ref, h_ref, meta_ref, g_ref, o_ref):
    tm, d = h_ref.shape
    gates = meta_ref[...].T
    y0 = y0_ref[...].reshape(tm, d).astype(F32)
    y1 = y1_ref[...].reshape(tm, d).astype(F32)
    out = h_ref[...] + gates[:, 4:5] * y0 + gates[:, 5:6] * y1
    o_ref[...] = _rms(out, g_ref[...])


def _combine(y, h, meta, g, *, tm):
    m, d = h.shape
    slab = d // LANES
    nt = m // tm
    return pl.pallas_call(
        _combine_kernel,
        grid=(nt,),
        in_specs=[
            pl.BlockSpec((tm, slab, LANES), lambda i: (i, 0, 0)),
            pl.BlockSpec((tm, slab, LANES), lambda i: (nt + i, 0, 0)),
            pl.BlockSpec((tm, d), lambda i: (i, 0)),
            pl.BlockSpec((META_ROWS, tm), lambda i: (0, i)),
            pl.BlockSpec((1, d), lambda i: (0, 0)),
        ],
        out_specs=pl.BlockSpec((tm, d), lambda i: (i, 0)),
        out_shape=jax.ShapeDtypeStruct((m, d), F32),
        compiler_params=_cparams(("parallel",)),
        name="combine",
    )(y, y, h, meta, g)


def kernel(x, mem, norm_mix_g, w_in, w_gla_alpha_up, b_gla_alpha, gla_out_norm_g, hgrn_lb_logits, hgrn_out_norm_g, w_mix_out, norm_xattn_g, norm_mem_g, w_xattn_q, w_xattn_kv, w_xattn_out, norm_ffn_g, w_router_group, b_router_group, w_router_expert, b_router_expert, w_expert_gate, w_expert_up, w_expert_down, norm_final_g):
    batch, seq, d = x.shape
    n_mem = mem.shape[1]
    m = batch * seq
    depth = norm_mix_g.shape[0]
    t = _tiles(m, seq)
    h = x.reshape(m, d)
    lb_all = jnp.cumsum(jax.nn.softmax(hgrn_lb_logits.astype(F32), axis=0), axis=0)
    gla_cols = 2 * GLA_HEADS * GLA_DK + 2 * GLA_HEADS * GLA_DV
    lr_rank = w_gla_alpha_up.shape[1]

    for l in range(depth):
        w_t = jnp.swapaxes(w_in[l], 0, 1)
        w_lr_t = jnp.pad(w_t[gla_cols:gla_cols + lr_rank], ((0, LANES - lr_rank), (0, 0))).astype(BF16)
        w_up = jnp.pad(w_gla_alpha_up[l], ((0, LANES - lr_rank), (0, 0))).astype(BF16)
        a_mix, la = _norm_gate(h, norm_mix_g[l][None, :], w_lr_t, w_up, b_gla_alpha[l][None, :], tm=t.norm_gate_rows)
        proj = _inproj(a_mix, w_t, tm=t.inproj_rows, tn=t.inproj_cols, skip_from=gla_cols, skip=lr_rank)
        lb = lb_all[l]
        lbc = jnp.zeros((8, lb.shape[0]), F32).at[0].set(jnp.log(lb)).at[1].set(jnp.log1p(-lb)).at[2].set(1.0 - lb)
        proj3 = proj.reshape(batch, seq, -1)
        o_gla = _gla(proj3, la.reshape(batch, seq, -1), gla_out_norm_g[l][None, :], tb=t.mixer_rows, hp=t.gla_heads)
        o_h = _hgrn(proj3, lbc, hgrn_out_norm_g[l][None, :], tb=t.mixer_rows, hp=t.hgrn_heads)
        o_gla, o_h = o_gla.reshape(m, -1), o_h.reshape(m, -1)
        h = _mm_res([o_gla, o_h], w_mix_out[l], h, tm=t.square_rows, tn=d, name="mix_out")

        kv = _norm_mm(mem.reshape(batch * n_mem, d), norm_mem_g[l][None, :], w_xattn_kv[l],
                      tm=batch * n_mem, tn=t.mem_kv_cols, out_dtype=BF16, name="mem_kv")
        o = _xattn(h, norm_xattn_g[l][None, :], w_xattn_q[l], kv, seq=seq, n_mem=n_mem, tq=t.xattn_rows)
        h = _mm_res([o], w_xattn_out[l], h, tm=t.square_rows, tn=d, name="xattn_out")

        n_logits = N_GROUPS + N_EXPERTS
        wr = jnp.pad(jnp.concatenate([w_router_group[l], w_router_expert[l]], axis=1).T,
                     ((0, ROUTER_ROWS - n_logits), (0, 0)))
        br = jnp.pad(jnp.concatenate([b_router_group[l], b_router_expert[l]]), (0, ROUTER_ROWS - n_logits))
        a, meta, cnt = _router(h, norm_ffn_g[l][None, :], wr, jnp.broadcast_to(br[:, None], (ROUTER_ROWS, LANES)),
                               tm=t.router_rows)

        e_idx = meta[0:2].astype(jnp.int32)
        rank = meta[2:4].astype(jnp.int32)
        counts = cnt[N_GROUPS:N_GROUPS + N_EXPERTS, 0].astype(jnp.int32)
        padded = ((counts + PAD_UNIT - 1) // PAD_UNIT) * PAD_UNIT
        pad_end = jnp.cumsum(padded)
        pad_start = pad_end - padded
        n_rows = 2 * m + N_EXPERTS * PAD_UNIT + PAD_UNIT
        n_units = (pad_end[-1:] // PAD_UNIT).astype(jnp.int32)
        n_full = padded // EXPERT_BLOCK
        n_blk_e = n_full + (padded % EXPERT_BLOCK) // PAD_UNIT
        blk_end = jnp.cumsum(n_blk_e)
        blk_start = blk_end - n_blk_e
        n_blocks = (2 * m) // EXPERT_BLOCK + N_EXPERTS
        n_blk = blk_end[-1:].astype(jnp.int32)
        blk = jnp.arange(n_blocks, dtype=jnp.int32)
        block_e = jnp.minimum(jnp.sum((blk_end[None, :] <= blk[:, None]).astype(jnp.int32), axis=1), N_EXPERTS - 1)
        blk_local = blk - blk_start[block_e]
        block_off = (pad_start[block_e] + EXPERT_BLOCK * blk_local).astype(jnp.int32)
        block_off = jnp.where(blk < n_blk[0], block_off, 0)
        block_half = (blk_local >= n_full[block_e]).astype(jnp.int32)
        after = blk_end[block_e]
        next_e = jnp.where(after < n_blk[0], block_e[jnp.minimum(after, n_blocks - 1)], -1).astype(jnp.int32)

        experts = jnp.arange(N_EXPERTS, dtype=jnp.int32)
        dest = [jnp.sum(jnp.where(e_idx[k][:, None] == experts[None, :], pad_start[None, :], 0), axis=1) + rank[k]
                for k in range(2)]
        xs, slot = _dispatch(dest[0], dest[1], pad_start + counts, padded - counts, n_units, a, n_rows=n_rows,
                             tm=t.dispatch_rows)
        y = _experts(slot, block_e, next_e, block_off, block_half, n_blk, xs, w_expert_gate[l], w_expert_up[l],
                     w_expert_down[l], rb=EXPERT_BLOCK, n_tok=m)
        last = l == depth - 1
        g_fin = norm_final_g[None, :] if last else jnp.ones((1, d), F32)
        h = _combine(y, h, meta, g_fin, tm=t.combine_rows)
        assert last, "the combine kernel fuses the final rmsnorm; deeper stacks need an un-normalised variant"

    return h.reshape(batch, seq, d)
```
